```python
import jax, jax.numpy as jnp
from jax import lax
import numpy as np

D_MODEL = 1024
BATCH = 2
SEQ = 8192
DEPTH = 2

GRID_W = 64
CTX_LEN = 256
HEAD_DIM = 64
HGRN_HEADS = 4
HGRN_DK = 64
HGRN_DV = 64
HGRN_CHUNK = 64
HGRN_QK = HGRN_HEADS * HGRN_DK
HGRN_W = HGRN_HEADS * HGRN_DV
LB_MIN = 1e-6
NA_HEADS = 4
NA_WIN_ROWS = 8
NA_WIN_COLS = 16
NA_KEY_COLS = 2 * NA_WIN_COLS
NA_W = NA_HEADS * HEAD_DIM
SWA_Q_HEADS = 8
SWA_KV_HEADS = 2
SWA_GROUP = SWA_Q_HEADS // SWA_KV_HEADS
SWA_WINDOW = 128
SWA_BLOCK = 128
SWA_W = SWA_Q_HEADS * HEAD_DIM
SWA_KV_W = SWA_KV_HEADS * HEAD_DIM
ROPE_BASE = 10000.0
MIX_W = HGRN_W + NA_W + SWA_W
IN_W = 3 * HGRN_QK + 2 * HGRN_W + 3 * NA_W + SWA_W + 2 * SWA_KV_W
N_EXPERTS = 16
N_GROUPS = 4
TOP_K = 2
D_EXPERT = 512
LN_EPS = 1e-5
RMS_EPS = 1e-6
NEG = -1e30

kernel_name = 'hybrid_hgrn2_natten_swa_moe_dit'

F32 = jnp.float32


def layer_norm(x, g, b):
    xf = x.astype(F32)
    mu = jnp.mean(xf, axis=-1, keepdims=True)
    var = jnp.mean(jnp.square(xf - mu), axis=-1, keepdims=True)
    y = (xf - mu) * lax.rsqrt(var + LN_EPS) * g.astype(F32) + b.astype(F32)
    return y.astype(x.dtype)


def rms_norm(x, g):
    xf = x.astype(F32)
    y = xf * lax.rsqrt(jnp.mean(jnp.square(xf), axis=-1, keepdims=True) + RMS_EPS) * g.astype(F32)
    return y.astype(x.dtype)


def heads(a, n):
    return a.reshape(a.shape[0], a.shape[1], n, -1).transpose(0, 2, 1, 3)


def merge_heads(a):
    return a.transpose(0, 2, 1, 3).reshape(a.shape[0], a.shape[2], -1)


def flip_t(a):
    return jnp.flip(a, axis=2)


def axial_rope(x, row, col):
    half = HEAD_DIM // 2
    nf = half // 2
    inv = ROPE_BASE ** (-jnp.arange(nf, dtype=F32) / nf)

    def rot(xp, pos):
        ang = pos.astype(F32)[:, None] * inv
        cos = jnp.cos(ang).astype(x.dtype)
        sin = jnp.sin(ang).astype(x.dtype)
        x1, x2 = xp[..., :nf], xp[..., nf:]
        return jnp.concatenate([x1 * cos - x2 * sin, x1 * sin + x2 * cos], axis=-1)

    return jnp.concatenate([rot(x[..., :half], row), rot(x[..., half:], col)], axis=-1)


def hgrn_gates(z, lb):
    zf = z.astype(F32)
    logf = jnp.logaddexp(jnp.log(lb), jnp.log1p(-lb) + jax.nn.log_sigmoid(zf))
    k = (1.0 - lb) * jax.nn.sigmoid(-zf)
    return heads(logf, HGRN_HEADS), heads(k, HGRN_HEADS)


def hgrn_scan(q, k, v, logf, s0):
    b_, h_, t_, _ = q.shape
    dv = v.shape[-1]
    nc = t_ // HGRN_CHUNK

    def chunks(a):
        return a.reshape(b_, h_, nc, HGRN_CHUNK, a.shape[-1]).transpose(2, 0, 1, 3, 4)

    lower = jnp.tril(jnp.ones((HGRN_CHUNK, HGRN_CHUNK), dtype=bool))[:, :, None]

    def step(state, inp):
        qc, kc, vc, lf = inp
        cum = jnp.cumsum(lf, axis=2)
        inter = jnp.einsum('bhtd,bhde->bhte', qc * jnp.exp(cum), state)
        decay = jnp.exp(jnp.where(lower, cum[:, :, :, None, :] - cum[:, :, None, :, :], NEG))
        scores = jnp.einsum('bhtd,bhsd,bhtsd->bhts', qc, kc, decay)
        out = inter + jnp.einsum('bhts,bhse->bhte', scores, vc)
        last = cum[:, :, -1:, :]
        new_state = jnp.exp(last[:, :, 0, :, None]) * state + jnp.einsum(
            'bhsd,bhse->bhde', kc * jnp.exp(last - cum), vc)
        return new_state, out

    s_fin, out = lax.scan(step, s0, (chunks(q), chunks(k), chunks(v), chunks(logf)))
    return out.transpose(1, 2, 0, 3, 4).reshape(b_, h_, t_, dv), s_fin


def ctx_attention(q, k, v, sink):
    s = jnp.einsum('bngtd,bnkd->bngtk', q, k).astype(F32) * q.shape[-1] ** -0.5
    if sink is None:
        p = jax.nn.softmax(s, axis=-1)
    else:
        s_sink = jnp.broadcast_to(sink.astype(F32)[None, :, :, None, None], s.shape[:-1] + (1,))
        p = jax.nn.softmax(jnp.concatenate([s, s_sink], axis=-1), axis=-1)[..., :-1]
    return jnp.einsum('bngtk,bnkd->bngtd', p.astype(v.dtype), v)


def neighborhood_attention(q, k, v, kc, vc, rpb):
    b_, h_, l_, dh = q.shape
    rows = l_ // GRID_W
    wr = min(NA_WIN_ROWS, rows)
    ncb = GRID_W // NA_WIN_COLS
    scale = dh ** -0.5
    qcol = jnp.arange(GRID_W).reshape(ncb, NA_WIN_COLS)
    cs = jnp.clip(qcol - NA_WIN_COLS // 2, 0, GRID_W - NA_WIN_COLS)
    blk = jnp.clip(jnp.arange(ncb) * NA_WIN_COLS - NA_WIN_COLS // 2, 0, GRID_W - NA_KEY_COLS)
    kcol = blk[:, None] + jnp.arange(NA_KEY_COLS)
    colmask = (kcol[:, None, :] >= cs[:, :, None]) & (kcol[:, None, :] < cs[:, :, None] + NA_WIN_COLS)
    coff = jnp.clip(kcol[:, None, :] - qcol[:, :, None] + NA_WIN_COLS - 1, 0, 2 * NA_WIN_COLS - 2)
    rpb_c = rpb[:, :, coff].astype(F32)
    kg = k.reshape(b_, h_, rows, GRID_W, dh)
    vg = v.reshape(b_, h_, rows, GRID_W, dh)
    qg = q.reshape(b_, h_, rows, ncb, NA_WIN_COLS, dh).transpose(2, 0, 1, 3, 4, 5)
    nw = wr * NA_KEY_COLS

    def row_block(inp):
        r, qr = inp
        rs = jnp.clip(r - wr // 2, 0, rows - wr)
        kw = jnp.take(lax.dynamic_slice_in_dim(kg, rs, wr, axis=2), kcol, axis=3)
        vw = jnp.take(lax.dynamic_slice_in_dim(vg, rs, wr, axis=2), kcol, axis=3)
        ridx = rs + jnp.arange(wr) - r + NA_WIN_ROWS - 1
        bias = jnp.take(rpb_c, ridx, axis=1).transpose(0, 2, 3, 1, 4)
        s_win = jnp.einsum('bhjud,bhrjvd->bhjurv', qr, kw).astype(F32) * scale + bias
        s_win = jnp.where(colmask[:, :, None, :], s_win, NEG)
        s_ctx = jnp.einsum('bhjud,bhkd->bhjuk', qr, kc).astype(F32) * scale
        p = jax.nn.softmax(jnp.concatenate(
            [s_win.reshape(b_, h_, ncb, NA_WIN_COLS, nw), s_ctx], axis=-1), axis=-1).astype(q.dtype)
        p_win = p[..., :nw].reshape(b_, h_, ncb, NA_WIN_COLS, wr, NA_KEY_COLS)
        return (jnp.einsum('bhjurv,bhrjvd->bhjud', p_win, vw)
                + jnp.einsum('bhjuk,bhkd->bhjud', p[..., nw:], vc))

    o = lax.map(row_block, (jnp.arange(rows), qg))
    return o.transpose(1, 2, 0, 3, 4, 5).reshape(b_, h_, l_, dh)


def swa_attention(q, k, v, kc, vc, sink):
    b_, n_, g_, l_, dh = q.shape
    nb = l_ // SWA_BLOCK
    nctx = kc.shape[2]
    scale = dh ** -0.5
    qb = q.reshape(b_, n_, g_, nb, SWA_BLOCK, dh)

    def band(a):
        ap = jnp.pad(a, ((0, 0), (0, 0), (SWA_BLOCK, SWA_BLOCK), (0, 0))).reshape(b_, n_, nb + 2, SWA_BLOCK, dh)
        return jnp.concatenate([ap[:, :, :-2], ap[:, :, 1:-1], ap[:, :, 2:]], axis=3)

    kb, vb = band(k), band(v)
    qpos = jnp.arange(nb)[:, None] * SWA_BLOCK + jnp.arange(SWA_BLOCK)[None, :]
    kpos = (jnp.arange(nb)[:, None] - 1) * SWA_BLOCK + jnp.arange(3 * SWA_BLOCK)[None, :]
    mask = ((kpos[:, None, :] >= 0) & (kpos[:, None, :] < l_)
            & (jnp.abs(qpos[:, :, None] - kpos[:, None, :]) <= SWA_WINDOW))
    s_band = jnp.where(mask, jnp.einsum('bngjud,bnjvd->bngjuv', qb, kb).astype(F32) * scale, NEG)
    s_ctx = jnp.einsum('bngjud,bnkd->bngjuk', qb, kc).astype(F32) * scale
    s_sink = jnp.broadcast_to(sink.astype(F32)[None, :, :, None, None, None], s_ctx.shape[:-1] + (1,))
    p = jax.nn.softmax(jnp.concatenate([s_band, s_ctx, s_sink], axis=-1), axis=-1).astype(q.dtype)
    nk = 3 * SWA_BLOCK
    o = (jnp.einsum('bngjuv,bnjvd->bngjud', p[..., :nk], vb)
         + jnp.einsum('bngjuk,bnkd->bngjud', p[..., nk:nk + nctx], vc))
    return o.reshape(b_, n_, g_, l_, dh)


def mixer(h, hc, w_in, lb_fwd, lb_bwd, hgrn_norm, rpb, sink, w_out, with_ctx_out):
    b_, l_, _ = h.shape
    t = jnp.arange(l_)
    row, col = t // GRID_W, t % GRID_W
    sizes = [HGRN_QK, HGRN_QK, HGRN_QK, HGRN_W, HGRN_W, NA_W, NA_W, NA_W, SWA_W, SWA_KV_W, SWA_KV_W]
    splits = np.cumsum(sizes)[:-1].tolist()
    aq, af, ab, ai, ag, nq, nk, nv, sq, sk, sv = jnp.split(h @ w_in, splits, axis=-1)
    caq, caf, cab, cai, cag, cnq, cnk, cnv, csq, csk, csv = jnp.split(hc @ w_in, splits, axis=-1)

    lf_f, k_f = hgrn_gates(af, lb_fwd)
    lf_b, k_b = hgrn_gates(ab, lb_bwd)
    clf_f, ck_f = hgrn_gates(caf, lb_fwd)
    clf_b, ck_b = hgrn_gates(cab, lb_bwd)
    q_a, v_a = heads(aq, HGRN_HEADS).astype(F32), heads(ai, HGRN_HEADS).astype(F32)
    cq_a, cv_a = heads(caq, HGRN_HEADS).astype(F32), heads(cai, HGRN_HEADS).astype(F32)
    s0 = jnp.zeros((b_, HGRN_HEADS, HGRN_DK, HGRN_DV), F32)
    oc_f, sc_f = hgrn_scan(cq_a, ck_f, cv_a, clf_f, s0)
    oc_b, sc_b = hgrn_scan(flip_t(cq_a), flip_t(ck_b), flip_t(cv_a), flip_t(clf_b), s0)
    o_f, _ = hgrn_scan(q_a, k_f, v_a, lf_f, sc_f)
    o_b, _ = hgrn_scan(flip_t(q_a), flip_t(k_b), flip_t(v_a), flip_t(lf_b), sc_b)
    y_a = merge_heads(rms_norm(o_f + flip_t(o_b), hgrn_norm)).astype(h.dtype) * jax.nn.silu(ag)

    kc_n, vc_n = heads(cnk, NA_HEADS), heads(cnv, NA_HEADS)
    y_b = merge_heads(neighborhood_attention(heads(nq, NA_HEADS), heads(nk, NA_HEADS),
                                             heads(nv, NA_HEADS), kc_n, vc_n, rpb))

    q_s = axial_rope(heads(sq, SWA_Q_HEADS), row, col).reshape(b_, SWA_KV_HEADS, SWA_GROUP, l_, HEAD_DIM)
    k_s = axial_rope(heads(sk, SWA_KV_HEADS), row, col)
    v_s = heads(sv, SWA_KV_HEADS)
    kc_s, vc_s = heads(csk, SWA_KV_HEADS), heads(csv, SWA_KV_HEADS)
    sink_g = sink.reshape(SWA_KV_HEADS, SWA_GROUP)
    y_c = merge_heads(swa_attention(q_s, k_s, v_s, kc_s, vc_s, sink_g).reshape(b_, SWA_Q_HEADS, l_, HEAD_DIM))

    y = jnp.concatenate([y_a, y_b, y_c], axis=-1) @ w_out
    if not with_ctx_out:
        return y, None
    nctx = hc.shape[1]
    yc_a = merge_heads(rms_norm(oc_f + flip_t(oc_b), hgrn_norm)).astype(hc.dtype) * jax.nn.silu(cag)
    yc_b = merge_heads(ctx_attention(heads(cnq, NA_HEADS)[:, :, None], kc_n, vc_n, None)[:, :, 0])
    cq_s = heads(csq, SWA_Q_HEADS).reshape(b_, SWA_KV_HEADS, SWA_GROUP, nctx, HEAD_DIM)
    yc_c = merge_heads(ctx_attention(cq_s, kc_s, vc_s, sink_g).reshape(b_, SWA_Q_HEADS, nctx, HEAD_DIM))
    yc = jnp.concatenate([yc_a, yc_b, yc_c], axis=-1) @ w_out
    return y, yc


def moe(h, w_router, router_bias, w_gate, w_up, w_down):
    n = h.shape[0]
    epg = N_EXPERTS // N_GROUPS
    s = jax.nn.sigmoid((h @ w_router).astype(F32))
    sel = s + router_bias.astype(F32)
    grp = lax.top_k(sel.reshape(n, N_GROUPS, epg), TOP_K)[0].sum(-1)
    best = jnp.argmax(grp, axis=-1)
    in_group = (jnp.arange(N_EXPERTS) // epg)[None, :] == best[:, None]
    _, idx = lax.top_k(jnp.where(in_group, sel, NEG), TOP_K)
    w = jnp.take_along_axis(s, idx, axis=-1)
    w = w / jnp.sum(w, axis=-1, keepdims=True)
    gates = jnp.sum(jax.nn.one_hot(idx, N_EXPERTS, dtype=F32) * w[..., None], axis=1).astype(h.dtype)
    y = jnp.zeros_like(h)
    for e in range(N_EXPERTS):
        he = jax.nn.silu(h @ w_gate[e]) * (h @ w_up[e])
        y = y + gates[:, e:e + 1] * (he @ w_down[e])
    return y


def ada_mod(cond, w, b):
    m = (jax.nn.silu(cond) @ w + b)[..., None, :]
    return jnp.split(m, 6, axis=-1)


def setup_inputs(seed: int = 0) -> dict:
    key = jax.random.key(seed)
    ks = jax.random.split(key, 21)
    nrm = jax.random.normal
    beta = (8.0 * DEPTH) ** -0.25
    d = D_MODEL
    return {
        'x': nrm(ks[0], (BATCH, SEQ, d), F32),
        'c': nrm(ks[1], (BATCH, d), F32),
        'ctx': nrm(ks[2], (BATCH, CTX_LEN, d), F32),
        'c_ctx': nrm(ks[3], (d,), F32),
        'w_ada': nrm(ks[4], (DEPTH, d, 6 * d), F32) * (0.5 * d ** -0.5),
        'b_ada': nrm(ks[5], (DEPTH, 6 * d), F32) * 0.02,
        'w_in': nrm(ks[6], (DEPTH, d, IN_W), F32) * d ** -0.5,
        'lb_logits': nrm(ks[7], (DEPTH, 2, HGRN_QK), F32),
        'hgrn_norm': 1.0 + 0.1 * nrm(ks[8], (DEPTH, HGRN_DV), F32),
        'na_rpb': 0.02 * nrm(ks[9], (DEPTH, NA_HEADS, 2 * NA_WIN_ROWS - 1, 2 * NA_WIN_COLS - 1), F32),
        'swa_sink': nrm(ks[10], (DEPTH, SWA_Q_HEADS), F32),
        'w_out': nrm(ks[11], (DEPTH, MIX_W, d), F32) * (MIX_W ** -0.5 * beta),
        'ln1_g': 1.0 + 0.1 * nrm(ks[12], (DEPTH, d), F32),
        'ln1_b': 0.02 * nrm(ks[13], (DEPTH, d), F32),
        'ln2_g': 1.0 + 0.1 * nrm(ks[14], (DEPTH, d), F32),
        'ln2_b': 0.02 * nrm(ks[15], (DEPTH, d), F32),
        'w_router': nrm(ks[16], (d, N_EXPERTS), F32) * d ** -0.5,
        'router_bias': 0.01 * nrm(ks[17], (N_EXPERTS,), F32),
        'w_gate': nrm(ks[18], (DEPTH, N_EXPERTS, d, D_EXPERT), F32) * d ** -0.5,
        'w_up': nrm(ks[19], (DEPTH, N_EXPERTS, d, D_EXPERT), F32) * d ** -0.5,
        'w_down': nrm(ks[20], (DEPTH, N_EXPERTS, D_EXPERT, d), F32) * (D_EXPERT ** -0.5 * beta),
    }


def reference(x, c, ctx, c_ctx, w_ada, b_ada, w_in, lb_logits, hgrn_norm, na_rpb, swa_sink, w_out,
              ln1_g, ln1_b, ln2_g, ln2_b, w_router, router_bias, w_gate, w_up, w_down):
    b_, l_, d = x.shape
    alpha = (2.0 * DEPTH) ** 0.25
    p_lb = jax.nn.softmax(lb_logits.astype(F32), axis=0)
    lb_all = jnp.maximum(jnp.cumsum(p_lb, axis=0) - p_lb[0:1], LB_MIN)
    for l in range(DEPTH):
        last = l == DEPTH - 1
        sh1, sc1, g1, sh2, sc2, g2 = ada_mod(c, w_ada[l], b_ada[l])
        csh1, csc1, cg1, csh2, csc2, cg2 = ada_mod(c_ctx, w_ada[l], b_ada[l])
        h = x * (1.0 + sc1) + sh1
        hc = ctx * (1.0 + csc1) + csh1
        y, yc = mixer(h, hc, w_in[l], lb_all[l, 0], lb_all[l, 1], hgrn_norm[l], na_rpb[l], swa_sink[l],
                      w_out[l], not last)
        x = layer_norm(alpha * x + g1 * y, ln1_g[l], ln1_b[l])
        h2 = x * (1.0 + sc2) + sh2
        if last:
            f = moe(h2.reshape(-1, d), w_router, router_bias, w_gate[l], w_up[l], w_down[l]).reshape(x.shape)
            x = layer_norm(alpha * x + g2 * f, ln2_g[l], ln2_b[l])
        else:
            ctx = layer_norm(alpha * ctx + cg1 * yc, ln1_g[l], ln1_b[l])
            hc2 = ctx * (1.0 + csc2) + csh2
            f = moe(jnp.concatenate([h2, hc2], axis=1).reshape(-1, d), w_router, router_bias,
                    w_gate[l], w_up[l], w_down[l]).reshape(b_, l_ + ctx.shape[1], d)
            x = layer_norm(alpha * x + g2 * f[:, :l_], ln2_g[l], ln2_b[l])
            ctx = layer_norm(alpha * ctx + cg2 * f[:, l_:], ln2_g[l], ln2_b[l])
    return x
```

```python
import functools

import numpy as np
import jax
import jax.numpy as jnp
from jax import lax
from jax.experimental import pallas as pl
from jax.experimental.pallas import tpu as pltpu

F32 = jnp.float32
BF16 = jnp.bfloat16

D_MODEL = 1024
GRID_W = 64
HEAD_DIM = 64
HGRN_W = 256
NA_HEADS = 4
NA_WIN_ROWS = 8
NA_WIN_COLS = 16
SWA_Q_HEADS = 8
SWA_KV_HEADS = 2
SWA_GROUP = 4
SWA_WINDOW = 128
SWA_BLOCK = 128
ROPE_BASE = 10000.0
N_EXPERTS = 16
N_GROUPS = 4
D_EXPERT = 512
LN_EPS = 1e-5
RMS_EPS = 1e-6
NEG = -1e30
LB_MIN = 1e-6
DEPTH = 2
ALPHA = (2.0 * DEPTH) ** 0.25

LANES = 128
MXU_N = 256
TM = 512
HB = 256
HC = 16
NA_QROWS = 4
VMEM_LIMIT = 56 * 1024 * 1024

N_COND = 8
IN_CHUNKS = 14


def _dot(a, b):
    return jnp.dot(a, b, preferred_element_type=F32)


def _dot_nt(a, b):
    return lax.dot_general(a, b, (((1,), (1,)), ((), ())), preferred_element_type=F32)


def _dot_tn(a, b):
    return lax.dot_general(a, b, (((0,), (0,)), ((), ())), preferred_element_type=F32)


def _sigmoid(x):
    return 1.0 / (1.0 + jnp.exp(-x))


def _split_bf16(x):
    hi = x.astype(BF16)
    lo = (x - hi.astype(F32)).astype(BF16)
    return hi, lo


def _params(sem):
    return pltpu.CompilerParams(dimension_semantics=sem, vmem_limit_bytes=VMEM_LIMIT)


def _ada_kernel(n_rows, condt_ref, w_ref, b_ref, o_ref):
    c = condt_ref[...]
    s = c * _sigmoid(c)
    w = w_ref[0]
    rows = [jnp.sum(w * s[:, r:r + 1], axis=0, keepdims=True) for r in range(n_rows)]
    rows.append(jnp.zeros((N_COND - n_rows, w.shape[1]), F32))
    o_ref[0] = jnp.concatenate(rows, axis=0) + b_ref[0]


def _ada(cond_t, w_ada, b_ada, n_rows):
    depth, d, n6 = w_ada.shape
    tn = 1024
    return pl.pallas_call(
        functools.partial(_ada_kernel, n_rows),
        grid=(depth, n6 // tn),
        in_specs=[pl.BlockSpec((d, N_COND), lambda l, n: (0, 0)),
                  pl.BlockSpec((1, d, tn), lambda l, n: (l, 0, n)),
                  pl.BlockSpec((1, 1, tn), lambda l, n: (l, 0, n))],
        out_specs=pl.BlockSpec((1, N_COND, tn), lambda l, n: (l, 0, n)),
        out_shape=jax.ShapeDtypeStruct((depth, N_COND, n6), F32),
        compiler_params=_params(("arbitrary", "arbitrary")),
        name="ada",
    )(cond_t, w_ada, b_ada.reshape(depth, 1, n6))


def _rope(z, cos, sin, first):
    sw = jnp.where(first, pltpu.roll(z, LANES - 16, axis=1), pltpu.roll(z, 16, axis=1))
    return z * cos + sw * sin


def _inproj_kernel(x_ref, mod_ref, w_ref, gp_ref, cos_ref, sin_ref,
                   hq_ref, hv_ref, hg_ref, lf_ref, kk_ref, nqkv_ref, sq_ref, kx_ref, vx_ref):
    mod = mod_ref[0]
    h = (x_ref[...] * (1.0 + mod[1:2]) + mod[0:1]).astype(BF16)
    gp = gp_ref[...]
    cos = cos_ref[...]
    sin = sin_ref[...]
    lane = lax.broadcasted_iota(jnp.int32, cos.shape, 1)
    first = (lane & 16) == 0
    scale = HEAD_DIM ** -0.5

    def chunk(c):
        return _dot(h, w_ref[:, c * MXU_N:(c + 1) * MXU_N])

    def gates(z, r):
        a = gp[r:r + 1, 0:MXU_N]
        b = gp[r + 1:r + 2, 0:MXU_N] + (jnp.minimum(z, 0.0) - jnp.log1p(jnp.exp(-jnp.abs(z))))
        logf = jnp.maximum(a, b) + jnp.log1p(jnp.exp(-jnp.abs(a - b)))
        k = gp[r + 2:r + 3, 0:MXU_N] * (1.0 / (1.0 + jnp.exp(z)))
        return logf, k

    def rope2(z):
        return jnp.concatenate([_rope(z[:, 0:LANES], cos, sin, first),
                                _rope(z[:, LANES:2 * LANES], cos, sin, first)], axis=1)

    hq_ref[...] = chunk(0).astype(BF16)
    for d in range(2):
        logf, k = gates(chunk(1 + d), 3 * d)
        lf_ref[:, d * MXU_N:(d + 1) * MXU_N] = logf
        kk_ref[:, d * MXU_N:(d + 1) * MXU_N] = k.astype(BF16)
    hv_ref[...] = chunk(3).astype(BF16)
    hg_ref[...] = chunk(4).astype(BF16)
    nqkv_ref[:, 0:MXU_N] = (chunk(5) * scale).astype(BF16)
    nqkv_ref[:, MXU_N:2 * MXU_N] = chunk(6).astype(BF16)
    nqkv_ref[:, 2 * MXU_N:3 * MXU_N] = chunk(7).astype(BF16)
    for d in range(2):
        sq_ref[:, d * MXU_N:(d + 1) * MXU_N] = (rope2(chunk(8 + d)) * scale).astype(BF16)
        kx_ref[:, d * MXU_N:(d + 1) * MXU_N] = rope2(chunk(10 + d)).astype(BF16)
        vx_ref[:, d * MXU_N:(d + 1) * MXU_N] = chunk(12 + d).astype(BF16)


def _inproj(xf, mod, w, gp, cos_t, sin_t, n_b, l_):
    nt, d = xf.shape
    tpb = l_ // TM
    n_lat = n_b * tpb

    def mod_idx(j):
        return (jnp.minimum(j // tpb, n_b), 0, 0)

    def rope_idx(j):
        return (jnp.where(j < n_lat, j % tpb, tpb), 0)

    def rows(width):
        return pl.BlockSpec((TM, width), lambda j: (j, 0))

    widths = [(HGRN_W, BF16), (HGRN_W, BF16), (HGRN_W, BF16), (2 * HGRN_W, F32), (2 * HGRN_W, BF16),
              (3 * 256, BF16), (512, BF16), (512, BF16), (512, BF16)]
    return pl.pallas_call(
        _inproj_kernel,
        grid=(nt // TM,),
        in_specs=[rows(d),
                  pl.BlockSpec((1, 6, d), mod_idx),
                  pl.BlockSpec(w.shape, lambda j: (0, 0)),
                  pl.BlockSpec(gp.shape, lambda j: (0, 0)),
                  pl.BlockSpec((TM, LANES), rope_idx),
                  pl.BlockSpec((TM, LANES), rope_idx)],
        out_specs=[rows(wd) for wd, _ in widths],
        out_shape=[jax.ShapeDtypeStruct((nt, wd), dt) for wd, dt in widths],
        compiler_params=_params(("arbitrary",)),
        name="inproj",
    )(xf, mod, w, gp, cos_t, sin_t)


def _hgrn_kernel(qf_ref, vf_ref, lff_ref, kf_ref, qb_ref, vb_ref, lfb_ref, kb_ref,
                 of_ref, ob_ref, st_ref, qd_ref, kd_ref, vv_ref, gt_ref, it_ref):
    nch = HB // HC

    @pl.when(pl.program_id(1) == 0)
    def _():
        st_ref[...] = jnp.zeros(st_ref.shape, F32)

    row = lax.broadcasted_iota(jnp.int32, (HB, HB), 0)
    col = lax.broadcasted_iota(jnp.int32, (HB, HB), 1)
    same_chunk = (row // HC) == (col // HC)
    same_head = (row // HEAD_DIM) == (col // HEAD_DIM)
    head_ones = jnp.where(same_head, 1.0, 0.0).astype(BF16)
    chunk_ones = jnp.where(same_chunk, 1.0, 0.0).astype(BF16)
    rid = lax.broadcasted_iota(jnp.int32, (nch, HC, HGRN_W), 1)

    def prep(d, q_ref, v_ref, lf_ref, k_ref):
        fwd = d == 0
        tri = jnp.where(same_chunk & ((col <= row) if fwd else (col >= row)), 1.0, 0.0).astype(BF16)
        hi, lo = _split_bf16(lf_ref[...])
        cum = _dot(tri, hi) + _dot(tri, lo)
        tot = _dot(chunk_ones, hi) + _dot(chunk_ones, lo)
        q = q_ref[...].astype(F32)
        k = k_ref[...].astype(F32)
        v = v_ref[...]
        qd_ref[d] = (q * jnp.exp(cum)).astype(BF16)
        kd_ref[d] = (k * jnp.exp(tot - cum)).astype(BF16)
        vv_ref[d] = v
        gt_ref[d] = jnp.exp(tot)
        cum3 = cum.reshape(nch, HC, HGRN_W)
        q3 = q.reshape(nch, HC, HGRN_W)
        k3 = k.reshape(nch, HC, HGRN_W)
        v3 = v.astype(F32).reshape(nch, HC, HGRN_W)
        od = jnp.zeros((HB, HGRN_W), F32)
        for s in range(HC):
            dec = jnp.exp(jnp.minimum(cum3 - cum3[:, s:s + 1, :], 0.0))
            valid = (rid >= s) if fwd else (rid <= s)
            a = jnp.where(valid, q3 * k3[:, s:s + 1, :] * dec, 0.0)
            sc = _dot(a.reshape(HB, HGRN_W).astype(BF16), head_ones)
            vs = jnp.broadcast_to(v3[:, s:s + 1, :], (nch, HC, HGRN_W)).reshape(HB, HGRN_W)
            od = od + sc * vs
        return od

    od_f = prep(0, qf_ref, vf_ref, lff_ref, kf_ref)
    od_b = prep(1, qb_ref, vb_ref, lfb_ref, kb_ref)

    def body(i, carry):
        for d, c in ((0, i), (1, nch - 1 - i)):
            off = pl.multiple_of(c * HC, HC)
            qd_c = qd_ref[d, pl.ds(off, HC), :]
            kd_c = kd_ref[d, pl.ds(off, HC), :]
            v_c = vv_ref[d, pl.ds(off, HC), :]
            g_c = gt_ref[d, pl.ds(off, 1), :]
            st = st_ref[d]
            it_ref[d, pl.ds(off, HC), :] = _dot_nt(qd_c, st.astype(BF16))
            upd = _dot_tn(v_c, kd_c)
            st_ref[d] = st * g_c + jnp.where(same_head, upd, 0.0)
        return carry

    lax.fori_loop(0, nch, body, 0)
    of_ref[...] = it_ref[0] + od_f
    ob_ref[...] = it_ref[1] + od_b


def _hgrn(hq, hv, lf, kk, n_b, l_, ctx_len):
    nt = hq.shape[0]
    assert ctx_len == HB
    nlb = l_ // HB
    ctx0 = n_b * nlb

    def fwd_idx(col):
        return lambda b, s: (jnp.where(s == 0, ctx0 + b, b * nlb + s - 1), col)

    def bwd_idx(col):
        return lambda b, s: (jnp.where(s == 0, ctx0 + b, b * nlb + nlb - s), col)

    def blk(idx):
        return pl.BlockSpec((HB, HGRN_W), idx)

    return pl.pallas_call(
        _hgrn_kernel,
        grid=(n_b, nlb + 1),
        in_specs=[blk(fwd_idx(0)), blk(fwd_idx(0)), blk(fwd_idx(0)), blk(fwd_idx(0)),
                  blk(bwd_idx(0)), blk(bwd_idx(0)), blk(bwd_idx(1)), blk(bwd_idx(1))],
        out_specs=[blk(fwd_idx(0)), blk(bwd_idx(0))],
        out_shape=[jax.ShapeDtypeStruct((nt, HGRN_W), F32)] * 2,
        scratch_shapes=[pltpu.VMEM((2, HGRN_W, HGRN_W), F32),
                        pltpu.VMEM((2, HB, HGRN_W), BF16),
                        pltpu.VMEM((2, HB, HGRN_W), BF16),
                        pltpu.VMEM((2, HB, HGRN_W), BF16),
                        pltpu.VMEM((2, HB, HGRN_W), F32),
                        pltpu.VMEM((2, HB, HGRN_W), F32)],
        compiler_params=_params(("arbitrary", "arbitrary")),
        name="hgrn",
    )(hq, hv, lf, kk, hq, hv, lf, kk)


def _head_stack(q, n_heads):
    lane = lax.broadcasted_iota(jnp.int32, q.shape, 1)
    zero = jnp.zeros_like(q)
    return jnp.concatenate([jnp.where(lane // HEAD_DIM == h, q, zero) for h in range(n_heads)], axis=0)


def _head_unstack(o, n_heads):
    rows = o.shape[0] // n_heads
    lane = lax.broadcasted_iota(jnp.int32, (rows, o.shape[1]), 1)
    acc = jnp.zeros((rows, o.shape[1]), F32)
    for h in range(n_heads):
        acc = acc + jnp.where(lane // HEAD_DIM == h, o[h * rows:(h + 1) * rows], 0.0)
    return acc


def _na_kernel(n_rows, n_lat_steps, q_ref, k_ref, v_ref, kc_ref, vc_ref, bias_ref, o_ref):
    j = pl.program_id(1)
    kc = kc_ref[...]
    vc = vc_ref[...]
    nwin = NA_WIN_ROWS * GRID_W

    def attend(rr, win):
        q4 = _head_stack(q_ref[rr * GRID_W:(rr + 1) * GRID_W, :], NA_HEADS)
        s_ctx = _dot_nt(q4, kc)
        m = jnp.max(s_ctx, axis=1, keepdims=True)
        if win is not None:
            kw, vw, bias = win
            s_win = _dot_nt(q4, kw) + bias
            m = jnp.maximum(m, jnp.max(s_win, axis=1, keepdims=True))
            p_win = jnp.exp(s_win - m)
        p_ctx = jnp.exp(s_ctx - m)
        den = jnp.sum(p_ctx, axis=1, keepdims=True)
        o4 = _dot(p_ctx.astype(BF16), vc)
        if win is not None:
            den = den + jnp.sum(p_win, axis=1, keepdims=True)
            o4 = o4 + _dot(p_win.astype(BF16), vw)
        o = _head_unstack(o4 * (1.0 / den), NA_HEADS)
        o_ref[rr * GRID_W:(rr + 1) * GRID_W, :] = o.astype(BF16)

    @pl.when(j < n_lat_steps)
    def _():
        for rr in range(NA_QROWS):
            r = j * NA_QROWS + rr
            rs = jnp.clip(r - NA_WIN_ROWS // 2, 0, n_rows - NA_WIN_ROWS)
            start = pl.multiple_of(rs * GRID_W, GRID_W)
            kw = k_ref[pl.ds(start, nwin), :]
            vw = v_ref[pl.ds(start, nwin), :]
            attend(rr, (kw, vw, bias_ref[r - rs]))

    @pl.when(j >= n_lat_steps)
    def _():
        for rr in range(NA_QROWS):
            attend(rr, None)


def _na(nqkv, bias, n_b, l_, ctx_len, with_ctx):
    nt = nqkv.shape[0]
    qb = NA_QROWS * GRID_W
    assert ctx_len == qb
    n_lat = l_ // qb
    ctx0 = n_b * n_lat
    n_rows = l_ // GRID_W

    def q_idx(b, j):
        return (jnp.where(j < n_lat, b * n_lat + j, ctx0 + b), 0)

    cblk = (n_b * l_) // ctx_len
    return pl.pallas_call(
        functools.partial(_na_kernel, n_rows, n_lat),
        grid=(n_b, n_lat + (1 if with_ctx else 0)),
        in_specs=[pl.BlockSpec((qb, 256), q_idx),
                  pl.BlockSpec((l_, 256), lambda b, j: (b, 1)),
                  pl.BlockSpec((l_, 256), lambda b, j: (b, 2)),
                  pl.BlockSpec((ctx_len, 256), lambda b, j: (cblk + b, 1)),
                  pl.BlockSpec((ctx_len, 256), lambda b, j: (cblk + b, 2)),
                  pl.BlockSpec(bias.shape, lambda b, j: (0, 0, 0))],
        out_specs=pl.BlockSpec((qb, 256), q_idx),
        out_shape=jax.ShapeDtypeStruct((nt, 256), BF16),
        compiler_params=_params(("arbitrary", "arbitrary")),
        name="natten",
    )(nqkv, nqkv, nqkv, nqkv, nqkv, bias)


def _swa_kernel(n_blocks, l_, sink_ref, q_ref, kp_ref, kq_ref, kn_ref, vp_ref, vq_ref, vn_ref,
                kc_ref, vc_ref, o_ref):
    j = pl.program_id(1)
    sb = SWA_BLOCK

    def attend(n, band):
        cols = slice(n * MXU_N, (n + 1) * MXU_N)
        q4 = _head_stack(q_ref[:, cols], SWA_GROUP)
        sink = jnp.concatenate([jnp.full((sb, 1), sink_ref[n * SWA_GROUP + g], F32)
                                for g in range(SWA_GROUP)], axis=0)
        s_ctx = _dot_nt(q4, kc_ref[:, cols])
        m = jnp.maximum(jnp.max(s_ctx, axis=1, keepdims=True), sink)
        if band is not None:
            kb = jnp.concatenate([kp_ref[:, cols], kq_ref[:, cols], kn_ref[:, cols]], axis=0)
            vb = jnp.concatenate([vp_ref[:, cols], vq_ref[:, cols], vn_ref[:, cols]], axis=0)
            s_band = jnp.where(band, _dot_nt(q4, kb), NEG)
            m = jnp.maximum(m, jnp.max(s_band, axis=1, keepdims=True))
            p_band = jnp.exp(s_band - m)
        p_ctx = jnp.exp(s_ctx - m)
        den = jnp.sum(p_ctx, axis=1, keepdims=True) + jnp.exp(sink - m)
        o4 = _dot(p_ctx.astype(BF16), vc_ref[:, cols])
        if band is not None:
            den = den + jnp.sum(p_band, axis=1, keepdims=True)
            o4 = o4 + _dot(p_band.astype(BF16), vb)
        o_ref[:, cols] = _head_unstack(o4 * (1.0 / den), SWA_GROUP).astype(BF16)

    @pl.when(j < n_blocks)
    def _():
        u = lax.broadcasted_iota(jnp.int32, (SWA_GROUP * sb, 3 * sb), 0) % sb
        v = lax.broadcasted_iota(jnp.int32, (SWA_GROUP * sb, 3 * sb), 1)
        kpos = (j - 1) * sb + v
        dist = j * sb + u - kpos
        band = (kpos >= 0) & (kpos < l_) & (dist <= SWA_WINDOW) & (dist >= -SWA_WINDOW)
        for n in range(SWA_KV_HEADS):
            attend(n, band)

    @pl.when(j >= n_blocks)
    def _():
        for n in range(SWA_KV_HEADS):
            attend(n, None)


def _swa(sq, kx, vx, sink, n_b, l_, ctx_len, with_ctx):
    nt = sq.shape[0]
    sb = SWA_BLOCK
    nb = l_ // sb
    cper = ctx_len // sb
    ctx0 = n_b * nb

    def q_idx(b, j):
        return (jnp.where(j < nb, b * nb + j, ctx0 + b * cper + (j - nb)), 0)

    def band_idx(off):
        return lambda b, j: (b * nb + jnp.clip(jnp.minimum(j, nb - 1) + off, 0, nb - 1), 0)

    cblk = (n_b * l_) // ctx_len
    blk = lambda idx: pl.BlockSpec((sb, 512), idx)
    cspec = pl.BlockSpec((ctx_len, 512), lambda b, j: (cblk + b, 0))
    return pl.pallas_call(
        functools.partial(_swa_kernel, nb, l_),
        grid=(n_b, nb + (cper if with_ctx else 0)),
        in_specs=[pl.BlockSpec(memory_space=pltpu.SMEM),
                  blk(q_idx), blk(band_idx(-1)), blk(band_idx(0)), blk(band_idx(1)),
                  blk(band_idx(-1)), blk(band_idx(0)), blk(band_idx(1)), cspec, cspec],
        out_specs=blk(q_idx),
        out_shape=jax.ShapeDtypeStruct((nt, 512), BF16),
        compiler_params=_params(("arbitrary", "arbitrary")),
        name="swa",
    )(sink, sq, kx, kx, kx, vx, vx, vx, kx, vx)


def _layer_norm(r, g, b):
    mu = jnp.mean(r, axis=-1, keepdims=True)
    rc = r - mu
    var = jnp.mean(rc * rc, axis=-1, keepdims=True)
    return rc * lax.rsqrt(var + LN_EPS) * g + b


def _route(logits_t, bias_col):
    epg = N_EXPERTS // N_GROUPS
    s = _sigmoid(logits_t)
    sel = s + bias_col
    sel_r = [sel[e:e + 1, :] for e in range(N_EXPERTS)]
    s_r = [s[e:e + 1, :] for e in range(N_EXPERTS)]
    grp = []
    for g in range(N_GROUPS):
        a = sel_r[g * epg:(g + 1) * epg]
        m1 = functools.reduce(jnp.maximum, a)
        m2 = functools.reduce(jnp.maximum,
                              [jnp.minimum(a[i], a[k]) for i in range(epg) for k in range(i + 1, epg)])
        grp.append(m1 + m2)
    w_r = []
    for g in range(N_GROUPS):
        best = None
        for k in range(N_GROUPS):
            if k == g:
                continue
            c = (grp[g] > grp[k]) if k < g else (grp[g] >= grp[k])
            best = c if best is None else (best & c)
        for e in range(g * epg, (g + 1) * epg):
            rank = jnp.zeros_like(sel_r[e])
            for k in range(g * epg, (g + 1) * epg):
                if k == e:
                    continue
                ahead = (sel_r[k] >= sel_r[e]) if k < e else (sel_r[k] > sel_r[e])
                rank = rank + jnp.where(ahead, 1.0, 0.0)
            w_r.append(jnp.where(best & (rank < 1.5), s_r[e], 0.0))
    den = functools.reduce(lambda a, b: a + b, w_r)
    inv = 1.0 / den
    return jnp.concatenate([w * inv for w in w_r], axis=0)


def _outproj_kernel(of_ref, ob_ref, hg_ref, yb_ref, yc_ref, x_ref, mod_ref, wo_ref, pv_ref,
                    wrh_ref, wrl_ref, rb_ref, x1_ref, h2_ref, gates_ref):
    row = lax.broadcasted_iota(jnp.int32, (HGRN_W, HGRN_W), 0)
    col = lax.broadcasted_iota(jnp.int32, (HGRN_W, HGRN_W), 1)
    head_ones = jnp.where((row // HEAD_DIM) == (col // HEAD_DIM), 1.0, 0.0).astype(BF16)
    pv = pv_ref[...]
    o = of_ref[...] + ob_ref[...]
    hi, lo = _split_bf16(o * o)
    ms = (_dot(hi, head_ones) + _dot(lo, head_ones)) * (1.0 / HEAD_DIM)
    gate = hg_ref[...].astype(F32)
    ya = o * lax.rsqrt(ms + RMS_EPS) * pv[2:3, 0:HGRN_W] * (gate * _sigmoid(gate))
    y = (_dot(ya.astype(BF16), wo_ref[0:256, :]) + _dot(yb_ref[...], wo_ref[256:512, :])
         + _dot(yc_ref[...], wo_ref[512:1024, :]))
    mod = mod_ref[0]
    x1 = _layer_norm(ALPHA * x_ref[...] + mod[2:3] * y, pv[0:1], pv[1:2])
    x1_ref[...] = x1
    h2 = x1 * (1.0 + mod[4:5]) + mod[3:4]
    hh, hl = _split_bf16(h2)
    h2_ref[...] = hh
    wrh = wrh_ref[...]
    logits_t = _dot_nt(wrh, hh) + _dot_nt(wrh, hl) + _dot_nt(wrl_ref[...], hh)
    gates_t = _route(logits_t, rb_ref[...])
    pad = jnp.zeros((LANES - N_EXPERTS, gates_t.shape[1]), F32)
    gates_ref[...] = jnp.concatenate([gates_t, pad], axis=0).T


def _outproj(o_f, o_b, hg, yb, yc, xf, mod, wo, pv, wrh, wrl, rb, n_b, l_, n_tiles):
    d = xf.shape[1]
    tpb = l_ // TM

    def mod_idx(j):
        return (jnp.minimum(j // tpb, n_b), 0, 0)

    def rows(width):
        return pl.BlockSpec((TM, width), lambda j: (j, 0))

    full = lambda a: pl.BlockSpec(a.shape, lambda j: (0,) * a.ndim)
    nr = n_tiles * TM
    return pl.pallas_call(
        _outproj_kernel,
        grid=(n_tiles,),
        in_specs=[rows(256), rows(256), rows(256), rows(256), rows(512), rows(d),
                  pl.BlockSpec((1, 6, d), mod_idx), full(wo), full(pv), full(wrh), full(wrl), full(rb)],
        out_specs=[rows(d), rows(d), rows(LANES)],
        out_shape=[jax.ShapeDtypeStruct((nr, d), F32), jax.ShapeDtypeStruct((nr, d), BF16),
                   jax.ShapeDtypeStruct((nr, LANES), F32)],
        compiler_params=_params(("arbitrary",)),
        name="outproj",
    )(o_f, o_b, hg, yb, yc, xf, mod, wo, pv, wrh, wrl, rb)


def _moe_kernel(h_ref, g_ref, wg_ref, wu_ref, wd_ref, x_ref, mod_ref, pv_ref, o_ref, acc_ref):
    e = pl.program_id(1)

    @pl.when(e == 0)
    def _():
        acc_ref[...] = jnp.zeros(acc_ref.shape, F32)

    h = h_ref[...]
    lane = lax.broadcasted_iota(jnp.int32, g_ref.shape, 1)
    gate = jnp.sum(jnp.where(lane == e, g_ref[...], 0.0), axis=1, keepdims=True)
    a = _dot(h, wg_ref[0])
    he = (a * _sigmoid(a)) * _dot(h, wu_ref[0]) * gate
    acc_ref[...] += _dot(he.astype(BF16), wd_ref[0])

    @pl.when(e == N_EXPERTS - 1)
    def _():
        mod = mod_ref[0]
        pv = pv_ref[...]
        o_ref[...] = _layer_norm(ALPHA * x_ref[...] + mod[5:6] * acc_ref[...], pv[0:1], pv[1:2])


def _moe(h2, gates, wg, wu, wd, x1, mod, pv, n_b, l_, n_tiles):
    d = x1.shape[1]
    tpb = l_ // TM

    def mod_idx(j, e):
        return (jnp.minimum(j // tpb, n_b), 0, 0)

    return pl.pallas_call(
        _moe_kernel,
        grid=(n_tiles, N_EXPERTS),
        in_specs=[pl.BlockSpec((TM, d), lambda j, e: (j, 0)),
                  pl.BlockSpec((TM, LANES), lambda j, e: (j, 0)),
                  pl.BlockSpec((1, d, D_EXPERT), lambda j, e: (e, 0, 0)),
                  pl.BlockSpec((1, d, D_EXPERT), lambda j, e: (e, 0, 0)),
                  pl.BlockSpec((1, D_EXPERT, d), lambda j, e: (e, 0, 0)),
                  pl.BlockSpec((TM, d), lambda j, e: (j, 0)),
                  pl.BlockSpec((1, 6, d), mod_idx),
                  pl.BlockSpec(pv.shape, lambda j, e: (0, 0))],
        out_specs=pl.BlockSpec((TM, d), lambda j, e: (j, 0)),
        out_shape=jax.ShapeDtypeStruct((n_tiles * TM, d), F32),
        scratch_shapes=[pltpu.VMEM((TM, d), F32)],
        compiler_params=_params(("arbitrary", "arbitrary")),
        name="moe",
    )(h2, gates, wg, wu, wd, x1, mod, pv)


def _relayout_w_in(w):
    base = w[:, :2304 + 0]
    sk = w[:, 2304 + 256:2304 + 384]
    sv = w[:, 2304 + 384:2304 + 512]
    rep = lambda a: jnp.concatenate(
        [jnp.tile(a[:, n * HEAD_DIM:(n + 1) * HEAD_DIM], (1, SWA_GROUP)) for n in range(SWA_KV_HEADS)], axis=1)
    return jnp.concatenate([w[:, :2560], rep(sk), rep(sv)], axis=1).astype(BF16)


def _rope_tables(l_):
    pos = np.arange(l_)
    nf = HEAD_DIM // 4
    inv = ROPE_BASE ** (-np.arange(nf, dtype=np.float64) / nf)
    ar = (pos // GRID_W)[:, None] * inv
    ac = (pos % GRID_W)[:, None] * inv
    cos = np.concatenate([np.cos(ar), np.cos(ar), np.cos(ac), np.cos(ac)], axis=1)
    sin = np.concatenate([-np.sin(ar), np.sin(ar), -np.sin(ac), np.sin(ac)], axis=1)
    cos = np.concatenate([np.tile(cos, (1, 2)), np.ones((TM, LANES))], axis=0)
    sin = np.concatenate([np.tile(sin, (1, 2)), np.zeros((TM, LANES))], axis=0)
    return cos.astype(np.float32), sin.astype(np.float32)


def _na_bias(rpb):
    u = np.arange(GRID_W)[:, None]
    v = np.arange(GRID_W)[None, :]
    cs = np.clip(u - NA_WIN_COLS // 2, 0, GRID_W - NA_WIN_COLS)
    colmask = (v >= cs) & (v < cs + NA_WIN_COLS)
    coff = np.clip(v - u + NA_WIN_COLS - 1, 0, 2 * NA_WIN_COLS - 2)
    ridx = np.arange(NA_WIN_ROWS)[None, :] - np.arange(NA_WIN_ROWS)[:, None] + NA_WIN_ROWS - 1
    tab = rpb[:, ridx[:, :, None, None], coff[None, None, :, :]]
    tab = jnp.where(colmask[None, None, None], tab, NEG)
    tab = tab.transpose(1, 0, 3, 2, 4)
    return tab.reshape(NA_WIN_ROWS, NA_HEADS * GRID_W, NA_WIN_ROWS * GRID_W).astype(F32)


def kernel(x, c, ctx, c_ctx, w_ada, b_ada, w_in, lb_logits, hgrn_norm, na_rpb, swa_sink, w_out,
           ln1_g, ln1_b, ln2_g, ln2_b, w_router, router_bias, w_gate, w_up, w_down):
    n_b, l_, d = x.shape
    ctx_len = ctx.shape[1]
    depth = w_ada.shape[0]
    assert n_b * ctx_len == TM and l_ % TM == 0 and n_b + 1 <= N_COND
    n_lat_tiles = (n_b * l_) // TM

    p_lb = jax.nn.softmax(lb_logits.astype(F32), axis=0)
    lb_all = jnp.maximum(jnp.cumsum(p_lb, axis=0) - p_lb[0:1], LB_MIN)
    gp_all = jnp.stack([jnp.log(lb_all), jnp.log1p(-lb_all), 1.0 - lb_all], axis=2)
    gp_all = jnp.pad(gp_all.reshape(depth, 6, HGRN_W), ((0, 0), (0, 2), (0, 0)))

    cond = jnp.concatenate([c, c_ctx[None, :], jnp.zeros((N_COND - n_b - 1, d), F32)], axis=0)
    mod_all = _ada(cond.T, w_ada, b_ada, n_b + 1).reshape(depth, N_COND, 6, d)

    cos_t, sin_t = _rope_tables(l_)
    wr_t = w_router.T
    wrh = wr_t.astype(BF16)
    wrl = (wr_t - wrh.astype(F32)).astype(BF16)
    rb = router_bias.astype(F32)[:, None]

    xf = jnp.concatenate([x.reshape(n_b * l_, d), ctx.reshape(n_b * ctx_len, d)], axis=0)
    for l in range(depth):
        last = l == depth - 1
        mod = mod_all[l]
        hq, hv, hg, lf, kk, nqkv, sq, kx, vx = _inproj(
            xf, mod, _relayout_w_in(w_in[l]), gp_all[l], cos_t, sin_t, n_b, l_)
        o_f, o_b = _hgrn(hq, hv, lf, kk, n_b, l_, ctx_len)
        yb = _na(nqkv, _na_bias(na_rpb[l]), n_b, l_, ctx_len, not last)
        yc = _swa(sq, kx, vx, swa_sink[l].astype(F32), n_b, l_, ctx_len, not last)
        n_tiles = n_lat_tiles if last else n_lat_tiles + 1
        pv1 = jnp.stack([ln1_g[l], ln1_b[l], jnp.tile(hgrn_norm[l], d // HEAD_DIM)]
                        + [jnp.zeros((d,), F32)] * 5, axis=0)
        x1, h2, gates = _outproj(o_f, o_b, hg, yb, yc, xf, mod, w_out[l].astype(BF16), pv1,
                                 wrh, wrl, rb, n_b, l_, n_tiles)
        pv2 = jnp.stack([ln2_g[l], ln2_b[l]] + [jnp.zeros((d,), F32)] * 6, axis=0)
        xf = _moe(h2, gates, w_gate[l].astype(BF16), w_up[l].astype(BF16), w_down[l].astype(BF16),
                  x1, mod, pv2, n_b, l_, n_tiles)
    return xf[:n_b * l_].reshape(n_b, l_, d)
```

```python
import functools

import numpy as np
import jax
import jax.numpy as jnp
from jax import lax
from jax.experimental import pallas as pl
from jax.experimental.pallas import tpu as pltpu

F32 = jnp.float32
BF16 = jnp.bfloat16

D_MODEL = 1024
GRID_W = 64
HEAD_DIM = 64
HGRN_W = 256
NA_HEADS = 4
NA_WIN_ROWS = 8
NA_WIN_COLS = 16
SWA_Q_HEADS = 8
SWA_KV_HEADS = 2
SWA_GROUP = 4
SWA_WINDOW = 128
SWA_BLOCK = 128
ROPE_BASE = 10000.0
N_EXPERTS = 16
N_GROUPS = 4
D_EXPERT = 512
LN_EPS = 1e-5
RMS_EPS = 1e-6
NEG = -1e30
LB_MIN = 1e-6
DEPTH = 2
ALPHA = (2.0 * DEPTH) ** 0.25

LANES = 128
MXU_N = 256
TM = 512
HB = 256
HC = 16
NA_QROWS = 4
VMEM_LIMIT = 56 * 1024 * 1024

N_COND = 8
IN_CHUNKS = 14


def _dot(a, b):
    return jnp.dot(a, b, preferred_element_type=F32)


def _dot_nt(a, b):
    return lax.dot_general(a, b, (((1,), (1,)), ((), ())), preferred_element_type=F32)


def _dot_tn(a, b):
    return lax.dot_general(a, b, (((0,), (0,)), ((), ())), preferred_element_type=F32)


def _sigmoid(x):
    return 1.0 / (1.0 + jnp.exp(-x))


def _split_bf16(x):
    hi = x.astype(BF16)
    lo = (x - hi.astype(F32)).astype(BF16)
    return hi, lo


def _params(sem):
    return pltpu.CompilerParams(dimension_semantics=sem, vmem_limit_bytes=VMEM_LIMIT)


def _ada_kernel(n_rows, condt_ref, w_ref, b_ref, o_ref):
    c = condt_ref[...]
    s = c * _sigmoid(c)
    w = w_ref[0]
    rows = [jnp.sum(w * s[:, r:r + 1], axis=0, keepdims=True) for r in range(n_rows)]
    rows.append(jnp.zeros((N_COND - n_rows, w.shape[1]), F32))
    o_ref[0] = jnp.concatenate(rows, axis=0) + b_ref[0]


def _ada(cond_t, w_ada, b_ada, n_rows):
    depth, d, n6 = w_ada.shape
    tn = 1024
    return pl.pallas_call(
        functools.partial(_ada_kernel, n_rows),
        grid=(depth, n6 // tn),
        in_specs=[pl.BlockSpec((d, N_COND), lambda l, n: (0, 0)),
                  pl.BlockSpec((1, d, tn), lambda l, n: (l, 0, n)),
                  pl.BlockSpec((1, 1, tn), lambda l, n: (l, 0, n))],
        out_specs=pl.BlockSpec((1, N_COND, tn), lambda l, n: (l, 0, n)),
        out_shape=jax.ShapeDtypeStruct((depth, N_COND, n6), F32),
        compiler_params=_params(("arbitrary", "arbitrary")),
        name="ada",
    )(cond_t, w_ada, b_ada.reshape(depth, 1, n6))


def _rope(z, cos, sin, first):
    sw = jnp.where(first, pltpu.roll(z, LANES - 16, axis=1), pltpu.roll(z, 16, axis=1))
    return z * cos + sw * sin


def _inproj_kernel(x_ref, mod_ref, w_ref, gp_ref, cos_ref, sin_ref,
                   hq_ref, hv_ref, hg_ref, lf_ref, kk_ref, nqkv_ref, sq_ref, kx_ref, vx_ref):
    mod = mod_ref[0]
    h = (x_ref[...] * (1.0 + mod[1:2]) + mod[0:1]).astype(BF16)
    gp = gp_ref[...]
    cos = cos_ref[...]
    sin = sin_ref[...]
    lane = lax.broadcasted_iota(jnp.int32, cos.shape, 1)
    first = (lane & 16) == 0
    scale = HEAD_DIM ** -0.5

    def chunk(c):
        return _dot(h, w_ref[:, c * MXU_N:(c + 1) * MXU_N])

    def gates(z, d):
        cols = slice(d * MXU_N, (d + 1) * MXU_N)
        a = gp[0:1, cols]
        b = gp[1:2, cols] + (jnp.minimum(z, 0.0) - jnp.log1p(jnp.exp(-jnp.abs(z))))
        logf = jnp.maximum(a, b) + jnp.log1p(jnp.exp(-jnp.abs(a - b)))
        k = gp[2:3, cols] * (1.0 / (1.0 + jnp.exp(z)))
        return logf, k

    def rope2(z):
        return jnp.concatenate([_rope(z[:, 0:LANES], cos, sin, first),
                                _rope(z[:, LANES:2 * LANES], cos, sin, first)], axis=1)

    hq_ref[...] = chunk(0).astype(BF16)
    for d in range(2):
        logf, k = gates(chunk(1 + d), d)
        lf_ref[:, d * MXU_N:(d + 1) * MXU_N] = logf
        kk_ref[:, d * MXU_N:(d + 1) * MXU_N] = k.astype(BF16)
    hv_ref[...] = chunk(3).astype(BF16)
    hg_ref[...] = chunk(4).astype(BF16)
    nqkv_ref[:, 0:MXU_N] = (chunk(5) * scale).astype(BF16)
    nqkv_ref[:, MXU_N:2 * MXU_N] = chunk(6).astype(BF16)
    nqkv_ref[:, 2 * MXU_N:3 * MXU_N] = chunk(7).astype(BF16)
    for d in range(2):
        sq_ref[:, d * MXU_N:(d + 1) * MXU_N] = (rope2(chunk(8 + d)) * scale).astype(BF16)
        kx_ref[:, d * MXU_N:(d + 1) * MXU_N] = rope2(chunk(10 + d)).astype(BF16)
        vx_ref[:, d * MXU_N:(d + 1) * MXU_N] = chunk(12 + d).astype(BF16)


def _inproj(xf, mod, w, gp, cos_t, sin_t, n_b, l_):
    nt, d = xf.shape
    tpb = l_ // TM
    n_lat = n_b * tpb

    def mod_idx(j):
        return (jnp.minimum(j // tpb, n_b), 0, 0)

    def rope_idx(j):
        return (jnp.where(j < n_lat, j % tpb, tpb), 0)

    def rows(width):
        return pl.BlockSpec((TM, width), lambda j: (j, 0))

    widths = [(HGRN_W, BF16), (HGRN_W, BF16), (HGRN_W, BF16), (2 * HGRN_W, F32), (2 * HGRN_W, BF16),
              (3 * 256, BF16), (512, BF16), (512, BF16), (512, BF16)]
    return pl.pallas_call(
        _inproj_kernel,
        grid=(nt // TM,),
        in_specs=[rows(d),
                  pl.BlockSpec((1, 6, d), mod_idx),
                  pl.BlockSpec(w.shape, lambda j: (0, 0)),
                  pl.BlockSpec(gp.shape, lambda j: (0, 0)),
                  pl.BlockSpec((TM, LANES), rope_idx),
                  pl.BlockSpec((TM, LANES), rope_idx)],
        out_specs=[rows(wd) for wd, _ in widths],
        out_shape=[jax.ShapeDtypeStruct((nt, wd), dt) for wd, dt in widths],
        compiler_params=_params(("arbitrary",)),
        name="inproj",
    )(xf, mod, w, gp, cos_t, sin_t)


def _hgrn_kernel(qf_ref, vf_ref, lff_ref, kf_ref, qb_ref, vb_ref, lfb_ref, kb_ref,
                 of_ref, ob_ref, st_ref, qd_ref, kd_ref, vv_ref, gt_ref, it_ref):
    nch = HB // HC

    @pl.when(pl.program_id(1) == 0)
    def _():
        st_ref[...] = jnp.zeros(st_ref.shape, F32)

    row = lax.broadcasted_iota(jnp.int32, (HB, HB), 0)
    col = lax.broadcasted_iota(jnp.int32, (HB, HB), 1)
    same_chunk = (row // HC) == (col // HC)
    same_head = (row // HEAD_DIM) == (col // HEAD_DIM)
    head_ones = jnp.where(same_head, 1.0, 0.0).astype(BF16)
    chunk_ones = jnp.where(same_chunk, 1.0, 0.0).astype(BF16)
    rid = lax.broadcasted_iota(jnp.int32, (nch, HC, HGRN_W), 1)

    def prep(d, q_ref, v_ref, lf_ref, k_ref):
        fwd = d == 0
        tri = jnp.where(same_chunk & ((col <= row) if fwd else (col >= row)), 1.0, 0.0).astype(BF16)
        hi, lo = _split_bf16(lf_ref[...])
        cum = _dot(tri, hi) + _dot(tri, lo)
        tot = _dot(chunk_ones, hi) + _dot(chunk_ones, lo)
        q = q_ref[...].astype(F32)
        k = k_ref[...].astype(F32)
        v = v_ref[...]
        qd_ref[d] = (q * jnp.exp(cum)).astype(BF16)
        kd_ref[d] = (k * jnp.exp(tot - cum)).astype(BF16)
        vv_ref[d] = v
        gt_ref[d] = jnp.exp(tot)
        cum3 = cum.reshape(nch, HC, HGRN_W)
        q3 = q.reshape(nch, HC, HGRN_W)
        k3 = k.reshape(nch, HC, HGRN_W)
        v3 = v.astype(F32).reshape(nch, HC, HGRN_W)
        od = jnp.zeros((HB, HGRN_W), F32)
        for s in range(HC):
            dec = jnp.exp(jnp.minimum(cum3 - cum3[:, s:s + 1, :], 0.0))
            valid = (rid >= s) if fwd else (rid <= s)
            a = jnp.where(valid, q3 * k3[:, s:s + 1, :] * dec, 0.0)
            sc = _dot(a.reshape(HB, HGRN_W).astype(BF16), head_ones)
            vs = jnp.broadcast_to(v3[:, s:s + 1, :], (nch, HC, HGRN_W)).reshape(HB, HGRN_W)
            od = od + sc * vs
        return od

    od_f = prep(0, qf_ref, vf_ref, lff_ref, kf_ref)
    od_b = prep(1, qb_ref, vb_ref, lfb_ref, kb_ref)

    def body(i, carry):
        for d, c in ((0, i), (1, nch - 1 - i)):
            off = pl.multiple_of(c * HC, HC)
            qd_c = qd_ref[d, pl.ds(off, HC), :]
            kd_c = kd_ref[d, pl.ds(off, HC), :]
            v_c = vv_ref[d, pl.ds(off, HC), :]
            g_c = gt_ref[d, pl.ds(off, 1), :]
            st = st_ref[d]
            it_ref[d, pl.ds(off, HC), :] = _dot_nt(qd_c, st.astype(BF16))
            upd = _dot_tn(v_c, kd_c)
            st_ref[d] = st * g_c + jnp.where(same_head, upd, 0.0)
        return carry

    lax.fori_loop(0, nch, body, 0)
    of_ref[...] = it_ref[0] + od_f
    ob_ref[...] = it_ref[1] + od_b


def _hgrn(hq, hv, lf, kk, n_b, l_, ctx_len):
    nt = hq.shape[0]
    assert ctx_len == HB
    nlb = l_ // HB
    ctx0 = n_b * nlb

    def fwd_idx(col):
        return lambda b, s: (jnp.where(s == 0, ctx0 + b, b * nlb + s - 1), col)

    def bwd_idx(col):
        return lambda b, s: (jnp.where(s == 0, ctx0 + b, b * nlb + nlb - s), col)

    def blk(idx):
        return pl.BlockSpec((HB, HGRN_W), idx)

    return pl.pallas_call(
        _hgrn_kernel,
        grid=(n_b, nlb + 1),
        in_specs=[blk(fwd_idx(0)), blk(fwd_idx(0)), blk(fwd_idx(0)), blk(fwd_idx(0)),
                  blk(bwd_idx(0)), blk(bwd_idx(0)), blk(bwd_idx(1)), blk(bwd_idx(1))],
        out_specs=[blk(fwd_idx(0)), blk(bwd_idx(0))],
        out_shape=[jax.ShapeDtypeStruct((nt, HGRN_W), F32)] * 2,
        scratch_shapes=[pltpu.VMEM((2, HGRN_W, HGRN_W), F32),
                        pltpu.VMEM((2, HB, HGRN_W), BF16),
                        pltpu.VMEM((2, HB, HGRN_W), BF16),
                        pltpu.VMEM((2, HB, HGRN_W), BF16),
                        pltpu.VMEM((2, HB, HGRN_W), F32),
                        pltpu.VMEM((2, HB, HGRN_W), F32)],
        compiler_params=_params(("arbitrary", "arbitrary")),
        name="hgrn",
    )(hq, hv, lf, kk, hq, hv, lf, kk)


def _head_stack(q, n_heads):
    lane = lax.broadcasted_iota(jnp.int32, q.shape, 1)
    zero = jnp.zeros_like(q)
    return jnp.concatenate([jnp.where(lane // HEAD_DIM == h, q, zero) for h in range(n_heads)], axis=0)


def _head_unstack(o, n_heads):
    rows = o.shape[0] // n_heads
    lane = lax.broadcasted_iota(jnp.int32, (rows, o.shape[1]), 1)
    acc = jnp.zeros((rows, o.shape[1]), F32)
    for h in range(n_heads):
        acc = acc + jnp.where(lane // HEAD_DIM == h, o[h * rows:(h + 1) * rows], 0.0)
    return acc


def _na_kernel(n_rows, n_lat_steps, q_ref, k_ref, v_ref, kc_ref, vc_ref, bias_ref, o_ref):
    j = pl.program_id(1)
    kc = kc_ref[...]
    vc = vc_ref[...]
    nwin = NA_WIN_ROWS * GRID_W

    def attend(rr, win):
        q4 = _head_stack(q_ref[rr * GRID_W:(rr + 1) * GRID_W, :], NA_HEADS)
        s_ctx = _dot_nt(q4, kc)
        m = jnp.max(s_ctx, axis=1, keepdims=True)
        if win is not None:
            kw, vw, bias = win
            s_win = _dot_nt(q4, kw) + bias
            m = jnp.maximum(m, jnp.max(s_win, axis=1, keepdims=True))
            p_win = jnp.exp(s_win - m)
        p_ctx = jnp.exp(s_ctx - m)
        den = jnp.sum(p_ctx, axis=1, keepdims=True)
        o4 = _dot(p_ctx.astype(BF16), vc)
        if win is not None:
            den = den + jnp.sum(p_win, axis=1, keepdims=True)
            o4 = o4 + _dot(p_win.astype(BF16), vw)
        o = _head_unstack(o4 * (1.0 / den), NA_HEADS)
        o_ref[rr * GRID_W:(rr + 1) * GRID_W, :] = o.astype(BF16)

    @pl.when(j < n_lat_steps)
    def _():
        for rr in range(NA_QROWS):
            r = j * NA_QROWS + rr
            rs = jnp.clip(r - NA_WIN_ROWS // 2, 0, n_rows - NA_WIN_ROWS)
            start = pl.multiple_of(rs * GRID_W, GRID_W)
            kw = k_ref[pl.ds(start, nwin), :]
            vw = v_ref[pl.ds(start, nwin), :]
            attend(rr, (kw, vw, bias_ref[r - rs]))

    @pl.when(j >= n_lat_steps)
    def _():
        for rr in range(NA_QROWS):
            attend(rr, None)


def _na(nqkv, bias, n_b, l_, ctx_len, with_ctx):
    nt = nqkv.shape[0]
    qb = NA_QROWS * GRID_W
    assert ctx_len == qb
    n_lat = l_ // qb
    ctx0 = n_b * n_lat
    n_rows = l_ // GRID_W

    def q_idx(b, j):
        return (jnp.where(j < n_lat, b * n_lat + j, ctx0 + b), 0)

    cblk = (n_b * l_) // ctx_len
    return pl.pallas_call(
        functools.partial(_na_kernel, n_rows, n_lat),
        grid=(n_b, n_lat + (1 if with_ctx else 0)),
        in_specs=[pl.BlockSpec((qb, 256), q_idx),
                  pl.BlockSpec((l_, 256), lambda b, j: (b, 1)),
                  pl.BlockSpec((l_, 256), lambda b, j: (b, 2)),
                  pl.BlockSpec((ctx_len, 256), lambda b, j: (cblk + b, 1)),
                  pl.BlockSpec((ctx_len, 256), lambda b, j: (cblk + b, 2)),
                  pl.BlockSpec(bias.shape, lambda b, j: (0, 0, 0))],
        out_specs=pl.BlockSpec((qb, 256), q_idx),
        out_shape=jax.ShapeDtypeStruct((nt, 256), BF16),
        compiler_params=_params(("arbitrary", "arbitrary")),
        name="natten",
    )(nqkv, nqkv, nqkv, nqkv, nqkv, bias)


def _swa_kernel(n_blocks, l_, sink_ref, q_ref, kp_ref, kq_ref, kn_ref, vp_ref, vq_ref, vn_ref,
                kc_ref, vc_ref, o_ref):
    j = pl.program_id(1)
    sb = SWA_BLOCK

    def attend(n, band):
        cols = slice(n * MXU_N, (n + 1) * MXU_N)
        q4 = _head_stack(q_ref[:, cols], SWA_GROUP)
        sink = jnp.concatenate([jnp.full((sb, 1), sink_ref[n * SWA_GROUP + g], F32)
                                for g in range(SWA_GROUP)], axis=0)
        s_ctx = _dot_nt(q4, kc_ref[:, cols])
        m = jnp.maximum(jnp.max(s_ctx, axis=1, keepdims=True), sink)
        if band is not None:
            kb = jnp.concatenate([kp_ref[:, cols], kq_ref[:, cols], kn_ref[:, cols]], axis=0)
            vb = jnp.concatenate([vp_ref[:, cols], vq_ref[:, cols], vn_ref[:, cols]], axis=0)
            s_band = jnp.where(band, _dot_nt(q4, kb), NEG)
            m = jnp.maximum(m, jnp.max(s_band, axis=1, keepdims=True))
            p_band = jnp.exp(s_band - m)
        p_ctx = jnp.exp(s_ctx - m)
        den = jnp.sum(p_ctx, axis=1, keepdims=True) + jnp.exp(sink - m)
        o4 = _dot(p_ctx.astype(BF16), vc_ref[:, cols])
        if band is not None:
            den = den + jnp.sum(p_band, axis=1, keepdims=True)
            o4 = o4 + _dot(p_band.astype(BF16), vb)
        o_ref[:, cols] = _head_unstack(o4 * (1.0 / den), SWA_GROUP).astype(BF16)

    @pl.when(j < n_blocks)
    def _():
        u = lax.broadcasted_iota(jnp.int32, (SWA_GROUP * sb, 3 * sb), 0) % sb
        v = lax.broadcasted_iota(jnp.int32, (SWA_GROUP * sb, 3 * sb), 1)
        kpos = (j - 1) * sb + v
        dist = j * sb + u - kpos
        band = (kpos >= 0) & (kpos < l_) & (dist <= SWA_WINDOW) & (dist >= -SWA_WINDOW)
        for n in range(SWA_KV_HEADS):
            attend(n, band)

    @pl.when(j >= n_blocks)
    def _():
        for n in range(SWA_KV_HEADS):
            attend(n, None)


def _swa(sq, kx, vx, sink, n_b, l_, ctx_len, with_ctx):
    nt = sq.shape[0]
    sb = SWA_BLOCK
    nb = l_ // sb
    cper = ctx_len // sb
    ctx0 = n_b * nb

    def q_idx(b, j):
        return (jnp.where(j < nb, b * nb + j, ctx0 + b * cper + (j - nb)), 0)

    def band_idx(off):
        return lambda b, j: (b * nb + jnp.clip(jnp.minimum(j, nb - 1) + off, 0, nb - 1), 0)

    cblk = (n_b * l_) // ctx_len
    blk = lambda idx: pl.BlockSpec((sb, 512), idx)
    cspec = pl.BlockSpec((ctx_len, 512), lambda b, j: (cblk + b, 0))
    return pl.pallas_call(
        functools.partial(_swa_kernel, nb, l_),
        grid=(n_b, nb + (cper if with_ctx else 0)),
        in_specs=[pl.BlockSpec(memory_space=pltpu.SMEM),
                  blk(q_idx), blk(band_idx(-1)), blk(band_idx(0)), blk(band_idx(1)),
                  blk(band_idx(-1)), blk(band_idx(0)), blk(band_idx(1)), cspec, cspec],
        out_specs=blk(q_idx),
        out_shape=jax.ShapeDtypeStruct((nt, 512), BF16),
        compiler_params=_params(("arbitrary", "arbitrary")),
        name="swa",
    )(sink, sq, kx, kx, kx, vx, vx, vx, kx, vx)


def _layer_norm(r, g, b):
    mu = jnp.mean(r, axis=-1, keepdims=True)
    rc = r - mu
    var = jnp.mean(rc * rc, axis=-1, keepdims=True)
    return rc * lax.rsqrt(var + LN_EPS) * g + b


def _route(logits_t, bias_col):
    epg = N_EXPERTS // N_GROUPS
    s = _sigmoid(logits_t)
    sel = s + bias_col
    sel_r = [sel[e:e + 1, :] for e in range(N_EXPERTS)]
    s_r = [s[e:e + 1, :] for e in range(N_EXPERTS)]
    grp = []
    for g in range(N_GROUPS):
        a = sel_r[g * epg:(g + 1) * epg]
        m1 = functools.reduce(jnp.maximum, a)
        m2 = functools.reduce(jnp.maximum,
                              [jnp.minimum(a[i], a[k]) for i in range(epg) for k in range(i + 1, epg)])
        grp.append(m1 + m2)
    w_r = []
    for g in range(N_GROUPS):
        best = None
        for k in range(N_GROUPS):
            if k == g:
                continue
            c = (grp[g] > grp[k]) if k < g else (grp[g] >= grp[k])
            best = c if best is None else (best & c)
        for e in range(g * epg, (g + 1) * epg):
            rank = jnp.zeros_like(sel_r[e])
            for k in range(g * epg, (g + 1) * epg):
                if k == e:
                    continue
                ahead = (sel_r[k] >= sel_r[e]) if k < e else (sel_r[k] > sel_r[e])
                rank = rank + jnp.where(ahead, 1.0, 0.0)
            w_r.append(jnp.where(best & (rank < 1.5), s_r[e], 0.0))
    den = functools.reduce(lambda a, b: a + b, w_r)
    inv = 1.0 / den
    return jnp.concatenate([w * inv for w in w_r], axis=0)


def _outproj_kernel(of_ref, ob_ref, hg_ref, yb_ref, yc_ref, x_ref, mod_ref, wo_ref, pv_ref,
                    wrh_ref, wrl_ref, rb_ref, x1_ref, h2_ref, gates_ref):
    row = lax.broadcasted_iota(jnp.int32, (HGRN_W, HGRN_W), 0)
    col = lax.broadcasted_iota(jnp.int32, (HGRN_W, HGRN_W), 1)
    head_ones = jnp.where((row // HEAD_DIM) == (col // HEAD_DIM), 1.0, 0.0).astype(BF16)
    pv = pv_ref[...]
    o = of_ref[...] + ob_ref[...]
    hi, lo = _split_bf16(o * o)
    ms = (_dot(hi, head_ones) + _dot(lo, head_ones)) * (1.0 / HEAD_DIM)
    gate = hg_ref[...].astype(F32)
    ya = o * lax.rsqrt(ms + RMS_EPS) * pv[2:3, 0:HGRN_W] * (gate * _sigmoid(gate))
    y = (_dot(ya.astype(BF16), wo_ref[0:256, :]) + _dot(yb_ref[...], wo_ref[256:512, :])
         + _dot(yc_ref[...], wo_ref[512:1024, :]))
    mod = mod_ref[0]
    x1 = _layer_norm(ALPHA * x_ref[...] + mod[2:3] * y, pv[0:1], pv[1:2])
    x1_ref[...] = x1
    h2 = x1 * (1.0 + mod[4:5]) + mod[3:4]
    hh, hl = _split_bf16(h2)
    h2_ref[...] = hh
    wrh = wrh_ref[...]
    logits_t = _dot_nt(wrh, hh) + _dot_nt(wrh, hl) + _dot_nt(wrl_ref[...], hh)
    gates_t = _route(logits_t, rb_ref[...])
    pad = jnp.zeros((LANES - N_EXPERTS, gates_t.shape[1]), F32)
    gates_ref[...] = jnp.concatenate([gates_t, pad], axis=0).T


def _outproj(o_f, o_b, hg, yb, yc, xf, mod, wo, pv, wrh, wrl, rb, n_b, l_, n_tiles):
    d = xf.shape[1]
    tpb = l_ // TM

    def mod_idx(j):
        return (jnp.minimum(j // tpb, n_b), 0, 0)

    def rows(width):
        return pl.BlockSpec((TM, width), lambda j: (j, 0))

    full = lambda a: pl.BlockSpec(a.shape, lambda j: (0,) * a.ndim)
    nr = n_tiles * TM
    return pl.pallas_call(
        _outproj_kernel,
        grid=(n_tiles,),
        in_specs=[rows(256), rows(256), rows(256), rows(256), rows(512), rows(d),
                  pl.BlockSpec((1, 6, d), mod_idx), full(wo), full(pv), full(wrh), full(wrl), full(rb)],
        out_specs=[rows(d), rows(d), rows(LANES)],
        out_shape=[jax.ShapeDtypeStruct((nr, d), F32), jax.ShapeDtypeStruct((nr, d), BF16),
                   jax.ShapeDtypeStruct((nr, LANES), F32)],
        compiler_params=_params(("arbitrary",)),
        name="outproj",
    )(o_f, o_b, hg, yb, yc, xf, mod, wo, pv, wrh, wrl, rb)


def _moe_kernel(h_ref, g_ref, wg_ref, wu_ref, wd_ref, x_ref, mod_ref, pv_ref, o_ref, acc_ref):
    e = pl.program_id(1)

    @pl.when(e == 0)
    def _():
        acc_ref[...] = jnp.zeros(acc_ref.shape, F32)

    h = h_ref[...]
    lane = lax.broadcasted_iota(jnp.int32, g_ref.shape, 1)
    gate = jnp.sum(jnp.where(lane == e, g_ref[...], 0.0), axis=1, keepdims=True)
    a = _dot(h, wg_ref[0])
    he = (a * _sigmoid(a)) * _dot(h, wu_ref[0]) * gate
    acc_ref[...] += _dot(he.astype(BF16), wd_ref[0])

    @pl.when(e == N_EXPERTS - 1)
    def _():
        mod = mod_ref[0]
        pv = pv_ref[...]
        o_ref[...] = _layer_norm(ALPHA * x_ref[...] + mod[5:6] * acc_ref[...], pv[0:1], pv[1:2])


def _moe(h2, gates, wg, wu, wd, x1, mod, pv, n_b, l_, n_tiles):
    d = x1.shape[1]
    tpb = l_ // TM

    def mod_idx(j, e):
        return (jnp.minimum(j // tpb, n_b), 0, 0)

    return pl.pallas_call(
        _moe_kernel,
        grid=(n_tiles, N_EXPERTS),
        in_specs=[pl.BlockSpec((TM, d), lambda j, e: (j, 0)),
                  pl.BlockSpec((TM, LANES), lambda j, e: (j, 0)),
                  pl.BlockSpec((1, d, D_EXPERT), lambda j, e: (e, 0, 0)),
                  pl.BlockSpec((1, d, D_EXPERT), lambda j, e: (e, 0, 0)),
                  pl.BlockSpec((1, D_EXPERT, d), lambda j, e: (e, 0, 0)),
                  pl.BlockSpec((TM, d), lambda j, e: (j, 0)),
                  pl.BlockSpec((1, 6, d), mod_idx),
                  pl.BlockSpec(pv.shape, lambda j, e: (0, 0))],
        out_specs=pl.BlockSpec((TM, d), lambda j, e: (j, 0)),
        out_shape=jax.ShapeDtypeStruct((n_tiles * TM, d), F32),
        scratch_shapes=[pltpu.VMEM((TM, d), F32)],
        compiler_params=_params(("arbitrary", "arbitrary")),
        name="moe",
    )(h2, gates, wg, wu, wd, x1, mod, pv)


def _relayout_w_in(w):
    sk = w[:, 2560:2688]
    sv = w[:, 2688:2816]
    rep = lambda a: jnp.concatenate(
        [jnp.tile(a[:, n * HEAD_DIM:(n + 1) * HEAD_DIM], (1, SWA_GROUP)) for n in range(SWA_KV_HEADS)], axis=1)
    return jnp.concatenate([w[:, :2560], rep(sk), rep(sv)], axis=1).astype(BF16)


def _rope_tables(l_):
    pos = np.arange(l_)
    nf = HEAD_DIM // 4
    inv = ROPE_BASE ** (-np.arange(nf, dtype=np.float64) / nf)
    ar = (pos // GRID_W)[:, None] * inv
    ac = (pos % GRID_W)[:, None] * inv
    cos = np.concatenate([np.cos(ar), np.cos(ar), np.cos(ac), np.cos(ac)], axis=1)
    sin = np.concatenate([-np.sin(ar), np.sin(ar), -np.sin(ac), np.sin(ac)], axis=1)
    cos = np.concatenate([np.tile(cos, (1, 2)), np.ones((TM, LANES))], axis=0)
    sin = np.concatenate([np.tile(sin, (1, 2)), np.zeros((TM, LANES))], axis=0)
    return cos.astype(np.float32), sin.astype(np.float32)


def _na_bias(rpb):
    u = np.arange(GRID_W)[:, None]
    v = np.arange(GRID_W)[None, :]
    cs = np.clip(u - NA_WIN_COLS // 2, 0, GRID_W - NA_WIN_COLS)
    colmask = (v >= cs) & (v < cs + NA_WIN_COLS)
    coff = np.clip(v - u + NA_WIN_COLS - 1, 0, 2 * NA_WIN_COLS - 2)
    n_off = 2 * NA_WIN_COLS - 1
    pick = (coff[None] == np.arange(n_off)[:, None, None]).astype(np.float32)
    band = jnp.einsum('hab,buv->ahuv', rpb.astype(F32), pick, precision=lax.Precision.HIGHEST)
    band = jnp.where(colmask[None, None], band, NEG)
    tabs = []
    for delta in range(NA_WIN_ROWS):
        rows = band[NA_WIN_ROWS - 1 - delta:2 * NA_WIN_ROWS - 1 - delta]
        tabs.append(rows.transpose(1, 2, 0, 3).reshape(NA_HEADS * GRID_W, NA_WIN_ROWS * GRID_W))
    return jnp.stack(tabs, axis=0)


def kernel(x, c, ctx, c_ctx, w_ada, b_ada, w_in, lb_logits, hgrn_norm, na_rpb, swa_sink, w_out,
           ln1_g, ln1_b, ln2_g, ln2_b, w_router, router_bias, w_gate, w_up, w_down):
    n_b, l_, d = x.shape
    ctx_len = ctx.shape[1]
    depth = w_ada.shape[0]
    assert n_b * ctx_len == TM and l_ % TM == 0 and n_b + 1 <= N_COND
    n_lat_tiles = (n_b * l_) // TM

    p_lb = jax.nn.softmax(lb_logits.astype(F32), axis=0)
    gp_all, run = [], jnp.zeros_like(p_lb[0])
    for l in range(depth):
        run = run + p_lb[l]
        lb = jnp.maximum(run - p_lb[0], LB_MIN).reshape(2 * HGRN_W)
        gp_l = jnp.stack([jnp.log(lb), jnp.log1p(-lb), 1.0 - lb] + [jnp.zeros_like(lb)] * 5, axis=0)
        gp_all.append(gp_l)

    cond = jnp.concatenate([c, c_ctx[None, :], jnp.zeros((N_COND - n_b - 1, d), F32)], axis=0)
    mod_all = _ada(cond.T, w_ada, b_ada, n_b + 1).reshape(depth, N_COND, 6, d)

    cos_t, sin_t = _rope_tables(l_)
    wr_t = w_router.T
    wrh = wr_t.astype(BF16)
    wrl = (wr_t - wrh.astype(F32)).astype(BF16)
    rb = router_bias.astype(F32)[:, None]

    xf = jnp.concatenate([x.reshape(n_b * l_, d), ctx.reshape(n_b * ctx_len, d)], axis=0)
    for l in range(depth):
        last = l == depth - 1
        mod = mod_all[l]
        hq, hv, hg, lf, kk, nqkv, sq, kx, vx = _inproj(
            xf, mod, _relayout_w_in(w_in[l]), gp_all[l], cos_t, sin_t, n_b, l_)
        o_f, o_b = _hgrn(hq, hv, lf, kk, n_b, l_, ctx_len)
        yb = _na(nqkv, _na_bias(na_rpb[l]), n_b, l_, ctx_len, not last)
        yc = _swa(sq, kx, vx, swa_sink[l].astype(F32), n_b, l_, ctx_len, not last)
        n_tiles = n_lat_tiles if last else n_lat_tiles + 1
        pv1 = jnp.stack([ln1_g[l], ln1_b[l], jnp.tile(hgrn_norm[l], d // HEAD_DIM)]
                        + [jnp.zeros((d,), F32)] * 5, axis=0)
        x1, h2, gates = _outproj(o_f, o_b, hg, yb, yc, xf, mod, w_out[l].astype(BF16), pv1,
                                 wrh, wrl, rb, n_b, l_, n_tiles)
        pv2 = jnp.stack([ln2_g[l], ln2_b[l]] + [jnp.zeros((d,), F32)] * 6, axis=0)
        xf = _moe(h2, gates, w_gate[l].astype(BF16), w_up[l].astype(BF16), w_down[l].astype(BF16),
                  x1, mod, pv2, n_b, l_, n_tiles)
    return xf[:n_b * l_].reshape(n_b, l_, d)
```

```python
import functools

import numpy as np
import jax
import jax.numpy as jnp
from jax import lax
from jax.experimental import pallas as pl
from jax.experimental.pallas import tpu as pltpu
from jax.experimental.pallas import tpu_sc as plsc

F32 = jnp.float32
BF16 = jnp.bfloat16

D_MODEL = 1024
GRID_W = 64
HEAD_DIM = 64
HGRN_W = 256
NA_HEADS = 4
NA_WIN_ROWS = 8
NA_WIN_COLS = 16
SWA_Q_HEADS = 8
SWA_KV_HEADS = 2
SWA_GROUP = 4
SWA_WINDOW = 128
SWA_BLOCK = 128
ROPE_BASE = 10000.0
N_EXPERTS = 16
N_GROUPS = 4
D_EXPERT = 512
LN_EPS = 1e-5
RMS_EPS = 1e-6
NEG = -1e30
LB_MIN = 1e-6
DEPTH = 2
ALPHA = (2.0 * DEPTH) ** 0.25

LANES = 128
MXU_N = 256
TM = 512
HB = 256
HC = 16
NA_QROWS = 4
VMEM_LIMIT = 56 * 1024 * 1024

SC_MAX_ROWS = 192
SC_ROW_QUANTUM = 2048

N_COND = 8


def _dot(a, b):
    return jnp.dot(a, b, preferred_element_type=F32)


def _dot_nt(a, b):
    return lax.dot_general(a, b, (((1,), (1,)), ((), ())), preferred_element_type=F32)


def _dot_tn(a, b):
    return lax.dot_general(a, b, (((0,), (0,)), ((), ())), preferred_element_type=F32)


def _sigmoid(x):
    return 1.0 / (1.0 + jnp.exp(-x))


def _split_bf16(x):
    hi = x.astype(BF16)
    lo = (x - hi.astype(F32)).astype(BF16)
    return hi, lo


def _params(sem):
    return pltpu.CompilerParams(dimension_semantics=sem, vmem_limit_bytes=VMEM_LIMIT)


def _ada_kernel(n_rows, condt_ref, w_ref, b_ref, o_ref):
    c = condt_ref[...]
    s = c * _sigmoid(c)
    w = w_ref[0]
    rows = [jnp.sum(w * s[:, r:r + 1], axis=0, keepdims=True) for r in range(n_rows)]
    rows.append(jnp.zeros((N_COND - n_rows, w.shape[1]), F32))
    o_ref[0] = jnp.concatenate(rows, axis=0) + b_ref[0]


def _ada(cond_t, w_ada, b_ada, n_rows):
    depth, d, n6 = w_ada.shape
    tn = 1024
    return pl.pallas_call(
        functools.partial(_ada_kernel, n_rows),
        grid=(depth, n6 // tn),
        in_specs=[pl.BlockSpec((d, N_COND), lambda l, n: (0, 0)),
                  pl.BlockSpec((1, d, tn), lambda l, n: (l, 0, n)),
                  pl.BlockSpec((1, 1, tn), lambda l, n: (l, 0, n))],
        out_specs=pl.BlockSpec((1, N_COND, tn), lambda l, n: (l, 0, n)),
        out_shape=jax.ShapeDtypeStruct((depth, N_COND, n6), F32),
        compiler_params=_params(("arbitrary", "arbitrary")),
        name="ada",
    )(cond_t, w_ada, b_ada.reshape(depth, 1, n6))


def _rope(z, cos, sin, first):
    sw = jnp.where(first, pltpu.roll(z, LANES - 16, axis=1), pltpu.roll(z, 16, axis=1))
    return z * cos + sw * sin


def _inproj_kernel(n_lat, xa_ref, xb_ref, mod_ref, w_ref, gp_ref, cos_ref, sin_ref,
                   hq_ref, hv_ref, hg_ref, lf_ref, kk_ref, nqkv_ref, sq_ref, kx_ref, vx_ref):
    mod = mod_ref[0]
    x = jnp.where(pl.program_id(0) < n_lat, xa_ref[...], xb_ref[...])
    h = (x * (1.0 + mod[1:2]) + mod[0:1]).astype(BF16)
    gp = gp_ref[...]
    cos = cos_ref[...]
    sin = sin_ref[...]
    lane = lax.broadcasted_iota(jnp.int32, cos.shape, 1)
    first = (lane & 16) == 0
    scale = HEAD_DIM ** -0.5

    def chunk(c):
        return _dot(h, w_ref[:, c * MXU_N:(c + 1) * MXU_N])

    def gates(z, d):
        cols = slice(d * MXU_N, (d + 1) * MXU_N)
        a = gp[0:1, cols]
        b = gp[1:2, cols] + (jnp.minimum(z, 0.0) - jnp.log1p(jnp.exp(-jnp.abs(z))))
        logf = jnp.maximum(a, b) + jnp.log1p(jnp.exp(-jnp.abs(a - b)))
        k = gp[2:3, cols] * (1.0 / (1.0 + jnp.exp(z)))
        return logf, k

    def rope2(z):
        return jnp.concatenate([_rope(z[:, 0:LANES], cos, sin, first),
                                _rope(z[:, LANES:2 * LANES], cos, sin, first)], axis=1)

    hq_ref[...] = chunk(0).astype(BF16)
    for d in range(2):
        logf, k = gates(chunk(1 + d), d)
        lf_ref[:, d * MXU_N:(d + 1) * MXU_N] = logf
        kk_ref[:, d * MXU_N:(d + 1) * MXU_N] = k.astype(BF16)
    hv_ref[...] = chunk(3).astype(BF16)
    hg_ref[...] = chunk(4).astype(BF16)
    nqkv_ref[:, 0:MXU_N] = (chunk(5) * scale).astype(BF16)
    nqkv_ref[:, MXU_N:2 * MXU_N] = chunk(6).astype(BF16)
    nqkv_ref[:, 2 * MXU_N:3 * MXU_N] = chunk(7).astype(BF16)
    for d in range(2):
        sq_ref[:, d * MXU_N:(d + 1) * MXU_N] = (rope2(chunk(8 + d)) * scale).astype(BF16)
        kx_ref[:, d * MXU_N:(d + 1) * MXU_N] = rope2(chunk(10 + d)).astype(BF16)
        vx_ref[:, d * MXU_N:(d + 1) * MXU_N] = chunk(12 + d).astype(BF16)


def _inproj(xa, xb, ctx_blk, mod, w, gp, cos_t, sin_t, n_b, l_):
    d = xa.shape[1]
    tpb = l_ // TM
    n_lat = n_b * tpb
    nt = (n_lat + 1) * TM

    def mod_idx(j):
        return (jnp.minimum(j // tpb, n_b), 0, 0)

    def rope_idx(j):
        return (jnp.where(j < n_lat, j % tpb, tpb), 0)

    def rows(width):
        return pl.BlockSpec((TM, width), lambda j: (j, 0))

    widths = [(HGRN_W, BF16), (HGRN_W, BF16), (HGRN_W, BF16), (2 * HGRN_W, F32), (2 * HGRN_W, BF16),
              (3 * 256, BF16), (512, BF16), (512, BF16), (512, BF16)]
    return pl.pallas_call(
        functools.partial(_inproj_kernel, n_lat),
        grid=(nt // TM,),
        in_specs=[pl.BlockSpec((TM, d), lambda j: (jnp.minimum(j, n_lat - 1), 0)),
                  pl.BlockSpec((TM, d), lambda j: (ctx_blk, 0)),
                  pl.BlockSpec((1, 6, d), mod_idx),
                  pl.BlockSpec(w.shape, lambda j: (0, 0)),
                  pl.BlockSpec(gp.shape, lambda j: (0, 0)),
                  pl.BlockSpec((TM, LANES), rope_idx),
                  pl.BlockSpec((TM, LANES), rope_idx)],
        out_specs=[rows(wd) for wd, _ in widths],
        out_shape=[jax.ShapeDtypeStruct((nt, wd), dt) for wd, dt in widths],
        compiler_params=_params(("arbitrary",)),
        name="inproj",
    )(xa, xb, mod, w, gp, cos_t, sin_t)


def _hgrn_kernel(qf_ref, vf_ref, lff_ref, kf_ref, qb_ref, vb_ref, lfb_ref, kb_ref,
                 of_ref, ob_ref, st_ref, qd_ref, kd_ref, vv_ref, gt_ref, it_ref):
    nch = HB // HC

    @pl.when(pl.program_id(1) == 0)
    def _():
        st_ref[...] = jnp.zeros(st_ref.shape, F32)

    row = lax.broadcasted_iota(jnp.int32, (HB, HB), 0)
    col = lax.broadcasted_iota(jnp.int32, (HB, HB), 1)
    same_chunk = (row // HC) == (col // HC)
    same_head = (row // HEAD_DIM) == (col // HEAD_DIM)
    head_ones = jnp.where(same_head, 1.0, 0.0).astype(BF16)
    chunk_ones = jnp.where(same_chunk, 1.0, 0.0).astype(BF16)
    rid = lax.broadcasted_iota(jnp.int32, (nch, HC, HGRN_W), 1)

    def prep(d, q_ref, v_ref, lf_ref, k_ref):
        fwd = d == 0
        tri = jnp.where(same_chunk & ((col <= row) if fwd else (col >= row)), 1.0, 0.0).astype(BF16)
        hi, lo = _split_bf16(lf_ref[...])
        cum = _dot(tri, hi) + _dot(tri, lo)
        tot = _dot(chunk_ones, hi) + _dot(chunk_ones, lo)
        q = q_ref[...].astype(F32)
        k = k_ref[...].astype(F32)
        v = v_ref[...]
        qd_ref[d] = (q * jnp.exp(cum)).astype(BF16)
        kd_ref[d] = (k * jnp.exp(tot - cum)).astype(BF16)
        vv_ref[d] = v
        gt_ref[d] = jnp.exp(tot)
        cum3 = cum.reshape(nch, HC, HGRN_W)
        q3 = q.reshape(nch, HC, HGRN_W)
        k3 = k.reshape(nch, HC, HGRN_W)
        v3 = v.astype(F32).reshape(nch, HC, HGRN_W)
        od = jnp.zeros((HB, HGRN_W), F32)
        for s in range(HC):
            dec = jnp.exp(jnp.minimum(cum3 - cum3[:, s:s + 1, :], 0.0))
            valid = (rid >= s) if fwd else (rid <= s)
            a = jnp.where(valid, q3 * k3[:, s:s + 1, :] * dec, 0.0)
            sc = _dot(a.reshape(HB, HGRN_W).astype(BF16), head_ones)
            vs = jnp.broadcast_to(v3[:, s:s + 1, :], (nch, HC, HGRN_W)).reshape(HB, HGRN_W)
            od = od + sc * vs
        return od

    od_f = prep(0, qf_ref, vf_ref, lff_ref, kf_ref)
    od_b = prep(1, qb_ref, vb_ref, lfb_ref, kb_ref)

    def body(i, carry):
        for d, c in ((0, i), (1, nch - 1 - i)):
            off = pl.multiple_of(c * HC, HC)
            qd_c = qd_ref[d, pl.ds(off, HC), :]
            kd_c = kd_ref[d, pl.ds(off, HC), :]
            v_c = vv_ref[d, pl.ds(off, HC), :]
            g_c = gt_ref[d, pl.ds(off, 1), :]
            st = st_ref[d]
            it_ref[d, pl.ds(off, HC), :] = _dot_nt(qd_c, st.astype(BF16))
            upd = _dot_tn(v_c, kd_c)
            st_ref[d] = st * g_c + jnp.where(same_head, upd, 0.0)
        return carry

    lax.fori_loop(0, nch, body, 0)
    of_ref[...] = it_ref[0] + od_f
    ob_ref[...] = it_ref[1] + od_b


def _hgrn(hq, hv, lf, kk, n_b, l_, ctx_len):
    nt = hq.shape[0]
    assert ctx_len == HB
    nlb = l_ // HB
    ctx0 = n_b * nlb

    def fwd_idx(col):
        return lambda b, s: (jnp.where(s == 0, ctx0 + b, b * nlb + s - 1), col)

    def bwd_idx(col):
        return lambda b, s: (jnp.where(s == 0, ctx0 + b, b * nlb + nlb - s), col)

    def blk(idx):
        return pl.BlockSpec((HB, HGRN_W), idx)

    return pl.pallas_call(
        _hgrn_kernel,
        grid=(n_b, nlb + 1),
        in_specs=[blk(fwd_idx(0)), blk(fwd_idx(0)), blk(fwd_idx(0)), blk(fwd_idx(0)),
                  blk(bwd_idx(0)), blk(bwd_idx(0)), blk(bwd_idx(1)), blk(bwd_idx(1))],
        out_specs=[blk(fwd_idx(0)), blk(bwd_idx(0))],
        out_shape=[jax.ShapeDtypeStruct((nt, HGRN_W), F32)] * 2,
        scratch_shapes=[pltpu.VMEM((2, HGRN_W, HGRN_W), F32),
                        pltpu.VMEM((2, HB, HGRN_W), BF16),
                        pltpu.VMEM((2, HB, HGRN_W), BF16),
                        pltpu.VMEM((2, HB, HGRN_W), BF16),
                        pltpu.VMEM((2, HB, HGRN_W), F32),
                        pltpu.VMEM((2, HB, HGRN_W), F32)],
        compiler_params=_params(("arbitrary", "arbitrary")),
        name="hgrn",
    )(hq, hv, lf, kk, hq, hv, lf, kk)


def _head_stack(q, n_heads):
    lane = lax.broadcasted_iota(jnp.int32, q.shape, 1)
    zero = jnp.zeros_like(q)
    return jnp.concatenate([jnp.where(lane // HEAD_DIM == h, q, zero) for h in range(n_heads)], axis=0)


def _head_unstack(o, n_heads):
    rows = o.shape[0] // n_heads
    lane = lax.broadcasted_iota(jnp.int32, (rows, o.shape[1]), 1)
    acc = jnp.zeros((rows, o.shape[1]), F32)
    for h in range(n_heads):
        acc = acc + jnp.where(lane // HEAD_DIM == h, o[h * rows:(h + 1) * rows], 0.0)
    return acc


def _na_kernel(n_rows, n_lat_steps, q_ref, k_ref, v_ref, kc_ref, vc_ref, bias_ref, o_ref):
    j = pl.program_id(1)
    kc = kc_ref[...]
    vc = vc_ref[...]
    nwin = NA_WIN_ROWS * GRID_W

    def attend(rr, win):
        q4 = _head_stack(q_ref[rr * GRID_W:(rr + 1) * GRID_W, :], NA_HEADS)
        s_ctx = _dot_nt(q4, kc)
        m = jnp.max(s_ctx, axis=1, keepdims=True)
        if win is not None:
            kw, vw, bias = win
            s_win = _dot_nt(q4, kw) + bias
            m = jnp.maximum(m, jnp.max(s_win, axis=1, keepdims=True))
            p_win = jnp.exp(s_win - m)
        p_ctx = jnp.exp(s_ctx - m)
        den = jnp.sum(p_ctx, axis=1, keepdims=True)
        o4 = _dot(p_ctx.astype(BF16), vc)
        if win is not None:
            den = den + jnp.sum(p_win, axis=1, keepdims=True)
            o4 = o4 + _dot(p_win.astype(BF16), vw)
        o = _head_unstack(o4 * (1.0 / den), NA_HEADS)
        o_ref[rr * GRID_W:(rr + 1) * GRID_W, :] = o.astype(BF16)

    @pl.when(j < n_lat_steps)
    def _():
        for rr in range(NA_QROWS):
            r = j * NA_QROWS + rr
            rs = jnp.clip(r - NA_WIN_ROWS // 2, 0, n_rows - NA_WIN_ROWS)
            start = pl.multiple_of(rs * GRID_W, GRID_W)
            kw = k_ref[pl.ds(start, nwin), :]
            vw = v_ref[pl.ds(start, nwin), :]
            attend(rr, (kw, vw, bias_ref[r - rs]))

    @pl.when(j >= n_lat_steps)
    def _():
        for rr in range(NA_QROWS):
            attend(rr, None)


def _na(nqkv, bias, n_b, l_, ctx_len, with_ctx):
    nt = nqkv.shape[0]
    qb = NA_QROWS * GRID_W
    assert ctx_len == qb
    n_lat = l_ // qb
    ctx0 = n_b * n_lat
    n_rows = l_ // GRID_W

    def q_idx(b, j):
        return (jnp.where(j < n_lat, b * n_lat + j, ctx0 + b), 0)

    cblk = (n_b * l_) // ctx_len
    return pl.pallas_call(
        functools.partial(_na_kernel, n_rows, n_lat),
        grid=(n_b, n_lat + (1 if with_ctx else 0)),
        in_specs=[pl.BlockSpec((qb, 256), q_idx),
                  pl.BlockSpec((l_, 256), lambda b, j: (b, 1)),
                  pl.BlockSpec((l_, 256), lambda b, j: (b, 2)),
                  pl.BlockSpec((ctx_len, 256), lambda b, j: (cblk + b, 1)),
                  pl.BlockSpec((ctx_len, 256), lambda b, j: (cblk + b, 2)),
                  pl.BlockSpec(bias.shape, lambda b, j: (0, 0, 0))],
        out_specs=pl.BlockSpec((qb, 256), q_idx),
        out_shape=jax.ShapeDtypeStruct((nt if with_ctx else n_b * l_, 256), BF16),
        compiler_params=_params(("arbitrary", "arbitrary")),
        name="natten",
    )(nqkv, nqkv, nqkv, nqkv, nqkv, bias)


def _swa_kernel(n_blocks, l_, sink_ref, q_ref, kp_ref, kq_ref, kn_ref, vp_ref, vq_ref, vn_ref,
                kc_ref, vc_ref, o_ref):
    j = pl.program_id(1)
    sb = SWA_BLOCK

    def attend(n, band):
        cols = slice(n * MXU_N, (n + 1) * MXU_N)
        q4 = _head_stack(q_ref[:, cols], SWA_GROUP)
        sink = jnp.concatenate([jnp.full((sb, 1), sink_ref[n * SWA_GROUP + g], F32)
                                for g in range(SWA_GROUP)], axis=0)
        s_ctx = _dot_nt(q4, kc_ref[:, cols])
        m = jnp.maximum(jnp.max(s_ctx, axis=1, keepdims=True), sink)
        if band is not None:
            kb = jnp.concatenate([kp_ref[:, cols], kq_ref[:, cols], kn_ref[:, cols]], axis=0)
            vb = jnp.concatenate([vp_ref[:, cols], vq_ref[:, cols], vn_ref[:, cols]], axis=0)
            s_band = jnp.where(band, _dot_nt(q4, kb), NEG)
            m = jnp.maximum(m, jnp.max(s_band, axis=1, keepdims=True))
            p_band = jnp.exp(s_band - m)
        p_ctx = jnp.exp(s_ctx - m)
        den = jnp.sum(p_ctx, axis=1, keepdims=True) + jnp.exp(sink - m)
        o4 = _dot(p_ctx.astype(BF16), vc_ref[:, cols])
        if band is not None:
            den = den + jnp.sum(p_band, axis=1, keepdims=True)
            o4 = o4 + _dot(p_band.astype(BF16), vb)
        o_ref[:, cols] = _head_unstack(o4 * (1.0 / den), SWA_GROUP).astype(BF16)

    @pl.when(j < n_blocks)
    def _():
        u = lax.broadcasted_iota(jnp.int32, (SWA_GROUP * sb, 3 * sb), 0) % sb
        v = lax.broadcasted_iota(jnp.int32, (SWA_GROUP * sb, 3 * sb), 1)
        kpos = (j - 1) * sb + v
        dist = j * sb + u - kpos
        band = (kpos >= 0) & (kpos < l_) & (dist <= SWA_WINDOW) & (dist >= -SWA_WINDOW)
        for n in range(SWA_KV_HEADS):
            attend(n, band)

    @pl.when(j >= n_blocks)
    def _():
        for n in range(SWA_KV_HEADS):
            attend(n, None)


def _swa(sq, kx, vx, sink, n_b, l_, ctx_len, with_ctx):
    nt = sq.shape[0]
    sb = SWA_BLOCK
    nb = l_ // sb
    cper = ctx_len // sb
    ctx0 = n_b * nb

    def q_idx(b, j):
        return (jnp.where(j < nb, b * nb + j, ctx0 + b * cper + (j - nb)), 0)

    def band_idx(off):
        return lambda b, j: (b * nb + jnp.clip(jnp.minimum(j, nb - 1) + off, 0, nb - 1), 0)

    cblk = (n_b * l_) // ctx_len
    blk = lambda idx: pl.BlockSpec((sb, 512), idx)
    cspec = pl.BlockSpec((ctx_len, 512), lambda b, j: (cblk + b, 0))
    return pl.pallas_call(
        functools.partial(_swa_kernel, nb, l_),
        grid=(n_b, nb + (cper if with_ctx else 0)),
        in_specs=[pl.BlockSpec(memory_space=pltpu.SMEM),
                  blk(q_idx), blk(band_idx(-1)), blk(band_idx(0)), blk(band_idx(1)),
                  blk(band_idx(-1)), blk(band_idx(0)), blk(band_idx(1)), cspec, cspec],
        out_specs=blk(q_idx),
        out_shape=jax.ShapeDtypeStruct((nt if with_ctx else n_b * l_, 512), BF16),
        compiler_params=_params(("arbitrary", "arbitrary")),
        name="swa",
    )(sink, sq, kx, kx, kx, vx, vx, vx, kx, vx)


def _layer_norm(r, g, b):
    mu = jnp.mean(r, axis=-1, keepdims=True)
    rc = r - mu
    var = jnp.mean(rc * rc, axis=-1, keepdims=True)
    return rc * lax.rsqrt(var + LN_EPS) * g + b


def _route(logits_t, bias_col):
    epg = N_EXPERTS // N_GROUPS
    s = _sigmoid(logits_t)
    sel = s + bias_col
    sel_r = [sel[e:e + 1, :] for e in range(N_EXPERTS)]
    s_r = [s[e:e + 1, :] for e in range(N_EXPERTS)]
    grp = []
    for g in range(N_GROUPS):
        a = sel_r[g * epg:(g + 1) * epg]
        m1 = functools.reduce(jnp.maximum, a)
        m2 = functools.reduce(jnp.maximum,
                              [jnp.minimum(a[i], a[k]) for i in range(epg) for k in range(i + 1, epg)])
        grp.append(m1 + m2)
    m_r, w_r = [], []
    for g in range(N_GROUPS):
        best = None
        for k in range(N_GROUPS):
            if k == g:
                continue
            c = (grp[g] > grp[k]) if k < g else (grp[g] >= grp[k])
            best = c if best is None else (best & c)
        for e in range(g * epg, (g + 1) * epg):
            rank = jnp.zeros_like(sel_r[e])
            for k in range(g * epg, (g + 1) * epg):
                if k == e:
                    continue
                ahead = (sel_r[k] >= sel_r[e]) if k < e else (sel_r[k] > sel_r[e])
                rank = rank + jnp.where(ahead, 1.0, 0.0)
            chosen = jnp.where(best & (rank < 1.5), 1.0, 0.0)
            m_r.append(chosen)
            w_r.append(chosen * s_r[e])
    inv = 1.0 / functools.reduce(lambda a, b: a + b, w_r)
    return m_r, [w * inv for w in w_r]


def _pack_bf16_pairs(a, b):
    ua = pltpu.bitcast(a.astype(BF16).astype(F32), jnp.uint32)
    ub = pltpu.bitcast(b.astype(BF16).astype(F32), jnp.uint32)
    return (ua >> 16) | ub


def _unpack_bf16_pairs(u):
    return (pltpu.bitcast(u << 16, F32), pltpu.bitcast(u & jnp.uint32(0xFFFF0000), F32))


def _outproj_kernel(n_lat, of_ref, ob_ref, hg_ref, yb_ref, yc_ref, xa_ref, xb_ref, mod_ref, wo_ref, pv_ref,
                    wrh_ref, wrl_ref, rb_ref, before_ref, x1_ref, h2u_ref, route_ref, gt_ref, cnt_ref):
    @pl.when(pl.program_id(0) == 0)
    def _():
        cnt_ref[...] = jnp.zeros(cnt_ref.shape, F32)

    row = lax.broadcasted_iota(jnp.int32, (HGRN_W, HGRN_W), 0)
    col = lax.broadcasted_iota(jnp.int32, (HGRN_W, HGRN_W), 1)
    head_ones = jnp.where((row // HEAD_DIM) == (col // HEAD_DIM), 1.0, 0.0).astype(BF16)
    pv = pv_ref[...]
    o = of_ref[...] + ob_ref[...]
    hi, lo = _split_bf16(o * o)
    ms = (_dot(hi, head_ones) + _dot(lo, head_ones)) * (1.0 / HEAD_DIM)
    gate = hg_ref[...].astype(F32)
    ya = o * lax.rsqrt(ms + RMS_EPS) * pv[2:3, 0:HGRN_W] * (gate * _sigmoid(gate))
    y = (_dot(ya.astype(BF16), wo_ref[0:256, :]) + _dot(yb_ref[...], wo_ref[256:512, :])
         + _dot(yc_ref[...], wo_ref[512:1024, :]))
    mod = mod_ref[0]
    x = jnp.where(pl.program_id(0) < n_lat, xa_ref[...], xb_ref[...])
    x1 = _layer_norm(ALPHA * x + mod[2:3] * y, pv[0:1], pv[1:2])
    x1_ref[...] = x1
    h2 = x1 * (1.0 + mod[4:5]) + mod[3:4]
    hh, hl = _split_bf16(h2)
    half = h2.shape[1] // 2
    h2u_ref[...] = _pack_bf16_pairs(h2[:, 0:half], h2[:, half:2 * half])
    wrh = wrh_ref[...]
    logits_t = _dot_nt(wrh, hh) + _dot_nt(wrh, hl) + _dot_nt(wrl_ref[...], hh)
    m_r, w_r = _route(logits_t, rb_ref[...])
    m_t = jnp.concatenate(m_r, axis=0)
    rank_t = _dot(m_t.astype(BF16), before_ref[...]) + cnt_ref[:, 0:1]
    cnt_ref[...] = cnt_ref[...] + jnp.sum(m_t, axis=1, keepdims=True)
    seen = jnp.zeros_like(m_r[0])
    e1 = e2 = r1 = r2 = g1 = g2 = jnp.zeros_like(m_r[0])
    for e in range(N_EXPERTS):
        is1 = m_r[e] * (1.0 - seen)
        is2 = m_r[e] * seen
        rk = rank_t[e:e + 1, :]
        e1, e2 = e1 + e * is1, e2 + e * is2
        r1, r2 = r1 + rk * is1, r2 + rk * is2
        g1, g2 = g1 + w_r[e] * is1, g2 + w_r[e] * is2
        seen = seen + is1
    zero = jnp.zeros_like(e1)
    route_ref[...] = jnp.concatenate([e1, e2, r1, r2, g1, g2, zero, zero], axis=0)
    pad = jnp.zeros((LANES - 2, g1.shape[1]), F32)
    gt_ref[...] = jnp.concatenate([g1, g2, pad], axis=0).T


def _outproj(o_f, o_b, hg, yb, yc, xa, xb, ctx_blk, mod, wo, pv, wrh, wrl, rb, n_b, l_, n_tiles):
    d = xa.shape[1]
    tpb = l_ // TM
    n_lat = n_b * tpb
    before = np.triu(np.ones((TM, TM), np.float32), 1).astype(jnp.bfloat16)

    def mod_idx(j):
        return (jnp.minimum(j // tpb, n_b), 0, 0)

    def rows(width):
        return pl.BlockSpec((TM, width), lambda j: (j, 0))

    full = lambda a: pl.BlockSpec(a.shape, lambda j: (0,) * a.ndim)
    nr = n_tiles * TM
    return pl.pallas_call(
        functools.partial(_outproj_kernel, n_lat),
        grid=(n_tiles,),
        in_specs=[rows(256), rows(256), rows(256), rows(256), rows(512),
                  pl.BlockSpec((TM, d), lambda j: (jnp.minimum(j, n_lat - 1), 0)),
                  pl.BlockSpec((TM, d), lambda j: (ctx_blk, 0)),
                  pl.BlockSpec((1, 6, d), mod_idx), full(wo), full(pv), full(wrh), full(wrl), full(rb),
                  full(before)],
        out_specs=[rows(d), rows(d // 2), pl.BlockSpec((8, TM), lambda j: (0, j)), rows(LANES),
                   pl.BlockSpec((N_EXPERTS, LANES), lambda j: (0, 0))],
        out_shape=[jax.ShapeDtypeStruct((nr, d), F32), jax.ShapeDtypeStruct((nr, d // 2), jnp.uint32),
                   jax.ShapeDtypeStruct((8, nr), F32), jax.ShapeDtypeStruct((nr, LANES), F32),
                   jax.ShapeDtypeStruct((N_EXPERTS, LANES), F32)],
        compiler_params=_params(("arbitrary",)),
        name="outproj",
    )(o_f, o_b, hg, yb, yc, xa, xb, mod, wo, pv, wrh, wrl, rb, before)


def _sc_split(rows_per_worker):
    best = None
    for chunk in (64, 48, 32):
        for nbuf in range(SC_MAX_ROWS // chunk, 0, -1):
            if rows_per_worker % (chunk * nbuf) == 0 and (best is None or chunk * nbuf > best[0] * best[1]):
                best = (chunk, nbuf)
    return best


def _sc_workers():
    info = plsc.get_sparse_core_info()
    return info.num_cores, info.num_cores * info.num_subcores


def _sc_scatter_rows(table, pos, n_slots):
    n_tok, width = table.shape
    n_cores, n_workers = _sc_workers()
    per_w = n_tok // n_workers
    assert n_tok % n_workers == 0
    chunk, nbuf = _sc_split(per_w)
    n_chunks = per_w // chunk
    mesh = plsc.VectorSubcoreMesh(core_axis_name="c", subcore_axis_name="s")

    @functools.partial(
        pl.kernel, mesh=mesh,
        out_type=jax.ShapeDtypeStruct((n_slots, width), table.dtype),
        scratch_types=[pltpu.VMEM((2 * n_chunks, chunk), jnp.int32),
                       pltpu.VMEM((nbuf, chunk, width), table.dtype),
                       pltpu.SemaphoreType.DMA((nbuf,)),
                       pltpu.SemaphoreType.DMA((nbuf,))],
    )
    def scatter(table_hbm, pos_hbm, out_hbm, idx_v, rows_v, rsem, wsem):
        wid = lax.axis_index("s") * n_cores + lax.axis_index("c")
        base = wid * per_w
        pltpu.sync_copy(pos_hbm.at[wid], idx_v)

        @pl.loop(0, n_chunks // nbuf)
        def _(g):
            c0 = g * nbuf
            gets = [pltpu.async_copy(table_hbm.at[pl.ds(base + (c0 + b) * chunk, chunk)], rows_v.at[b],
                                     rsem.at[b]) for b in range(nbuf)]
            puts = []
            for b in range(nbuf):
                gets[b].wait()
                for k in range(2):
                    puts.append(pltpu.async_copy(rows_v.at[b], out_hbm.at[idx_v.at[k * n_chunks + c0 + b]],
                                                 wsem.at[b]))
            for put in puts:
                put.wait()

    pos_w = pos.reshape(2, n_workers, n_chunks, chunk).transpose(1, 0, 2, 3).reshape(n_workers, 2 * n_chunks, chunk)
    return scatter(table, pos_w)


def _sc_gather_rows(table, idx):
    n_out = idx.shape[0]
    width = table.shape[1]
    n_cores, n_workers = _sc_workers()
    per_w = n_out // n_workers
    assert n_out % n_workers == 0
    chunk, nbuf = _sc_split(per_w)
    mesh = plsc.VectorSubcoreMesh(core_axis_name="c", subcore_axis_name="s")

    @functools.partial(
        pl.kernel, mesh=mesh,
        out_type=jax.ShapeDtypeStruct((n_out, width), table.dtype),
        scratch_types=[pltpu.VMEM((per_w,), jnp.int32),
                       pltpu.VMEM((nbuf, chunk, width), table.dtype),
                       pltpu.SemaphoreType.DMA((nbuf,)),
                       pltpu.SemaphoreType.DMA((nbuf,))],
    )
    def gather(table_hbm, idx_hbm, out_hbm, idx_v, rows_v, gsem, wsem):
        wid = lax.axis_index("s") * n_cores + lax.axis_index("c")
        base = wid * per_w
        pltpu.sync_copy(idx_hbm.at[pl.ds(base, per_w)], idx_v)

        @pl.loop(0, per_w // (chunk * nbuf))
        def _(g):
            off = g * (chunk * nbuf)
            gets = [pltpu.async_copy(table_hbm.at[idx_v.at[pl.ds(off + b * chunk, chunk)]], rows_v.at[b],
                                     gsem.at[b]) for b in range(nbuf)]
            puts = []
            for b in range(nbuf):
                gets[b].wait()
                puts.append(pltpu.async_copy(rows_v.at[b], out_hbm.at[pl.ds(base + off + b * chunk, chunk)],
                                             wsem.at[b]))
            for put in puts:
                put.wait()

    return gather(table, idx)


def _gmm_kernel(te_ref, nv_ref, nu_ref, xs_ref, wg_ref, wu_ref, wd_ref, ys_ref, wgb_ref, wub_ref, wdb_ref):
    i = pl.program_id(0)
    used = i < nu_ref[0]
    fresh = jnp.logical_or(i == 0, te_ref[i] != te_ref[jnp.maximum(i - 1, 0)])

    @pl.when(jnp.logical_and(used, fresh))
    def _():
        wgb_ref[...] = wg_ref[0, 0].astype(BF16)
        wub_ref[...] = wu_ref[0, 0].astype(BF16)
        wdb_ref[...] = wd_ref[0, 0].astype(BF16)

    @pl.when(used)
    def _():
        rowid = lax.broadcasted_iota(jnp.int32, xs_ref.shape, 0)
        lo, hi = _unpack_bf16_pairs(jnp.where(rowid < nv_ref[i], xs_ref[...], jnp.uint32(0)))
        lo, hi = lo.astype(BF16), hi.astype(BF16)
        half = lo.shape[1]
        a = _dot(lo, wgb_ref[0:half, :]) + _dot(hi, wgb_ref[half:2 * half, :])
        u = _dot(lo, wub_ref[0:half, :]) + _dot(hi, wub_ref[half:2 * half, :])
        y = _dot((a * _sigmoid(a) * u).astype(BF16), wdb_ref[...])
        ys_ref[...] = _pack_bf16_pairs(y[:, 0:half], y[:, half:2 * half])

    @pl.when(jnp.logical_not(used))
    def _():
        ys_ref[...] = jnp.zeros(ys_ref.shape, jnp.uint32)


def _gmm(tile_expert, tile_valid, n_used, xs, wg, wu, wd, layer):
    p_rows, half = xs.shape
    d = 2 * half
    wspec = lambda r, c: pl.BlockSpec((1, 1, r, c), lambda i, te, nv, nu: (layer, te[i], 0, 0))
    return pl.pallas_call(
        _gmm_kernel,
        grid_spec=pltpu.PrefetchScalarGridSpec(
            num_scalar_prefetch=3,
            grid=(p_rows // TM,),
            in_specs=[pl.BlockSpec((TM, half), lambda i, te, nv, nu: (i, 0)),
                      wspec(d, D_EXPERT), wspec(d, D_EXPERT), wspec(D_EXPERT, d)],
            out_specs=pl.BlockSpec((TM, half), lambda i, te, nv, nu: (i, 0)),
            scratch_shapes=[pltpu.VMEM((d, D_EXPERT), BF16), pltpu.VMEM((d, D_EXPERT), BF16),
                            pltpu.VMEM((D_EXPERT, d), BF16)]),
        out_shape=jax.ShapeDtypeStruct((p_rows, half), jnp.uint32),
        compiler_params=_params(("arbitrary",)),
        name="moe_experts",
    )(tile_expert, tile_valid, n_used, xs, wg, wu, wd)


def _moefin_kernel(o1_ref, o2_ref, gt_ref, x_ref, mod_ref, pv_ref, out_ref):
    gt = gt_ref[...]
    g1, g2 = gt[:, 0:1], gt[:, 1:2]
    a1, b1 = _unpack_bf16_pairs(o1_ref[...])
    a2, b2 = _unpack_bf16_pairs(o2_ref[...])
    f = jnp.concatenate([g1 * a1 + g2 * a2, g1 * b1 + g2 * b2], axis=1)
    mod = mod_ref[0]
    pv = pv_ref[...]
    out_ref[...] = _layer_norm(ALPHA * x_ref[...] + mod[5:6] * f, pv[0:1], pv[1:2])


def _moefin(o12, gt, x1, mod, pv, n_b, l_, n_tiles):
    d = x1.shape[1]
    tpb = l_ // TM

    def mod_idx(j):
        return (jnp.minimum(j // tpb, n_b), 0, 0)

    return pl.pallas_call(
        _moefin_kernel,
        grid=(n_tiles,),
        in_specs=[pl.BlockSpec((TM, d // 2), lambda j: (j, 0)),
                  pl.BlockSpec((TM, d // 2), lambda j: (j + n_tiles, 0)),
                  pl.BlockSpec((TM, LANES), lambda j: (j, 0)),
                  pl.BlockSpec((TM, d), lambda j: (j, 0)),
                  pl.BlockSpec((1, 6, d), mod_idx),
                  pl.BlockSpec(pv.shape, lambda j: (0, 0))],
        out_specs=pl.BlockSpec((TM, d), lambda j: (j, 0)),
        out_shape=jax.ShapeDtypeStruct((n_tiles * TM, d), F32),
        compiler_params=_params(("arbitrary",)),
        name="moe_combine",
    )(o12, o12, gt, x1, mod, pv)


def _moe_routed(h2u, route, gt, counts, wg, wu, wd, layer, x1, mod, pv, n_b, l_, n_tiles):
    n_tok = n_tiles * TM
    n_slots = -(-(2 * n_tok + N_EXPERTS * TM) // SC_ROW_QUANTUM) * SC_ROW_QUANTUM
    cnt = counts[:, 0].astype(jnp.int32)
    padded = ((cnt + TM - 1) // TM) * TM
    upto = jnp.arange(N_EXPERTS)[None, :] <= jnp.arange(N_EXPERTS)[:, None]
    ends = jnp.sum(jnp.where(upto, padded[None, :], 0), axis=1)
    offs = ends - padded
    e12 = route[0:2].astype(jnp.int32)
    r12 = route[2:4].astype(jnp.int32)
    onehot = e12[:, :, None] == jnp.arange(N_EXPERTS, dtype=jnp.int32)
    pos = r12 + jnp.sum(jnp.where(onehot, offs, 0), axis=-1)
    tile_start = jnp.arange(n_slots // TM, dtype=jnp.int32) * TM
    tile_expert = jnp.minimum(jnp.sum(tile_start[:, None] >= ends[None, :], axis=1), N_EXPERTS - 1)
    pick = tile_expert[:, None] == jnp.arange(N_EXPERTS, dtype=jnp.int32)
    tile_valid = jnp.clip(jnp.sum(jnp.where(pick, (offs + cnt)[None, :], 0), axis=1) - tile_start, 0, TM)
    n_used = (ends[N_EXPERTS - 1] // TM).reshape(1)
    xs = _sc_scatter_rows(h2u, pos, n_slots)
    ys = _gmm(tile_expert.astype(jnp.int32), tile_valid.astype(jnp.int32), n_used.astype(jnp.int32),
              xs, wg, wu, wd, layer)
    o12 = _sc_gather_rows(ys, pos.reshape(2 * n_tok))
    return _moefin(o12, gt, x1, mod, pv, n_b, l_, n_tiles)


def _relayout_w_in(w):
    sk = w[:, 2560:2688]
    sv = w[:, 2688:2816]
    rep = lambda a: jnp.concatenate(
        [jnp.tile(a[:, n * HEAD_DIM:(n + 1) * HEAD_DIM], (1, SWA_GROUP)) for n in range(SWA_KV_HEADS)], axis=1)
    return jnp.concatenate([w[:, :2560], rep(sk), rep(sv)], axis=1).astype(BF16)


def _rope_tables(l_):
    pos = np.arange(l_)
    nf = HEAD_DIM // 4
    inv = ROPE_BASE ** (-np.arange(nf, dtype=np.float64) / nf)
    ar = (pos // GRID_W)[:, None] * inv
    ac = (pos % GRID_W)[:, None] * inv
    cos = np.concatenate([np.cos(ar), np.cos(ar), np.cos(ac), np.cos(ac)], axis=1)
    sin = np.concatenate([-np.sin(ar), np.sin(ar), -np.sin(ac), np.sin(ac)], axis=1)
    cos = np.concatenate([np.tile(cos, (1, 2)), np.ones((TM, LANES))], axis=0)
    sin = np.concatenate([np.tile(sin, (1, 2)), np.zeros((TM, LANES))], axis=0)
    return cos.astype(np.float32), sin.astype(np.float32)


def _na_bias(rpb):
    u = np.arange(GRID_W)[:, None]
    v = np.arange(GRID_W)[None, :]
    cs = np.clip(u - NA_WIN_COLS // 2, 0, GRID_W - NA_WIN_COLS)
    colmask = (v >= cs) & (v < cs + NA_WIN_COLS)
    coff = np.clip(v - u + NA_WIN_COLS - 1, 0, 2 * NA_WIN_COLS - 2)
    n_off = 2 * NA_WIN_COLS - 1
    pick = (coff[None] == np.arange(n_off)[:, None, None]).astype(np.float32)
    band = jnp.einsum('hab,buv->ahuv', rpb.astype(F32), pick, precision=lax.Precision.HIGHEST)
    band = jnp.where(colmask[None, None], band, NEG)
    tabs = []
    for delta in range(NA_WIN_ROWS):
        rows = band[NA_WIN_ROWS - 1 - delta:2 * NA_WIN_ROWS - 1 - delta]
        tabs.append(rows.transpose(1, 2, 0, 3).reshape(NA_HEADS * GRID_W, NA_WIN_ROWS * GRID_W))
    return jnp.stack(tabs, axis=0)


def kernel(x, c, ctx, c_ctx, w_ada, b_ada, w_in, lb_logits, hgrn_norm, na_rpb, swa_sink, w_out,
           ln1_g, ln1_b, ln2_g, ln2_b, w_router, router_bias, w_gate, w_up, w_down):
    n_b, l_, d = x.shape
    ctx_len = ctx.shape[1]
    depth = w_ada.shape[0]
    assert n_b * ctx_len == TM and l_ % TM == 0 and n_b + 1 <= N_COND
    n_lat_tiles = (n_b * l_) // TM

    p_lb = jax.nn.softmax(lb_logits.astype(F32), axis=0)
    gp_all, run = [], jnp.zeros_like(p_lb[0])
    for l in range(depth):
        run = run + p_lb[l]
        lb = jnp.maximum(run - p_lb[0], LB_MIN).reshape(2 * HGRN_W)
        gp_l = jnp.stack([jnp.log(lb), jnp.log1p(-lb), 1.0 - lb] + [jnp.zeros_like(lb)] * 5, axis=0)
        gp_all.append(gp_l)

    cond = jnp.concatenate([c, c_ctx[None, :], jnp.zeros((N_COND - n_b - 1, d), F32)], axis=0)
    mod_all = _ada(cond.T, w_ada, b_ada, n_b + 1).reshape(depth, N_COND, 6, d)

    cos_t, sin_t = _rope_tables(l_)
    wr_t = w_router.T
    wrh = wr_t.astype(BF16)
    wrl = (wr_t - wrh.astype(F32)).astype(BF16)
    rb = router_bias.astype(F32)[:, None]

    xa, xb, ctx_blk = x.reshape(n_b * l_, d), ctx.reshape(n_b * ctx_len, d), 0
    for l in range(depth):
        last = l == depth - 1
        mod = mod_all[l]
        hq, hv, hg, lf, kk, nqkv, sq, kx, vx = _inproj(
            xa, xb, ctx_blk, mod, _relayout_w_in(w_in[l]), gp_all[l], cos_t, sin_t, n_b, l_)
        o_f, o_b = _hgrn(hq, hv, lf, kk, n_b, l_, ctx_len)
        yb = _na(nqkv, _na_bias(na_rpb[l]), n_b, l_, ctx_len, not last)
        yc = _swa(sq, kx, vx, swa_sink[l].astype(F32), n_b, l_, ctx_len, not last)
        n_tiles = n_lat_tiles if last else n_lat_tiles + 1
        pv1 = jnp.stack([ln1_g[l], ln1_b[l], jnp.tile(hgrn_norm[l], d // HEAD_DIM)]
                        + [jnp.zeros((d,), F32)] * 5, axis=0)
        x1, h2u, route, gt, counts = _outproj(o_f, o_b, hg, yb, yc, xa, xb, ctx_blk, mod,
                                              w_out[l].astype(BF16), pv1, wrh, wrl, rb, n_b, l_, n_tiles)
        pv2 = jnp.stack([ln2_g[l], ln2_b[l]] + [jnp.zeros((d,), F32)] * 6, axis=0)
        xa = _moe_routed(h2u, route, gt, counts, w_gate, w_up, w_down, l, x1, mod, pv2, n_b, l_, n_tiles)
        xb, ctx_blk = xa, n_lat_tiles
    return xa[:n_b * l_].reshape(n_b, l_, d)
```

```python
import functools

import numpy as np
import jax
import jax.numpy as jnp
from jax import lax
from jax.experimental import pallas as pl
from jax.experimental.pallas import tpu as pltpu
from jax.experimental.pallas import tpu_sc as plsc

F32 = jnp.float32
BF16 = jnp.bfloat16

D_MODEL = 1024
GRID_W = 64
HEAD_DIM = 64
HGRN_W = 256
NA_HEADS = 4
NA_WIN_ROWS = 8
NA_WIN_COLS = 16
SWA_Q_HEADS = 8
SWA_KV_HEADS = 2
SWA_GROUP = 4
SWA_WINDOW = 128
SWA_BLOCK = 128
ROPE_BASE = 10000.0
N_EXPERTS = 16
N_GROUPS = 4
D_EXPERT = 512
LN_EPS = 1e-5
RMS_EPS = 1e-6
NEG = -1e30
LB_MIN = 1e-6
DEPTH = 2
ALPHA = (2.0 * DEPTH) ** 0.25

LANES = 128
MXU_N = 256
TM = 512
HB = 256
HC = 16
NA_QROWS = 4
VMEM_LIMIT = 56 * 1024 * 1024

SC_MAX_ROWS = 192
SC_ROW_QUANTUM = 2048

N_COND = 8


def _dot(a, b):
    return jnp.dot(a, b, preferred_element_type=F32)


def _dot_nt(a, b):
    return lax.dot_general(a, b, (((1,), (1,)), ((), ())), preferred_element_type=F32)


def _dot_tn(a, b):
    return lax.dot_general(a, b, (((0,), (0,)), ((), ())), preferred_element_type=F32)


def _sigmoid(x):
    return 1.0 / (1.0 + jnp.exp(-x))


def _split_bf16(x):
    hi = x.astype(BF16)
    lo = (x - hi.astype(F32)).astype(BF16)
    return hi, lo


def _params(sem):
    return pltpu.CompilerParams(dimension_semantics=sem, vmem_limit_bytes=VMEM_LIMIT)


def _ada_kernel(n_rows, condt_ref, w_ref, b_ref, o_ref):
    c = condt_ref[...]
    s = c * _sigmoid(c)
    w = w_ref[0]
    rows = [jnp.sum(w * s[:, r:r + 1], axis=0, keepdims=True) for r in range(n_rows)]
    rows.append(jnp.zeros((N_COND - n_rows, w.shape[1]), F32))
    o_ref[0] = jnp.concatenate(rows, axis=0) + b_ref[0]


def _ada(cond_t, w_ada, b_ada, n_rows):
    depth, d, n6 = w_ada.shape
    tn = 1024
    return pl.pallas_call(
        functools.partial(_ada_kernel, n_rows),
        grid=(depth, n6 // tn),
        in_specs=[pl.BlockSpec((d, N_COND), lambda l, n: (0, 0)),
                  pl.BlockSpec((1, d, tn), lambda l, n: (l, 0, n)),
                  pl.BlockSpec((1, 1, tn), lambda l, n: (l, 0, n))],
        out_specs=pl.BlockSpec((1, N_COND, tn), lambda l, n: (l, 0, n)),
        out_shape=jax.ShapeDtypeStruct((depth, N_COND, n6), F32),
        compiler_params=_params(("arbitrary", "arbitrary")),
        name="ada",
    )(cond_t, w_ada, b_ada.reshape(depth, 1, n6))


def _rope(z, cos, sin, first):
    sw = jnp.where(first, pltpu.roll(z, LANES - 16, axis=1), pltpu.roll(z, 16, axis=1))
    return z * cos + sw * sin


def _inproj_kernel(n_lat, xa_ref, xb_ref, mod_ref, w_ref, gp_ref, cos_ref, sin_ref,
                   hq_ref, hv_ref, hg_ref, lf_ref, kk_ref, nqkv_ref, sq_ref, kx_ref, vx_ref):
    mod = mod_ref[0]
    x = jnp.where(pl.program_id(0) < n_lat, xa_ref[...], xb_ref[...])
    h = (x * (1.0 + mod[1:2]) + mod[0:1]).astype(BF16)
    gp = gp_ref[...]
    cos = cos_ref[...]
    sin = sin_ref[...]
    lane = lax.broadcasted_iota(jnp.int32, cos.shape, 1)
    first = (lane & 16) == 0
    scale = HEAD_DIM ** -0.5

    def chunk(c):
        return _dot(h, w_ref[:, c * MXU_N:(c + 1) * MXU_N])

    def gates(z, d):
        cols = slice(d * MXU_N, (d + 1) * MXU_N)
        a = gp[0:1, cols]
        b = gp[1:2, cols] + (jnp.minimum(z, 0.0) - jnp.log1p(jnp.exp(-jnp.abs(z))))
        logf = jnp.maximum(a, b) + jnp.log1p(jnp.exp(-jnp.abs(a - b)))
        k = gp[2:3, cols] * (1.0 / (1.0 + jnp.exp(z)))
        return logf, k

    def rope2(z):
        return jnp.concatenate([_rope(z[:, 0:LANES], cos, sin, first),
                                _rope(z[:, LANES:2 * LANES], cos, sin, first)], axis=1)

    hq_ref[...] = chunk(0).astype(BF16)
    for d in range(2):
        logf, k = gates(chunk(1 + d), d)
        lf_ref[:, d * MXU_N:(d + 1) * MXU_N] = logf
        kk_ref[:, d * MXU_N:(d + 1) * MXU_N] = k.astype(BF16)
    hv_ref[...] = chunk(3).astype(BF16)
    hg_ref[...] = chunk(4).astype(BF16)
    nqkv_ref[:, 0:MXU_N] = (chunk(5) * scale).astype(BF16)
    nqkv_ref[:, MXU_N:2 * MXU_N] = chunk(6).astype(BF16)
    nqkv_ref[:, 2 * MXU_N:3 * MXU_N] = chunk(7).astype(BF16)
    for d in range(2):
        sq_ref[:, d * MXU_N:(d + 1) * MXU_N] = (rope2(chunk(8 + d)) * scale).astype(BF16)
        kx_ref[:, d * MXU_N:(d + 1) * MXU_N] = rope2(chunk(10 + d)).astype(BF16)
        vx_ref[:, d * MXU_N:(d + 1) * MXU_N] = chunk(12 + d).astype(BF16)


def _inproj(xa, xb, ctx_blk, mod, w, gp, cos_t, sin_t, n_b, l_):
    d = xa.shape[1]
    tpb = l_ // TM
    n_lat = n_b * tpb
    nt = (n_lat + 1) * TM

    def mod_idx(j):
        return (jnp.minimum(j // tpb, n_b), 0, 0)

    def rope_idx(j):
        return (jnp.where(j < n_lat, j % tpb, tpb), 0)

    def rows(width):
        return pl.BlockSpec((TM, width), lambda j: (j, 0))

    widths = [(HGRN_W, BF16), (HGRN_W, BF16), (HGRN_W, BF16), (2 * HGRN_W, F32), (2 * HGRN_W, BF16),
              (3 * 256, BF16), (512, BF16), (512, BF16), (512, BF16)]
    return pl.pallas_call(
        functools.partial(_inproj_kernel, n_lat),
        grid=(nt // TM,),
        in_specs=[pl.BlockSpec((TM, d), lambda j: (jnp.minimum(j, n_lat - 1), 0)),
                  pl.BlockSpec((TM, d), lambda j: (ctx_blk, 0)),
                  pl.BlockSpec((1, 6, d), mod_idx),
                  pl.BlockSpec(w.shape, lambda j: (0, 0)),
                  pl.BlockSpec(gp.shape, lambda j: (0, 0)),
                  pl.BlockSpec((TM, LANES), rope_idx),
                  pl.BlockSpec((TM, LANES), rope_idx)],
        out_specs=[rows(wd) for wd, _ in widths],
        out_shape=[jax.ShapeDtypeStruct((nt, wd), dt) for wd, dt in widths],
        compiler_params=_params(("arbitrary",)),
        name="inproj",
    )(xa, xb, mod, w, gp, cos_t, sin_t)


def _hgrn_consts():
    t = np.arange(HB)[:, None]
    u = np.arange(HB)[None, :]
    same = (t // HC) == (u // HC)
    tr, ur = t % HC, u % HC
    half = HC // 2
    mats, codes = [], []
    for fwd in (True, False):
        if fwd:
            incl = ur <= tr
            mid = ur <= (tr // half) * half + half // 2 - 1
            edge = ur <= half - 1
            code = np.where((t // half == u // half) & (u <= t), 1, np.where(same & (tr >= half) & (ur < half), 2, 0))
        else:
            incl = ur >= tr
            mid = ur >= (tr // half) * half + half // 2
            edge = ur >= half
            code = np.where((t // half == u // half) & (u >= t), 1, np.where(same & (tr < half) & (ur >= half), 2, 0))
        mats.append(np.concatenate([same & incl, same, same & mid, same & edge], axis=0))
        codes.append(code)
    return (np.stack(mats).astype(np.float32).astype(jnp.bfloat16), np.stack(codes).astype(np.float32))


def _hgrn_kernel(qf_ref, vf_ref, lff_ref, kf_ref, qb_ref, vb_ref, lfb_ref, kb_ref, cm_ref, code_ref,
                 of_ref, ob_ref, st_ref, qd_ref, kd_ref, vv_ref, gt_ref, it_ref):
    nch = HB // HC

    @pl.when(pl.program_id(1) == 0)
    def _():
        st_ref[...] = jnp.zeros(st_ref.shape, F32)

    row = lax.broadcasted_iota(jnp.int32, (HGRN_W, HGRN_W), 0)
    col = lax.broadcasted_iota(jnp.int32, (HGRN_W, HGRN_W), 1)
    same_head = (row // HEAD_DIM) == (col // HEAD_DIM)
    n_heads = HGRN_W // HEAD_DIM

    def prep(d, q_ref, v_ref, lf_ref, k_ref):
        hi, lo = _split_bf16(lf_ref[...])
        sums = _dot(cm_ref[d], hi) + _dot(cm_ref[d], lo)
        cum, tot = sums[0:HB], sums[HB:2 * HB]
        mid, edge = sums[2 * HB:3 * HB], sums[3 * HB:4 * HB]
        q = q_ref[...].astype(F32)
        k = k_ref[...].astype(F32)
        v = v_ref[...]
        qd_ref[d] = (q * jnp.exp(cum)).astype(BF16)
        kd_ref[d] = (k * jnp.exp(tot - cum)).astype(BF16)
        vv_ref[d] = v
        gt_ref[d] = jnp.exp(tot)
        q1 = (q * jnp.exp(cum - mid)).astype(BF16)
        k1 = (k * jnp.exp(mid - cum)).astype(BF16)
        q2 = (q * jnp.exp(jnp.minimum(cum - edge, 0.0))).astype(BF16)
        k2 = (k * jnp.exp(jnp.minimum(edge - cum, 0.0))).astype(BF16)
        s1 = _dot_nt(_head_stack(q1, n_heads), k1)
        s2 = _dot_nt(_head_stack(q2, n_heads), k2)
        code = jnp.concatenate([code_ref[d]] * n_heads, axis=0)
        p = jnp.where(code == 1.0, s1, jnp.where(code == 2.0, s2, 0.0))
        return _head_unstack(_dot(p.astype(BF16), v), n_heads)

    od_f = prep(0, qf_ref, vf_ref, lff_ref, kf_ref)
    od_b = prep(1, qb_ref, vb_ref, lfb_ref, kb_ref)

    for i in range(nch):
        for d, c in ((0, i), (1, nch - 1 - i)):
            rows = slice(c * HC, (c + 1) * HC)
            st = st_ref[d]
            it_ref[d, rows, :] = _dot_nt(qd_ref[d, rows, :], st.astype(BF16))
            upd = _dot_tn(vv_ref[d, rows, :], kd_ref[d, rows, :])
            st_ref[d] = st * gt_ref[d, c * HC:c * HC + 1, :] + jnp.where(same_head, upd, 0.0)

    of_ref[...] = it_ref[0] + od_f
    ob_ref[...] = it_ref[1] + od_b


def _hgrn(hq, hv, lf, kk, n_b, l_, ctx_len):
    nt = hq.shape[0]
    assert ctx_len == HB
    nlb = l_ // HB
    ctx0 = n_b * nlb
    cmats, codes = _hgrn_consts()

    def fwd_idx(col):
        return lambda b, s: (jnp.where(s == 0, ctx0 + b, b * nlb + s - 1), col)

    def bwd_idx(col):
        return lambda b, s: (jnp.where(s == 0, ctx0 + b, b * nlb + nlb - s), col)

    def blk(idx):
        return pl.BlockSpec((HB, HGRN_W), idx)

    full = lambda a: pl.BlockSpec(a.shape, lambda b, s: (0,) * a.ndim)
    return pl.pallas_call(
        _hgrn_kernel,
        grid=(n_b, nlb + 1),
        in_specs=[blk(fwd_idx(0)), blk(fwd_idx(0)), blk(fwd_idx(0)), blk(fwd_idx(0)),
                  blk(bwd_idx(0)), blk(bwd_idx(0)), blk(bwd_idx(1)), blk(bwd_idx(1)),
                  full(cmats), full(codes)],
        out_specs=[blk(fwd_idx(0)), blk(bwd_idx(0))],
        out_shape=[jax.ShapeDtypeStruct((nt, HGRN_W), F32)] * 2,
        scratch_shapes=[pltpu.VMEM((2, HGRN_W, HGRN_W), F32),
                        pltpu.VMEM((2, HB, HGRN_W), BF16),
                        pltpu.VMEM((2, HB, HGRN_W), BF16),
                        pltpu.VMEM((2, HB, HGRN_W), BF16),
                        pltpu.VMEM((2, HB, HGRN_W), F32),
                        pltpu.VMEM((2, HB, HGRN_W), F32)],
        compiler_params=_params(("arbitrary", "arbitrary")),
        name="hgrn",
    )(hq, hv, lf, kk, hq, hv, lf, kk, cmats, codes)


def _head_stack(q, n_heads):
    lane = lax.broadcasted_iota(jnp.int32, q.shape, 1)
    zero = jnp.zeros_like(q)
    return jnp.concatenate([jnp.where(lane // HEAD_DIM == h, q, zero) for h in range(n_heads)], axis=0)


def _head_unstack(o, n_heads):
    rows = o.shape[0] // n_heads
    lane = lax.broadcasted_iota(jnp.int32, (rows, o.shape[1]), 1)
    acc = jnp.zeros((rows, o.shape[1]), F32)
    for h in range(n_heads):
        acc = acc + jnp.where(lane // HEAD_DIM == h, o[h * rows:(h + 1) * rows], 0.0)
    return acc


def _na_kernel(n_rows, n_lat_steps, q_ref, k_ref, v_ref, kc_ref, vc_ref, bias_ref, o_ref):
    j = pl.program_id(1)
    kc = kc_ref[...]
    vc = vc_ref[...]
    nwin = NA_WIN_ROWS * GRID_W

    def attend(rr, win):
        q4 = _head_stack(q_ref[rr * GRID_W:(rr + 1) * GRID_W, :], NA_HEADS)
        s_ctx = _dot_nt(q4, kc)
        m = jnp.max(s_ctx, axis=1, keepdims=True)
        if win is not None:
            kw, vw, bias = win
            s_win = _dot_nt(q4, kw) + bias
            m = jnp.maximum(m, jnp.max(s_win, axis=1, keepdims=True))
            p_win = jnp.exp(s_win - m)
        p_ctx = jnp.exp(s_ctx - m)
        den = jnp.sum(p_ctx, axis=1, keepdims=True)
        o4 = _dot(p_ctx.astype(BF16), vc)
        if win is not None:
            den = den + jnp.sum(p_win, axis=1, keepdims=True)
            o4 = o4 + _dot(p_win.astype(BF16), vw)
        o = _head_unstack(o4 * (1.0 / den), NA_HEADS)
        o_ref[rr * GRID_W:(rr + 1) * GRID_W, :] = o.astype(BF16)

    @pl.when(j < n_lat_steps)
    def _():
        for rr in range(NA_QROWS):
            r = j * NA_QROWS + rr
            rs = jnp.clip(r - NA_WIN_ROWS // 2, 0, n_rows - NA_WIN_ROWS)
            start = pl.multiple_of(rs * GRID_W, GRID_W)
            kw = k_ref[pl.ds(start, nwin), :]
            vw = v_ref[pl.ds(start, nwin), :]
            attend(rr, (kw, vw, bias_ref[r - rs]))

    @pl.when(j >= n_lat_steps)
    def _():
        for rr in range(NA_QROWS):
            attend(rr, None)


def _na(nqkv, bias, n_b, l_, ctx_len, with_ctx):
    nt = nqkv.shape[0]
    qb = NA_QROWS * GRID_W
    assert ctx_len == qb
    n_lat = l_ // qb
    ctx0 = n_b * n_lat
    n_rows = l_ // GRID_W

    def q_idx(b, j):
        return (jnp.where(j < n_lat, b * n_lat + j, ctx0 + b), 0)

    cblk = (n_b * l_) // ctx_len
    return pl.pallas_call(
        functools.partial(_na_kernel, n_rows, n_lat),
        grid=(n_b, n_lat + (1 if with_ctx else 0)),
        in_specs=[pl.BlockSpec((qb, 256), q_idx),
                  pl.BlockSpec((l_, 256), lambda b, j: (b, 1)),
                  pl.BlockSpec((l_, 256), lambda b, j: (b, 2)),
                  pl.BlockSpec((ctx_len, 256), lambda b, j: (cblk + b, 1)),
                  pl.BlockSpec((ctx_len, 256), lambda b, j: (cblk + b, 2)),
                  pl.BlockSpec(bias.shape, lambda b, j: (0, 0, 0))],
        out_specs=pl.BlockSpec((qb, 256), q_idx),
        out_shape=jax.ShapeDtypeStruct((nt if with_ctx else n_b * l_, 256), BF16),
        compiler_params=_params(("arbitrary", "arbitrary")),
        name="natten",
    )(nqkv, nqkv, nqkv, nqkv, nqkv, bias)


def _swa_kernel(n_blocks, l_, sink_ref, q_ref, kp_ref, kq_ref, kn_ref, vp_ref, vq_ref, vn_ref,
                kc_ref, vc_ref, o_ref):
    j = pl.program_id(1)
    sb = SWA_BLOCK

    def attend(n, band):
        cols = slice(n * MXU_N, (n + 1) * MXU_N)
        q4 = _head_stack(q_ref[:, cols], SWA_GROUP)
        sink = jnp.concatenate([jnp.full((sb, 1), sink_ref[n * SWA_GROUP + g], F32)
                                for g in range(SWA_GROUP)], axis=0)
        s_ctx = _dot_nt(q4, kc_ref[:, cols])
        m = jnp.maximum(jnp.max(s_ctx, axis=1, keepdims=True), sink)
        if band is not None:
            kb = jnp.concatenate([kp_ref[:, cols], kq_ref[:, cols], kn_ref[:, cols]], axis=0)
            vb = jnp.concatenate([vp_ref[:, cols], vq_ref[:, cols], vn_ref[:, cols]], axis=0)
            s_band = jnp.where(band, _dot_nt(q4, kb), NEG)
            m = jnp.maximum(m, jnp.max(s_band, axis=1, keepdims=True))
            p_band = jnp.exp(s_band - m)
        p_ctx = jnp.exp(s_ctx - m)
        den = jnp.sum(p_ctx, axis=1, keepdims=True) + jnp.exp(sink - m)
        o4 = _dot(p_ctx.astype(BF16), vc_ref[:, cols])
        if band is not None:
            den = den + jnp.sum(p_band, axis=1, keepdims=True)
            o4 = o4 + _dot(p_band.astype(BF16), vb)
        o_ref[:, cols] = _head_unstack(o4 * (1.0 / den), SWA_GROUP).astype(BF16)

    @pl.when(j < n_blocks)
    def _():
        u = lax.broadcasted_iota(jnp.int32, (SWA_GROUP * sb, 3 * sb), 0) % sb
        v = lax.broadcasted_iota(jnp.int32, (SWA_GROUP * sb, 3 * sb), 1)
        kpos = (j - 1) * sb + v
        dist = j * sb + u - kpos
        band = (kpos >= 0) & (kpos < l_) & (dist <= SWA_WINDOW) & (dist >= -SWA_WINDOW)
        for n in range(SWA_KV_HEADS):
            attend(n, band)

    @pl.when(j >= n_blocks)
    def _():
        for n in range(SWA_KV_HEADS):
            attend(n, None)


def _swa(sq, kx, vx, sink, n_b, l_, ctx_len, with_ctx):
    nt = sq.shape[0]
    sb = SWA_BLOCK
    nb = l_ // sb
    cper = ctx_len // sb
    ctx0 = n_b * nb

    def q_idx(b, j):
        return (jnp.where(j < nb, b * nb + j, ctx0 + b * cper + (j - nb)), 0)

    def band_idx(off):
        return lambda b, j: (b * nb + jnp.clip(jnp.minimum(j, nb - 1) + off, 0, nb - 1), 0)

    cblk = (n_b * l_) // ctx_len
    blk = lambda idx: pl.BlockSpec((sb, 512), idx)
    cspec = pl.BlockSpec((ctx_len, 512), lambda b, j: (cblk + b, 0))
    return pl.pallas_call(
        functools.partial(_swa_kernel, nb, l_),
        grid=(n_b, nb + (cper if with_ctx else 0)),
        in_specs=[pl.BlockSpec(memory_space=pltpu.SMEM),
                  blk(q_idx), blk(band_idx(-1)), blk(band_idx(0)), blk(band_idx(1)),
                  blk(band_idx(-1)), blk(band_idx(0)), blk(band_idx(1)), cspec, cspec],
        out_specs=blk(q_idx),
        out_shape=jax.ShapeDtypeStruct((nt if with_ctx else n_b * l_, 512), BF16),
        compiler_params=_params(("arbitrary", "arbitrary")),
        name="swa",
    )(sink, sq, kx, kx, kx, vx, vx, vx, kx, vx)


def _layer_norm(r, g, b):
    mu = jnp.mean(r, axis=-1, keepdims=True)
    rc = r - mu
    var = jnp.mean(rc * rc, axis=-1, keepdims=True)
    return rc * lax.rsqrt(var + LN_EPS) * g + b


def _route(logits_t, bias_col):
    epg = N_EXPERTS // N_GROUPS
    s = _sigmoid(logits_t)
    sel = s + bias_col
    sel_r = [sel[e:e + 1, :] for e in range(N_EXPERTS)]
    s_r = [s[e:e + 1, :] for e in range(N_EXPERTS)]
    grp = []
    for g in range(N_GROUPS):
        a = sel_r[g * epg:(g + 1) * epg]
        m1 = functools.reduce(jnp.maximum, a)
        m2 = functools.reduce(jnp.maximum,
                              [jnp.minimum(a[i], a[k]) for i in range(epg) for k in range(i + 1, epg)])
        grp.append(m1 + m2)
    m_r, w_r = [], []
    for g in range(N_GROUPS):
        best = None
        for k in range(N_GROUPS):
            if k == g:
                continue
            c = (grp[g] > grp[k]) if k < g else (grp[g] >= grp[k])
            best = c if best is None else (best & c)
        for e in range(g * epg, (g + 1) * epg):
            rank = jnp.zeros_like(sel_r[e])
            for k in range(g * epg, (g + 1) * epg):
                if k == e:
                    continue
                ahead = (sel_r[k] >= sel_r[e]) if k < e else (sel_r[k] > sel_r[e])
                rank = rank + jnp.where(ahead, 1.0, 0.0)
            chosen = jnp.where(best & (rank < 1.5), 1.0, 0.0)
            m_r.append(chosen)
            w_r.append(chosen * s_r[e])
    inv = 1.0 / functools.reduce(lambda a, b: a + b, w_r)
    return m_r, [w * inv for w in w_r]


def _pack_bf16_pairs(a, b):
    ua = pltpu.bitcast(a.astype(BF16).astype(F32), jnp.uint32)
    ub = pltpu.bitcast(b.astype(BF16).astype(F32), jnp.uint32)
    return (ua >> 16) | ub


def _unpack_bf16_pairs(u):
    return (pltpu.bitcast(u << 16, F32), pltpu.bitcast(u & jnp.uint32(0xFFFF0000), F32))


def _outproj_kernel(n_lat, of_ref, ob_ref, hg_ref, yb_ref, yc_ref, xa_ref, xb_ref, mod_ref, wo_ref, pv_ref,
                    wrh_ref, wrl_ref, rb_ref, before_ref, x1_ref, h2u_ref, route_ref, gt_ref, cnt_ref):
    @pl.when(pl.program_id(0) == 0)
    def _():
        cnt_ref[...] = jnp.zeros(cnt_ref.shape, F32)

    row = lax.broadcasted_iota(jnp.int32, (HGRN_W, HGRN_W), 0)
    col = lax.broadcasted_iota(jnp.int32, (HGRN_W, HGRN_W), 1)
    head_ones = jnp.where((row // HEAD_DIM) == (col // HEAD_DIM), 1.0, 0.0).astype(BF16)
    pv = pv_ref[...]
    o = of_ref[...] + ob_ref[...]
    hi, lo = _split_bf16(o * o)
    ms = (_dot(hi, head_ones) + _dot(lo, head_ones)) * (1.0 / HEAD_DIM)
    gate = hg_ref[...].astype(F32)
    ya = o * lax.rsqrt(ms + RMS_EPS) * pv[2:3, 0:HGRN_W] * (gate * _sigmoid(gate))
    y = (_dot(ya.astype(BF16), wo_ref[0:256, :]) + _dot(yb_ref[...], wo_ref[256:512, :])
         + _dot(yc_ref[...], wo_ref[512:1024, :]))
    mod = mod_ref[0]
    x = jnp.where(pl.program_id(0) < n_lat, xa_ref[...], xb_ref[...])
    x1 = _layer_norm(ALPHA * x + mod[2:3] * y, pv[0:1], pv[1:2])
    x1_ref[...] = x1
    h2 = x1 * (1.0 + mod[4:5]) + mod[3:4]
    hh, hl = _split_bf16(h2)
    half = h2.shape[1] // 2
    h2u_ref[...] = _pack_bf16_pairs(h2[:, 0:half], h2[:, half:2 * half])
    wrh = wrh_ref[...]
    logits_t = _dot_nt(wrh, hh) + _dot_nt(wrh, hl) + _dot_nt(wrl_ref[...], hh)
    m_r, w_r = _route(logits_t, rb_ref[...])
    m_t = jnp.concatenate(m_r, axis=0)
    rank_t = _dot(m_t.astype(BF16), before_ref[...]) + cnt_ref[:, 0:1]
    cnt_ref[...] = cnt_ref[...] + jnp.sum(m_t, axis=1, keepdims=True)
    seen = jnp.zeros_like(m_r[0])
    e1 = e2 = r1 = r2 = g1 = g2 = jnp.zeros_like(m_r[0])
    for e in range(N_EXPERTS):
        is1 = m_r[e] * (1.0 - seen)
        is2 = m_r[e] * seen
        rk = rank_t[e:e + 1, :]
        e1, e2 = e1 + e * is1, e2 + e * is2
        r1, r2 = r1 + rk * is1, r2 + rk * is2
        g1, g2 = g1 + w_r[e] * is1, g2 + w_r[e] * is2
        seen = seen + is1
    zero = jnp.zeros_like(e1)
    route_ref[...] = jnp.concatenate([e1, e2, r1, r2, g1, g2, zero, zero], axis=0)
    pad = jnp.zeros((LANES - 2, g1.shape[1]), F32)
    gt_ref[...] = jnp.concatenate([g1, g2, pad], axis=0).T


def _outproj(o_f, o_b, hg, yb, yc, xa, xb, ctx_blk, mod, wo, pv, wrh, wrl, rb, n_b, l_, n_tiles):
    d = xa.shape[1]
    tpb = l_ // TM
    n_lat = n_b * tpb
    before = np.triu(np.ones((TM, TM), np.float32), 1).astype(jnp.bfloat16)

    def mod_idx(j):
        return (jnp.minimum(j // tpb, n_b), 0, 0)

    def rows(width):
        return pl.BlockSpec((TM, width), lambda j: (j, 0))

    full = lambda a: pl.BlockSpec(a.shape, lambda j: (0,) * a.ndim)
    nr = n_tiles * TM
    return pl.pallas_call(
        functools.partial(_outproj_kernel, n_lat),
        grid=(n_tiles,),
        in_specs=[rows(256), rows(256), rows(256), rows(256), rows(512),
                  pl.BlockSpec((TM, d), lambda j: (jnp.minimum(j, n_lat - 1), 0)),
                  pl.BlockSpec((TM, d), lambda j: (ctx_blk, 0)),
                  pl.BlockSpec((1, 6, d), mod_idx), full(wo), full(pv), full(wrh), full(wrl), full(rb),
                  full(before)],
        out_specs=[rows(d), rows(d // 2), pl.BlockSpec((8, TM), lambda j: (0, j)), rows(LANES),
                   pl.BlockSpec((N_EXPERTS, LANES), lambda j: (0, 0))],
        out_shape=[jax.ShapeDtypeStruct((nr, d), F32), jax.ShapeDtypeStruct((nr, d // 2), jnp.uint32),
                   jax.ShapeDtypeStruct((8, nr), F32), jax.ShapeDtypeStruct((nr, LANES), F32),
                   jax.ShapeDtypeStruct((N_EXPERTS, LANES), F32)],
        compiler_params=_params(("arbitrary",)),
        name="outproj",
    )(o_f, o_b, hg, yb, yc, xa, xb, mod, wo, pv, wrh, wrl, rb, before)


def _sc_split(rows_per_worker):
    best = None
    for chunk in (64, 48, 32):
        for nbuf in range(SC_MAX_ROWS // chunk, 0, -1):
            if rows_per_worker % (chunk * nbuf) == 0 and (best is None or chunk * nbuf > best[0] * best[1]):
                best = (chunk, nbuf)
    return best


def _sc_workers():
    info = plsc.get_sparse_core_info()
    return info.num_cores, info.num_cores * info.num_subcores


def _sc_scatter_rows(table, pos, n_slots):
    n_tok, width = table.shape
    n_cores, n_workers = _sc_workers()
    per_w = n_tok // n_workers
    assert n_tok % n_workers == 0
    chunk, nbuf = _sc_split(per_w)
    n_chunks = per_w // chunk
    mesh = plsc.VectorSubcoreMesh(core_axis_name="c", subcore_axis_name="s")

    @functools.partial(
        pl.kernel, mesh=mesh,
        out_type=jax.ShapeDtypeStruct((n_slots, width), table.dtype),
        scratch_types=[pltpu.VMEM((2 * n_chunks, chunk), jnp.int32),
                       pltpu.VMEM((nbuf, chunk, width), table.dtype),
                       pltpu.SemaphoreType.DMA((nbuf,)),
                       pltpu.SemaphoreType.DMA((nbuf,))],
    )
    def scatter(table_hbm, pos_hbm, out_hbm, idx_v, rows_v, rsem, wsem):
        wid = lax.axis_index("s") * n_cores + lax.axis_index("c")
        base = wid * per_w
        pltpu.sync_copy(pos_hbm.at[wid], idx_v)

        @pl.loop(0, n_chunks // nbuf)
        def _(g):
            c0 = g * nbuf
            gets = [pltpu.async_copy(table_hbm.at[pl.ds(base + (c0 + b) * chunk, chunk)], rows_v.at[b],
                                     rsem.at[b]) for b in range(nbuf)]
            puts = []
            for b in range(nbuf):
                gets[b].wait()
                for k in range(2):
                    puts.append(pltpu.async_copy(rows_v.at[b], out_hbm.at[idx_v.at[k * n_chunks + c0 + b]],
                                                 wsem.at[b]))
            for put in puts:
                put.wait()

    pos_w = pos.reshape(2, n_workers, n_chunks, chunk).transpose(1, 0, 2, 3).reshape(n_workers, 2 * n_chunks, chunk)
    return scatter(table, pos_w)


def _sc_gather_rows(table, idx):
    n_out = idx.shape[0]
    width = table.shape[1]
    n_cores, n_workers = _sc_workers()
    per_w = n_out // n_workers
    assert n_out % n_workers == 0
    chunk, nbuf = _sc_split(per_w)
    mesh = plsc.VectorSubcoreMesh(core_axis_name="c", subcore_axis_name="s")

    @functools.partial(
        pl.kernel, mesh=mesh,
        out_type=jax.ShapeDtypeStruct((n_out, width), table.dtype),
        scratch_types=[pltpu.VMEM((per_w,), jnp.int32),
                       pltpu.VMEM((nbuf, chunk, width), table.dtype),
                       pltpu.SemaphoreType.DMA((nbuf,)),
                       pltpu.SemaphoreType.DMA((nbuf,))],
    )
    def gather(table_hbm, idx_hbm, out_hbm, idx_v, rows_v, gsem, wsem):
        wid = lax.axis_index("s") * n_cores + lax.axis_index("c")
        base = wid * per_w
        pltpu.sync_copy(idx_hbm.at[pl.ds(base, per_w)], idx_v)

        @pl.loop(0, per_w // (chunk * nbuf))
        def _(g):
            off = g * (chunk * nbuf)
            gets = [pltpu.async_copy(table_hbm.at[idx_v.at[pl.ds(off + b * chunk, chunk)]], rows_v.at[b],
                                     gsem.at[b]) for b in range(nbuf)]
            puts = []
            for b in range(nbuf):
                gets[b].wait()
                puts.append(pltpu.async_copy(rows_v.at[b], out_hbm.at[pl.ds(base + off + b * chunk, chunk)],
                                             wsem.at[b]))
            for put in puts:
                put.wait()

    return gather(table, idx)


def _gmm_kernel(te_ref, nv_ref, nu_ref, xs_ref, wg_ref, wu_ref, wd_ref, ys_ref, wgb_ref, wub_ref, wdb_ref):
    i = pl.program_id(0)
    used = i < nu_ref[0]
    fresh = jnp.logical_or(i == 0, te_ref[i] != te_ref[jnp.maximum(i - 1, 0)])

    @pl.when(jnp.logical_and(used, fresh))
    def _():
        wgb_ref[...] = wg_ref[0, 0].astype(BF16)
        wub_ref[...] = wu_ref[0, 0].astype(BF16)
        wdb_ref[...] = wd_ref[0, 0].astype(BF16)

    @pl.when(used)
    def _():
        rowid = lax.broadcasted_iota(jnp.int32, xs_ref.shape, 0)
        lo, hi = _unpack_bf16_pairs(jnp.where(rowid < nv_ref[i], xs_ref[...], jnp.uint32(0)))
        lo, hi = lo.astype(BF16), hi.astype(BF16)
        half = lo.shape[1]
        a = _dot(lo, wgb_ref[0:half, :]) + _dot(hi, wgb_ref[half:2 * half, :])
        u = _dot(lo, wub_ref[0:half, :]) + _dot(hi, wub_ref[half:2 * half, :])
        y = _dot((a * _sigmoid(a) * u).astype(BF16), wdb_ref[...])
        ys_ref[...] = _pack_bf16_pairs(y[:, 0:half], y[:, half:2 * half])

    @pl.when(jnp.logical_not(used))
    def _():
        ys_ref[...] = jnp.zeros(ys_ref.shape, jnp.uint32)


def _gmm(tile_expert, tile_valid, n_used, xs, wg, wu, wd, layer):
    p_rows, half = xs.shape
    d = 2 * half
    wspec = lambda r, c: pl.BlockSpec((1, 1, r, c), lambda i, te, nv, nu: (layer, te[i], 0, 0))
    return pl.pallas_call(
        _gmm_kernel,
        grid_spec=pltpu.PrefetchScalarGridSpec(
            num_scalar_prefetch=3,
            grid=(p_rows // TM,),
            in_specs=[pl.BlockSpec((TM, half), lambda i, te, nv, nu: (i, 0)),
                      wspec(d, D_EXPERT), wspec(d, D_EXPERT), wspec(D_EXPERT, d)],
            out_specs=pl.BlockSpec((TM, half), lambda i, te, nv, nu: (i, 0)),
            scratch_shapes=[pltpu.VMEM((d, D_EXPERT), BF16), pltpu.VMEM((d, D_EXPERT), BF16),
                            pltpu.VMEM((D_EXPERT, d), BF16)]),
        out_shape=jax.ShapeDtypeStruct((p_rows, half), jnp.uint32),
        compiler_params=_params(("arbitrary",)),
        name="moe_experts",
    )(tile_expert, tile_valid, n_used, xs, wg, wu, wd)


def _moefin_kernel(o1_ref, o2_ref, gt_ref, x_ref, mod_ref, pv_ref, out_ref):
    gt = gt_ref[...]
    g1, g2 = gt[:, 0:1], gt[:, 1:2]
    a1, b1 = _unpack_bf16_pairs(o1_ref[...])
    a2, b2 = _unpack_bf16_pairs(o2_ref[...])
    f = jnp.concatenate([g1 * a1 + g2 * a2, g1 * b1 + g2 * b2], axis=1)
    mod = mod_ref[0]
    pv = pv_ref[...]
    out_ref[...] = _layer_norm(ALPHA * x_ref[...] + mod[5:6] * f, pv[0:1], pv[1:2])


def _moefin(o12, gt, x1, mod, pv, n_b, l_, n_tiles):
    d = x1.shape[1]
    tpb = l_ // TM

    def mod_idx(j):
        return (jnp.minimum(j // tpb, n_b), 0, 0)

    return pl.pallas_call(
        _moefin_kernel,
        grid=(n_tiles,),
        in_specs=[pl.BlockSpec((TM, d // 2), lambda j: (j, 0)),
                  pl.BlockSpec((TM, d // 2), lambda j: (j + n_tiles, 0)),
                  pl.BlockSpec((TM, LANES), lambda j: (j, 0)),
                  pl.BlockSpec((TM, d), lambda j: (j, 0)),
                  pl.BlockSpec((1, 6, d), mod_idx),
                  pl.BlockSpec(pv.shape, lambda j: (0, 0))],
        out_specs=pl.BlockSpec((TM, d), lambda j: (j, 0)),
        out_shape=jax.ShapeDtypeStruct((n_tiles * TM, d), F32),
        compiler_params=_params(("arbitrary",)),
        name="moe_combine",
    )(o12, o12, gt, x1, mod, pv)


def _moe_routed(h2u, route, gt, counts, wg, wu, wd, layer, x1, mod, pv, n_b, l_, n_tiles):
    n_tok = n_tiles * TM
    n_slots = -(-(2 * n_tok + N_EXPERTS * TM) // SC_ROW_QUANTUM) * SC_ROW_QUANTUM
    cnt = counts[:, 0].astype(jnp.int32)
    padded = ((cnt + TM - 1) // TM) * TM
    upto = jnp.arange(N_EXPERTS)[None, :] <= jnp.arange(N_EXPERTS)[:, None]
    ends = jnp.sum(jnp.where(upto, padded[None, :], 0), axis=1)
    offs = ends - padded
    e12 = route[0:2].astype(jnp.int32)
    r12 = route[2:4].astype(jnp.int32)
    onehot = e12[:, :, None] == jnp.arange(N_EXPERTS, dtype=jnp.int32)
    pos = r12 + jnp.sum(jnp.where(onehot, offs, 0), axis=-1)
    tile_start = jnp.arange(n_slots // TM, dtype=jnp.int32) * TM
    tile_expert = jnp.minimum(jnp.sum(tile_start[:, None] >= ends[None, :], axis=1), N_EXPERTS - 1)
    pick = tile_expert[:, None] == jnp.arange(N_EXPERTS, dtype=jnp.int32)
    tile_valid = jnp.clip(jnp.sum(jnp.where(pick, (offs + cnt)[None, :], 0), axis=1) - tile_start, 0, TM)
    n_used = (ends[N_EXPERTS - 1] // TM).reshape(1)
    xs = _sc_scatter_rows(h2u, pos, n_slots)
    ys = _gmm(tile_expert.astype(jnp.int32), tile_valid.astype(jnp.int32), n_used.astype(jnp.int32),
              xs, wg, wu, wd, layer)
    o12 = _sc_gather_rows(ys, pos.reshape(2 * n_tok))
    return _moefin(o12, gt, x1, mod, pv, n_b, l_, n_tiles)


def _relayout_w_in(w):
    sk = w[:, 2560:2688]
    sv = w[:, 2688:2816]
    rep = lambda a: jnp.concatenate(
        [jnp.tile(a[:, n * HEAD_DIM:(n + 1) * HEAD_DIM], (1, SWA_GROUP)) for n in range(SWA_KV_HEADS)], axis=1)
    return jnp.concatenate([w[:, :2560], rep(sk), rep(sv)], axis=1).astype(BF16)


def _rope_tables(l_):
    pos = np.arange(l_)
    nf = HEAD_DIM // 4
    inv = ROPE_BASE ** (-np.arange(nf, dtype=np.float64) / nf)
    ar = (pos // GRID_W)[:, None] * inv
    ac = (pos % GRID_W)[:, None] * inv
    cos = np.concatenate([np.cos(ar), np.cos(ar), np.cos(ac), np.cos(ac)], axis=1)
    sin = np.concatenate([-np.sin(ar), np.sin(ar), -np.sin(ac), np.sin(ac)], axis=1)
    cos = np.concatenate([np.tile(cos, (1, 2)), np.ones((TM, LANES))], axis=0)
    sin = np.concatenate([np.tile(sin, (1, 2)), np.zeros((TM, LANES))], axis=0)
    return cos.astype(np.float32), sin.astype(np.float32)


def _na_bias(rpb):
    u = np.arange(GRID_W)[:, None]
    v = np.arange(GRID_W)[None, :]
    cs = np.clip(u - NA_WIN_COLS // 2, 0, GRID_W - NA_WIN_COLS)
    colmask = (v >= cs) & (v < cs + NA_WIN_COLS)
    coff = np.clip(v - u + NA_WIN_COLS - 1, 0, 2 * NA_WIN_COLS - 2)
    n_off = 2 * NA_WIN_COLS - 1
    pick = (coff[None] == np.arange(n_off)[:, None, None]).astype(np.float32)
    band = jnp.einsum('hab,buv->ahuv', rpb.astype(F32), pick, precision=lax.Precision.HIGHEST)
    band = jnp.where(colmask[None, None], band, NEG)
    tabs = []
    for delta in range(NA_WIN_ROWS):
        rows = band[NA_WIN_ROWS - 1 - delta:2 * NA_WIN_ROWS - 1 - delta]
        tabs.append(rows.transpose(1, 2, 0, 3).reshape(NA_HEADS * GRID_W, NA_WIN_ROWS * GRID_W))
    return jnp.stack(tabs, axis=0)


def kernel(x, c, ctx, c_ctx, w_ada, b_ada, w_in, lb_logits, hgrn_norm, na_rpb, swa_sink, w_out,
           ln1_g, ln1_b, ln2_g, ln2_b, w_router, router_bias, w_gate, w_up, w_down):
    n_b, l_, d = x.shape
    ctx_len = ctx.shape[1]
    depth = w_ada.shape[0]
    assert n_b * ctx_len == TM and l_ % TM == 0 and n_b + 1 <= N_COND
    n_lat_tiles = (n_b * l_) // TM

    p_lb = jax.nn.softmax(lb_logits.astype(F32), axis=0)
    gp_all, run = [], jnp.zeros_like(p_lb[0])
    for l in range(depth):
        run = run + p_lb[l]
        lb = jnp.maximum(run - p_lb[0], LB_MIN).reshape(2 * HGRN_W)
        gp_l = jnp.stack([jnp.log(lb), jnp.log1p(-lb), 1.0 - lb] + [jnp.zeros_like(lb)] * 5, axis=0)
        gp_all.append(gp_l)

    cond = jnp.concatenate([c, c_ctx[None, :], jnp.zeros((N_COND - n_b - 1, d), F32)], axis=0)
    mod_all = _ada(cond.T, w_ada, b_ada, n_b + 1).reshape(depth, N_COND, 6, d)

    cos_t, sin_t = _rope_tables(l_)
    wr_t = w_router.T
    wrh = wr_t.astype(BF16)
    wrl = (wr_t - wrh.astype(F32)).astype(BF16)
    rb = router_bias.astype(F32)[:, None]

    xa, xb, ctx_blk = x.reshape(n_b * l_, d), ctx.reshape(n_b * ctx_len, d), 0
    for l in range(depth):
        last = l == depth - 1
        mod = mod_all[l]
        hq, hv, hg, lf, kk, nqkv, sq, kx, vx = _inproj(
            xa, xb, ctx_blk, mod, _relayout_w_in(w_in[l]), gp_all[l], cos_t, sin_t, n_b, l_)
        o_f, o_b = _hgrn(hq, hv, lf, kk, n_b, l_, ctx_len)
        yb = _na(nqkv, _na_bias(na_rpb[l]), n_b, l_, ctx_len, not last)
        yc = _swa(sq, kx, vx, swa_sink[l].astype(F32), n_b, l_, ctx_len, not last)
        n_tiles = n_lat_tiles if last else n_lat_tiles + 1
        pv1 = jnp.stack([ln1_g[l], ln1_b[l], jnp.tile(hgrn_norm[l], d // HEAD_DIM)]
                        + [jnp.zeros((d,), F32)] * 5, axis=0)
        x1, h2u, route, gt, counts = _outproj(o_f, o_b, hg, yb, yc, xa, xb, ctx_blk, mod,
                                              w_out[l].astype(BF16), pv1, wrh, wrl, rb, n_b, l_, n_tiles)
        pv2 = jnp.stack([ln2_g[l], ln2_b[l]] + [jnp.zeros((d,), F32)] * 6, axis=0)
        xa = _moe_routed(h2u, route, gt, counts, w_gate, w_up, w_down, l, x1, mod, pv2, n_b, l_, n_tiles)
        xb, ctx_blk = xa, n_lat_tiles
    return xa[:n_b * l_].reshape(n_b, l_, d)
```

```python
import functools

import numpy as np
import jax
import jax.numpy as jnp
from jax import lax
from jax.experimental import pallas as pl
from jax.experimental.pallas import tpu as pltpu
from jax.experimental.pallas import tpu_sc as plsc

F32 = jnp.float32
BF16 = jnp.bfloat16

D_MODEL = 1024
GRID_W = 64
HEAD_DIM = 64
HGRN_W = 256
NA_HEADS = 4
NA_WIN_ROWS = 8
NA_WIN_COLS = 16
SWA_Q_HEADS = 8
SWA_KV_HEADS = 2
SWA_GROUP = 4
SWA_WINDOW = 128
SWA_BLOCK = 128
ROPE_BASE = 10000.0
N_EXPERTS = 16
N_GROUPS = 4
D_EXPERT = 512
LN_EPS = 1e-5
RMS_EPS = 1e-6
NEG = -1e30
LB_MIN = 1e-6
DEPTH = 2
ALPHA = (2.0 * DEPTH) ** 0.25

LANES = 128
MXU_N = 256
TM = 512
HB = 256
HC = 16
NA_QROWS = 4
VMEM_LIMIT = 56 * 1024 * 1024

SC_MAX_ROWS = 192
SC_ROW_QUANTUM = 2048

N_COND = 8


def _dot(a, b):
    return jnp.dot(a, b, preferred_element_type=F32)


def _dot_nt(a, b):
    return lax.dot_general(a, b, (((1,), (1,)), ((), ())), preferred_element_type=F32)


def _dot_tn(a, b):
    return lax.dot_general(a, b, (((0,), (0,)), ((), ())), preferred_element_type=F32)


def _sigmoid(x):
    return 1.0 / (1.0 + jnp.exp(-x))


def _split_bf16(x):
    hi = x.astype(BF16)
    lo = (x - hi.astype(F32)).astype(BF16)
    return hi, lo


def _params(sem):
    return pltpu.CompilerParams(dimension_semantics=sem, vmem_limit_bytes=VMEM_LIMIT)


def _ada_kernel(n_rows, condt_ref, w_ref, b_ref, o_ref):
    c = condt_ref[...]
    s = c * _sigmoid(c)
    w = w_ref[0]
    rows = [jnp.sum(w * s[:, r:r + 1], axis=0, keepdims=True) for r in range(n_rows)]
    rows.append(jnp.zeros((N_COND - n_rows, w.shape[1]), F32))
    o_ref[0] = jnp.concatenate(rows, axis=0) + b_ref[0]


def _ada(cond_t, w_ada, b_ada, n_rows):
    depth, d, n6 = w_ada.shape
    tn = 1024
    return pl.pallas_call(
        functools.partial(_ada_kernel, n_rows),
        grid=(depth, n6 // tn),
        in_specs=[pl.BlockSpec((d, N_COND), lambda l, n: (0, 0)),
                  pl.BlockSpec((1, d, tn), lambda l, n: (l, 0, n)),
                  pl.BlockSpec((1, 1, tn), lambda l, n: (l, 0, n))],
        out_specs=pl.BlockSpec((1, N_COND, tn), lambda l, n: (l, 0, n)),
        out_shape=jax.ShapeDtypeStruct((depth, N_COND, n6), F32),
        compiler_params=_params(("arbitrary", "arbitrary")),
        name="ada",
    )(cond_t, w_ada, b_ada.reshape(depth, 1, n6))


def _rope(z, cos, sin, first):
    sw = jnp.where(first, pltpu.roll(z, LANES - 16, axis=1), pltpu.roll(z, 16, axis=1))
    return z * cos + sw * sin


def _inproj_kernel(n_lat, xa_ref, xb_ref, mod_ref, w_ref, gp_ref, cos_ref, sin_ref,
                   hq_ref, hv_ref, hg_ref, lf_ref, kk_ref, nqkv_ref, sq_ref, kx_ref, vx_ref):
    mod = mod_ref[0, 0]
    x = jnp.where(pl.program_id(0) < n_lat, xa_ref[...], xb_ref[...])
    h = (x * (1.0 + mod[1:2]) + mod[0:1]).astype(BF16)
    gp = gp_ref[0]
    cos = cos_ref[...]
    sin = sin_ref[...]
    lane = lax.broadcasted_iota(jnp.int32, cos.shape, 1)
    first = (lane & 16) == 0
    scale = HEAD_DIM ** -0.5

    def chunk(c):
        return _dot(h, w_ref[0, :, c * MXU_N:(c + 1) * MXU_N])

    def per_query_head(z2):
        swapped = pltpu.roll(z2, HEAD_DIM, axis=1)
        low = lane < HEAD_DIM
        h0 = jnp.where(low, z2, swapped)
        h1 = jnp.where(low, swapped, z2)
        return jnp.concatenate([h0, h0], axis=1), jnp.concatenate([h1, h1], axis=1)

    def gates(z, d):
        cols = slice(d * MXU_N, (d + 1) * MXU_N)
        a = gp[0:1, cols]
        b = gp[1:2, cols] + (jnp.minimum(z, 0.0) - jnp.log1p(jnp.exp(-jnp.abs(z))))
        logf = jnp.maximum(a, b) + jnp.log1p(jnp.exp(-jnp.abs(a - b)))
        k = gp[2:3, cols] * (1.0 / (1.0 + jnp.exp(z)))
        return logf, k

    def rope2(z):
        return jnp.concatenate([_rope(z[:, 0:LANES], cos, sin, first),
                                _rope(z[:, LANES:2 * LANES], cos, sin, first)], axis=1)

    hq_ref[...] = chunk(0).astype(BF16)
    for d in range(2):
        logf, k = gates(chunk(1 + d), d)
        lf_ref[:, d * MXU_N:(d + 1) * MXU_N] = logf
        kk_ref[:, d * MXU_N:(d + 1) * MXU_N] = k.astype(BF16)
    hv_ref[...] = chunk(3).astype(BF16)
    hg_ref[...] = chunk(4).astype(BF16)
    nqkv_ref[:, 0:MXU_N] = (chunk(5) * scale).astype(BF16)
    nqkv_ref[:, MXU_N:2 * MXU_N] = chunk(6).astype(BF16)
    nqkv_ref[:, 2 * MXU_N:3 * MXU_N] = chunk(7).astype(BF16)
    for d in range(2):
        sq_ref[:, d * MXU_N:(d + 1) * MXU_N] = (rope2(chunk(8 + d)) * scale).astype(BF16)
    zkv = chunk(10)
    k0, k1 = per_query_head(_rope(zkv[:, 0:LANES], cos, sin, first))
    v0, v1 = per_query_head(zkv[:, LANES:2 * LANES])
    kx_ref[:, 0:MXU_N] = k0.astype(BF16)
    kx_ref[:, MXU_N:2 * MXU_N] = k1.astype(BF16)
    vx_ref[:, 0:MXU_N] = v0.astype(BF16)
    vx_ref[:, MXU_N:2 * MXU_N] = v1.astype(BF16)


def _inproj(xa, xb, ctx_blk, mod, w, gp, layer, cos_t, sin_t, n_b, l_):
    d = xa.shape[1]
    tpb = l_ // TM
    n_lat = n_b * tpb
    nt = (n_lat + 1) * TM

    def mod_idx(j):
        return (layer, jnp.minimum(j // tpb, n_b), 0, 0)

    def rope_idx(j):
        return (jnp.where(j < n_lat, j % tpb, tpb), 0)

    def rows(width):
        return pl.BlockSpec((TM, width), lambda j: (j, 0))

    widths = [(HGRN_W, BF16), (HGRN_W, BF16), (HGRN_W, BF16), (2 * HGRN_W, F32), (2 * HGRN_W, BF16),
              (3 * 256, BF16), (512, BF16), (512, BF16), (512, BF16)]
    return pl.pallas_call(
        functools.partial(_inproj_kernel, n_lat),
        grid=(nt // TM,),
        in_specs=[pl.BlockSpec((TM, d), lambda j: (jnp.minimum(j, n_lat - 1), 0)),
                  pl.BlockSpec((TM, d), lambda j: (ctx_blk, 0)),
                  pl.BlockSpec((1, 1, 6, d), mod_idx),
                  pl.BlockSpec((1,) + w.shape[1:], lambda j: (layer, 0, 0)),
                  pl.BlockSpec((1,) + gp.shape[1:], lambda j: (layer, 0, 0)),
                  pl.BlockSpec((TM, LANES), rope_idx),
                  pl.BlockSpec((TM, LANES), rope_idx)],
        out_specs=[rows(wd) for wd, _ in widths],
        out_shape=[jax.ShapeDtypeStruct((nt, wd), dt) for wd, dt in widths],
        compiler_params=_params(("arbitrary",)),
        name="inproj",
    )(xa, xb, mod, w, gp, cos_t, sin_t)


def _hgrn_consts():
    t = np.arange(HB)[:, None]
    u = np.arange(HB)[None, :]
    same = (t // HC) == (u // HC)
    tr, ur = t % HC, u % HC
    half = HC // 2
    mats, codes = [], []
    for fwd in (True, False):
        if fwd:
            incl = ur <= tr
            mid = ur <= (tr // half) * half + half // 2 - 1
            edge = ur <= half - 1
            code = np.where((t // half == u // half) & (u <= t), 1, np.where(same & (tr >= half) & (ur < half), 2, 0))
        else:
            incl = ur >= tr
            mid = ur >= (tr // half) * half + half // 2
            edge = ur >= half
            code = np.where((t // half == u // half) & (u >= t), 1, np.where(same & (tr < half) & (ur >= half), 2, 0))
        mats.append(np.concatenate([same & incl, same, same & mid, same & edge], axis=0))
        codes.append(code)
    return (np.stack(mats).astype(np.float32).astype(jnp.bfloat16), np.stack(codes).astype(np.float32))


def _hgrn_kernel(qf_ref, vf_ref, lff_ref, kf_ref, qb_ref, vb_ref, lfb_ref, kb_ref, cm_ref, code_ref,
                 of_ref, ob_ref, st_ref, qd_ref, kd_ref, vv_ref, gt_ref, it_ref):
    nch = HB // HC

    @pl.when(pl.program_id(1) == 0)
    def _():
        st_ref[...] = jnp.zeros(st_ref.shape, F32)

    row = lax.broadcasted_iota(jnp.int32, (HGRN_W, HGRN_W), 0)
    col = lax.broadcasted_iota(jnp.int32, (HGRN_W, HGRN_W), 1)
    same_head = (row // HEAD_DIM) == (col // HEAD_DIM)
    n_heads = HGRN_W // HEAD_DIM

    def prep(d, q_ref, v_ref, lf_ref, k_ref):
        hi, lo = _split_bf16(lf_ref[...])
        sums = _dot(cm_ref[d], hi) + _dot(cm_ref[d], lo)
        cum, tot = sums[0:HB], sums[HB:2 * HB]
        mid, edge = sums[2 * HB:3 * HB], sums[3 * HB:4 * HB]
        q = q_ref[...].astype(F32)
        k = k_ref[...].astype(F32)
        v = v_ref[...]
        qd_ref[d] = (q * jnp.exp(cum)).astype(BF16)
        kd_ref[d] = (k * jnp.exp(tot - cum)).astype(BF16)
        vv_ref[d] = v
        gt_ref[d] = jnp.exp(tot)
        q1 = (q * jnp.exp(cum - mid)).astype(BF16)
        k1 = (k * jnp.exp(mid - cum)).astype(BF16)
        q2 = (q * jnp.exp(jnp.minimum(cum - edge, 0.0))).astype(BF16)
        k2 = (k * jnp.exp(jnp.minimum(edge - cum, 0.0))).astype(BF16)
        s1 = _dot_nt(_head_stack(q1, n_heads), k1)
        s2 = _dot_nt(_head_stack(q2, n_heads), k2)
        code = jnp.concatenate([code_ref[d]] * n_heads, axis=0)
        p = jnp.where(code == 1.0, s1, jnp.where(code == 2.0, s2, 0.0))
        return _head_unstack(_dot(p.astype(BF16), v), n_heads)

    od_f = prep(0, qf_ref, vf_ref, lff_ref, kf_ref)
    od_b = prep(1, qb_ref, vb_ref, lfb_ref, kb_ref)

    for i in range(nch):
        for d, c in ((0, i), (1, nch - 1 - i)):
            rows = slice(c * HC, (c + 1) * HC)
            st = st_ref[d]
            it_ref[d, rows, :] = _dot_nt(qd_ref[d, rows, :], st.astype(BF16))
            upd = _dot_tn(vv_ref[d, rows, :], kd_ref[d, rows, :])
            st_ref[d] = st * gt_ref[d, c * HC:c * HC + 1, :] + jnp.where(same_head, upd, 0.0)

    of_ref[...] = it_ref[0] + od_f
    ob_ref[...] = it_ref[1] + od_b


def _hgrn(hq, hv, lf, kk, n_b, l_, ctx_len):
    nt = hq.shape[0]
    assert ctx_len == HB
    nlb = l_ // HB
    ctx0 = n_b * nlb
    cmats, codes = _hgrn_consts()

    def fwd_idx(col):
        return lambda b, s: (jnp.where(s == 0, ctx0 + b, b * nlb + s - 1), col)

    def bwd_idx(col):
        return lambda b, s: (jnp.where(s == 0, ctx0 + b, b * nlb + nlb - s), col)

    def blk(idx):
        return pl.BlockSpec((HB, HGRN_W), idx)

    full = lambda a: pl.BlockSpec(a.shape, lambda b, s: (0,) * a.ndim)
    return pl.pallas_call(
        _hgrn_kernel,
        grid=(n_b, nlb + 1),
        in_specs=[blk(fwd_idx(0)), blk(fwd_idx(0)), blk(fwd_idx(0)), blk(fwd_idx(0)),
                  blk(bwd_idx(0)), blk(bwd_idx(0)), blk(bwd_idx(1)), blk(bwd_idx(1)),
                  full(cmats), full(codes)],
        out_specs=[blk(fwd_idx(0)), blk(bwd_idx(0))],
        out_shape=[jax.ShapeDtypeStruct((nt, HGRN_W), F32)] * 2,
        scratch_shapes=[pltpu.VMEM((2, HGRN_W, HGRN_W), F32),
                        pltpu.VMEM((2, HB, HGRN_W), BF16),
                        pltpu.VMEM((2, HB, HGRN_W), BF16),
                        pltpu.VMEM((2, HB, HGRN_W), BF16),
                        pltpu.VMEM((2, HB, HGRN_W), F32),
                        pltpu.VMEM((2, HB, HGRN_W), F32)],
        compiler_params=_params(("arbitrary", "arbitrary")),
        name="hgrn",
    )(hq, hv, lf, kk, hq, hv, lf, kk, cmats, codes)


def _head_stack(q, n_heads):
    lane = lax.broadcasted_iota(jnp.int32, q.shape, 1)
    zero = jnp.zeros_like(q)
    return jnp.concatenate([jnp.where(lane // HEAD_DIM == h, q, zero) for h in range(n_heads)], axis=0)


def _head_unstack(o, n_heads):
    rows = o.shape[0] // n_heads
    lane = lax.broadcasted_iota(jnp.int32, (rows, o.shape[1]), 1)
    acc = jnp.zeros((rows, o.shape[1]), F32)
    for h in range(n_heads):
        acc = acc + jnp.where(lane // HEAD_DIM == h, o[h * rows:(h + 1) * rows], 0.0)
    return acc


def _na_kernel(n_rows, n_lat_steps, q_ref, k_ref, v_ref, kc_ref, vc_ref, bias_ref, o_ref):
    j = pl.program_id(1)
    kc = kc_ref[...]
    vc = vc_ref[...]
    nwin = NA_WIN_ROWS * GRID_W

    def attend(rr, win):
        q4 = _head_stack(q_ref[rr * GRID_W:(rr + 1) * GRID_W, :], NA_HEADS)
        if win is None:
            keys, vals = kc, vc
            s = _dot_nt(q4, keys)
        else:
            kw, vw, bias = win
            keys = jnp.concatenate([kw, kc], axis=0)
            vals = jnp.concatenate([vw, vc], axis=0)
            s = _dot_nt(q4, keys)
            s = jnp.concatenate([s[:, 0:nwin] + bias, s[:, nwin:]], axis=1)
        m = jnp.max(s, axis=1, keepdims=True)
        p = jnp.exp(s - m)
        den = jnp.sum(p, axis=1, keepdims=True)
        o = _head_unstack(_dot(p.astype(BF16), vals) * (1.0 / den), NA_HEADS)
        o_ref[rr * GRID_W:(rr + 1) * GRID_W, :] = o.astype(BF16)

    @pl.when(j < n_lat_steps)
    def _():
        for rr in range(NA_QROWS):
            r = j * NA_QROWS + rr
            rs = jnp.clip(r - NA_WIN_ROWS // 2, 0, n_rows - NA_WIN_ROWS)
            start = pl.multiple_of(rs * GRID_W, GRID_W)
            kw = k_ref[pl.ds(start, nwin), :]
            vw = v_ref[pl.ds(start, nwin), :]
            attend(rr, (kw, vw, bias_ref[0, r - rs]))

    @pl.when(j >= n_lat_steps)
    def _():
        for rr in range(NA_QROWS):
            attend(rr, None)


def _na(nqkv, bias, layer, n_b, l_, ctx_len, with_ctx):
    nt = nqkv.shape[0]
    qb = NA_QROWS * GRID_W
    assert ctx_len == qb
    n_lat = l_ // qb
    ctx0 = n_b * n_lat
    n_rows = l_ // GRID_W

    def q_idx(b, j):
        return (jnp.where(j < n_lat, b * n_lat + j, ctx0 + b), 0)

    cblk = (n_b * l_) // ctx_len
    return pl.pallas_call(
        functools.partial(_na_kernel, n_rows, n_lat),
        grid=(n_b, n_lat + (1 if with_ctx else 0)),
        in_specs=[pl.BlockSpec((qb, 256), q_idx),
                  pl.BlockSpec((l_, 256), lambda b, j: (b, 1)),
                  pl.BlockSpec((l_, 256), lambda b, j: (b, 2)),
                  pl.BlockSpec((ctx_len, 256), lambda b, j: (cblk + b, 1)),
                  pl.BlockSpec((ctx_len, 256), lambda b, j: (cblk + b, 2)),
                  pl.BlockSpec((1,) + bias.shape[1:], lambda b, j: (layer, 0, 0, 0))],
        out_specs=pl.BlockSpec((qb, 256), q_idx),
        out_shape=jax.ShapeDtypeStruct((nt if with_ctx else n_b * l_, 256), BF16),
        compiler_params=_params(("arbitrary", "arbitrary")),
        name="natten",
    )(nqkv, nqkv, nqkv, nqkv, nqkv, bias)


def _swa_band_bias(n_blocks):
    sb = SWA_BLOCK
    u = np.arange(sb)[:, None]
    v = np.arange(3 * sb)[None, :]
    tabs = []
    for j in (0, 1, n_blocks - 1):
        kpos = (j - 1) * sb + v
        ok = (kpos >= 0) & (kpos < n_blocks * sb) & (np.abs(j * sb + u - kpos) <= SWA_WINDOW)
        tabs.append(np.tile(np.where(ok, 0.0, NEG), (SWA_GROUP, 1)))
    return np.stack(tabs).astype(np.float32)


def _swa_kernel(n_blocks, layer, sink_ref, q_ref, kp_ref, kq_ref, kn_ref, vp_ref, vq_ref, vn_ref,
                kc_ref, vc_ref, bias_ref, o_ref):
    j = pl.program_id(1)
    sb = SWA_BLOCK

    def attend(n, bias):
        cols = slice(n * MXU_N, (n + 1) * MXU_N)
        q4 = _head_stack(q_ref[:, cols], SWA_GROUP)
        sink = jnp.concatenate([jnp.full((sb, 1), sink_ref[layer, n * SWA_GROUP + g], F32)
                                for g in range(SWA_GROUP)], axis=0)
        if bias is None:
            keys, vals = kc_ref[:, cols], vc_ref[:, cols]
            s = _dot_nt(q4, keys)
        else:
            keys = jnp.concatenate([kp_ref[:, cols], kq_ref[:, cols], kn_ref[:, cols], kc_ref[:, cols]], axis=0)
            vals = jnp.concatenate([vp_ref[:, cols], vq_ref[:, cols], vn_ref[:, cols], vc_ref[:, cols]], axis=0)
            s = _dot_nt(q4, keys)
            s = jnp.concatenate([s[:, 0:3 * sb] + bias, s[:, 3 * sb:]], axis=1)
        m = jnp.maximum(jnp.max(s, axis=1, keepdims=True), sink)
        p = jnp.exp(s - m)
        den = jnp.sum(p, axis=1, keepdims=True) + jnp.exp(sink - m)
        o4 = _dot(p.astype(BF16), vals)
        o_ref[:, cols] = _head_unstack(o4 * (1.0 / den), SWA_GROUP).astype(BF16)

    @pl.when(j < n_blocks)
    def _():
        bias = bias_ref[jnp.where(j == 0, 0, jnp.where(j == n_blocks - 1, 2, 1))]
        for n in range(SWA_KV_HEADS):
            attend(n, bias)

    @pl.when(j >= n_blocks)
    def _():
        for n in range(SWA_KV_HEADS):
            attend(n, None)


def _swa(sq, kx, vx, sink, layer, n_b, l_, ctx_len, with_ctx):
    nt = sq.shape[0]
    sb = SWA_BLOCK
    nb = l_ // sb
    assert nb >= 3
    cper = ctx_len // sb
    ctx0 = n_b * nb
    bias = _swa_band_bias(nb)

    def q_idx(b, j):
        return (jnp.where(j < nb, b * nb + j, ctx0 + b * cper + (j - nb)), 0)

    def band_idx(off):
        return lambda b, j: (b * nb + jnp.clip(jnp.minimum(j, nb - 1) + off, 0, nb - 1), 0)

    cblk = (n_b * l_) // ctx_len
    blk = lambda idx: pl.BlockSpec((sb, 512), idx)
    cspec = pl.BlockSpec((ctx_len, 512), lambda b, j: (cblk + b, 0))
    return pl.pallas_call(
        functools.partial(_swa_kernel, nb, layer),
        grid=(n_b, nb + (cper if with_ctx else 0)),
        in_specs=[pl.BlockSpec(memory_space=pltpu.SMEM),
                  blk(q_idx), blk(band_idx(-1)), blk(band_idx(0)), blk(band_idx(1)),
                  blk(band_idx(-1)), blk(band_idx(0)), blk(band_idx(1)), cspec, cspec,
                  pl.BlockSpec(bias.shape, lambda b, j: (0, 0, 0))],
        out_specs=blk(q_idx),
        out_shape=jax.ShapeDtypeStruct((nt if with_ctx else n_b * l_, 512), BF16),
        compiler_params=_params(("arbitrary", "arbitrary")),
        name="swa",
    )(sink, sq, kx, kx, kx, vx, vx, vx, kx, vx, bias)


def _layer_norm(r, g, b):
    mu = jnp.mean(r, axis=-1, keepdims=True)
    rc = r - mu
    var = jnp.mean(rc * rc, axis=-1, keepdims=True)
    return rc * lax.rsqrt(var + LN_EPS) * g + b


def _route(logits_t, bias_col):
    epg = N_EXPERTS // N_GROUPS
    s = _sigmoid(logits_t)
    sel = s + bias_col
    sel_r = [sel[e:e + 1, :] for e in range(N_EXPERTS)]
    s_r = [s[e:e + 1, :] for e in range(N_EXPERTS)]
    grp = []
    for g in range(N_GROUPS):
        a = sel_r[g * epg:(g + 1) * epg]
        m1 = functools.reduce(jnp.maximum, a)
        m2 = functools.reduce(jnp.maximum,
                              [jnp.minimum(a[i], a[k]) for i in range(epg) for k in range(i + 1, epg)])
        grp.append(m1 + m2)
    m_r, w_r = [], []
    for g in range(N_GROUPS):
        best = None
        for k in range(N_GROUPS):
            if k == g:
                continue
            c = (grp[g] > grp[k]) if k < g else (grp[g] >= grp[k])
            best = c if best is None else (best & c)
        for e in range(g * epg, (g + 1) * epg):
            rank = jnp.zeros_like(sel_r[e])
            for k in range(g * epg, (g + 1) * epg):
                if k == e:
                    continue
                ahead = (sel_r[k] >= sel_r[e]) if k < e else (sel_r[k] > sel_r[e])
                rank = rank + jnp.where(ahead, 1.0, 0.0)
            chosen = jnp.where(best & (rank < 1.5), 1.0, 0.0)
            m_r.append(chosen)
            w_r.append(chosen * s_r[e])
    inv = 1.0 / functools.reduce(lambda a, b: a + b, w_r)
    return m_r, [w * inv for w in w_r]


def _pack_bf16_pairs(a, b):
    ua = pltpu.bitcast(a.astype(BF16).astype(F32), jnp.uint32)
    ub = pltpu.bitcast(b.astype(BF16).astype(F32), jnp.uint32)
    return (ua >> 16) | ub


def _unpack_bf16_pairs(u):
    return (pltpu.bitcast(u << 16, F32), pltpu.bitcast(u & jnp.uint32(0xFFFF0000), F32))


def _outproj_kernel(n_lat, of_ref, ob_ref, hg_ref, yb_ref, yc_ref, xa_ref, xb_ref, mod_ref, wo_ref, pv_ref,
                    wrh_ref, wrl_ref, rb_ref, before_ref, x1_ref, h2u_ref, route_ref, gt_ref, cnt_ref):
    @pl.when(pl.program_id(0) == 0)
    def _():
        cnt_ref[...] = jnp.zeros(cnt_ref.shape, F32)

    row = lax.broadcasted_iota(jnp.int32, (HGRN_W, HGRN_W), 0)
    col = lax.broadcasted_iota(jnp.int32, (HGRN_W, HGRN_W), 1)
    head_ones = jnp.where((row // HEAD_DIM) == (col // HEAD_DIM), 1.0, 0.0).astype(BF16)
    pv = pv_ref[0]
    o = of_ref[...] + ob_ref[...]
    hi, lo = _split_bf16(o * o)
    ms = (_dot(hi, head_ones) + _dot(lo, head_ones)) * (1.0 / HEAD_DIM)
    gate = hg_ref[...].astype(F32)
    ya = o * lax.rsqrt(ms + RMS_EPS) * pv[2:3, 0:HGRN_W] * (gate * _sigmoid(gate))
    y = (_dot(ya.astype(BF16), wo_ref[0, 0:256, :]) + _dot(yb_ref[...], wo_ref[0, 256:512, :])
         + _dot(yc_ref[...], wo_ref[0, 512:1024, :]))
    mod = mod_ref[0, 0]
    x = jnp.where(pl.program_id(0) < n_lat, xa_ref[...], xb_ref[...])
    x1 = _layer_norm(ALPHA * x + mod[2:3] * y, pv[0:1], pv[1:2])
    x1_ref[...] = x1
    h2 = x1 * (1.0 + mod[4:5]) + mod[3:4]
    hh, hl = _split_bf16(h2)
    half = h2.shape[1] // 2
    h2u_ref[...] = _pack_bf16_pairs(h2[:, 0:half], h2[:, half:2 * half])
    wrh = wrh_ref[...]
    logits_t = _dot_nt(wrh, hh) + _dot_nt(wrh, hl) + _dot_nt(wrl_ref[...], hh)
    m_r, w_r = _route(logits_t, rb_ref[...])
    m_t = jnp.concatenate(m_r, axis=0)
    rank_t = _dot(m_t.astype(BF16), before_ref[...]) + cnt_ref[:, 0:1]
    cnt_ref[...] = cnt_ref[...] + jnp.sum(m_t, axis=1, keepdims=True)
    seen = jnp.zeros_like(m_r[0])
    e1 = e2 = r1 = r2 = g1 = g2 = jnp.zeros_like(m_r[0])
    for e in range(N_EXPERTS):
        is1 = m_r[e] * (1.0 - seen)
        is2 = m_r[e] * seen
        rk = rank_t[e:e + 1, :]
        e1, e2 = e1 + e * is1, e2 + e * is2
        r1, r2 = r1 + rk * is1, r2 + rk * is2
        g1, g2 = g1 + w_r[e] * is1, g2 + w_r[e] * is2
        seen = seen + is1
    zero = jnp.zeros_like(e1)
    route_ref[...] = jnp.concatenate([e1, e2, r1, r2, g1, g2, zero, zero], axis=0)
    pad = jnp.zeros((LANES - 2, g1.shape[1]), F32)
    gt_ref[...] = jnp.concatenate([g1, g2, pad], axis=0).T


def _outproj(o_f, o_b, hg, yb, yc, xa, xb, ctx_blk, mod, wo, pv, layer, wrh, wrl, rb, n_b, l_, n_tiles):
    d = xa.shape[1]
    tpb = l_ // TM
    n_lat = n_b * tpb
    before = np.triu(np.ones((TM, TM), np.float32), 1).astype(jnp.bfloat16)

    def mod_idx(j):
        return (layer, jnp.minimum(j // tpb, n_b), 0, 0)

    def rows(width):
        return pl.BlockSpec((TM, width), lambda j: (j, 0))

    full = lambda a: pl.BlockSpec(a.shape, lambda j: (0,) * a.ndim)
    of_layer = lambda a: pl.BlockSpec((1,) + a.shape[1:], lambda j: (layer,) + (0,) * (a.ndim - 1))
    nr = n_tiles * TM
    return pl.pallas_call(
        functools.partial(_outproj_kernel, n_lat),
        grid=(n_tiles,),
        in_specs=[rows(256), rows(256), rows(256), rows(256), rows(512),
                  pl.BlockSpec((TM, d), lambda j: (jnp.minimum(j, n_lat - 1), 0)),
                  pl.BlockSpec((TM, d), lambda j: (ctx_blk, 0)),
                  pl.BlockSpec((1, 1, 6, d), mod_idx), of_layer(wo), of_layer(pv), full(wrh), full(wrl), full(rb),
                  full(before)],
        out_specs=[rows(d), rows(d // 2), pl.BlockSpec((8, TM), lambda j: (0, j)), rows(LANES),
                   pl.BlockSpec((N_EXPERTS, LANES), lambda j: (0, 0))],
        out_shape=[jax.ShapeDtypeStruct((nr, d), F32), jax.ShapeDtypeStruct((nr, d // 2), jnp.uint32),
                   jax.ShapeDtypeStruct((8, nr), F32), jax.ShapeDtypeStruct((nr, LANES), F32),
                   jax.ShapeDtypeStruct((N_EXPERTS, LANES), F32)],
        compiler_params=_params(("arbitrary",)),
        name="outproj",
    )(o_f, o_b, hg, yb, yc, xa, xb, mod, wo, pv, wrh, wrl, rb, before)


def _sc_split(rows_per_worker):
    best = None
    for chunk in (64, 48, 32):
        for nbuf in range(SC_MAX_ROWS // chunk, 0, -1):
            if rows_per_worker % (chunk * nbuf) == 0 and (best is None or chunk * nbuf > best[0] * best[1]):
                best = (chunk, nbuf)
    return best


def _sc_workers():
    info = plsc.get_sparse_core_info()
    return info.num_cores, info.num_cores * info.num_subcores


def _sc_scatter_rows(table, pos, n_slots):
    n_tok, width = table.shape
    n_cores, n_workers = _sc_workers()
    per_w = n_tok // n_workers
    assert n_tok % n_workers == 0
    chunk, nbuf = _sc_split(per_w)
    n_chunks = per_w // chunk
    mesh = plsc.VectorSubcoreMesh(core_axis_name="c", subcore_axis_name="s")

    @functools.partial(
        pl.kernel, mesh=mesh,
        out_type=jax.ShapeDtypeStruct((n_slots, width), table.dtype),
        scratch_types=[pltpu.VMEM((2 * n_chunks, chunk), jnp.int32),
                       pltpu.VMEM((nbuf, chunk, width), table.dtype),
                       pltpu.SemaphoreType.DMA((nbuf,)),
                       pltpu.SemaphoreType.DMA((nbuf,))],
    )
    def scatter(table_hbm, pos_hbm, out_hbm, idx_v, rows_v, rsem, wsem):
        wid = lax.axis_index("s") * n_cores + lax.axis_index("c")
        base = wid * per_w
        pltpu.sync_copy(pos_hbm.at[wid], idx_v)

        @pl.loop(0, n_chunks // nbuf)
        def _(g):
            c0 = g * nbuf
            gets = [pltpu.async_copy(table_hbm.at[pl.ds(base + (c0 + b) * chunk, chunk)], rows_v.at[b],
                                     rsem.at[b]) for b in range(nbuf)]
            puts = []
            for b in range(nbuf):
                gets[b].wait()
                for k in range(2):
                    puts.append(pltpu.async_copy(rows_v.at[b], out_hbm.at[idx_v.at[k * n_chunks + c0 + b]],
                                                 wsem.at[b]))
            for put in puts:
                put.wait()

    pos_w = pos.reshape(2, n_workers, n_chunks, chunk).transpose(1, 0, 2, 3).reshape(n_workers, 2 * n_chunks, chunk)
    return scatter(table, pos_w)


def _sc_gather_rows(table, idx):
    n_out = idx.shape[0]
    width = table.shape[1]
    n_cores, n_workers = _sc_workers()
    per_w = n_out // n_workers
    assert n_out % n_workers == 0
    chunk, nbuf = _sc_split(per_w)
    mesh = plsc.VectorSubcoreMesh(core_axis_name="c", subcore_axis_name="s")

    @functools.partial(
        pl.kernel, mesh=mesh,
        out_type=jax.ShapeDtypeStruct((n_out, width), table.dtype),
        scratch_types=[pltpu.VMEM((per_w,), jnp.int32),
                       pltpu.VMEM((nbuf, chunk, width), table.dtype),
                       pltpu.SemaphoreType.DMA((nbuf,)),
                       pltpu.SemaphoreType.DMA((nbuf,))],
    )
    def gather(table_hbm, idx_hbm, out_hbm, idx_v, rows_v, gsem, wsem):
        wid = lax.axis_index("s") * n_cores + lax.axis_index("c")
        base = wid * per_w
        pltpu.sync_copy(idx_hbm.at[pl.ds(base, per_w)], idx_v)

        @pl.loop(0, per_w // (chunk * nbuf))
        def _(g):
            off = g * (chunk * nbuf)
            gets = [pltpu.async_copy(table_hbm.at[idx_v.at[pl.ds(off + b * chunk, chunk)]], rows_v.at[b],
                                     gsem.at[b]) for b in range(nbuf)]
            puts = []
            for b in range(nbuf):
                gets[b].wait()
                puts.append(pltpu.async_copy(rows_v.at[b], out_hbm.at[pl.ds(base + off + b * chunk, chunk)],
                                             wsem.at[b]))
            for put in puts:
                put.wait()

    return gather(table, idx)


def _gmm_kernel(te_ref, nv_ref, nu_ref, xs_ref, wg_ref, wu_ref, wd_ref, ys_ref, wgb_ref, wub_ref, wdb_ref):
    i = pl.program_id(0)
    used = i < nu_ref[0]
    fresh = jnp.logical_or(i == 0, te_ref[i] != te_ref[jnp.maximum(i - 1, 0)])

    @pl.when(jnp.logical_and(used, fresh))
    def _():
        wgb_ref[...] = wg_ref[0, 0].astype(BF16)
        wub_ref[...] = wu_ref[0, 0].astype(BF16)
        wdb_ref[...] = wd_ref[0, 0].astype(BF16)

    @pl.when(used)
    def _():
        rowid = lax.broadcasted_iota(jnp.int32, xs_ref.shape, 0)
        lo, hi = _unpack_bf16_pairs(jnp.where(rowid < nv_ref[i], xs_ref[...], jnp.uint32(0)))
        lo, hi = lo.astype(BF16), hi.astype(BF16)
        half = lo.shape[1]
        a = _dot(lo, wgb_ref[0:half, :]) + _dot(hi, wgb_ref[half:2 * half, :])
        u = _dot(lo, wub_ref[0:half, :]) + _dot(hi, wub_ref[half:2 * half, :])
        y = _dot((a * _sigmoid(a) * u).astype(BF16), wdb_ref[...])
        ys_ref[...] = _pack_bf16_pairs(y[:, 0:half], y[:, half:2 * half])

    @pl.when(jnp.logical_not(used))
    def _():
        ys_ref[...] = jnp.zeros(ys_ref.shape, jnp.uint32)


def _gmm(tile_expert, tile_valid, n_used, xs, wg, wu, wd, layer):
    p_rows, half = xs.shape
    d = 2 * half
    wspec = lambda r, c: pl.BlockSpec((1, 1, r, c), lambda i, te, nv, nu: (layer, te[i], 0, 0))
    return pl.pallas_call(
        _gmm_kernel,
        grid_spec=pltpu.PrefetchScalarGridSpec(
            num_scalar_prefetch=3,
            grid=(p_rows // TM,),
            in_specs=[pl.BlockSpec((TM, half), lambda i, te, nv, nu: (i, 0)),
                      wspec(d, D_EXPERT), wspec(d, D_EXPERT), wspec(D_EXPERT, d)],
            out_specs=pl.BlockSpec((TM, half), lambda i, te, nv, nu: (i, 0)),
            scratch_shapes=[pltpu.VMEM((d, D_EXPERT), BF16), pltpu.VMEM((d, D_EXPERT), BF16),
                            pltpu.VMEM((D_EXPERT, d), BF16)]),
        out_shape=jax.ShapeDtypeStruct((p_rows, half), jnp.uint32),
        compiler_params=_params(("arbitrary",)),
        name="moe_experts",
    )(tile_expert, tile_valid, n_used, xs, wg, wu, wd)


def _moefin_kernel(o1_ref, o2_ref, gt_ref, x_ref, mod_ref, pv_ref, out_ref):
    gt = gt_ref[...]
    g1, g2 = gt[:, 0:1], gt[:, 1:2]
    a1, b1 = _unpack_bf16_pairs(o1_ref[...])
    a2, b2 = _unpack_bf16_pairs(o2_ref[...])
    f = jnp.concatenate([g1 * a1 + g2 * a2, g1 * b1 + g2 * b2], axis=1)
    mod = mod_ref[0, 0]
    pv = pv_ref[0]
    out_ref[...] = _layer_norm(ALPHA * x_ref[...] + mod[5:6] * f, pv[3:4], pv[4:5])


def _moefin(o12, gt, x1, mod, pv, layer, n_b, l_, n_tiles):
    d = x1.shape[1]
    tpb = l_ // TM

    def mod_idx(j):
        return (layer, jnp.minimum(j // tpb, n_b), 0, 0)

    return pl.pallas_call(
        _moefin_kernel,
        grid=(n_tiles,),
        in_specs=[pl.BlockSpec((TM, d // 2), lambda j: (j, 0)),
                  pl.BlockSpec((TM, d // 2), lambda j: (j + n_tiles, 0)),
                  pl.BlockSpec((TM, LANES), lambda j: (j, 0)),
                  pl.BlockSpec((TM, d), lambda j: (j, 0)),
                  pl.BlockSpec((1, 1, 6, d), mod_idx),
                  pl.BlockSpec((1,) + pv.shape[1:], lambda j: (layer, 0, 0))],
        out_specs=pl.BlockSpec((TM, d), lambda j: (j, 0)),
        out_shape=jax.ShapeDtypeStruct((n_tiles * TM, d), F32),
        compiler_params=_params(("arbitrary",)),
        name="moe_combine",
    )(o12, o12, gt, x1, mod, pv)


def _moe_routed(h2u, route, gt, counts, wg, wu, wd, layer, x1, mod, pv, n_b, l_, n_tiles):
    n_tok = n_tiles * TM
    n_slots = -(-(2 * n_tok + N_EXPERTS * TM) // SC_ROW_QUANTUM) * SC_ROW_QUANTUM
    cnt = counts[:, 0].astype(jnp.int32)
    padded = ((cnt + TM - 1) // TM) * TM
    upto = jnp.arange(N_EXPERTS)[None, :] <= jnp.arange(N_EXPERTS)[:, None]
    ends = jnp.sum(jnp.where(upto, padded[None, :], 0), axis=1)
    offs = ends - padded
    e12 = route[0:2].astype(jnp.int32)
    r12 = route[2:4].astype(jnp.int32)
    onehot = e12[:, :, None] == jnp.arange(N_EXPERTS, dtype=jnp.int32)
    pos = r12 + jnp.sum(jnp.where(onehot, offs, 0), axis=-1)
    tile_start = jnp.arange(n_slots // TM, dtype=jnp.int32) * TM
    tile_expert = jnp.minimum(jnp.sum(tile_start[:, None] >= ends[None, :], axis=1), N_EXPERTS - 1)
    pick = tile_expert[:, None] == jnp.arange(N_EXPERTS, dtype=jnp.int32)
    tile_valid = jnp.clip(jnp.sum(jnp.where(pick, (offs + cnt)[None, :], 0), axis=1) - tile_start, 0, TM)
    n_used = (ends[N_EXPERTS - 1] // TM).reshape(1)
    xs = _sc_scatter_rows(h2u, pos, n_slots)
    ys = _gmm(tile_expert.astype(jnp.int32), tile_valid.astype(jnp.int32), n_used.astype(jnp.int32),
              xs, wg, wu, wd, layer)
    o12 = _sc_gather_rows(ys, pos.reshape(2 * n_tok))
    return _moefin(o12, gt, x1, mod, pv, layer, n_b, l_, n_tiles)


def _rope_tables(l_):
    pos = np.arange(l_)
    nf = HEAD_DIM // 4
    inv = ROPE_BASE ** (-np.arange(nf, dtype=np.float64) / nf)
    ar = (pos // GRID_W)[:, None] * inv
    ac = (pos % GRID_W)[:, None] * inv
    cos = np.concatenate([np.cos(ar), np.cos(ar), np.cos(ac), np.cos(ac)], axis=1)
    sin = np.concatenate([-np.sin(ar), np.sin(ar), -np.sin(ac), np.sin(ac)], axis=1)
    cos = np.concatenate([np.tile(cos, (1, 2)), np.ones((TM, LANES))], axis=0)
    sin = np.concatenate([np.tile(sin, (1, 2)), np.zeros((TM, LANES))], axis=0)
    return cos.astype(np.float32), sin.astype(np.float32)


def _na_bias(rpb):
    u = np.arange(GRID_W)[:, None]
    v = np.arange(GRID_W)[None, :]
    cs = np.clip(u - NA_WIN_COLS // 2, 0, GRID_W - NA_WIN_COLS)
    colmask = (v >= cs) & (v < cs + NA_WIN_COLS)
    coff = np.clip(v - u + NA_WIN_COLS - 1, 0, 2 * NA_WIN_COLS - 2)
    n_off = 2 * NA_WIN_COLS - 1
    pick_col = (coff[None] == np.arange(n_off)[:, None, None]).astype(np.float32)
    ridx = np.arange(NA_WIN_ROWS)[None, :] - np.arange(NA_WIN_ROWS)[:, None] + NA_WIN_ROWS - 1
    pick_row = (ridx[:, :, None] == np.arange(2 * NA_WIN_ROWS - 1)).astype(np.float32)
    rows = jnp.einsum('dia,lhab->ldhib', pick_row, rpb.astype(F32), precision=lax.Precision.HIGHEST)
    tab = jnp.einsum('ldhib,buv->ldhuiv', rows, pick_col, precision=lax.Precision.HIGHEST)
    tab = jnp.where(colmask[None, None, None, :, None, :], tab, NEG)
    return tab.reshape(rpb.shape[0], NA_WIN_ROWS, NA_HEADS * GRID_W, NA_WIN_ROWS * GRID_W)


def kernel(x, c, ctx, c_ctx, w_ada, b_ada, w_in, lb_logits, hgrn_norm, na_rpb, swa_sink, w_out,
           ln1_g, ln1_b, ln2_g, ln2_b, w_router, router_bias, w_gate, w_up, w_down):
    n_b, l_, d = x.shape
    ctx_len = ctx.shape[1]
    depth = w_ada.shape[0]
    assert n_b * ctx_len == TM and l_ % TM == 0 and n_b + 1 <= N_COND
    n_lat_tiles = (n_b * l_) // TM

    p_lb = jax.nn.softmax(lb_logits.astype(F32), axis=0).reshape(depth, 2 * HGRN_W)
    upto = (np.arange(depth)[None, :] <= np.arange(depth)[:, None]).astype(np.float32)
    lb = jnp.maximum(jnp.sum(upto[:, :, None] * p_lb[None], axis=1) - p_lb[0:1], LB_MIN)
    gp = jnp.stack([jnp.log(lb), jnp.log1p(-lb), 1.0 - lb] + [jnp.zeros_like(lb)] * 5, axis=1)

    cond = jnp.concatenate([c, c_ctx[None, :], jnp.zeros((N_COND - n_b - 1, d), F32)], axis=0)
    mod = _ada(cond.T, w_ada, b_ada, n_b + 1).reshape(depth, N_COND, 6, d)

    cos_t, sin_t = _rope_tables(l_)
    wr_t = w_router.T
    wrh = wr_t.astype(BF16)
    wrl = (wr_t - wrh.astype(F32)).astype(BF16)
    rb = router_bias.astype(F32)[:, None]
    w_in_b = w_in.astype(BF16)
    w_out_b = w_out.astype(BF16)
    zeros = jnp.zeros_like(ln1_g)
    pv = jnp.stack([ln1_g, ln1_b, jnp.tile(hgrn_norm, (1, d // HEAD_DIM)), ln2_g, ln2_b, zeros, zeros, zeros],
                   axis=1).astype(F32)
    na_bias = _na_bias(na_rpb)
    sink = swa_sink.astype(F32)

    xa, xb, ctx_blk = x.reshape(n_b * l_, d), ctx.reshape(n_b * ctx_len, d), 0
    for l in range(depth):
        last = l == depth - 1
        hq, hv, hg, lf, kk, nqkv, sq, kx, vx = _inproj(xa, xb, ctx_blk, mod, w_in_b, gp, l, cos_t, sin_t, n_b, l_)
        o_f, o_b = _hgrn(hq, hv, lf, kk, n_b, l_, ctx_len)
        yb = _na(nqkv, na_bias, l, n_b, l_, ctx_len, not last)
        yc = _swa(sq, kx, vx, sink, l, n_b, l_, ctx_len, not last)
        n_tiles = n_lat_tiles if last else n_lat_tiles + 1
        x1, h2u, route, gt, counts = _outproj(o_f, o_b, hg, yb, yc, xa, xb, ctx_blk, mod, w_out_b, pv, l,
                                              wrh, wrl, rb, n_b, l_, n_tiles)
        xa = _moe_routed(h2u, route, gt, counts, w_gate, w_up, w_down, l, x1, mod, pv, n_b, l_, n_tiles)
        xb, ctx_blk = xa, n_lat_tiles
    return xa[:n_b * l_].reshape(n_b, l_, d)
```

```python
import functools

import numpy as np
import jax
import jax.numpy as jnp
from jax import lax
from jax.experimental import pallas as pl
from jax.experimental.pallas import tpu as pltpu
from jax.experimental.pallas import tpu_sc as plsc

F32 = jnp.float32
BF16 = jnp.bfloat16

D_MODEL = 1024
GRID_W = 64
HEAD_DIM = 64
HGRN_W = 256
NA_HEADS = 4
NA_WIN_ROWS = 8
NA_WIN_COLS = 16
SWA_Q_HEADS = 8
SWA_KV_HEADS = 2
SWA_GROUP = 4
SWA_WINDOW = 128
SWA_BLOCK = 128
ROPE_BASE = 10000.0
N_EXPERTS = 16
N_GROUPS = 4
D_EXPERT = 512
LN_EPS = 1e-5
RMS_EPS = 1e-6
NEG = -1e30
LB_MIN = 1e-6
DEPTH = 2
ALPHA = (2.0 * DEPTH) ** 0.25

LANES = 128
MXU_N = 256
TM = 512
HB = 256
HC = 16
NA_QROWS = 4
SWA_STEP_BLOCKS = 2
VMEM_LIMIT = 56 * 1024 * 1024

SC_MAX_ROWS = 192
SC_ROW_QUANTUM = 2048

N_COND = 8


def _dot(a, b):
    return jnp.dot(a, b, preferred_element_type=F32)


def _dot_nt(a, b):
    return lax.dot_general(a, b, (((1,), (1,)), ((), ())), preferred_element_type=F32)


def _dot_tn(a, b):
    return lax.dot_general(a, b, (((0,), (0,)), ((), ())), preferred_element_type=F32)


def _sigmoid(x):
    return 1.0 / (1.0 + jnp.exp(-x))


def _split_bf16(x):
    hi = x.astype(BF16)
    lo = (x - hi.astype(F32)).astype(BF16)
    return hi, lo


def _params(sem):
    return pltpu.CompilerParams(dimension_semantics=sem, vmem_limit_bytes=VMEM_LIMIT)


def _ada_kernel(n_rows, condt_ref, w_ref, b_ref, o_ref):
    c = condt_ref[...]
    s = c * _sigmoid(c)
    w = w_ref[0]
    rows = [jnp.sum(w * s[:, r:r + 1], axis=0, keepdims=True) for r in range(n_rows)]
    rows.append(jnp.zeros((N_COND - n_rows, w.shape[1]), F32))
    o_ref[0] = jnp.concatenate(rows, axis=0) + b_ref[0]


def _ada(cond_t, w_ada, b_ada, n_rows):
    depth, d, n6 = w_ada.shape
    tn = 1024
    return pl.pallas_call(
        functools.partial(_ada_kernel, n_rows),
        grid=(depth, n6 // tn),
        in_specs=[pl.BlockSpec((d, N_COND), lambda l, n: (0, 0)),
                  pl.BlockSpec((1, d, tn), lambda l, n: (l, 0, n)),
                  pl.BlockSpec((1, 1, tn), lambda l, n: (l, 0, n))],
        out_specs=pl.BlockSpec((1, N_COND, tn), lambda l, n: (l, 0, n)),
        out_shape=jax.ShapeDtypeStruct((depth, N_COND, n6), F32),
        compiler_params=_params(("arbitrary", "arbitrary")),
        name="ada",
    )(cond_t, w_ada, b_ada.reshape(depth, 1, n6))


def _rope(z, cos, sin, first):
    sw = jnp.where(first, pltpu.roll(z, LANES - 16, axis=1), pltpu.roll(z, 16, axis=1))
    return z * cos + sw * sin


def _inproj_kernel(n_lat, xa_ref, xb_ref, mod_ref, w_ref, gp_ref, cos_ref, sin_ref,
                   hq_ref, hv_ref, hg_ref, lf_ref, kk_ref, nqkv_ref, sq_ref, kx_ref, vx_ref):
    mod = mod_ref[0, 0]
    x = jnp.where(pl.program_id(0) < n_lat, xa_ref[...], xb_ref[...])
    h = (x * (1.0 + mod[1:2]) + mod[0:1]).astype(BF16)
    gp = gp_ref[0]
    cos = cos_ref[...]
    sin = sin_ref[...]
    lane = lax.broadcasted_iota(jnp.int32, cos.shape, 1)
    first = (lane & 16) == 0
    scale = HEAD_DIM ** -0.5

    def chunk(c):
        return _dot(h, w_ref[0, :, c * MXU_N:(c + 1) * MXU_N])

    def per_query_head(z2):
        swapped = pltpu.roll(z2, HEAD_DIM, axis=1)
        low = lane < HEAD_DIM
        h0 = jnp.where(low, z2, swapped)
        h1 = jnp.where(low, swapped, z2)
        return jnp.concatenate([h0, h0], axis=1), jnp.concatenate([h1, h1], axis=1)

    def gates(z, d):
        cols = slice(d * MXU_N, (d + 1) * MXU_N)
        a = gp[0:1, cols]
        b = gp[1:2, cols] + (jnp.minimum(z, 0.0) - jnp.log1p(jnp.exp(-jnp.abs(z))))
        logf = jnp.maximum(a, b) + jnp.log1p(jnp.exp(-jnp.abs(a - b)))
        k = gp[2:3, cols] * (1.0 / (1.0 + jnp.exp(z)))
        return logf, k

    def rope2(z):
        return jnp.concatenate([_rope(z[:, 0:LANES], cos, sin, first),
                                _rope(z[:, LANES:2 * LANES], cos, sin, first)], axis=1)

    hq_ref[...] = chunk(0).astype(BF16)
    for d in range(2):
        logf, k = gates(chunk(1 + d), d)
        lf_ref[:, d * MXU_N:(d + 1) * MXU_N] = logf
        kk_ref[:, d * MXU_N:(d + 1) * MXU_N] = k.astype(BF16)
    hv_ref[...] = chunk(3).astype(BF16)
    hg_ref[...] = chunk(4).astype(BF16)
    nqkv_ref[:, 0:MXU_N] = (chunk(5) * scale).astype(BF16)
    nqkv_ref[:, MXU_N:2 * MXU_N] = chunk(6).astype(BF16)
    nqkv_ref[:, 2 * MXU_N:3 * MXU_N] = chunk(7).astype(BF16)
    for d in range(2):
        sq_ref[:, d * MXU_N:(d + 1) * MXU_N] = (rope2(chunk(8 + d)) * scale).astype(BF16)
    zkv = chunk(10)
    k0, k1 = per_query_head(_rope(zkv[:, 0:LANES], cos, sin, first))
    v0, v1 = per_query_head(zkv[:, LANES:2 * LANES])
    kx_ref[:, 0:MXU_N] = k0.astype(BF16)
    kx_ref[:, MXU_N:2 * MXU_N] = k1.astype(BF16)
    vx_ref[:, 0:MXU_N] = v0.astype(BF16)
    vx_ref[:, MXU_N:2 * MXU_N] = v1.astype(BF16)


def _inproj(xa, xb, ctx_blk, mod, w, gp, layer, cos_t, sin_t, n_b, l_):
    d = xa.shape[1]
    tpb = l_ // TM
    n_lat = n_b * tpb
    nt = (n_lat + 1) * TM

    def mod_idx(j):
        return (layer, jnp.minimum(j // tpb, n_b), 0, 0)

    def rope_idx(j):
        return (jnp.where(j < n_lat, j % tpb, tpb), 0)

    def rows(width):
        return pl.BlockSpec((TM, width), lambda j: (j, 0))

    widths = [(HGRN_W, BF16), (HGRN_W, BF16), (HGRN_W, BF16), (2 * HGRN_W, F32), (2 * HGRN_W, BF16),
              (3 * 256, BF16), (512, BF16), (512, BF16), (512, BF16)]
    return pl.pallas_call(
        functools.partial(_inproj_kernel, n_lat),
        grid=(nt // TM,),
        in_specs=[pl.BlockSpec((TM, d), lambda j: (jnp.minimum(j, n_lat - 1), 0)),
                  pl.BlockSpec((TM, d), lambda j: (ctx_blk, 0)),
                  pl.BlockSpec((1, 1, 6, d), mod_idx),
                  pl.BlockSpec((1,) + w.shape[1:], lambda j: (layer, 0, 0)),
                  pl.BlockSpec((1,) + gp.shape[1:], lambda j: (layer, 0, 0)),
                  pl.BlockSpec((TM, LANES), rope_idx),
                  pl.BlockSpec((TM, LANES), rope_idx)],
        out_specs=[rows(wd) for wd, _ in widths],
        out_shape=[jax.ShapeDtypeStruct((nt, wd), dt) for wd, dt in widths],
        compiler_params=_params(("arbitrary",)),
        name="inproj",
    )(xa, xb, mod, w, gp, cos_t, sin_t)


def _hgrn_consts():
    t = np.arange(HB)[:, None]
    u = np.arange(HB)[None, :]
    same = (t // HC) == (u // HC)
    tr, ur = t % HC, u % HC
    half = HC // 2
    mats, codes = [], []
    for fwd in (True, False):
        if fwd:
            incl = ur <= tr
            mid = ur <= (tr // half) * half + half // 2 - 1
            edge = ur <= half - 1
            code = np.where((t // half == u // half) & (u <= t), 1, np.where(same & (tr >= half) & (ur < half), 2, 0))
        else:
            incl = ur >= tr
            mid = ur >= (tr // half) * half + half // 2
            edge = ur >= half
            code = np.where((t // half == u // half) & (u >= t), 1, np.where(same & (tr < half) & (ur >= half), 2, 0))
        mats.append(np.concatenate([same & incl, same, same & mid, same & edge], axis=0))
        codes.append(code)
    return (np.stack(mats).astype(np.float32).astype(jnp.bfloat16), np.stack(codes).astype(np.float32))


def _hgrn_kernel(qf_ref, vf_ref, lff_ref, kf_ref, qb_ref, vb_ref, lfb_ref, kb_ref, cm_ref, code_ref,
                 of_ref, ob_ref, st_ref, qd_ref, kd_ref, vv_ref, gt_ref, it_ref):
    nch = HB // HC

    @pl.when(pl.program_id(1) == 0)
    def _():
        st_ref[...] = jnp.zeros(st_ref.shape, F32)

    row = lax.broadcasted_iota(jnp.int32, (HGRN_W, HGRN_W), 0)
    col = lax.broadcasted_iota(jnp.int32, (HGRN_W, HGRN_W), 1)
    same_head = (row // HEAD_DIM) == (col // HEAD_DIM)
    n_heads = HGRN_W // HEAD_DIM

    def prep(d, q_ref, v_ref, lf_ref, k_ref):
        hi, lo = _split_bf16(lf_ref[...])
        sums = _dot(cm_ref[d], hi) + _dot(cm_ref[d], lo)
        cum, tot = sums[0:HB], sums[HB:2 * HB]
        mid, edge = sums[2 * HB:3 * HB], sums[3 * HB:4 * HB]
        q = q_ref[...].astype(F32)
        k = k_ref[...].astype(F32)
        v = v_ref[...]
        qd_ref[d] = (q * jnp.exp(cum)).astype(BF16)
        kd_ref[d] = (k * jnp.exp(tot - cum)).astype(BF16)
        vv_ref[d] = v
        gt_ref[d] = jnp.exp(tot)
        q1 = (q * jnp.exp(cum - mid)).astype(BF16)
        k1 = (k * jnp.exp(mid - cum)).astype(BF16)
        q2 = (q * jnp.exp(jnp.minimum(cum - edge, 0.0))).astype(BF16)
        k2 = (k * jnp.exp(jnp.minimum(edge - cum, 0.0))).astype(BF16)
        s1 = _dot_nt(_head_stack(q1, n_heads), k1)
        s2 = _dot_nt(_head_stack(q2, n_heads), k2)
        code = jnp.concatenate([code_ref[d]] * n_heads, axis=0)
        p = jnp.where(code == 1.0, s1, jnp.where(code == 2.0, s2, 0.0))
        return _head_unstack(_dot(p.astype(BF16), v), n_heads)

    od_f = prep(0, qf_ref, vf_ref, lff_ref, kf_ref)
    od_b = prep(1, qb_ref, vb_ref, lfb_ref, kb_ref)

    for i in range(nch):
        for d, c in ((0, i), (1, nch - 1 - i)):
            rows = slice(c * HC, (c + 1) * HC)
            st = st_ref[d]
            it_ref[d, rows, :] = _dot_nt(qd_ref[d, rows, :], st.astype(BF16))
            upd = _dot_tn(vv_ref[d, rows, :], kd_ref[d, rows, :])
            st_ref[d] = st * gt_ref[d, c * HC:c * HC + 1, :] + jnp.where(same_head, upd, 0.0)

    of_ref[...] = it_ref[0] + od_f
    ob_ref[...] = it_ref[1] + od_b


def _hgrn(hq, hv, lf, kk, n_b, l_, ctx_len):
    nt = hq.shape[0]
    assert ctx_len == HB
    nlb = l_ // HB
    ctx0 = n_b * nlb
    cmats, codes = _hgrn_consts()

    def fwd_idx(col):
        return lambda b, s: (jnp.where(s == 0, ctx0 + b, b * nlb + s - 1), col)

    def bwd_idx(col):
        return lambda b, s: (jnp.where(s == 0, ctx0 + b, b * nlb + nlb - s), col)

    def blk(idx):
        return pl.BlockSpec((HB, HGRN_W), idx)

    full = lambda a: pl.BlockSpec(a.shape, lambda b, s: (0,) * a.ndim)
    return pl.pallas_call(
        _hgrn_kernel,
        grid=(n_b, nlb + 1),
        in_specs=[blk(fwd_idx(0)), blk(fwd_idx(0)), blk(fwd_idx(0)), blk(fwd_idx(0)),
                  blk(bwd_idx(0)), blk(bwd_idx(0)), blk(bwd_idx(1)), blk(bwd_idx(1)),
                  full(cmats), full(codes)],
        out_specs=[blk(fwd_idx(0)), blk(bwd_idx(0))],
        out_shape=[jax.ShapeDtypeStruct((nt, HGRN_W), F32)] * 2,
        scratch_shapes=[pltpu.VMEM((2, HGRN_W, HGRN_W), F32),
                        pltpu.VMEM((2, HB, HGRN_W), BF16),
                        pltpu.VMEM((2, HB, HGRN_W), BF16),
                        pltpu.VMEM((2, HB, HGRN_W), BF16),
                        pltpu.VMEM((2, HB, HGRN_W), F32),
                        pltpu.VMEM((2, HB, HGRN_W), F32)],
        compiler_params=_params(("arbitrary", "arbitrary")),
        name="hgrn",
    )(hq, hv, lf, kk, hq, hv, lf, kk, cmats, codes)


def _head_stack(q, n_heads):
    lane = lax.broadcasted_iota(jnp.int32, q.shape, 1)
    zero = jnp.zeros_like(q)
    return jnp.concatenate([jnp.where(lane // HEAD_DIM == h, q, zero) for h in range(n_heads)], axis=0)


def _head_unstack(o, n_heads):
    rows = o.shape[0] // n_heads
    lane = lax.broadcasted_iota(jnp.int32, (rows, o.shape[1]), 1)
    acc = jnp.zeros((rows, o.shape[1]), F32)
    for h in range(n_heads):
        acc = acc + jnp.where(lane // HEAD_DIM == h, o[h * rows:(h + 1) * rows], 0.0)
    return acc


def _na_kernel(n_rows, n_lat_steps, q_ref, k_ref, v_ref, kc_ref, vc_ref, bias_ref, o_ref):
    j = pl.program_id(1)
    kc = kc_ref[...]
    vc = vc_ref[...]
    nwin = NA_WIN_ROWS * GRID_W

    def attend(rr, win):
        q4 = _head_stack(q_ref[rr * GRID_W:(rr + 1) * GRID_W, :], NA_HEADS)
        s_ctx = _dot_nt(q4, kc)
        m = jnp.max(s_ctx, axis=1, keepdims=True)
        if win is not None:
            kw, vw, bias = win
            s_win = _dot_nt(q4, kw) + bias
            m = jnp.maximum(m, jnp.max(s_win, axis=1, keepdims=True))
            p_win = jnp.exp(s_win - m)
        p_ctx = jnp.exp(s_ctx - m)
        den = jnp.sum(p_ctx, axis=1, keepdims=True)
        o4 = _dot(p_ctx.astype(BF16), vc)
        if win is not None:
            den = den + jnp.sum(p_win, axis=1, keepdims=True)
            o4 = o4 + _dot(p_win.astype(BF16), vw)
        o = _head_unstack(o4 * (1.0 / den), NA_HEADS)
        o_ref[rr * GRID_W:(rr + 1) * GRID_W, :] = o.astype(BF16)

    @pl.when(j < n_lat_steps)
    def _():
        for rr in range(NA_QROWS):
            r = j * NA_QROWS + rr
            rs = jnp.clip(r - NA_WIN_ROWS // 2, 0, n_rows - NA_WIN_ROWS)
            start = pl.multiple_of(rs * GRID_W, GRID_W)
            kw = k_ref[pl.ds(start, nwin), :]
            vw = v_ref[pl.ds(start, nwin), :]
            attend(rr, (kw, vw, bias_ref[0, r - rs]))

    @pl.when(j >= n_lat_steps)
    def _():
        for rr in range(NA_QROWS):
            attend(rr, None)


def _na(nqkv, bias, layer, n_b, l_, ctx_len, with_ctx):
    nt = nqkv.shape[0]
    qb = NA_QROWS * GRID_W
    assert ctx_len == qb
    n_lat = l_ // qb
    ctx0 = n_b * n_lat
    n_rows = l_ // GRID_W

    def q_idx(b, j):
        return (jnp.where(j < n_lat, b * n_lat + j, ctx0 + b), 0)

    cblk = (n_b * l_) // ctx_len
    return pl.pallas_call(
        functools.partial(_na_kernel, n_rows, n_lat),
        grid=(n_b, n_lat + (1 if with_ctx else 0)),
        in_specs=[pl.BlockSpec((qb, 256), q_idx),
                  pl.BlockSpec((l_, 256), lambda b, j: (b, 1)),
                  pl.BlockSpec((l_, 256), lambda b, j: (b, 2)),
                  pl.BlockSpec((ctx_len, 256), lambda b, j: (cblk + b, 1)),
                  pl.BlockSpec((ctx_len, 256), lambda b, j: (cblk + b, 2)),
                  pl.BlockSpec((1,) + bias.shape[1:], lambda b, j: (layer, 0, 0, 0))],
        out_specs=pl.BlockSpec((qb, 256), q_idx),
        out_shape=jax.ShapeDtypeStruct((nt if with_ctx else n_b * l_, 256), BF16),
        compiler_params=_params(("arbitrary", "arbitrary")),
        name="natten",
    )(nqkv, nqkv, nqkv, nqkv, nqkv, bias)


def _swa_band_bias(n_blocks):
    sb = SWA_BLOCK
    u = np.arange(sb)[:, None]
    v = np.arange(3 * sb)[None, :]
    tabs = []
    for j in (0, 1, n_blocks - 1):
        kpos = (j - 1) * sb + v
        ok = (kpos >= 0) & (kpos < n_blocks * sb) & (np.abs(j * sb + u - kpos) <= SWA_WINDOW)
        tabs.append(np.tile(np.where(ok, 0.0, NEG), (SWA_GROUP, 1)))
    return np.stack(tabs).astype(np.float32)


def _swa_kernel(n_steps, layer, sink_ref, q_ref, kp_ref, kq_ref, kn_ref, vp_ref, vq_ref, vn_ref,
                kc_ref, vc_ref, bias_ref, o_ref):
    j = pl.program_id(1)
    sb = SWA_BLOCK

    def attend(a, n, band):
        cols = slice(n * MXU_N, (n + 1) * MXU_N)
        rows = slice(a * sb, (a + 1) * sb)
        q4 = _head_stack(q_ref[rows, cols], SWA_GROUP)
        sink = jnp.concatenate([jnp.full((sb, 1), sink_ref[layer, n * SWA_GROUP + g], F32)
                                for g in range(SWA_GROUP)], axis=0)
        if band is None:
            keys, vals = kc_ref[:, cols], vc_ref[:, cols]
            s = _dot_nt(q4, keys)
        else:
            k_refs, v_refs, bias = band
            keys = jnp.concatenate([r[rs, cols] for r, rs in k_refs] + [kc_ref[:, cols]], axis=0)
            vals = jnp.concatenate([r[rs, cols] for r, rs in v_refs] + [vc_ref[:, cols]], axis=0)
            s = _dot_nt(q4, keys)
            s = jnp.concatenate([s[:, 0:3 * sb] + bias, s[:, 3 * sb:]], axis=1)
        m = jnp.maximum(jnp.max(s, axis=1, keepdims=True), sink)
        p = jnp.exp(s - m)
        den = jnp.sum(p, axis=1, keepdims=True) + jnp.exp(sink - m)
        o4 = _dot(p.astype(BF16), vals)
        o_ref[rows, cols] = _head_unstack(o4 * (1.0 / den), SWA_GROUP).astype(BF16)

    lo, hi, whole = slice(0, sb), slice(sb, 2 * sb), slice(0, sb)

    @pl.when(j < n_steps)
    def _():
        bands = [([(kp_ref, whole), (kq_ref, lo), (kq_ref, hi)], [(vp_ref, whole), (vq_ref, lo), (vq_ref, hi)],
                  bias_ref[jnp.where(j == 0, 0, 1)]),
                 ([(kq_ref, lo), (kq_ref, hi), (kn_ref, whole)], [(vq_ref, lo), (vq_ref, hi), (vn_ref, whole)],
                  bias_ref[jnp.where(j == n_steps - 1, 2, 1)])]
        for a in range(SWA_STEP_BLOCKS):
            for n in range(SWA_KV_HEADS):
                attend(a, n, bands[a])

    @pl.when(j >= n_steps)
    def _():
        for a in range(SWA_STEP_BLOCKS):
            for n in range(SWA_KV_HEADS):
                attend(a, n, None)


def _swa(sq, kx, vx, sink, layer, n_b, l_, ctx_len, with_ctx):
    nt = sq.shape[0]
    sb = SWA_BLOCK
    qb = SWA_STEP_BLOCKS * sb
    nb = l_ // sb
    ns = l_ // qb
    assert SWA_STEP_BLOCKS == 2 and nb % 2 == 0 and nb >= 4 and ctx_len == qb
    ctx0 = (n_b * l_) // qb
    bias = _swa_band_bias(nb)

    def q_idx(b, j):
        return (jnp.where(j < ns, b * ns + j, ctx0 + b), 0)

    def cur_idx(b, j):
        return (b * ns + jnp.minimum(j, ns - 1), 0)

    def edge_idx(off):
        return lambda b, j: (b * nb + jnp.clip(SWA_STEP_BLOCKS * jnp.minimum(j, ns - 1) + off, 0, nb - 1), 0)

    edge = lambda idx: pl.BlockSpec((sb, 512), idx)
    wide = lambda idx: pl.BlockSpec((qb, 512), idx)
    cspec = pl.BlockSpec((ctx_len, 512), lambda b, j: (ctx0 + b, 0))
    return pl.pallas_call(
        functools.partial(_swa_kernel, ns, layer),
        grid=(n_b, ns + (1 if with_ctx else 0)),
        in_specs=[pl.BlockSpec(memory_space=pltpu.SMEM),
                  wide(q_idx), edge(edge_idx(-1)), wide(cur_idx), edge(edge_idx(SWA_STEP_BLOCKS)),
                  edge(edge_idx(-1)), wide(cur_idx), edge(edge_idx(SWA_STEP_BLOCKS)), cspec, cspec,
                  pl.BlockSpec(bias.shape, lambda b, j: (0, 0, 0))],
        out_specs=wide(q_idx),
        out_shape=jax.ShapeDtypeStruct((nt if with_ctx else n_b * l_, 512), BF16),
        compiler_params=_params(("arbitrary", "arbitrary")),
        name="swa",
    )(sink, sq, kx, kx, kx, vx, vx, vx, kx, vx, bias)


def _layer_norm(r, g, b):
    mu = jnp.mean(r, axis=-1, keepdims=True)
    rc = r - mu
    var = jnp.mean(rc * rc, axis=-1, keepdims=True)
    return rc * lax.rsqrt(var + LN_EPS) * g + b


def _route(logits_t, bias_col):
    epg = N_EXPERTS // N_GROUPS
    s = _sigmoid(logits_t)
    sel = s + bias_col
    sel_r = [sel[e:e + 1, :] for e in range(N_EXPERTS)]
    s_r = [s[e:e + 1, :] for e in range(N_EXPERTS)]
    grp = []
    for g in range(N_GROUPS):
        a = sel_r[g * epg:(g + 1) * epg]
        m1 = functools.reduce(jnp.maximum, a)
        m2 = functools.reduce(jnp.maximum,
                              [jnp.minimum(a[i], a[k]) for i in range(epg) for k in range(i + 1, epg)])
        grp.append(m1 + m2)
    m_r, w_r = [], []
    for g in range(N_GROUPS):
        best = None
        for k in range(N_GROUPS):
            if k == g:
                continue
            c = (grp[g] > grp[k]) if k < g else (grp[g] >= grp[k])
            best = c if best is None else (best & c)
        for e in range(g * epg, (g + 1) * epg):
            rank = jnp.zeros_like(sel_r[e])
            for k in range(g * epg, (g + 1) * epg):
                if k == e:
                    continue
                ahead = (sel_r[k] >= sel_r[e]) if k < e else (sel_r[k] > sel_r[e])
                rank = rank + jnp.where(ahead, 1.0, 0.0)
            chosen = jnp.where(best & (rank < 1.5), 1.0, 0.0)
            m_r.append(chosen)
            w_r.append(chosen * s_r[e])
    inv = 1.0 / functools.reduce(lambda a, b: a + b, w_r)
    return m_r, [w * inv for w in w_r]


def _pack_bf16_pairs(a, b):
    ua = pltpu.bitcast(a.astype(BF16).astype(F32), jnp.uint32)
    ub = pltpu.bitcast(b.astype(BF16).astype(F32), jnp.uint32)
    return (ua >> 16) | ub


def _unpack_bf16_pairs(u):
    return (pltpu.bitcast(u << 16, F32), pltpu.bitcast(u & jnp.uint32(0xFFFF0000), F32))


def _outproj_kernel(n_lat, of_ref, ob_ref, hg_ref, yb_ref, yc_ref, xa_ref, xb_ref, mod_ref, wo_ref, pv_ref,
                    wrh_ref, wrl_ref, rb_ref, before_ref, x1_ref, h2u_ref, route_ref, gt_ref, cnt_ref):
    @pl.when(pl.program_id(0) == 0)
    def _():
        cnt_ref[...] = jnp.zeros(cnt_ref.shape, F32)

    row = lax.broadcasted_iota(jnp.int32, (HGRN_W, HGRN_W), 0)
    col = lax.broadcasted_iota(jnp.int32, (HGRN_W, HGRN_W), 1)
    head_ones = jnp.where((row // HEAD_DIM) == (col // HEAD_DIM), 1.0, 0.0).astype(BF16)
    pv = pv_ref[0]
    o = of_ref[...] + ob_ref[...]
    hi, lo = _split_bf16(o * o)
    ms = (_dot(hi, head_ones) + _dot(lo, head_ones)) * (1.0 / HEAD_DIM)
    gate = hg_ref[...].astype(F32)
    ya = o * lax.rsqrt(ms + RMS_EPS) * pv[2:3, 0:HGRN_W] * (gate * _sigmoid(gate))
    y = (_dot(ya.astype(BF16), wo_ref[0, 0:256, :]) + _dot(yb_ref[...], wo_ref[0, 256:512, :])
         + _dot(yc_ref[...], wo_ref[0, 512:1024, :]))
    mod = mod_ref[0, 0]
    x = jnp.where(pl.program_id(0) < n_lat, xa_ref[...], xb_ref[...])
    x1 = _layer_norm(ALPHA * x + mod[2:3] * y, pv[0:1], pv[1:2])
    x1_ref[...] = x1
    h2 = x1 * (1.0 + mod[4:5]) + mod[3:4]
    hh, hl = _split_bf16(h2)
    half = h2.shape[1] // 2
    h2u_ref[...] = _pack_bf16_pairs(h2[:, 0:half], h2[:, half:2 * half])
    wrh = wrh_ref[...]
    logits_t = _dot_nt(wrh, hh) + _dot_nt(wrh, hl) + _dot_nt(wrl_ref[...], hh)
    m_r, w_r = _route(logits_t, rb_ref[...])
    m_t = jnp.concatenate(m_r, axis=0)
    rank_t = _dot(m_t.astype(BF16), before_ref[...]) + cnt_ref[:, 0:1]
    cnt_ref[...] = cnt_ref[...] + jnp.sum(m_t, axis=1, keepdims=True)
    seen = jnp.zeros_like(m_r[0])
    e1 = e2 = r1 = r2 = g1 = g2 = jnp.zeros_like(m_r[0])
    for e in range(N_EXPERTS):
        is1 = m_r[e] * (1.0 - seen)
        is2 = m_r[e] * seen
        rk = rank_t[e:e + 1, :]
        e1, e2 = e1 + e * is1, e2 + e * is2
        r1, r2 = r1 + rk * is1, r2 + rk * is2
        g1, g2 = g1 + w_r[e] * is1, g2 + w_r[e] * is2
        seen = seen + is1
    zero = jnp.zeros_like(e1)
    route_ref[...] = jnp.concatenate([e1, e2, r1, r2, g1, g2, zero, zero], axis=0)
    pad = jnp.zeros((LANES - 2, g1.shape[1]), F32)
    gt_ref[...] = jnp.concatenate([g1, g2, pad], axis=0).T


def _outproj(o_f, o_b, hg, yb, yc, xa, xb, ctx_blk, mod, wo, pv, layer, wrh, wrl, rb, n_b, l_, n_tiles):
    d = xa.shape[1]
    tpb = l_ // TM
    n_lat = n_b * tpb
    before = np.triu(np.ones((TM, TM), np.float32), 1).astype(jnp.bfloat16)

    def mod_idx(j):
        return (layer, jnp.minimum(j // tpb, n_b), 0, 0)

    def rows(width):
        return pl.BlockSpec((TM, width), lambda j: (j, 0))

    full = lambda a: pl.BlockSpec(a.shape, lambda j: (0,) * a.ndim)
    of_layer = lambda a: pl.BlockSpec((1,) + a.shape[1:], lambda j: (layer,) + (0,) * (a.ndim - 1))
    nr = n_tiles * TM
    return pl.pallas_call(
        functools.partial(_outproj_kernel, n_lat),
        grid=(n_tiles,),
        in_specs=[rows(256), rows(256), rows(256), rows(256), rows(512),
                  pl.BlockSpec((TM, d), lambda j: (jnp.minimum(j, n_lat - 1), 0)),
                  pl.BlockSpec((TM, d), lambda j: (ctx_blk, 0)),
                  pl.BlockSpec((1, 1, 6, d), mod_idx), of_layer(wo), of_layer(pv), full(wrh), full(wrl), full(rb),
                  full(before)],
        out_specs=[rows(d), rows(d // 2), pl.BlockSpec((8, TM), lambda j: (0, j)), rows(LANES),
                   pl.BlockSpec((N_EXPERTS, LANES), lambda j: (0, 0))],
        out_shape=[jax.ShapeDtypeStruct((nr, d), F32), jax.ShapeDtypeStruct((nr, d // 2), jnp.uint32),
                   jax.ShapeDtypeStruct((8, nr), F32), jax.ShapeDtypeStruct((nr, LANES), F32),
                   jax.ShapeDtypeStruct((N_EXPERTS, LANES), F32)],
        compiler_params=_params(("arbitrary",)),
        name="outproj",
    )(o_f, o_b, hg, yb, yc, xa, xb, mod, wo, pv, wrh, wrl, rb, before)


def _sc_split(rows_per_worker):
    best = None
    for chunk in (64, 48, 32):
        for nbuf in range(SC_MAX_ROWS // chunk, 0, -1):
            if rows_per_worker % (chunk * nbuf) == 0 and (best is None or chunk * nbuf > best[0] * best[1]):
                best = (chunk, nbuf)
    return best


def _sc_workers():
    info = plsc.get_sparse_core_info()
    return info.num_cores, info.num_cores * info.num_subcores


def _sc_scatter_rows(table, pos, n_slots):
    n_tok, width = table.shape
    n_cores, n_workers = _sc_workers()
    per_w = n_tok // n_workers
    assert n_tok % n_workers == 0
    chunk, nbuf = _sc_split(per_w)
    n_chunks = per_w // chunk
    mesh = plsc.VectorSubcoreMesh(core_axis_name="c", subcore_axis_name="s")

    @functools.partial(
        pl.kernel, mesh=mesh,
        out_type=jax.ShapeDtypeStruct((n_slots, width), table.dtype),
        scratch_types=[pltpu.VMEM((2 * n_chunks, chunk), jnp.int32),
                       pltpu.VMEM((nbuf, chunk, width), table.dtype),
                       pltpu.SemaphoreType.DMA((nbuf,)),
                       pltpu.SemaphoreType.DMA((nbuf,))],
    )
    def scatter(table_hbm, pos_hbm, out_hbm, idx_v, rows_v, rsem, wsem):
        wid = lax.axis_index("s") * n_cores + lax.axis_index("c")
        base = wid * per_w
        pltpu.sync_copy(pos_hbm.at[wid], idx_v)

        @pl.loop(0, n_chunks // nbuf)
        def _(g):
            c0 = g * nbuf
            gets = [pltpu.async_copy(table_hbm.at[pl.ds(base + (c0 + b) * chunk, chunk)], rows_v.at[b],
                                     rsem.at[b]) for b in range(nbuf)]
            puts = []
            for b in range(nbuf):
                gets[b].wait()
                for k in range(2):
                    puts.append(pltpu.async_copy(rows_v.at[b], out_hbm.at[idx_v.at[k * n_chunks + c0 + b]],
                                                 wsem.at[b]))
            for put in puts:
                put.wait()

    pos_w = pos.reshape(2, n_workers, n_chunks, chunk).transpose(1, 0, 2, 3).reshape(n_workers, 2 * n_chunks, chunk)
    return scatter(table, pos_w)


def _sc_gather_rows(table, idx):
    n_out = idx.shape[0]
    width = table.shape[1]
    n_cores, n_workers = _sc_workers()
    per_w = n_out // n_workers
    assert n_out % n_workers == 0
    chunk, nbuf = _sc_split(per_w)
    mesh = plsc.VectorSubcoreMesh(core_axis_name="c", subcore_axis_name="s")

    @functools.partial(
        pl.kernel, mesh=mesh,
        out_type=jax.ShapeDtypeStruct((n_out, width), table.dtype),
        scratch_types=[pltpu.VMEM((per_w,), jnp.int32),
                       pltpu.VMEM((nbuf, chunk, width), table.dtype),
                       pltpu.SemaphoreType.DMA((nbuf,)),
                       pltpu.SemaphoreType.DMA((nbuf,))],
    )
    def gather(table_hbm, idx_hbm, out_hbm, idx_v, rows_v, gsem, wsem):
        wid = lax.axis_index("s") * n_cores + lax.axis_index("c")
        base = wid * per_w
        pltpu.sync_copy(idx_hbm.at[pl.ds(base, per_w)], idx_v)

        @pl.loop(0, per_w // (chunk * nbuf))
        def _(g):
            off = g * (chunk * nbuf)
            gets = [pltpu.async_copy(table_hbm.at[idx_v.at[pl.ds(off + b * chunk, chunk)]], rows_v.at[b],
                                     gsem.at[b]) for b in range(nbuf)]
            puts = []
            for b in range(nbuf):
                gets[b].wait()
                puts.append(pltpu.async_copy(rows_v.at[b], out_hbm.at[pl.ds(base + off + b * chunk, chunk)],
                                             wsem.at[b]))
            for put in puts:
                put.wait()

    return gather(table, idx)


def _gmm_kernel(te_ref, nv_ref, nu_ref, xs_ref, wg_ref, wu_ref, wd_ref, ys_ref, wgb_ref, wub_ref, wdb_ref):
    i = pl.program_id(0)
    used = i < nu_ref[0]
    fresh = jnp.logical_or(i == 0, te_ref[i] != te_ref[jnp.maximum(i - 1, 0)])

    @pl.when(jnp.logical_and(used, fresh))
    def _():
        wgb_ref[...] = wg_ref[0, 0].astype(BF16)
        wub_ref[...] = wu_ref[0, 0].astype(BF16)
        wdb_ref[...] = wd_ref[0, 0].astype(BF16)

    @pl.when(used)
    def _():
        rowid = lax.broadcasted_iota(jnp.int32, xs_ref.shape, 0)
        lo, hi = _unpack_bf16_pairs(jnp.where(rowid < nv_ref[i], xs_ref[...], jnp.uint32(0)))
        lo, hi = lo.astype(BF16), hi.astype(BF16)
        half = lo.shape[1]
        a = _dot(lo, wgb_ref[0:half, :]) + _dot(hi, wgb_ref[half:2 * half, :])
        u = _dot(lo, wub_ref[0:half, :]) + _dot(hi, wub_ref[half:2 * half, :])
        y = _dot((a * _sigmoid(a) * u).astype(BF16), wdb_ref[...])
        ys_ref[...] = _pack_bf16_pairs(y[:, 0:half], y[:, half:2 * half])

    @pl.when(jnp.logical_not(used))
    def _():
        ys_ref[...] = jnp.zeros(ys_ref.shape, jnp.uint32)


def _gmm(tile_expert, tile_valid, n_used, xs, wg, wu, wd, layer):
    p_rows, half = xs.shape
    d = 2 * half
    wspec = lambda r, c: pl.BlockSpec((1, 1, r, c), lambda i, te, nv, nu: (layer, te[i], 0, 0))
    return pl.pallas_call(
        _gmm_kernel,
        grid_spec=pltpu.PrefetchScalarGridSpec(
            num_scalar_prefetch=3,
            grid=(p_rows // TM,),
            in_specs=[pl.BlockSpec((TM, half), lambda i, te, nv, nu: (i, 0)),
                      wspec(d, D_EXPERT), wspec(d, D_EXPERT), wspec(D_EXPERT, d)],
            out_specs=pl.BlockSpec((TM, half), lambda i, te, nv, nu: (i, 0)),
            scratch_shapes=[pltpu.VMEM((d, D_EXPERT), BF16), pltpu.VMEM((d, D_EXPERT), BF16),
                            pltpu.VMEM((D_EXPERT, d), BF16)]),
        out_shape=jax.ShapeDtypeStruct((p_rows, half), jnp.uint32),
        compiler_params=_params(("arbitrary",)),
        name="moe_experts",
    )(tile_expert, tile_valid, n_used, xs, wg, wu, wd)


def _moefin_kernel(o1_ref, o2_ref, gt_ref, x_ref, mod_ref, pv_ref, out_ref):
    gt = gt_ref[...]
    g1, g2 = gt[:, 0:1], gt[:, 1:2]
    a1, b1 = _unpack_bf16_pairs(o1_ref[...])
    a2, b2 = _unpack_bf16_pairs(o2_ref[...])
    f = jnp.concatenate([g1 * a1 + g2 * a2, g1 * b1 + g2 * b2], axis=1)
    mod = mod_ref[0, 0]
    pv = pv_ref[0]
    out_ref[...] = _layer_norm(ALPHA * x_ref[...] + mod[5:6] * f, pv[3:4], pv[4:5])


def _moefin(o12, gt, x1, mod, pv, layer, n_b, l_, n_tiles):
    d = x1.shape[1]
    tpb = l_ // TM

    def mod_idx(j):
        return (layer, jnp.minimum(j // tpb, n_b), 0, 0)

    return pl.pallas_call(
        _moefin_kernel,
        grid=(n_tiles,),
        in_specs=[pl.BlockSpec((TM, d // 2), lambda j: (j, 0)),
                  pl.BlockSpec((TM, d // 2), lambda j: (j + n_tiles, 0)),
                  pl.BlockSpec((TM, LANES), lambda j: (j, 0)),
                  pl.BlockSpec((TM, d), lambda j: (j, 0)),
                  pl.BlockSpec((1, 1, 6, d), mod_idx),
                  pl.BlockSpec((1,) + pv.shape[1:], lambda j: (layer, 0, 0))],
        out_specs=pl.BlockSpec((TM, d), lambda j: (j, 0)),
        out_shape=jax.ShapeDtypeStruct((n_tiles * TM, d), F32),
        compiler_params=_params(("arbitrary",)),
        name="moe_combine",
    )(o12, o12, gt, x1, mod, pv)


def _moe_routed(h2u, route, gt, counts, wg, wu, wd, layer, x1, mod, pv, n_b, l_, n_tiles):
    n_tok = n_tiles * TM
    n_slots = -(-(2 * n_tok + N_EXPERTS * TM) // SC_ROW_QUANTUM) * SC_ROW_QUANTUM
    cnt = counts[:, 0].astype(jnp.int32)
    padded = ((cnt + TM - 1) // TM) * TM
    upto = jnp.arange(N_EXPERTS)[None, :] <= jnp.arange(N_EXPERTS)[:, None]
    ends = jnp.sum(jnp.where(upto, padded[None, :], 0), axis=1)
    offs = ends - padded
    e12 = route[0:2].astype(jnp.int32)
    r12 = route[2:4].astype(jnp.int32)
    onehot = e12[:, :, None] == jnp.arange(N_EXPERTS, dtype=jnp.int32)
    pos = r12 + jnp.sum(jnp.where(onehot, offs, 0), axis=-1)
    tile_start = jnp.arange(n_slots // TM, dtype=jnp.int32) * TM
    tile_expert = jnp.minimum(jnp.sum(tile_start[:, None] >= ends[None, :], axis=1), N_EXPERTS - 1)
    pick = tile_expert[:, None] == jnp.arange(N_EXPERTS, dtype=jnp.int32)
    tile_valid = jnp.clip(jnp.sum(jnp.where(pick, (offs + cnt)[None, :], 0), axis=1) - tile_start, 0, TM)
    n_used = (ends[N_EXPERTS - 1] // TM).reshape(1)
    xs = _sc_scatter_rows(h2u, pos, n_slots)
    ys = _gmm(tile_expert.astype(jnp.int32), tile_valid.astype(jnp.int32), n_used.astype(jnp.int32),
              xs, wg, wu, wd, layer)
    o12 = _sc_gather_rows(ys, pos.reshape(2 * n_tok))
    return _moefin(o12, gt, x1, mod, pv, layer, n_b, l_, n_tiles)


def _rope_tables(l_):
    pos = np.arange(l_)
    nf = HEAD_DIM // 4
    inv = ROPE_BASE ** (-np.arange(nf, dtype=np.float64) / nf)
    ar = (pos // GRID_W)[:, None] * inv
    ac = (pos % GRID_W)[:, None] * inv
    cos = np.concatenate([np.cos(ar), np.cos(ar), np.cos(ac), np.cos(ac)], axis=1)
    sin = np.concatenate([-np.sin(ar), np.sin(ar), -np.sin(ac), np.sin(ac)], axis=1)
    cos = np.concatenate([np.tile(cos, (1, 2)), np.ones((TM, LANES))], axis=0)
    sin = np.concatenate([np.tile(sin, (1, 2)), np.zeros((TM, LANES))], axis=0)
    return cos.astype(np.float32), sin.astype(np.float32)


def _na_bias(rpb):
    u = np.arange(GRID_W)[:, None]
    v = np.arange(GRID_W)[None, :]
    cs = np.clip(u - NA_WIN_COLS // 2, 0, GRID_W - NA_WIN_COLS)
    colmask = (v >= cs) & (v < cs + NA_WIN_COLS)
    coff = np.clip(v - u + NA_WIN_COLS - 1, 0, 2 * NA_WIN_COLS - 2)
    n_off = 2 * NA_WIN_COLS - 1
    pick_col = (coff[None] == np.arange(n_off)[:, None, None]).astype(np.float32)
    ridx = np.arange(NA_WIN_ROWS)[None, :] - np.arange(NA_WIN_ROWS)[:, None] + NA_WIN_ROWS - 1
    pick_row = (ridx[:, :, None] == np.arange(2 * NA_WIN_ROWS - 1)).astype(np.float32)
    rows = jnp.einsum('dia,lhab->ldhib', pick_row, rpb.astype(F32), precision=lax.Precision.HIGHEST)
    tab = jnp.einsum('ldhib,buv->ldhuiv', rows, pick_col, precision=lax.Precision.HIGHEST)
    tab = jnp.where(colmask[None, None, None, :, None, :], tab, NEG)
    return tab.reshape(rpb.shape[0], NA_WIN_ROWS, NA_HEADS * GRID_W, NA_WIN_ROWS * GRID_W)


def kernel(x, c, ctx, c_ctx, w_ada, b_ada, w_in, lb_logits, hgrn_norm, na_rpb, swa_sink, w_out,
           ln1_g, ln1_b, ln2_g, ln2_b, w_router, router_bias, w_gate, w_up, w_down):
    n_b, l_, d = x.shape
    ctx_len = ctx.shape[1]
    depth = w_ada.shape[0]
    assert n_b * ctx_len == TM and l_ % TM == 0 and n_b + 1 <= N_COND
    n_lat_tiles = (n_b * l_) // TM

    p_lb = jax.nn.softmax(lb_logits.astype(F32), axis=0).reshape(depth, 2 * HGRN_W)
    upto = (np.arange(depth)[None, :] <= np.arange(depth)[:, None]).astype(np.float32)
    lb = jnp.maximum(jnp.sum(upto[:, :, None] * p_lb[None], axis=1) - p_lb[0:1], LB_MIN)
    gp = jnp.stack([jnp.log(lb), jnp.log1p(-lb), 1.0 - lb] + [jnp.zeros_like(lb)] * 5, axis=1)

    cond = jnp.concatenate([c, c_ctx[None, :], jnp.zeros((N_COND - n_b - 1, d), F32)], axis=0)
    mod = _ada(cond.T, w_ada, b_ada, n_b + 1).reshape(depth, N_COND, 6, d)

    cos_t, sin_t = _rope_tables(l_)
    wr_t = w_router.T
    wrh = wr_t.astype(BF16)
    wrl = (wr_t - wrh.astype(F32)).astype(BF16)
    rb = router_bias.astype(F32)[:, None]
    w_in_b = w_in.astype(BF16)
    w_out_b = w_out.astype(BF16)
    zeros = jnp.zeros_like(ln1_g)
    pv = jnp.stack([ln1_g, ln1_b, jnp.tile(hgrn_norm, (1, d // HEAD_DIM)), ln2_g, ln2_b, zeros, zeros, zeros],
                   axis=1).astype(F32)
    na_bias = _na_bias(na_rpb)
    sink = swa_sink.astype(F32)

    xa, xb, ctx_blk = x.reshape(n_b * l_, d), ctx.reshape(n_b * ctx_len, d), 0
    for l in range(depth):
        last = l == depth - 1
        hq, hv, hg, lf, kk, nqkv, sq, kx, vx = _inproj(xa, xb, ctx_blk, mod, w_in_b, gp, l, cos_t, sin_t, n_b, l_)
        o_f, o_b = _hgrn(hq, hv, lf, kk, n_b, l_, ctx_len)
        yb = _na(nqkv, na_bias, l, n_b, l_, ctx_len, not last)
        yc = _swa(sq, kx, vx, sink, l, n_b, l_, ctx_len, not last)
        n_tiles = n_lat_tiles if last else n_lat_tiles + 1
        x1, h2u, route, gt, counts = _outproj(o_f, o_b, hg, yb, yc, xa, xb, ctx_blk, mod, w_out_b, pv, l,
                                              wrh, wrl, rb, n_b, l_, n_tiles)
        xa = _moe_routed(h2u, route, gt, counts, w_gate, w_up, w_down, l, x1, mod, pv, n_b, l_, n_tiles)
        xb, ctx_blk = xa, n_lat_tiles
    return xa[:n_b * l_].reshape(n_b, l_, d)
```

```python
import functools

import numpy as np
import jax
import jax.numpy as jnp
from jax import lax
from jax.experimental import pallas as pl
from jax.experimental.pallas import tpu as pltpu
from jax.experimental.pallas import tpu_sc as plsc

F32 = jnp.float32
BF16 = jnp.bfloat16

D_MODEL = 1024
GRID_W = 64
HEAD_DIM = 64
HGRN_W = 256
NA_HEADS = 4
NA_WIN_ROWS = 8
NA_WIN_COLS = 16
SWA_Q_HEADS = 8
SWA_KV_HEADS = 2
SWA_GROUP = 4
SWA_WINDOW = 128
SWA_BLOCK = 128
ROPE_BASE = 10000.0
N_EXPERTS = 16
N_GROUPS = 4
D_EXPERT = 512
LN_EPS = 1e-5
RMS_EPS = 1e-6
NEG = -1e30
LB_MIN = 1e-6
DEPTH = 2
ALPHA = (2.0 * DEPTH) ** 0.25

LANES = 128
MXU_N = 256
TM = 512
HB = 256
HC = 16
NA_QROWS = 4
SWA_STEP_BLOCKS = 2
VMEM_LIMIT = 56 * 1024 * 1024

SC_MAX_ROWS = 192
SC_ROW_QUANTUM = 2048

N_COND = 8


def _dot(a, b):
    return jnp.dot(a, b, preferred_element_type=F32)


def _dot_nt(a, b):
    return lax.dot_general(a, b, (((1,), (1,)), ((), ())), preferred_element_type=F32)


def _dot_tn(a, b):
    return lax.dot_general(a, b, (((0,), (0,)), ((), ())), preferred_element_type=F32)


def _sigmoid(x):
    return 1.0 / (1.0 + jnp.exp(-x))


def _split_bf16(x):
    hi = x.astype(BF16)
    lo = (x - hi.astype(F32)).astype(BF16)
    return hi, lo


def _params(sem):
    return pltpu.CompilerParams(dimension_semantics=sem, vmem_limit_bytes=VMEM_LIMIT)


def _ada_kernel(n_rows, condt_ref, w_ref, b_ref, o_ref):
    c = condt_ref[...]
    s = c * _sigmoid(c)
    w = w_ref[0]
    rows = [jnp.sum(w * s[:, r:r + 1], axis=0, keepdims=True) for r in range(n_rows)]
    rows.append(jnp.zeros((N_COND - n_rows, w.shape[1]), F32))
    o_ref[0] = jnp.concatenate(rows, axis=0) + b_ref[0]


def _ada(cond_t, w_ada, b_ada, n_rows):
    depth, d, n6 = w_ada.shape
    tn = 1024
    return pl.pallas_call(
        functools.partial(_ada_kernel, n_rows),
        grid=(depth, n6 // tn),
        in_specs=[pl.BlockSpec((d, N_COND), lambda l, n: (0, 0)),
                  pl.BlockSpec((1, d, tn), lambda l, n: (l, 0, n)),
                  pl.BlockSpec((1, 1, tn), lambda l, n: (l, 0, n))],
        out_specs=pl.BlockSpec((1, N_COND, tn), lambda l, n: (l, 0, n)),
        out_shape=jax.ShapeDtypeStruct((depth, N_COND, n6), F32),
        compiler_params=_params(("arbitrary", "arbitrary")),
        name="ada",
    )(cond_t, w_ada, b_ada.reshape(depth, 1, n6))


def _rope(z, cos, sin, first):
    sw = jnp.where(first, pltpu.roll(z, LANES - 16, axis=1), pltpu.roll(z, 16, axis=1))
    return z * cos + sw * sin


def _inproj_kernel(n_lat, xa_ref, xb_ref, mod_ref, w_ref, gp_ref, cos_ref, sin_ref,
                   hq_ref, hv_ref, hg_ref, lf_ref, kk_ref, nqkv_ref, sq_ref, kx_ref, vx_ref):
    mod = mod_ref[0, 0]
    x = jnp.where(pl.program_id(0) < n_lat, xa_ref[...], xb_ref[...])
    h = (x * (1.0 + mod[1:2]) + mod[0:1]).astype(BF16)
    gp = gp_ref[0]
    cos = cos_ref[...]
    sin = sin_ref[...]
    lane = lax.broadcasted_iota(jnp.int32, cos.shape, 1)
    first = (lane & 16) == 0
    scale = HEAD_DIM ** -0.5

    def chunk(c):
        return _dot(h, w_ref[0, :, c * MXU_N:(c + 1) * MXU_N])

    def per_query_head(z2):
        swapped = pltpu.roll(z2, HEAD_DIM, axis=1)
        low = lane < HEAD_DIM
        h0 = jnp.where(low, z2, swapped)
        h1 = jnp.where(low, swapped, z2)
        return jnp.concatenate([h0, h0], axis=1), jnp.concatenate([h1, h1], axis=1)

    def gates(z, d):
        cols = slice(d * MXU_N, (d + 1) * MXU_N)
        a = gp[0:1, cols]
        b = gp[1:2, cols] + (jnp.minimum(z, 0.0) - jnp.log1p(jnp.exp(-jnp.abs(z))))
        logf = jnp.maximum(a, b) + jnp.log1p(jnp.exp(-jnp.abs(a - b)))
        k = gp[2:3, cols] * (1.0 / (1.0 + jnp.exp(z)))
        return logf, k

    def rope2(z):
        return jnp.concatenate([_rope(z[:, 0:LANES], cos, sin, first),
                                _rope(z[:, LANES:2 * LANES], cos, sin, first)], axis=1)

    hq_ref[...] = chunk(0).astype(BF16)
    for d in range(2):
        logf, k = gates(chunk(1 + d), d)
        lf_ref[:, d * MXU_N:(d + 1) * MXU_N] = logf
        kk_ref[:, d * MXU_N:(d + 1) * MXU_N] = k.astype(BF16)
    hv_ref[...] = chunk(3).astype(BF16)
    hg_ref[...] = chunk(4).astype(BF16)
    nqkv_ref[:, 0:MXU_N] = (chunk(5) * scale).astype(BF16)
    nqkv_ref[:, MXU_N:2 * MXU_N] = chunk(6).astype(BF16)
    nqkv_ref[:, 2 * MXU_N:3 * MXU_N] = chunk(7).astype(BF16)
    for d in range(2):
        sq_ref[:, d * MXU_N:(d + 1) * MXU_N] = (rope2(chunk(8 + d)) * scale).astype(BF16)
    zkv = chunk(10)
    k0, k1 = per_query_head(_rope(zkv[:, 0:LANES], cos, sin, first))
    v0, v1 = per_query_head(zkv[:, LANES:2 * LANES])
    kx_ref[:, 0:MXU_N] = k0.astype(BF16)
    kx_ref[:, MXU_N:2 * MXU_N] = k1.astype(BF16)
    vx_ref[:, 0:MXU_N] = v0.astype(BF16)
    vx_ref[:, MXU_N:2 * MXU_N] = v1.astype(BF16)


def _inproj(xa, xb, ctx_blk, mod, w, gp, layer, cos_t, sin_t, n_b, l_):
    d = xa.shape[1]
    tpb = l_ // TM
    n_lat = n_b * tpb
    nt = (n_lat + 1) * TM

    def mod_idx(j):
        return (layer, jnp.minimum(j // tpb, n_b), 0, 0)

    def rope_idx(j):
        return (jnp.where(j < n_lat, j % tpb, tpb), 0)

    def rows(width):
        return pl.BlockSpec((TM, width), lambda j: (j, 0))

    widths = [(HGRN_W, BF16), (HGRN_W, BF16), (HGRN_W, BF16), (2 * HGRN_W, F32), (2 * HGRN_W, BF16),
              (3 * 256, BF16), (512, BF16), (512, BF16), (512, BF16)]
    return pl.pallas_call(
        functools.partial(_inproj_kernel, n_lat),
        grid=(nt // TM,),
        in_specs=[pl.BlockSpec((TM, d), lambda j: (jnp.minimum(j, n_lat - 1), 0)),
                  pl.BlockSpec((TM, d), lambda j: (ctx_blk, 0)),
                  pl.BlockSpec((1, 1, 6, d), mod_idx),
                  pl.BlockSpec((1,) + w.shape[1:], lambda j: (layer, 0, 0)),
                  pl.BlockSpec((1,) + gp.shape[1:], lambda j: (layer, 0, 0)),
                  pl.BlockSpec((TM, LANES), rope_idx),
                  pl.BlockSpec((TM, LANES), rope_idx)],
        out_specs=[rows(wd) for wd, _ in widths],
        out_shape=[jax.ShapeDtypeStruct((nt, wd), dt) for wd, dt in widths],
        compiler_params=_params(("arbitrary",)),
        name="inproj",
    )(xa, xb, mod, w, gp, cos_t, sin_t)


def _hgrn_consts():
    t = np.arange(HB)[:, None]
    u = np.arange(HB)[None, :]
    same = (t // HC) == (u // HC)
    tr, ur = t % HC, u % HC
    half = HC // 2
    mats, codes = [], []
    for fwd in (True, False):
        if fwd:
            incl = ur <= tr
            mid = ur <= (tr // half) * half + half // 2 - 1
            edge = ur <= half - 1
            code = np.where((t // half == u // half) & (u <= t), 1, np.where(same & (tr >= half) & (ur < half), 2, 0))
        else:
            incl = ur >= tr
            mid = ur >= (tr // half) * half + half // 2
            edge = ur >= half
            code = np.where((t // half == u // half) & (u >= t), 1, np.where(same & (tr < half) & (ur >= half), 2, 0))
        mats.append(np.concatenate([same & incl, same, same & mid, same & edge], axis=0))
        codes.append(code)
    return (np.stack(mats).astype(np.float32).astype(jnp.bfloat16), np.stack(codes).astype(np.float32))


def _hgrn_kernel(qf_ref, vf_ref, lff_ref, kf_ref, qb_ref, vb_ref, lfb_ref, kb_ref, cm_ref, code_ref,
                 of_ref, ob_ref, st_ref, qd_ref, kd_ref, vv_ref, gt_ref, it_ref):
    nch = HB // HC
    n_heads = HGRN_W // HEAD_DIM

    @pl.when(pl.program_id(1) == 0)
    def _():
        st_ref[...] = jnp.zeros(st_ref.shape, F32)

    lane3 = lax.broadcasted_iota(jnp.int32, (nch, HC, HGRN_W), 2) // HEAD_DIM

    def head_rows(x):
        x3 = x.astype(BF16).reshape(nch, HC, HGRN_W)
        zero = jnp.zeros_like(x3)
        return jnp.concatenate([jnp.where(lane3 == h, x3, zero) for h in range(n_heads)], axis=1)

    def prep(d, q_ref, v_ref, lf_ref, k_ref):
        hi, lo = _split_bf16(lf_ref[...])
        sums = _dot(cm_ref[d], hi) + _dot(cm_ref[d], lo)
        cum, tot = sums[0:HB], sums[HB:2 * HB]
        mid, edge = sums[2 * HB:3 * HB], sums[3 * HB:4 * HB]
        q = q_ref[...].astype(F32)
        k = k_ref[...].astype(F32)
        v = v_ref[...]
        qd_ref[d] = head_rows(q * jnp.exp(cum))
        kd_ref[d] = head_rows(k * jnp.exp(tot - cum))
        vv_ref[d] = jnp.concatenate([v[:, h * HEAD_DIM:(h + 1) * HEAD_DIM].reshape(nch, HC, HEAD_DIM)
                                     for h in range(n_heads)], axis=1)
        gt_ref[d] = jnp.exp(tot)
        q1 = (q * jnp.exp(cum - mid)).astype(BF16)
        k1 = (k * jnp.exp(mid - cum)).astype(BF16)
        q2 = (q * jnp.exp(jnp.minimum(cum - edge, 0.0))).astype(BF16)
        k2 = (k * jnp.exp(jnp.minimum(edge - cum, 0.0))).astype(BF16)
        s1 = _dot_nt(_head_stack(q1, n_heads), k1)
        s2 = _dot_nt(_head_stack(q2, n_heads), k2)
        code = jnp.concatenate([code_ref[d]] * n_heads, axis=0)
        p = jnp.where(code == 1.0, s1, jnp.where(code == 2.0, s2, 0.0))
        return _head_unstack(_dot(p.astype(BF16), v), n_heads)

    od_f = prep(0, qf_ref, vf_ref, lff_ref, kf_ref)
    od_b = prep(1, qb_ref, vb_ref, lfb_ref, kb_ref)

    for i in range(nch):
        for d, c in ((0, i), (1, nch - 1 - i)):
            st = st_ref[d]
            it_ref[d, c] = _dot_nt(qd_ref[d, c], st.astype(BF16))
            upd = _dot_tn(vv_ref[d, c], kd_ref[d, c])
            st_ref[d] = st * gt_ref[d, c * HC:c * HC + 1, :] + upd

    def inter(d):
        it = it_ref[d]
        return jnp.concatenate([it[:, h * HC:(h + 1) * HC, :].reshape(HB, HEAD_DIM) for h in range(n_heads)],
                               axis=1)

    of_ref[...] = inter(0) + od_f
    ob_ref[...] = inter(1) + od_b


def _hgrn(hq, hv, lf, kk, n_b, l_, ctx_len):
    nt = hq.shape[0]
    assert ctx_len == HB
    nlb = l_ // HB
    ctx0 = n_b * nlb
    cmats, codes = _hgrn_consts()
    heads = HGRN_W // HEAD_DIM

    def fwd_idx(col):
        return lambda b, s: (jnp.where(s == 0, ctx0 + b, b * nlb + s - 1), col)

    def bwd_idx(col):
        return lambda b, s: (jnp.where(s == 0, ctx0 + b, b * nlb + nlb - s), col)

    def blk(idx):
        return pl.BlockSpec((HB, HGRN_W), idx)

    full = lambda a: pl.BlockSpec(a.shape, lambda b, s: (0,) * a.ndim)
    return pl.pallas_call(
        _hgrn_kernel,
        grid=(n_b, nlb + 1),
        in_specs=[blk(fwd_idx(0)), blk(fwd_idx(0)), blk(fwd_idx(0)), blk(fwd_idx(0)),
                  blk(bwd_idx(0)), blk(bwd_idx(0)), blk(bwd_idx(1)), blk(bwd_idx(1)),
                  full(cmats), full(codes)],
        out_specs=[blk(fwd_idx(0)), blk(bwd_idx(0))],
        out_shape=[jax.ShapeDtypeStruct((nt, HGRN_W), F32)] * 2,
        scratch_shapes=[pltpu.VMEM((2, HEAD_DIM, HGRN_W), F32),
                        pltpu.VMEM((2, HB // HC, heads * HC, HGRN_W), BF16),
                        pltpu.VMEM((2, HB // HC, heads * HC, HGRN_W), BF16),
                        pltpu.VMEM((2, HB // HC, heads * HC, HEAD_DIM), BF16),
                        pltpu.VMEM((2, HB, HGRN_W), F32),
                        pltpu.VMEM((2, HB // HC, heads * HC, HEAD_DIM), F32)],
        compiler_params=_params(("arbitrary", "arbitrary")),
        name="hgrn",
    )(hq, hv, lf, kk, hq, hv, lf, kk, cmats, codes)


def _head_stack(q, n_heads):
    lane = lax.broadcasted_iota(jnp.int32, q.shape, 1)
    zero = jnp.zeros_like(q)
    return jnp.concatenate([jnp.where(lane // HEAD_DIM == h, q, zero) for h in range(n_heads)], axis=0)


def _head_unstack(o, n_heads):
    rows = o.shape[0] // n_heads
    lane = lax.broadcasted_iota(jnp.int32, (rows, o.shape[1]), 1)
    acc = jnp.zeros((rows, o.shape[1]), F32)
    for h in range(n_heads):
        acc = acc + jnp.where(lane // HEAD_DIM == h, o[h * rows:(h + 1) * rows], 0.0)
    return acc


def _na_kernel(n_rows, n_lat_steps, q_ref, k_ref, v_ref, kc_ref, vc_ref, bias_ref, o_ref):
    j = pl.program_id(1)
    kc = kc_ref[...]
    vc = vc_ref[...]
    nwin = NA_WIN_ROWS * GRID_W

    def attend(rr, win):
        q4 = _head_stack(q_ref[rr * GRID_W:(rr + 1) * GRID_W, :], NA_HEADS)
        s_ctx = _dot_nt(q4, kc)
        m = jnp.max(s_ctx, axis=1, keepdims=True)
        if win is not None:
            kw, vw, bias = win
            s_win = _dot_nt(q4, kw) + bias
            m = jnp.maximum(m, jnp.max(s_win, axis=1, keepdims=True))
            p_win = jnp.exp(s_win - m)
        p_ctx = jnp.exp(s_ctx - m)
        den = jnp.sum(p_ctx, axis=1, keepdims=True)
        o4 = _dot(p_ctx.astype(BF16), vc)
        if win is not None:
            den = den + jnp.sum(p_win, axis=1, keepdims=True)
            o4 = o4 + _dot(p_win.astype(BF16), vw)
        o = _head_unstack(o4 * (1.0 / den), NA_HEADS)
        o_ref[rr * GRID_W:(rr + 1) * GRID_W, :] = o.astype(BF16)

    @pl.when(j < n_lat_steps)
    def _():
        for rr in range(NA_QROWS):
            r = j * NA_QROWS + rr
            rs = jnp.clip(r - NA_WIN_ROWS // 2, 0, n_rows - NA_WIN_ROWS)
            start = pl.multiple_of(rs * GRID_W, GRID_W)
            kw = k_ref[pl.ds(start, nwin), :]
            vw = v_ref[pl.ds(start, nwin), :]
            attend(rr, (kw, vw, bias_ref[0, r - rs]))

    @pl.when(j >= n_lat_steps)
    def _():
        for rr in range(NA_QROWS):
            attend(rr, None)


def _na(nqkv, bias, layer, n_b, l_, ctx_len, with_ctx):
    nt = nqkv.shape[0]
    qb = NA_QROWS * GRID_W
    assert ctx_len == qb
    n_lat = l_ // qb
    ctx0 = n_b * n_lat
    n_rows = l_ // GRID_W

    def q_idx(b, j):
        return (jnp.where(j < n_lat, b * n_lat + j, ctx0 + b), 0)

    cblk = (n_b * l_) // ctx_len
    return pl.pallas_call(
        functools.partial(_na_kernel, n_rows, n_lat),
        grid=(n_b, n_lat + (1 if with_ctx else 0)),
        in_specs=[pl.BlockSpec((qb, 256), q_idx),
                  pl.BlockSpec((l_, 256), lambda b, j: (b, 1)),
                  pl.BlockSpec((l_, 256), lambda b, j: (b, 2)),
                  pl.BlockSpec((ctx_len, 256), lambda b, j: (cblk + b, 1)),
                  pl.BlockSpec((ctx_len, 256), lambda b, j: (cblk + b, 2)),
                  pl.BlockSpec((1,) + bias.shape[1:], lambda b, j: (layer, 0, 0, 0))],
        out_specs=pl.BlockSpec((qb, 256), q_idx),
        out_shape=jax.ShapeDtypeStruct((nt if with_ctx else n_b * l_, 256), BF16),
        compiler_params=_params(("arbitrary", "arbitrary")),
        name="natten",
    )(nqkv, nqkv, nqkv, nqkv, nqkv, bias)


def _swa_band_bias(n_blocks):
    sb = SWA_BLOCK
    u = np.arange(sb)[:, None]
    v = np.arange(3 * sb)[None, :]
    tabs = []
    for j in (0, 1, n_blocks - 1):
        kpos = (j - 1) * sb + v
        ok = (kpos >= 0) & (kpos < n_blocks * sb) & (np.abs(j * sb + u - kpos) <= SWA_WINDOW)
        tabs.append(np.tile(np.where(ok, 0.0, NEG), (SWA_GROUP, 1)))
    return np.stack(tabs).astype(np.float32)


def _swa_kernel(n_steps, layer, sink_ref, q_ref, kp_ref, kq_ref, kn_ref, vp_ref, vq_ref, vn_ref,
                kc_ref, vc_ref, bias_ref, o_ref):
    j = pl.program_id(1)
    sb = SWA_BLOCK

    def attend(a, n, band):
        cols = slice(n * MXU_N, (n + 1) * MXU_N)
        rows = slice(a * sb, (a + 1) * sb)
        q4 = _head_stack(q_ref[rows, cols], SWA_GROUP)
        sink = jnp.concatenate([jnp.full((sb, 1), sink_ref[layer, n * SWA_GROUP + g], F32)
                                for g in range(SWA_GROUP)], axis=0)
        if band is None:
            keys, vals = kc_ref[:, cols], vc_ref[:, cols]
            s = _dot_nt(q4, keys)
        else:
            k_refs, v_refs, bias = band
            keys = jnp.concatenate([r[rs, cols] for r, rs in k_refs] + [kc_ref[:, cols]], axis=0)
            vals = jnp.concatenate([r[rs, cols] for r, rs in v_refs] + [vc_ref[:, cols]], axis=0)
            s = _dot_nt(q4, keys)
            s = jnp.concatenate([s[:, 0:3 * sb] + bias, s[:, 3 * sb:]], axis=1)
        m = jnp.maximum(jnp.max(s, axis=1, keepdims=True), sink)
        p = jnp.exp(s - m)
        den = jnp.sum(p, axis=1, keepdims=True) + jnp.exp(sink - m)
        o4 = _dot(p.astype(BF16), vals)
        o_ref[rows, cols] = _head_unstack(o4 * (1.0 / den), SWA_GROUP).astype(BF16)

    lo, hi, whole = slice(0, sb), slice(sb, 2 * sb), slice(0, sb)

    @pl.when(j < n_steps)
    def _():
        bands = [([(kp_ref, whole), (kq_ref, lo), (kq_ref, hi)], [(vp_ref, whole), (vq_ref, lo), (vq_ref, hi)],
                  bias_ref[jnp.where(j == 0, 0, 1)]),
                 ([(kq_ref, lo), (kq_ref, hi), (kn_ref, whole)], [(vq_ref, lo), (vq_ref, hi), (vn_ref, whole)],
                  bias_ref[jnp.where(j == n_steps - 1, 2, 1)])]
        for a in range(SWA_STEP_BLOCKS):
            for n in range(SWA_KV_HEADS):
                attend(a, n, bands[a])

    @pl.when(j >= n_steps)
    def _():
        for a in range(SWA_STEP_BLOCKS):
            for n in range(SWA_KV_HEADS):
                attend(a, n, None)


def _swa(sq, kx, vx, sink, layer, n_b, l_, ctx_len, with_ctx):
    nt = sq.shape[0]
    sb = SWA_BLOCK
    qb = SWA_STEP_BLOCKS * sb
    nb = l_ // sb
    ns = l_ // qb
    assert SWA_STEP_BLOCKS == 2 and nb % 2 == 0 and nb >= 4 and ctx_len == qb
    ctx0 = (n_b * l_) // qb
    bias = _swa_band_bias(nb)

    def q_idx(b, j):
        return (jnp.where(j < ns, b * ns + j, ctx0 + b), 0)

    def cur_idx(b, j):
        return (b * ns + jnp.minimum(j, ns - 1), 0)

    def edge_idx(off):
        return lambda b, j: (b * nb + jnp.clip(SWA_STEP_BLOCKS * jnp.minimum(j, ns - 1) + off, 0, nb - 1), 0)

    edge = lambda idx: pl.BlockSpec((sb, 512), idx)
    wide = lambda idx: pl.BlockSpec((qb, 512), idx)
    cspec = pl.BlockSpec((ctx_len, 512), lambda b, j: (ctx0 + b, 0))
    return pl.pallas_call(
        functools.partial(_swa_kernel, ns, layer),
        grid=(n_b, ns + (1 if with_ctx else 0)),
        in_specs=[pl.BlockSpec(memory_space=pltpu.SMEM),
                  wide(q_idx), edge(edge_idx(-1)), wide(cur_idx), edge(edge_idx(SWA_STEP_BLOCKS)),
                  edge(edge_idx(-1)), wide(cur_idx), edge(edge_idx(SWA_STEP_BLOCKS)), cspec, cspec,
                  pl.BlockSpec(bias.shape, lambda b, j: (0, 0, 0))],
        out_specs=wide(q_idx),
        out_shape=jax.ShapeDtypeStruct((nt if with_ctx else n_b * l_, 512), BF16),
        compiler_params=_params(("arbitrary", "arbitrary")),
        name="swa",
    )(sink, sq, kx, kx, kx, vx, vx, vx, kx, vx, bias)


def _layer_norm(r, g, b):
    mu = jnp.mean(r, axis=-1, keepdims=True)
    rc = r - mu
    var = jnp.mean(rc * rc, axis=-1, keepdims=True)
    return rc * lax.rsqrt(var + LN_EPS) * g + b


def _route(logits_t, bias_col):
    epg = N_EXPERTS // N_GROUPS
    s = _sigmoid(logits_t)
    sel = s + bias_col
    sel_r = [sel[e:e + 1, :] for e in range(N_EXPERTS)]
    s_r = [s[e:e + 1, :] for e in range(N_EXPERTS)]
    grp = []
    for g in range(N_GROUPS):
        a = sel_r[g * epg:(g + 1) * epg]
        m1 = functools.reduce(jnp.maximum, a)
        m2 = functools.reduce(jnp.maximum,
                              [jnp.minimum(a[i], a[k]) for i in range(epg) for k in range(i + 1, epg)])
        grp.append(m1 + m2)
    m_r, w_r = [], []
    for g in range(N_GROUPS):
        best = None
        for k in range(N_GROUPS):
            if k == g:
                continue
            c = (grp[g] > grp[k]) if k < g else (grp[g] >= grp[k])
            best = c if best is None else (best & c)
        for e in range(g * epg, (g + 1) * epg):
            rank = jnp.zeros_like(sel_r[e])
            for k in range(g * epg, (g + 1) * epg):
                if k == e:
                    continue
                ahead = (sel_r[k] >= sel_r[e]) if k < e else (sel_r[k] > sel_r[e])
                rank = rank + jnp.where(ahead, 1.0, 0.0)
            chosen = jnp.where(best & (rank < 1.5), 1.0, 0.0)
            m_r.append(chosen)
            w_r.append(chosen * s_r[e])
    inv = 1.0 / functools.reduce(lambda a, b: a + b, w_r)
    return m_r, [w * inv for w in w_r]


def _pack_bf16_pairs(a, b):
    ua = pltpu.bitcast(a.astype(BF16).astype(F32), jnp.uint32)
    ub = pltpu.bitcast(b.astype(BF16).astype(F32), jnp.uint32)
    return (ua >> 16) | ub


def _unpack_bf16_pairs(u):
    return (pltpu.bitcast(u << 16, F32), pltpu.bitcast(u & jnp.uint32(0xFFFF0000), F32))


def _outproj_kernel(n_lat, of_ref, ob_ref, hg_ref, yb_ref, yc_ref, xa_ref, xb_ref, mod_ref, wo_ref, pv_ref,
                    wrh_ref, wrl_ref, rb_ref, before_ref, x1_ref, h2u_ref, route_ref, gt_ref, cnt_ref):
    @pl.when(pl.program_id(0) == 0)
    def _():
        cnt_ref[...] = jnp.zeros(cnt_ref.shape, F32)

    row = lax.broadcasted_iota(jnp.int32, (HGRN_W, HGRN_W), 0)
    col = lax.broadcasted_iota(jnp.int32, (HGRN_W, HGRN_W), 1)
    head_ones = jnp.where((row // HEAD_DIM) == (col // HEAD_DIM), 1.0, 0.0).astype(BF16)
    pv = pv_ref[0]
    o = of_ref[...] + ob_ref[...]
    hi, lo = _split_bf16(o * o)
    ms = (_dot(hi, head_ones) + _dot(lo, head_ones)) * (1.0 / HEAD_DIM)
    gate = hg_ref[...].astype(F32)
    ya = o * lax.rsqrt(ms + RMS_EPS) * pv[2:3, 0:HGRN_W] * (gate * _sigmoid(gate))
    y = (_dot(ya.astype(BF16), wo_ref[0, 0:256, :]) + _dot(yb_ref[...], wo_ref[0, 256:512, :])
         + _dot(yc_ref[...], wo_ref[0, 512:1024, :]))
    mod = mod_ref[0, 0]
    x = jnp.where(pl.program_id(0) < n_lat, xa_ref[...], xb_ref[...])
    x1 = _layer_norm(ALPHA * x + mod[2:3] * y, pv[0:1], pv[1:2])
    x1_ref[...] = x1
    h2 = x1 * (1.0 + mod[4:5]) + mod[3:4]
    hh, hl = _split_bf16(h2)
    half = h2.shape[1] // 2
    h2u_ref[...] = _pack_bf16_pairs(h2[:, 0:half], h2[:, half:2 * half])
    wrh = wrh_ref[...]
    logits_t = _dot_nt(wrh, hh) + _dot_nt(wrh, hl) + _dot_nt(wrl_ref[...], hh)
    m_r, w_r = _route(logits_t, rb_ref[...])
    m_t = jnp.concatenate(m_r, axis=0)
    rank_t = _dot(m_t.astype(BF16), before_ref[...]) + cnt_ref[:, 0:1]
    cnt_ref[...] = cnt_ref[...] + jnp.sum(m_t, axis=1, keepdims=True)
    seen = jnp.zeros_like(m_r[0])
    e1 = e2 = r1 = r2 = g1 = g2 = jnp.zeros_like(m_r[0])
    for e in range(N_EXPERTS):
        is1 = m_r[e] * (1.0 - seen)
        is2 = m_r[e] * seen
        rk = rank_t[e:e + 1, :]
        e1, e2 = e1 + e * is1, e2 + e * is2
        r1, r2 = r1 + rk * is1, r2 + rk * is2
        g1, g2 = g1 + w_r[e] * is1, g2 + w_r[e] * is2
        seen = seen + is1
    zero = jnp.zeros_like(e1)
    route_ref[...] = jnp.concatenate([e1, e2, r1, r2, g1, g2, zero, zero], axis=0)
    pad = jnp.zeros((LANES - 2, g1.shape[1]), F32)
    gt_ref[...] = jnp.concatenate([g1, g2, pad], axis=0).T


def _outproj(o_f, o_b, hg, yb, yc, xa, xb, ctx_blk, mod, wo, pv, layer, wrh, wrl, rb, n_b, l_, n_tiles):
    d = xa.shape[1]
    tpb = l_ // TM
    n_lat = n_b * tpb
    before = np.triu(np.ones((TM, TM), np.float32), 1).astype(jnp.bfloat16)

    def mod_idx(j):
        return (layer, jnp.minimum(j // tpb, n_b), 0, 0)

    def rows(width):
        return pl.BlockSpec((TM, width), lambda j: (j, 0))

    full = lambda a: pl.BlockSpec(a.shape, lambda j: (0,) * a.ndim)
    of_layer = lambda a: pl.BlockSpec((1,) + a.shape[1:], lambda j: (layer,) + (0,) * (a.ndim - 1))
    nr = n_tiles * TM
    return pl.pallas_call(
        functools.partial(_outproj_kernel, n_lat),
        grid=(n_tiles,),
        in_specs=[rows(256), rows(256), rows(256), rows(256), rows(512),
                  pl.BlockSpec((TM, d), lambda j: (jnp.minimum(j, n_lat - 1), 0)),
                  pl.BlockSpec((TM, d), lambda j: (ctx_blk, 0)),
                  pl.BlockSpec((1, 1, 6, d), mod_idx), of_layer(wo), of_layer(pv), full(wrh), full(wrl), full(rb),
                  full(before)],
        out_specs=[rows(d), rows(d // 2), pl.BlockSpec((8, TM), lambda j: (0, j)), rows(LANES),
                   pl.BlockSpec((N_EXPERTS, LANES), lambda j: (0, 0))],
        out_shape=[jax.ShapeDtypeStruct((nr, d), F32), jax.ShapeDtypeStruct((nr, d // 2), jnp.uint32),
                   jax.ShapeDtypeStruct((8, nr), F32), jax.ShapeDtypeStruct((nr, LANES), F32),
                   jax.ShapeDtypeStruct((N_EXPERTS, LANES), F32)],
        compiler_params=_params(("arbitrary",)),
        name="outproj",
    )(o_f, o_b, hg, yb, yc, xa, xb, mod, wo, pv, wrh, wrl, rb, before)


def _sc_split(rows_per_worker):
    best = None
    for chunk in (64, 48, 32):
        for nbuf in range(SC_MAX_ROWS // chunk, 0, -1):
            if rows_per_worker % (chunk * nbuf) == 0 and (best is None or chunk * nbuf > best[0] * best[1]):
                best = (chunk, nbuf)
    return best


def _sc_workers():
    info = plsc.get_sparse_core_info()
    return info.num_cores, info.num_cores * info.num_subcores


def _sc_scatter_rows(table, pos, n_slots):
    n_tok, width = table.shape
    n_cores, n_workers = _sc_workers()
    per_w = n_tok // n_workers
    assert n_tok % n_workers == 0
    chunk, nbuf = _sc_split(per_w)
    n_chunks = per_w // chunk
    mesh = plsc.VectorSubcoreMesh(core_axis_name="c", subcore_axis_name="s")

    @functools.partial(
        pl.kernel, mesh=mesh,
        out_type=jax.ShapeDtypeStruct((n_slots, width), table.dtype),
        scratch_types=[pltpu.VMEM((2 * n_chunks, chunk), jnp.int32),
                       pltpu.VMEM((nbuf, chunk, width), table.dtype),
                       pltpu.SemaphoreType.DMA((nbuf,)),
                       pltpu.SemaphoreType.DMA((nbuf,))],
    )
    def scatter(table_hbm, pos_hbm, out_hbm, idx_v, rows_v, rsem, wsem):
        wid = lax.axis_index("s") * n_cores + lax.axis_index("c")
        base = wid * per_w
        pltpu.sync_copy(pos_hbm.at[wid], idx_v)

        @pl.loop(0, n_chunks // nbuf)
        def _(g):
            c0 = g * nbuf
            gets = [pltpu.async_copy(table_hbm.at[pl.ds(base + (c0 + b) * chunk, chunk)], rows_v.at[b],
                                     rsem.at[b]) for b in range(nbuf)]
            puts = []
            for b in range(nbuf):
                gets[b].wait()
                for k in range(2):
                    puts.append(pltpu.async_copy(rows_v.at[b], out_hbm.at[idx_v.at[k * n_chunks + c0 + b]],
                                                 wsem.at[b]))
            for put in puts:
                put.wait()

    pos_w = pos.reshape(2, n_workers, n_chunks, chunk).transpose(1, 0, 2, 3).reshape(n_workers, 2 * n_chunks, chunk)
    return scatter(table, pos_w)


def _sc_gather_rows(table, idx):
    n_out = idx.shape[0]
    width = table.shape[1]
    n_cores, n_workers = _sc_workers()
    per_w = n_out // n_workers
    assert n_out % n_workers == 0
    chunk, nbuf = _sc_split(per_w)
    mesh = plsc.VectorSubcoreMesh(core_axis_name="c", subcore_axis_name="s")

    @functools.partial(
        pl.kernel, mesh=mesh,
        out_type=jax.ShapeDtypeStruct((n_out, width), table.dtype),
        scratch_types=[pltpu.VMEM((per_w,), jnp.int32),
                       pltpu.VMEM((nbuf, chunk, width), table.dtype),
                       pltpu.SemaphoreType.DMA((nbuf,)),
                       pltpu.SemaphoreType.DMA((nbuf,))],
    )
    def gather(table_hbm, idx_hbm, out_hbm, idx_v, rows_v, gsem, wsem):
        wid = lax.axis_index("s") * n_cores + lax.axis_index("c")
        base = wid * per_w
        pltpu.sync_copy(idx_hbm.at[pl.ds(base, per_w)], idx_v)

        @pl.loop(0, per_w // (chunk * nbuf))
        def _(g):
            off = g * (chunk * nbuf)
            gets = [pltpu.async_copy(table_hbm.at[idx_v.at[pl.ds(off + b * chunk, chunk)]], rows_v.at[b],
                                     gsem.at[b]) for b in range(nbuf)]
            puts = []
            for b in range(nbuf):
                gets[b].wait()
                puts.append(pltpu.async_copy(rows_v.at[b], out_hbm.at[pl.ds(base + off + b * chunk, chunk)],
                                             wsem.at[b]))
            for put in puts:
                put.wait()

    return gather(table, idx)


def _gmm_kernel(te_ref, nv_ref, nu_ref, xs_ref, wg_ref, wu_ref, wd_ref, ys_ref, wgb_ref, wub_ref, wdb_ref):
    i = pl.program_id(0)
    used = i < nu_ref[0]
    fresh = jnp.logical_or(i == 0, te_ref[i] != te_ref[jnp.maximum(i - 1, 0)])

    @pl.when(jnp.logical_and(used, fresh))
    def _():
        wgb_ref[...] = wg_ref[0, 0].astype(BF16)
        wub_ref[...] = wu_ref[0, 0].astype(BF16)
        wdb_ref[...] = wd_ref[0, 0].astype(BF16)

    @pl.when(used)
    def _():
        rowid = lax.broadcasted_iota(jnp.int32, xs_ref.shape, 0)
        lo, hi = _unpack_bf16_pairs(jnp.where(rowid < nv_ref[i], xs_ref[...], jnp.uint32(0)))
        lo, hi = lo.astype(BF16), hi.astype(BF16)
        half = lo.shape[1]
        a = _dot(lo, wgb_ref[0:half, :]) + _dot(hi, wgb_ref[half:2 * half, :])
        u = _dot(lo, wub_ref[0:half, :]) + _dot(hi, wub_ref[half:2 * half, :])
        y = _dot((a * _sigmoid(a) * u).astype(BF16), wdb_ref[...])
        ys_ref[...] = _pack_bf16_pairs(y[:, 0:half], y[:, half:2 * half])

    @pl.when(jnp.logical_not(used))
    def _():
        ys_ref[...] = jnp.zeros(ys_ref.shape, jnp.uint32)


def _gmm(tile_expert, tile_valid, n_used, xs, wg, wu, wd, layer):
    p_rows, half = xs.shape
    d = 2 * half
    wspec = lambda r, c: pl.BlockSpec((1, 1, r, c), lambda i, te, nv, nu: (layer, te[i], 0, 0))
    return pl.pallas_call(
        _gmm_kernel,
        grid_spec=pltpu.PrefetchScalarGridSpec(
            num_scalar_prefetch=3,
            grid=(p_rows // TM,),
            in_specs=[pl.BlockSpec((TM, half), lambda i, te, nv, nu: (i, 0)),
                      wspec(d, D_EXPERT), wspec(d, D_EXPERT), wspec(D_EXPERT, d)],
            out_specs=pl.BlockSpec((TM, half), lambda i, te, nv, nu: (i, 0)),
            scratch_shapes=[pltpu.VMEM((d, D_EXPERT), BF16), pltpu.VMEM((d, D_EXPERT), BF16),
                            pltpu.VMEM((D_EXPERT, d), BF16)]),
        out_shape=jax.ShapeDtypeStruct((p_rows, half), jnp.uint32),
        compiler_params=_params(("arbitrary",)),
        name="moe_experts",
    )(tile_expert, tile_valid, n_used, xs, wg, wu, wd)


def _moefin_kernel(o1_ref, o2_ref, gt_ref, x_ref, mod_ref, pv_ref, out_ref):
    gt = gt_ref[...]
    g1, g2 = gt[:, 0:1], gt[:, 1:2]
    a1, b1 = _unpack_bf16_pairs(o1_ref[...])
    a2, b2 = _unpack_bf16_pairs(o2_ref[...])
    f = jnp.concatenate([g1 * a1 + g2 * a2, g1 * b1 + g2 * b2], axis=1)
    mod = mod_ref[0, 0]
    pv = pv_ref[0]
    out_ref[...] = _layer_norm(ALPHA * x_ref[...] + mod[5:6] * f, pv[3:4], pv[4:5])


def _moefin(o12, gt, x1, mod, pv, layer, n_b, l_, n_tiles):
    d = x1.shape[1]
    tpb = l_ // TM

    def mod_idx(j):
        return (layer, jnp.minimum(j // tpb, n_b), 0, 0)

    return pl.pallas_call(
        _moefin_kernel,
        grid=(n_tiles,),
        in_specs=[pl.BlockSpec((TM, d // 2), lambda j: (j, 0)),
                  pl.BlockSpec((TM, d // 2), lambda j: (j + n_tiles, 0)),
                  pl.BlockSpec((TM, LANES), lambda j: (j, 0)),
                  pl.BlockSpec((TM, d), lambda j: (j, 0)),
                  pl.BlockSpec((1, 1, 6, d), mod_idx),
                  pl.BlockSpec((1,) + pv.shape[1:], lambda j: (layer, 0, 0))],
        out_specs=pl.BlockSpec((TM, d), lambda j: (j, 0)),
        out_shape=jax.ShapeDtypeStruct((n_tiles * TM, d), F32),
        compiler_params=_params(("arbitrary",)),
        name="moe_combine",
    )(o12, o12, gt, x1, mod, pv)


def _moe_routed(h2u, route, gt, counts, wg, wu, wd, layer, x1, mod, pv, n_b, l_, n_tiles):
    n_tok = n_tiles * TM
    n_slots = -(-(2 * n_tok + N_EXPERTS * TM) // SC_ROW_QUANTUM) * SC_ROW_QUANTUM
    cnt = counts[:, 0].astype(jnp.int32)
    padded = ((cnt + TM - 1) // TM) * TM
    upto = jnp.arange(N_EXPERTS)[None, :] <= jnp.arange(N_EXPERTS)[:, None]
    ends = jnp.sum(jnp.where(upto, padded[None, :], 0), axis=1)
    offs = ends - padded
    e12 = route[0:2].astype(jnp.int32)
    r12 = route[2:4].astype(jnp.int32)
    onehot = e12[:, :, None] == jnp.arange(N_EXPERTS, dtype=jnp.int32)
    pos = r12 + jnp.sum(jnp.where(onehot, offs, 0), axis=-1)
    tile_start = jnp.arange(n_slots // TM, dtype=jnp.int32) * TM
    tile_expert = jnp.minimum(jnp.sum(tile_start[:, None] >= ends[None, :], axis=1), N_EXPERTS - 1)
    pick = tile_expert[:, None] == jnp.arange(N_EXPERTS, dtype=jnp.int32)
    tile_valid = jnp.clip(jnp.sum(jnp.where(pick, (offs + cnt)[None, :], 0), axis=1) - tile_start, 0, TM)
    n_used = (ends[N_EXPERTS - 1] // TM).reshape(1)
    xs = _sc_scatter_rows(h2u, pos, n_slots)
    ys = _gmm(tile_expert.astype(jnp.int32), tile_valid.astype(jnp.int32), n_used.astype(jnp.int32),
              xs, wg, wu, wd, layer)
    o12 = _sc_gather_rows(ys, pos.reshape(2 * n_tok))
    return _moefin(o12, gt, x1, mod, pv, layer, n_b, l_, n_tiles)


def _rope_tables(l_):
    pos = np.arange(l_)
    nf = HEAD_DIM // 4
    inv = ROPE_BASE ** (-np.arange(nf, dtype=np.float64) / nf)
    ar = (pos // GRID_W)[:, None] * inv
    ac = (pos % GRID_W)[:, None] * inv
    cos = np.concatenate([np.cos(ar), np.cos(ar), np.cos(ac), np.cos(ac)], axis=1)
    sin = np.concatenate([-np.sin(ar), np.sin(ar), -np.sin(ac), np.sin(ac)], axis=1)
    cos = np.concatenate([np.tile(cos, (1, 2)), np.ones((TM, LANES))], axis=0)
    sin = np.concatenate([np.tile(sin, (1, 2)), np.zeros((TM, LANES))], axis=0)
    return cos.astype(np.float32), sin.astype(np.float32)


def _na_bias(rpb):
    u = np.arange(GRID_W)[:, None]
    v = np.arange(GRID_W)[None, :]
    cs = np.clip(u - NA_WIN_COLS // 2, 0, GRID_W - NA_WIN_COLS)
    colmask = (v >= cs) & (v < cs + NA_WIN_COLS)
    coff = np.clip(v - u + NA_WIN_COLS - 1, 0, 2 * NA_WIN_COLS - 2)
    n_off = 2 * NA_WIN_COLS - 1
    pick_col = (coff[None] == np.arange(n_off)[:, None, None]).astype(np.float32)
    ridx = np.arange(NA_WIN_ROWS)[None, :] - np.arange(NA_WIN_ROWS)[:, None] + NA_WIN_ROWS - 1
    pick_row = (ridx[:, :, None] == np.arange(2 * NA_WIN_ROWS - 1)).astype(np.float32)
    rows = jnp.einsum('dia,lhab->ldhib', pick_row, rpb.astype(F32), precision=lax.Precision.HIGHEST)
    tab = jnp.einsum('ldhib,buv->ldhuiv', rows, pick_col, precision=lax.Precision.HIGHEST)
    tab = jnp.where(colmask[None, None, None, :, None, :], tab, NEG)
    return tab.reshape(rpb.shape[0], NA_WIN_ROWS, NA_HEADS * GRID_W, NA_WIN_ROWS * GRID_W)


def kernel(x, c, ctx, c_ctx, w_ada, b_ada, w_in, lb_logits, hgrn_norm, na_rpb, swa_sink, w_out,
           ln1_g, ln1_b, ln2_g, ln2_b, w_router, router_bias, w_gate, w_up, w_down):
    n_b, l_, d = x.shape
    ctx_len = ctx.shape[1]
    depth = w_ada.shape[0]
    assert n_b * ctx_len == TM and l_ % TM == 0 and n_b + 1 <= N_COND
    n_lat_tiles = (n_b * l_) // TM

    p_lb = jax.nn.softmax(lb_logits.astype(F32), axis=0).reshape(depth, 2 * HGRN_W)
    upto = (np.arange(depth)[None, :] <= np.arange(depth)[:, None]).astype(np.float32)
    lb = jnp.maximum(jnp.sum(upto[:, :, None] * p_lb[None], axis=1) - p_lb[0:1], LB_MIN)
    gp = jnp.stack([jnp.log(lb), jnp.log1p(-lb), 1.0 - lb] + [jnp.zeros_like(lb)] * 5, axis=1)

    cond = jnp.concatenate([c, c_ctx[None, :], jnp.zeros((N_COND - n_b - 1, d), F32)], axis=0)
    mod = _ada(cond.T, w_ada, b_ada, n_b + 1).reshape(depth, N_COND, 6, d)

    cos_t, sin_t = _rope_tables(l_)
    wr_t = w_router.T
    wrh = wr_t.astype(BF16)
    wrl = (wr_t - wrh.astype(F32)).astype(BF16)
    rb = router_bias.astype(F32)[:, None]
    w_in_b = w_in.astype(BF16)
    w_out_b = w_out.astype(BF16)
    zeros = jnp.zeros_like(ln1_g)
    pv = jnp.stack([ln1_g, ln1_b, jnp.tile(hgrn_norm, (1, d // HEAD_DIM)), ln2_g, ln2_b, zeros, zeros, zeros],
                   axis=1).astype(F32)
    na_bias = _na_bias(na_rpb)
    sink = swa_sink.astype(F32)

    xa, xb, ctx_blk = x.reshape(n_b * l_, d), ctx.reshape(n_b * ctx_len, d), 0
    for l in range(depth):
        last = l == depth - 1
        hq, hv, hg, lf, kk, nqkv, sq, kx, vx = _inproj(xa, xb, ctx_blk, mod, w_in_b, gp, l, cos_t, sin_t, n_b, l_)
        o_f, o_b = _hgrn(hq, hv, lf, kk, n_b, l_, ctx_len)
        yb = _na(nqkv, na_bias, l, n_b, l_, ctx_len, not last)
        yc = _swa(sq, kx, vx, sink, l, n_b, l_, ctx_len, not last)
        n_tiles = n_lat_tiles if last else n_lat_tiles + 1
        x1, h2u, route, gt, counts = _outproj(o_f, o_b, hg, yb, yc, xa, xb, ctx_blk, mod, w_out_b, pv, l,
                                              wrh, wrl, rb, n_b, l_, n_tiles)
        xa = _moe_routed(h2u, route, gt, counts, w_gate, w_up, w_down, l, x1, mod, pv, n_b, l_, n_tiles)
        xb, ctx_blk = xa, n_lat_tiles
    return xa[:n_b * l_].reshape(n_b, l_, d)
```

```python
import functools

import numpy as np
import jax
import jax.numpy as jnp
from jax import lax
from jax.experimental import pallas as pl
from jax.experimental.pallas import tpu as pltpu
from jax.experimental.pallas import tpu_sc as plsc

F32 = jnp.float32
BF16 = jnp.bfloat16

D_MODEL = 1024
GRID_W = 64
HEAD_DIM = 64
HGRN_W = 256
NA_HEADS = 4
NA_WIN_ROWS = 8
NA_WIN_COLS = 16
SWA_Q_HEADS = 8
SWA_KV_HEADS = 2
SWA_GROUP = 4
SWA_WINDOW = 128
SWA_BLOCK = 128
ROPE_BASE = 10000.0
N_EXPERTS = 16
N_GROUPS = 4
D_EXPERT = 512
LN_EPS = 1e-5
RMS_EPS = 1e-6
NEG = -1e30
LB_MIN = 1e-6
DEPTH = 2
ALPHA = (2.0 * DEPTH) ** 0.25

LANES = 128
MXU_N = 256
TM = 512
HB = 256
HC = 16
NA_QROWS = 4
SWA_STEP_BLOCKS = 2
VMEM_LIMIT = 56 * 1024 * 1024

SC_MAX_ROWS = 192
SC_ROW_QUANTUM = 2048

N_COND = 8


def _dot(a, b):
    return jnp.dot(a, b, preferred_element_type=F32)


def _dot_nt(a, b):
    return lax.dot_general(a, b, (((1,), (1,)), ((), ())), preferred_element_type=F32)


def _dot_tn(a, b):
    return lax.dot_general(a, b, (((0,), (0,)), ((), ())), preferred_element_type=F32)


def _sigmoid(x):
    return 1.0 / (1.0 + jnp.exp(-x))


def _split_bf16(x):
    hi = x.astype(BF16)
    lo = (x - hi.astype(F32)).astype(BF16)
    return hi, lo


def _params(sem):
    return pltpu.CompilerParams(dimension_semantics=sem, vmem_limit_bytes=VMEM_LIMIT)


def _ada_kernel(n_rows, condt_ref, w_ref, b_ref, o_ref):
    c = condt_ref[...]
    s = c * _sigmoid(c)
    w = w_ref[0]
    rows = [jnp.sum(w * s[:, r:r + 1], axis=0, keepdims=True) for r in range(n_rows)]
    rows.append(jnp.zeros((N_COND - n_rows, w.shape[1]), F32))
    o_ref[0] = jnp.concatenate(rows, axis=0) + b_ref[0]


def _ada(cond_t, w_ada, b_ada, n_rows):
    depth, d, n6 = w_ada.shape
    tn = 1024
    return pl.pallas_call(
        functools.partial(_ada_kernel, n_rows),
        grid=(depth, n6 // tn),
        in_specs=[pl.BlockSpec((d, N_COND), lambda l, n: (0, 0)),
                  pl.BlockSpec((1, d, tn), lambda l, n: (l, 0, n)),
                  pl.BlockSpec((1, 1, tn), lambda l, n: (l, 0, n))],
        out_specs=pl.BlockSpec((1, N_COND, tn), lambda l, n: (l, 0, n)),
        out_shape=jax.ShapeDtypeStruct((depth, N_COND, n6), F32),
        compiler_params=_params(("arbitrary", "arbitrary")),
        name="ada",
    )(cond_t, w_ada, b_ada.reshape(depth, 1, n6))


def _rope(z, cos, sin, first):
    sw = jnp.where(first, pltpu.roll(z, LANES - 16, axis=1), pltpu.roll(z, 16, axis=1))
    return z * cos + sw * sin


def _inproj_kernel(n_lat, xa_ref, xb_ref, mod_ref, w_ref, gp_ref, cos_ref, sin_ref,
                   hq_ref, hv_ref, hg_ref, lf_ref, kk_ref, nqkv_ref, sq_ref, kx_ref, vx_ref):
    mod = mod_ref[0, 0]
    x = jnp.where(pl.program_id(0) < n_lat, xa_ref[...], xb_ref[...])
    h = (x * (1.0 + mod[1:2]) + mod[0:1]).astype(BF16)
    gp = gp_ref[0]
    cos = cos_ref[...]
    sin = sin_ref[...]
    lane = lax.broadcasted_iota(jnp.int32, cos.shape, 1)
    first = (lane & 16) == 0
    scale = HEAD_DIM ** -0.5

    def chunk(c):
        return _dot(h, w_ref[0, :, c * MXU_N:(c + 1) * MXU_N])

    def per_query_head(z2):
        swapped = pltpu.roll(z2, HEAD_DIM, axis=1)
        low = lane < HEAD_DIM
        h0 = jnp.where(low, z2, swapped)
        h1 = jnp.where(low, swapped, z2)
        return jnp.concatenate([h0, h0], axis=1), jnp.concatenate([h1, h1], axis=1)

    def gates(z, d):
        cols = slice(d * MXU_N, (d + 1) * MXU_N)
        a = gp[0:1, cols]
        b = gp[1:2, cols] + (jnp.minimum(z, 0.0) - jnp.log1p(jnp.exp(-jnp.abs(z))))
        logf = jnp.maximum(a, b) + jnp.log1p(jnp.exp(-jnp.abs(a - b)))
        k = gp[2:3, cols] * (1.0 / (1.0 + jnp.exp(z)))
        return logf, k

    def rope2(z):
        return jnp.concatenate([_rope(z[:, 0:LANES], cos, sin, first),
                                _rope(z[:, LANES:2 * LANES], cos, sin, first)], axis=1)

    for d in range(2):
        logf, k = gates(chunk(1 + d), d)
        lf_ref[:, d * MXU_N:(d + 1) * MXU_N] = logf
        kk_ref[:, d * MXU_N:(d + 1) * MXU_N] = k.astype(BF16)
    for d in range(2):
        sq_ref[:, d * MXU_N:(d + 1) * MXU_N] = (rope2(chunk(8 + d)) * scale).astype(BF16)
    zkv = chunk(10)
    k0, k1 = per_query_head(_rope(zkv[:, 0:LANES], cos, sin, first))
    v0, v1 = per_query_head(zkv[:, LANES:2 * LANES])
    kx_ref[:, 0:MXU_N] = k0.astype(BF16)
    kx_ref[:, MXU_N:2 * MXU_N] = k1.astype(BF16)
    vx_ref[:, 0:MXU_N] = v0.astype(BF16)
    vx_ref[:, MXU_N:2 * MXU_N] = v1.astype(BF16)
    hq_ref[...] = chunk(0).astype(BF16)
    hv_ref[...] = chunk(3).astype(BF16)
    hg_ref[...] = chunk(4).astype(BF16)
    nqkv_ref[:, 0:MXU_N] = (chunk(5) * scale).astype(BF16)
    nqkv_ref[:, MXU_N:2 * MXU_N] = chunk(6).astype(BF16)
    nqkv_ref[:, 2 * MXU_N:3 * MXU_N] = chunk(7).astype(BF16)


def _inproj(xa, xb, ctx_blk, mod, w, gp, layer, cos_t, sin_t, n_b, l_):
    d = xa.shape[1]
    tpb = l_ // TM
    n_lat = n_b * tpb
    nt = (n_lat + 1) * TM

    def mod_idx(j):
        return (layer, jnp.minimum(j // tpb, n_b), 0, 0)

    def rope_idx(j):
        return (jnp.where(j < n_lat, j % tpb, tpb), 0)

    def rows(width):
        return pl.BlockSpec((TM, width), lambda j: (j, 0))

    widths = [(HGRN_W, BF16), (HGRN_W, BF16), (HGRN_W, BF16), (2 * HGRN_W, F32), (2 * HGRN_W, BF16),
              (3 * 256, BF16), (512, BF16), (512, BF16), (512, BF16)]
    return pl.pallas_call(
        functools.partial(_inproj_kernel, n_lat),
        grid=(nt // TM,),
        in_specs=[pl.BlockSpec((TM, d), lambda j: (jnp.minimum(j, n_lat - 1), 0)),
                  pl.BlockSpec((TM, d), lambda j: (ctx_blk, 0)),
                  pl.BlockSpec((1, 1, 6, d), mod_idx),
                  pl.BlockSpec((1,) + w.shape[1:], lambda j: (layer, 0, 0)),
                  pl.BlockSpec((1,) + gp.shape[1:], lambda j: (layer, 0, 0)),
                  pl.BlockSpec((TM, LANES), rope_idx),
                  pl.BlockSpec((TM, LANES), rope_idx)],
        out_specs=[rows(wd) for wd, _ in widths],
        out_shape=[jax.ShapeDtypeStruct((nt, wd), dt) for wd, dt in widths],
        compiler_params=_params(("arbitrary",)),
        name="inproj",
    )(xa, xb, mod, w, gp, cos_t, sin_t)


def _hgrn_consts():
    t = np.arange(HB)[:, None]
    u = np.arange(HB)[None, :]
    same = (t // HC) == (u // HC)
    tr, ur = t % HC, u % HC
    half = HC // 2
    mats, codes = [], []
    for fwd in (True, False):
        if fwd:
            incl = ur <= tr
            mid = ur <= (tr // half) * half + half // 2 - 1
            edge = ur <= half - 1
            code = np.where((t // half == u // half) & (u <= t), 1, np.where(same & (tr >= half) & (ur < half), 2, 0))
        else:
            incl = ur >= tr
            mid = ur >= (tr // half) * half + half // 2
            edge = ur >= half
            code = np.where((t // half == u // half) & (u >= t), 1, np.where(same & (tr < half) & (ur >= half), 2, 0))
        mats.append(np.concatenate([same & incl, same, same & mid, same & edge], axis=0))
        codes.append(code)
    return (np.stack(mats).astype(np.float32).astype(jnp.bfloat16), np.stack(codes).astype(np.float32))


def _hgrn_kernel(qf_ref, vf_ref, lff_ref, kf_ref, qb_ref, vb_ref, lfb_ref, kb_ref, cm_ref, code_ref,
                 of_ref, ob_ref, st_ref, qd_ref, kd_ref, vv_ref, gt_ref, it_ref):
    nch = HB // HC
    n_heads = HGRN_W // HEAD_DIM

    @pl.when(pl.program_id(1) == 0)
    def _():
        st_ref[...] = jnp.zeros(st_ref.shape, F32)

    lane3 = lax.broadcasted_iota(jnp.int32, (nch, HC, HGRN_W), 2) // HEAD_DIM

    def head_rows(x):
        x3 = x.astype(BF16).reshape(nch, HC, HGRN_W)
        zero = jnp.zeros_like(x3)
        return jnp.concatenate([jnp.where(lane3 == h, x3, zero) for h in range(n_heads)], axis=1)

    def prep(d, q_ref, v_ref, lf_ref, k_ref):
        hi, lo = _split_bf16(lf_ref[...])
        cm = cm_ref[d]
        sums = _dot(jnp.concatenate([cm, cm], axis=1), jnp.concatenate([hi, lo], axis=0))
        cum, tot = sums[0:HB], sums[HB:2 * HB]
        mid, edge = sums[2 * HB:3 * HB], sums[3 * HB:4 * HB]
        q = q_ref[...].astype(F32)
        k = k_ref[...].astype(F32)
        v = v_ref[...]
        qd_ref[d] = head_rows(q * jnp.exp(cum))
        kd_ref[d] = head_rows(k * jnp.exp(tot - cum))
        vv_ref[d] = jnp.concatenate([v[:, h * HEAD_DIM:(h + 1) * HEAD_DIM].reshape(nch, HC, HEAD_DIM)
                                     for h in range(n_heads)], axis=1)
        gt_ref[d] = jnp.exp(tot)
        q1 = (q * jnp.exp(cum - mid)).astype(BF16)
        k1 = (k * jnp.exp(mid - cum)).astype(BF16)
        q2 = (q * jnp.exp(jnp.minimum(cum - edge, 0.0))).astype(BF16)
        k2 = (k * jnp.exp(jnp.minimum(edge - cum, 0.0))).astype(BF16)
        s1 = _dot_nt(_head_stack(q1, n_heads), k1)
        s2 = _dot_nt(_head_stack(q2, n_heads), k2)
        code = jnp.concatenate([code_ref[d]] * n_heads, axis=0)
        p = jnp.where(code == 1.0, s1, jnp.where(code == 2.0, s2, 0.0))
        return _head_unstack(_dot(p.astype(BF16), v), n_heads)

    od_f = prep(0, qf_ref, vf_ref, lff_ref, kf_ref)
    od_b = prep(1, qb_ref, vb_ref, lfb_ref, kb_ref)

    for i in range(nch):
        for d, c in ((0, i), (1, nch - 1 - i)):
            st = st_ref[d]
            it_ref[d, c] = _dot_nt(qd_ref[d, c], st.astype(BF16))
            upd = _dot_tn(vv_ref[d, c], kd_ref[d, c])
            st_ref[d] = st * gt_ref[d, c * HC:c * HC + 1, :] + upd

    def inter(d):
        it = it_ref[d]
        return jnp.concatenate([it[:, h * HC:(h + 1) * HC, :].reshape(HB, HEAD_DIM) for h in range(n_heads)],
                               axis=1)

    of_ref[...] = inter(0) + od_f
    ob_ref[...] = inter(1) + od_b


def _hgrn(hq, hv, lf, kk, n_b, l_, ctx_len):
    nt = hq.shape[0]
    assert ctx_len == HB
    nlb = l_ // HB
    ctx0 = n_b * nlb
    cmats, codes = _hgrn_consts()
    heads = HGRN_W // HEAD_DIM

    def fwd_idx(col):
        return lambda b, s: (jnp.where(s == 0, ctx0 + b, b * nlb + s - 1), col)

    def bwd_idx(col):
        return lambda b, s: (jnp.where(s == 0, ctx0 + b, b * nlb + nlb - s), col)

    def blk(idx):
        return pl.BlockSpec((HB, HGRN_W), idx)

    full = lambda a: pl.BlockSpec(a.shape, lambda b, s: (0,) * a.ndim)
    return pl.pallas_call(
        _hgrn_kernel,
        grid=(n_b, nlb + 1),
        in_specs=[blk(fwd_idx(0)), blk(fwd_idx(0)), blk(fwd_idx(0)), blk(fwd_idx(0)),
                  blk(bwd_idx(0)), blk(bwd_idx(0)), blk(bwd_idx(1)), blk(bwd_idx(1)),
                  full(cmats), full(codes)],
        out_specs=[blk(fwd_idx(0)), blk(bwd_idx(0))],
        out_shape=[jax.ShapeDtypeStruct((nt, HGRN_W), F32)] * 2,
        scratch_shapes=[pltpu.VMEM((2, HEAD_DIM, HGRN_W), F32),
                        pltpu.VMEM((2, HB // HC, heads * HC, HGRN_W), BF16),
                        pltpu.VMEM((2, HB // HC, heads * HC, HGRN_W), BF16),
                        pltpu.VMEM((2, HB // HC, heads * HC, HEAD_DIM), BF16),
                        pltpu.VMEM((2, HB, HGRN_W), F32),
                        pltpu.VMEM((2, HB // HC, heads * HC, HEAD_DIM), F32)],
        compiler_params=_params(("arbitrary", "arbitrary")),
        name="hgrn",
    )(hq, hv, lf, kk, hq, hv, lf, kk, cmats, codes)


def _head_stack(q, n_heads):
    lane = lax.broadcasted_iota(jnp.int32, q.shape, 1)
    zero = jnp.zeros_like(q)
    return jnp.concatenate([jnp.where(lane // HEAD_DIM == h, q, zero) for h in range(n_heads)], axis=0)


def _head_unstack(o, n_heads):
    rows = o.shape[0] // n_heads
    lane = lax.broadcasted_iota(jnp.int32, (rows, o.shape[1]), 1)
    acc = jnp.zeros((rows, o.shape[1]), F32)
    for h in range(n_heads):
        acc = acc + jnp.where(lane // HEAD_DIM == h, o[h * rows:(h + 1) * rows], 0.0)
    return acc


def _na_kernel(n_rows, n_lat_steps, q_ref, k_ref, v_ref, kc_ref, vc_ref, bias_ref, o_ref):
    j = pl.program_id(1)
    kc = kc_ref[...]
    vc = vc_ref[...]
    nwin = NA_WIN_ROWS * GRID_W

    def attend(rr, win):
        q4 = _head_stack(q_ref[rr * GRID_W:(rr + 1) * GRID_W, :], NA_HEADS)
        s_ctx = _dot_nt(q4, kc)
        m = jnp.max(s_ctx, axis=1, keepdims=True)
        if win is not None:
            kw, vw, bias = win
            s_win = _dot_nt(q4, kw) + bias
            m = jnp.maximum(m, jnp.max(s_win, axis=1, keepdims=True))
            p_win = jnp.exp(s_win - m)
        p_ctx = jnp.exp(s_ctx - m)
        den = jnp.sum(p_ctx, axis=1, keepdims=True)
        o4 = _dot(p_ctx.astype(BF16), vc)
        if win is not None:
            den = den + jnp.sum(p_win, axis=1, keepdims=True)
            o4 = o4 + _dot(p_win.astype(BF16), vw)
        o = _head_unstack(o4 * (1.0 / den), NA_HEADS)
        o_ref[rr * GRID_W:(rr + 1) * GRID_W, :] = o.astype(BF16)

    @pl.when(j < n_lat_steps)
    def _():
        for rr in range(NA_QROWS):
            r = j * NA_QROWS + rr
            rs = jnp.clip(r - NA_WIN_ROWS // 2, 0, n_rows - NA_WIN_ROWS)
            start = pl.multiple_of(rs * GRID_W, GRID_W)
            kw = k_ref[pl.ds(start, nwin), :]
            vw = v_ref[pl.ds(start, nwin), :]
            attend(rr, (kw, vw, bias_ref[0, r - rs]))

    @pl.when(j >= n_lat_steps)
    def _():
        for rr in range(NA_QROWS):
            attend(rr, None)


def _na(nqkv, bias, layer, n_b, l_, ctx_len, with_ctx):
    nt = nqkv.shape[0]
    qb = NA_QROWS * GRID_W
    assert ctx_len == qb
    n_lat = l_ // qb
    ctx0 = n_b * n_lat
    n_rows = l_ // GRID_W

    def q_idx(b, j):
        return (jnp.where(j < n_lat, b * n_lat + j, ctx0 + b), 0)

    cblk = (n_b * l_) // ctx_len
    return pl.pallas_call(
        functools.partial(_na_kernel, n_rows, n_lat),
        grid=(n_b, n_lat + (1 if with_ctx else 0)),
        in_specs=[pl.BlockSpec((qb, 256), q_idx),
                  pl.BlockSpec((l_, 256), lambda b, j: (b, 1)),
                  pl.BlockSpec((l_, 256), lambda b, j: (b, 2)),
                  pl.BlockSpec((ctx_len, 256), lambda b, j: (cblk + b, 1)),
                  pl.BlockSpec((ctx_len, 256), lambda b, j: (cblk + b, 2)),
                  pl.BlockSpec((1,) + bias.shape[1:], lambda b, j: (layer, 0, 0, 0))],
        out_specs=pl.BlockSpec((qb, 256), q_idx),
        out_shape=jax.ShapeDtypeStruct((nt if with_ctx else n_b * l_, 256), BF16),
        compiler_params=_params(("arbitrary", "arbitrary")),
        name="natten",
    )(nqkv, nqkv, nqkv, nqkv, nqkv, bias)


def _swa_band_bias(n_blocks):
    sb = SWA_BLOCK
    u = np.arange(sb)[:, None]
    v = np.arange(3 * sb)[None, :]
    tabs = []
    for j in (0, 1, n_blocks - 1):
        kpos = (j - 1) * sb + v
        ok = (kpos >= 0) & (kpos < n_blocks * sb) & (np.abs(j * sb + u - kpos) <= SWA_WINDOW)
        tabs.append(np.tile(np.where(ok, 0.0, NEG), (SWA_GROUP, 1)))
    return np.stack(tabs).astype(np.float32)


def _swa_kernel(n_steps, layer, sink_ref, q_ref, kp_ref, kq_ref, kn_ref, vp_ref, vq_ref, vn_ref,
                kc_ref, vc_ref, bias_ref, o_ref):
    j = pl.program_id(1)
    sb = SWA_BLOCK

    def attend(a, n, band):
        cols = slice(n * MXU_N, (n + 1) * MXU_N)
        rows = slice(a * sb, (a + 1) * sb)
        q4 = _head_stack(q_ref[rows, cols], SWA_GROUP)
        sink = jnp.concatenate([jnp.full((sb, 1), sink_ref[layer, n * SWA_GROUP + g], F32)
                                for g in range(SWA_GROUP)], axis=0)
        if band is None:
            keys, vals = kc_ref[:, cols], vc_ref[:, cols]
            s = _dot_nt(q4, keys)
        else:
            k_refs, v_refs, bias = band
            keys = jnp.concatenate([r[rs, cols] for r, rs in k_refs] + [kc_ref[:, cols]], axis=0)
            vals = jnp.concatenate([r[rs, cols] for r, rs in v_refs] + [vc_ref[:, cols]], axis=0)
            s = _dot_nt(q4, keys)
            s = jnp.concatenate([s[:, 0:3 * sb] + bias, s[:, 3 * sb:]], axis=1)
        m = jnp.maximum(jnp.max(s, axis=1, keepdims=True), sink)
        p = jnp.exp(s - m)
        den = jnp.sum(p, axis=1, keepdims=True) + jnp.exp(sink - m)
        o4 = _dot(p.astype(BF16), vals)
        o_ref[rows, cols] = _head_unstack(o4 * (1.0 / den), SWA_GROUP).astype(BF16)

    lo, hi, whole = slice(0, sb), slice(sb, 2 * sb), slice(0, sb)

    @pl.when(j < n_steps)
    def _():
        bands = [([(kp_ref, whole), (kq_ref, lo), (kq_ref, hi)], [(vp_ref, whole), (vq_ref, lo), (vq_ref, hi)],
                  bias_ref[jnp.where(j == 0, 0, 1)]),
                 ([(kq_ref, lo), (kq_ref, hi), (kn_ref, whole)], [(vq_ref, lo), (vq_ref, hi), (vn_ref, whole)],
                  bias_ref[jnp.where(j == n_steps - 1, 2, 1)])]
        for a in range(SWA_STEP_BLOCKS):
            for n in range(SWA_KV_HEADS):
                attend(a, n, bands[a])

    @pl.when(j >= n_steps)
    def _():
        for a in range(SWA_STEP_BLOCKS):
            for n in range(SWA_KV_HEADS):
                attend(a, n, None)


def _swa(sq, kx, vx, sink, layer, n_b, l_, ctx_len, with_ctx):
    nt = sq.shape[0]
    sb = SWA_BLOCK
    qb = SWA_STEP_BLOCKS * sb
    nb = l_ // sb
    ns = l_ // qb
    assert SWA_STEP_BLOCKS == 2 and nb % 2 == 0 and nb >= 4 and ctx_len == qb
    ctx0 = (n_b * l_) // qb
    bias = _swa_band_bias(nb)

    def q_idx(b, j):
        return (jnp.where(j < ns, b * ns + j, ctx0 + b), 0)

    def cur_idx(b, j):
        return (b * ns + jnp.minimum(j, ns - 1), 0)

    def edge_idx(off):
        return lambda b, j: (b * nb + jnp.clip(SWA_STEP_BLOCKS * jnp.minimum(j, ns - 1) + off, 0, nb - 1), 0)

    edge = lambda idx: pl.BlockSpec((sb, 512), idx)
    wide = lambda idx: pl.BlockSpec((qb, 512), idx)
    cspec = pl.BlockSpec((ctx_len, 512), lambda b, j: (ctx0 + b, 0))
    return pl.pallas_call(
        functools.partial(_swa_kernel, ns, layer),
        grid=(n_b, ns + (1 if with_ctx else 0)),
        in_specs=[pl.BlockSpec(memory_space=pltpu.SMEM),
                  wide(q_idx), edge(edge_idx(-1)), wide(cur_idx), edge(edge_idx(SWA_STEP_BLOCKS)),
                  edge(edge_idx(-1)), wide(cur_idx), edge(edge_idx(SWA_STEP_BLOCKS)), cspec, cspec,
                  pl.BlockSpec(bias.shape, lambda b, j: (0, 0, 0))],
        out_specs=wide(q_idx),
        out_shape=jax.ShapeDtypeStruct((nt if with_ctx else n_b * l_, 512), BF16),
        compiler_params=_params(("arbitrary", "arbitrary")),
        name="swa",
    )(sink, sq, kx, kx, kx, vx, vx, vx, kx, vx, bias)


def _layer_norm(r, g, b):
    mu = jnp.mean(r, axis=-1, keepdims=True)
    rc = r - mu
    var = jnp.mean(rc * rc, axis=-1, keepdims=True)
    return rc * lax.rsqrt(var + LN_EPS) * g + b


def _route(logits_t, bias_col):
    epg = N_EXPERTS // N_GROUPS
    s = _sigmoid(logits_t)
    sel = s + bias_col
    sel_r = [sel[e:e + 1, :] for e in range(N_EXPERTS)]
    s_r = [s[e:e + 1, :] for e in range(N_EXPERTS)]
    grp = []
    for g in range(N_GROUPS):
        a = sel_r[g * epg:(g + 1) * epg]
        m1 = functools.reduce(jnp.maximum, a)
        m2 = functools.reduce(jnp.maximum,
                              [jnp.minimum(a[i], a[k]) for i in range(epg) for k in range(i + 1, epg)])
        grp.append(m1 + m2)
    m_r, w_r = [], []
    for g in range(N_GROUPS):
        best = None
        for k in range(N_GROUPS):
            if k == g:
                continue
            c = (grp[g] > grp[k]) if k < g else (grp[g] >= grp[k])
            best = c if best is None else (best & c)
        for e in range(g * epg, (g + 1) * epg):
            rank = jnp.zeros_like(sel_r[e])
            for k in range(g * epg, (g + 1) * epg):
                if k == e:
                    continue
                ahead = (sel_r[k] >= sel_r[e]) if k < e else (sel_r[k] > sel_r[e])
                rank = rank + jnp.where(ahead, 1.0, 0.0)
            chosen = jnp.where(best & (rank < 1.5), 1.0, 0.0)
            m_r.append(chosen)
            w_r.append(chosen * s_r[e])
    inv = 1.0 / functools.reduce(lambda a, b: a + b, w_r)
    return m_r, [w * inv for w in w_r]


def _pack_bf16_pairs(a, b):
    ua = pltpu.bitcast(a.astype(BF16).astype(F32), jnp.uint32)
    ub = pltpu.bitcast(b.astype(BF16).astype(F32), jnp.uint32)
    return (ua >> 16) | ub


def _unpack_bf16_pairs(u):
    return (pltpu.bitcast(u << 16, F32), pltpu.bitcast(u & jnp.uint32(0xFFFF0000), F32))


def _outproj_kernel(n_lat, of_ref, ob_ref, hg_ref, yb_ref, yc_ref, xa_ref, xb_ref, mod_ref, wo_ref, pv_ref,
                    wrh_ref, wrl_ref, rb_ref, before_ref, x1_ref, h2u_ref, route_ref, gt_ref, cnt_ref):
    @pl.when(pl.program_id(0) == 0)
    def _():
        cnt_ref[...] = jnp.zeros(cnt_ref.shape, F32)

    row = lax.broadcasted_iota(jnp.int32, (HGRN_W, HGRN_W), 0)
    col = lax.broadcasted_iota(jnp.int32, (HGRN_W, HGRN_W), 1)
    head_ones = jnp.where((row // HEAD_DIM) == (col // HEAD_DIM), 1.0, 0.0).astype(BF16)
    pv = pv_ref[0]
    o = of_ref[...] + ob_ref[...]
    hi, lo = _split_bf16(o * o)
    ms = _dot(jnp.concatenate([hi, lo], axis=1), jnp.concatenate([head_ones, head_ones], axis=0)) * (1.0 / HEAD_DIM)
    gate = hg_ref[...].astype(F32)
    ya = o * lax.rsqrt(ms + RMS_EPS) * pv[2:3, 0:HGRN_W] * (gate * _sigmoid(gate))
    y = _dot(jnp.concatenate([ya.astype(BF16), yb_ref[...], yc_ref[...]], axis=1), wo_ref[0])
    mod = mod_ref[0, 0]
    x = jnp.where(pl.program_id(0) < n_lat, xa_ref[...], xb_ref[...])
    x1 = _layer_norm(ALPHA * x + mod[2:3] * y, pv[0:1], pv[1:2])
    x1_ref[...] = x1
    h2 = x1 * (1.0 + mod[4:5]) + mod[3:4]
    hh, hl = _split_bf16(h2)
    half = h2.shape[1] // 2
    h2u_ref[...] = _pack_bf16_pairs(h2[:, 0:half], h2[:, half:2 * half])
    wrh = wrh_ref[...]
    logits_t = _dot_nt(jnp.concatenate([wrh, wrh, wrl_ref[...]], axis=1),
                       jnp.concatenate([hh, hl, hh], axis=1))
    m_r, w_r = _route(logits_t, rb_ref[...])
    m_t = jnp.concatenate(m_r, axis=0)
    rank_t = _dot(m_t.astype(BF16), before_ref[...]) + cnt_ref[:, 0:1]
    cnt_ref[...] = cnt_ref[...] + jnp.sum(m_t, axis=1, keepdims=True)
    seen = jnp.zeros_like(m_r[0])
    e1 = e2 = r1 = r2 = g1 = g2 = jnp.zeros_like(m_r[0])
    for e in range(N_EXPERTS):
        is1 = m_r[e] * (1.0 - seen)
        is2 = m_r[e] * seen
        rk = rank_t[e:e + 1, :]
        e1, e2 = e1 + e * is1, e2 + e * is2
        r1, r2 = r1 + rk * is1, r2 + rk * is2
        g1, g2 = g1 + w_r[e] * is1, g2 + w_r[e] * is2
        seen = seen + is1
    zero = jnp.zeros_like(e1)
    route_ref[...] = jnp.concatenate([e1, e2, r1, r2, g1, g2, zero, zero], axis=0)
    pad = jnp.zeros((LANES - 2, g1.shape[1]), F32)
    gt_ref[...] = jnp.concatenate([g1, g2, pad], axis=0).T


def _outproj(o_f, o_b, hg, yb, yc, xa, xb, ctx_blk, mod, wo, pv, layer, wrh, wrl, rb, n_b, l_, n_tiles):
    d = xa.shape[1]
    tpb = l_ // TM
    n_lat = n_b * tpb
    before = np.triu(np.ones((TM, TM), np.float32), 1).astype(jnp.bfloat16)

    def mod_idx(j):
        return (layer, jnp.minimum(j // tpb, n_b), 0, 0)

    def rows(width):
        return pl.BlockSpec((TM, width), lambda j: (j, 0))

    full = lambda a: pl.BlockSpec(a.shape, lambda j: (0,) * a.ndim)
    of_layer = lambda a: pl.BlockSpec((1,) + a.shape[1:], lambda j: (layer,) + (0,) * (a.ndim - 1))
    nr = n_tiles * TM
    return pl.pallas_call(
        functools.partial(_outproj_kernel, n_lat),
        grid=(n_tiles,),
        in_specs=[rows(256), rows(256), rows(256), rows(256), rows(512),
                  pl.BlockSpec((TM, d), lambda j: (jnp.minimum(j, n_lat - 1), 0)),
                  pl.BlockSpec((TM, d), lambda j: (ctx_blk, 0)),
                  pl.BlockSpec((1, 1, 6, d), mod_idx), of_layer(wo), of_layer(pv), full(wrh), full(wrl), full(rb),
                  full(before)],
        out_specs=[rows(d), rows(d // 2), pl.BlockSpec((8, TM), lambda j: (0, j)), rows(LANES),
                   pl.BlockSpec((N_EXPERTS, LANES), lambda j: (0, 0))],
        out_shape=[jax.ShapeDtypeStruct((nr, d), F32), jax.ShapeDtypeStruct((nr, d // 2), jnp.uint32),
                   jax.ShapeDtypeStruct((8, nr), F32), jax.ShapeDtypeStruct((nr, LANES), F32),
                   jax.ShapeDtypeStruct((N_EXPERTS, LANES), F32)],
        compiler_params=_params(("arbitrary",)),
        name="outproj",
    )(o_f, o_b, hg, yb, yc, xa, xb, mod, wo, pv, wrh, wrl, rb, before)


def _sc_split(rows_per_worker):
    best = None
    for chunk in (64, 48, 32):
        for nbuf in range(SC_MAX_ROWS // chunk, 0, -1):
            if rows_per_worker % (chunk * nbuf) == 0 and (best is None or chunk * nbuf > best[0] * best[1]):
                best = (chunk, nbuf)
    return best


def _sc_workers():
    info = plsc.get_sparse_core_info()
    return info.num_cores, info.num_cores * info.num_subcores


def _sc_scatter_rows(table, pos, n_slots):
    n_tok, width = table.shape
    n_cores, n_workers = _sc_workers()
    per_w = n_tok // n_workers
    assert n_tok % n_workers == 0
    chunk, nbuf = _sc_split(per_w)
    n_chunks = per_w // chunk
    mesh = plsc.VectorSubcoreMesh(core_axis_name="c", subcore_axis_name="s")

    @functools.partial(
        pl.kernel, mesh=mesh,
        out_type=jax.ShapeDtypeStruct((n_slots, width), table.dtype),
        scratch_types=[pltpu.VMEM((2 * n_chunks, chunk), jnp.int32),
                       pltpu.VMEM((nbuf, chunk, width), table.dtype),
                       pltpu.SemaphoreType.DMA((nbuf,)),
                       pltpu.SemaphoreType.DMA((nbuf,))],
    )
    def scatter(table_hbm, pos_hbm, out_hbm, idx_v, rows_v, rsem, wsem):
        wid = lax.axis_index("s") * n_cores + lax.axis_index("c")
        base = wid * per_w
        pltpu.sync_copy(pos_hbm.at[wid], idx_v)

        @pl.loop(0, n_chunks // nbuf)
        def _(g):
            c0 = g * nbuf
            gets = [pltpu.async_copy(table_hbm.at[pl.ds(base + (c0 + b) * chunk, chunk)], rows_v.at[b],
                                     rsem.at[b]) for b in range(nbuf)]
            puts = []
            for b in range(nbuf):
                gets[b].wait()
                for k in range(2):
                    puts.append(pltpu.async_copy(rows_v.at[b], out_hbm.at[idx_v.at[k * n_chunks + c0 + b]],
                                                 wsem.at[b]))
            for put in puts:
                put.wait()

    pos_w = pos.reshape(2, n_workers, n_chunks, chunk).transpose(1, 0, 2, 3).reshape(n_workers, 2 * n_chunks, chunk)
    return scatter(table, pos_w)


def _sc_gather_rows(table, idx):
    n_out = idx.shape[0]
    width = table.shape[1]
    n_cores, n_workers = _sc_workers()
    per_w = n_out // n_workers
    assert n_out % n_workers == 0
    chunk, nbuf = _sc_split(per_w)
    mesh = plsc.VectorSubcoreMesh(core_axis_name="c", subcore_axis_name="s")

    @functools.partial(
        pl.kernel, mesh=mesh,
        out_type=jax.ShapeDtypeStruct((n_out, width), table.dtype),
        scratch_types=[pltpu.VMEM((per_w,), jnp.int32),
                       pltpu.VMEM((nbuf, chunk, width), table.dtype),
                       pltpu.SemaphoreType.DMA((nbuf,)),
                       pltpu.SemaphoreType.DMA((nbuf,))],
    )
    def gather(table_hbm, idx_hbm, out_hbm, idx_v, rows_v, gsem, wsem):
        wid = lax.axis_index("s") * n_cores + lax.axis_index("c")
        base = wid * per_w
        pltpu.sync_copy(idx_hbm.at[pl.ds(base, per_w)], idx_v)

        @pl.loop(0, per_w // (chunk * nbuf))
        def _(g):
            off = g * (chunk * nbuf)
            gets = [pltpu.async_copy(table_hbm.at[idx_v.at[pl.ds(off + b * chunk, chunk)]], rows_v.at[b],
                                     gsem.at[b]) for b in range(nbuf)]
            puts = []
            for b in range(nbuf):
                gets[b].wait()
                puts.append(pltpu.async_copy(rows_v.at[b], out_hbm.at[pl.ds(base + off + b * chunk, chunk)],
                                             wsem.at[b]))
            for put in puts:
                put.wait()

    return gather(table, idx)


def _gmm_kernel(te_ref, nv_ref, nu_ref, xs_ref, wg_ref, wu_ref, wd_ref, ys_ref, wgb_ref, wub_ref, wdb_ref):
    i = pl.program_id(0)
    used = i < nu_ref[0]
    fresh = jnp.logical_or(i == 0, te_ref[i] != te_ref[jnp.maximum(i - 1, 0)])

    @pl.when(jnp.logical_and(used, fresh))
    def _():
        wgb_ref[...] = wg_ref[0, 0].astype(BF16)
        wub_ref[...] = wu_ref[0, 0].astype(BF16)
        wdb_ref[...] = wd_ref[0, 0].astype(BF16)

    @pl.when(used)
    def _():
        rowid = lax.broadcasted_iota(jnp.int32, xs_ref.shape, 0)
        lo, hi = _unpack_bf16_pairs(jnp.where(rowid < nv_ref[i], xs_ref[...], jnp.uint32(0)))
        lo, hi = lo.astype(BF16), hi.astype(BF16)
        half = lo.shape[1]
        x = jnp.concatenate([lo, hi], axis=1)
        a = _dot(x, wgb_ref[...])
        u = _dot(x, wub_ref[...])
        y = _dot((a * _sigmoid(a) * u).astype(BF16), wdb_ref[...])
        ys_ref[...] = _pack_bf16_pairs(y[:, 0:half], y[:, half:2 * half])

    @pl.when(jnp.logical_not(used))
    def _():
        ys_ref[...] = jnp.zeros(ys_ref.shape, jnp.uint32)


def _gmm(tile_expert, tile_valid, n_used, xs, wg, wu, wd, layer):
    p_rows, half = xs.shape
    d = 2 * half
    wspec = lambda r, c: pl.BlockSpec((1, 1, r, c), lambda i, te, nv, nu: (layer, te[i], 0, 0))
    return pl.pallas_call(
        _gmm_kernel,
        grid_spec=pltpu.PrefetchScalarGridSpec(
            num_scalar_prefetch=3,
            grid=(p_rows // TM,),
            in_specs=[pl.BlockSpec((TM, half), lambda i, te, nv, nu: (i, 0)),
                      wspec(d, D_EXPERT), wspec(d, D_EXPERT), wspec(D_EXPERT, d)],
            out_specs=pl.BlockSpec((TM, half), lambda i, te, nv, nu: (i, 0)),
            scratch_shapes=[pltpu.VMEM((d, D_EXPERT), BF16), pltpu.VMEM((d, D_EXPERT), BF16),
                            pltpu.VMEM((D_EXPERT, d), BF16)]),
        out_shape=jax.ShapeDtypeStruct((p_rows, half), jnp.uint32),
        compiler_params=_params(("arbitrary",)),
        name="moe_experts",
    )(tile_expert, tile_valid, n_used, xs, wg, wu, wd)


def _moefin_kernel(o1_ref, o2_ref, gt_ref, x_ref, mod_ref, pv_ref, out_ref):
    gt = gt_ref[...]
    g1, g2 = gt[:, 0:1], gt[:, 1:2]
    a1, b1 = _unpack_bf16_pairs(o1_ref[...])
    a2, b2 = _unpack_bf16_pairs(o2_ref[...])
    f = jnp.concatenate([g1 * a1 + g2 * a2, g1 * b1 + g2 * b2], axis=1)
    mod = mod_ref[0, 0]
    pv = pv_ref[0]
    out_ref[...] = _layer_norm(ALPHA * x_ref[...] + mod[5:6] * f, pv[3:4], pv[4:5])


def _moefin(o12, gt, x1, mod, pv, layer, n_b, l_, n_tiles):
    d = x1.shape[1]
    tpb = l_ // TM

    def mod_idx(j):
        return (layer, jnp.minimum(j // tpb, n_b), 0, 0)

    return pl.pallas_call(
        _moefin_kernel,
        grid=(n_tiles,),
        in_specs=[pl.BlockSpec((TM, d // 2), lambda j: (j, 0)),
                  pl.BlockSpec((TM, d // 2), lambda j: (j + n_tiles, 0)),
                  pl.BlockSpec((TM, LANES), lambda j: (j, 0)),
                  pl.BlockSpec((TM, d), lambda j: (j, 0)),
                  pl.BlockSpec((1, 1, 6, d), mod_idx),
                  pl.BlockSpec((1,) + pv.shape[1:], lambda j: (layer, 0, 0))],
        out_specs=pl.BlockSpec((TM, d), lambda j: (j, 0)),
        out_shape=jax.ShapeDtypeStruct((n_tiles * TM, d), F32),
        compiler_params=_params(("arbitrary",)),
        name="moe_combine",
    )(o12, o12, gt, x1, mod, pv)


def _moe_routed(h2u, route, gt, counts, wg, wu, wd, layer, x1, mod, pv, n_b, l_, n_tiles):
    n_tok = n_tiles * TM
    n_slots = -(-(2 * n_tok + N_EXPERTS * TM) // SC_ROW_QUANTUM) * SC_ROW_QUANTUM
    cnt = counts[:, 0].astype(jnp.int32)
    padded = ((cnt + TM - 1) // TM) * TM
    upto = jnp.arange(N_EXPERTS)[None, :] <= jnp.arange(N_EXPERTS)[:, None]
    ends = jnp.sum(jnp.where(upto, padded[None, :], 0), axis=1)
    offs = ends - padded
    e12 = route[0:2].astype(jnp.int32)
    r12 = route[2:4].astype(jnp.int32)
    onehot = e12[:, :, None] == jnp.arange(N_EXPERTS, dtype=jnp.int32)
    pos = r12 + jnp.sum(jnp.where(onehot, offs, 0), axis=-1)
    tile_start = jnp.arange(n_slots // TM, dtype=jnp.int32) * TM
    tile_expert = jnp.minimum(jnp.sum(tile_start[:, None] >= ends[None, :], axis=1), N_EXPERTS - 1)
    pick = tile_expert[:, None] == jnp.arange(N_EXPERTS, dtype=jnp.int32)
    tile_valid = jnp.clip(jnp.sum(jnp.where(pick, (offs + cnt)[None, :], 0), axis=1) - tile_start, 0, TM)
    n_used = (ends[N_EXPERTS - 1] // TM).reshape(1)
    xs = _sc_scatter_rows(h2u, pos, n_slots)
    ys = _gmm(tile_expert.astype(jnp.int32), tile_valid.astype(jnp.int32), n_used.astype(jnp.int32),
              xs, wg, wu, wd, layer)
    o12 = _sc_gather_rows(ys, pos.reshape(2 * n_tok))
    return _moefin(o12, gt, x1, mod, pv, layer, n_b, l_, n_tiles)


def _rope_tables(l_):
    pos = np.arange(l_)
    nf = HEAD_DIM // 4
    inv = ROPE_BASE ** (-np.arange(nf, dtype=np.float64) / nf)
    ar = (pos // GRID_W)[:, None] * inv
    ac = (pos % GRID_W)[:, None] * inv
    cos = np.concatenate([np.cos(ar), np.cos(ar), np.cos(ac), np.cos(ac)], axis=1)
    sin = np.concatenate([-np.sin(ar), np.sin(ar), -np.sin(ac), np.sin(ac)], axis=1)
    cos = np.concatenate([np.tile(cos, (1, 2)), np.ones((TM, LANES))], axis=0)
    sin = np.concatenate([np.tile(sin, (1, 2)), np.zeros((TM, LANES))], axis=0)
    return cos.astype(np.float32), sin.astype(np.float32)


def _na_bias(rpb):
    u = np.arange(GRID_W)[:, None]
    v = np.arange(GRID_W)[None, :]
    cs = np.clip(u - NA_WIN_COLS // 2, 0, GRID_W - NA_WIN_COLS)
    colmask = (v >= cs) & (v < cs + NA_WIN_COLS)
    coff = np.clip(v - u + NA_WIN_COLS - 1, 0, 2 * NA_WIN_COLS - 2)
    n_off = 2 * NA_WIN_COLS - 1
    pick_col = (coff[None] == np.arange(n_off)[:, None, None]).astype(np.float32)
    ridx = np.arange(NA_WIN_ROWS)[None, :] - np.arange(NA_WIN_ROWS)[:, None] + NA_WIN_ROWS - 1
    pick_row = (ridx[:, :, None] == np.arange(2 * NA_WIN_ROWS - 1)).astype(np.float32)
    rows = jnp.einsum('dia,lhab->ldhib', pick_row, rpb.astype(F32), precision=lax.Precision.HIGHEST)
    tab = jnp.einsum('ldhib,buv->ldhuiv', rows, pick_col, precision=lax.Precision.HIGHEST)
    tab = jnp.where(colmask[None, None, None, :, None, :], tab, NEG)
    return tab.reshape(rpb.shape[0], NA_WIN_ROWS, NA_HEADS * GRID_W, NA_WIN_ROWS * GRID_W)


def kernel(x, c, ctx, c_ctx, w_ada, b_ada, w_in, lb_logits, hgrn_norm, na_rpb, swa_sink, w_out,
           ln1_g, ln1_b, ln2_g, ln2_b, w_router, router_bias, w_gate, w_up, w_down):
    n_b, l_, d = x.shape
    ctx_len = ctx.shape[1]
    depth = w_ada.shape[0]
    assert n_b * ctx_len == TM and l_ % TM == 0 and n_b + 1 <= N_COND
    n_lat_tiles = (n_b * l_) // TM

    p_lb = jax.nn.softmax(lb_logits.astype(F32), axis=0).reshape(depth, 2 * HGRN_W)
    upto = (np.arange(depth)[None, :] <= np.arange(depth)[:, None]).astype(np.float32)
    lb = jnp.maximum(jnp.sum(upto[:, :, None] * p_lb[None], axis=1) - p_lb[0:1], LB_MIN)
    gp = jnp.stack([jnp.log(lb), jnp.log1p(-lb), 1.0 - lb] + [jnp.zeros_like(lb)] * 5, axis=1)

    cond = jnp.concatenate([c, c_ctx[None, :], jnp.zeros((N_COND - n_b - 1, d), F32)], axis=0)
    mod = _ada(cond.T, w_ada, b_ada, n_b + 1).reshape(depth, N_COND, 6, d)

    cos_t, sin_t = _rope_tables(l_)
    wr_t = w_router.T
    wrh = wr_t.astype(BF16)
    wrl = (wr_t - wrh.astype(F32)).astype(BF16)
    rb = router_bias.astype(F32)[:, None]
    w_in_b = w_in.astype(BF16)
    w_out_b = w_out.astype(BF16)
    zeros = jnp.zeros_like(ln1_g)
    pv = jnp.stack([ln1_g, ln1_b, jnp.tile(hgrn_norm, (1, d // HEAD_DIM)), ln2_g, ln2_b, zeros, zeros, zeros],
                   axis=1).astype(F32)
    na_bias = _na_bias(na_rpb)
    sink = swa_sink.astype(F32)

    xa, xb, ctx_blk = x.reshape(n_b * l_, d), ctx.reshape(n_b * ctx_len, d), 0
    for l in range(depth):
        last = l == depth - 1
        hq, hv, hg, lf, kk, nqkv, sq, kx, vx = _inproj(xa, xb, ctx_blk, mod, w_in_b, gp, l, cos_t, sin_t, n_b, l_)
        o_f, o_b = _hgrn(hq, hv, lf, kk, n_b, l_, ctx_len)
        yb = _na(nqkv, na_bias, l, n_b, l_, ctx_len, not last)
        yc = _swa(sq, kx, vx, sink, l, n_b, l_, ctx_len, not last)
        n_tiles = n_lat_tiles if last else n_lat_tiles + 1
        x1, h2u, route, gt, counts = _outproj(o_f, o_b, hg, yb, yc, xa, xb, ctx_blk, mod, w_out_b, pv, l,
                                              wrh, wrl, rb, n_b, l_, n_tiles)
        xa = _moe_routed(h2u, route, gt, counts, w_gate, w_up, w_down, l, x1, mod, pv, n_b, l_, n_tiles)
        xb, ctx_blk = xa, n_lat_tiles
    return xa[:n_b * l_].reshape(n_b, l_, d)
```

```python
import functools

import numpy as np
import jax
import jax.numpy as jnp
from jax import lax
from jax.experimental import pallas as pl
from jax.experimental.pallas import tpu as pltpu
from jax.experimental.pallas import tpu_sc as plsc

F32 = jnp.float32
BF16 = jnp.bfloat16

D_MODEL = 1024
GRID_W = 64
HEAD_DIM = 64
HGRN_W = 256
NA_HEADS = 4
NA_WIN_ROWS = 8
NA_WIN_COLS = 16
SWA_Q_HEADS = 8
SWA_KV_HEADS = 2
SWA_GROUP = 4
SWA_WINDOW = 128
SWA_BLOCK = 128
ROPE_BASE = 10000.0
N_EXPERTS = 16
N_GROUPS = 4
D_EXPERT = 512
LN_EPS = 1e-5
RMS_EPS = 1e-6
NEG = -1e30
LB_MIN = 1e-6
DEPTH = 2
ALPHA = (2.0 * DEPTH) ** 0.25

LANES = 128
MXU_N = 256
TM = 512
HB = 256
HC = 16
NA_QROWS = 4
SWA_STEP_BLOCKS = 2
VMEM_LIMIT = 56 * 1024 * 1024

SC_MAX_ROWS = 192
SC_ROW_QUANTUM = 2048

N_COND = 8


def _dot(a, b):
    return jnp.dot(a, b, preferred_element_type=F32)


def _dot_nt(a, b):
    return lax.dot_general(a, b, (((1,), (1,)), ((), ())), preferred_element_type=F32)


def _dot_tn(a, b):
    return lax.dot_general(a, b, (((0,), (0,)), ((), ())), preferred_element_type=F32)


def _sigmoid(x):
    return 1.0 / (1.0 + jnp.exp(-x))


def _split_bf16(x):
    hi = x.astype(BF16)
    lo = (x - hi.astype(F32)).astype(BF16)
    return hi, lo


def _params(sem):
    return pltpu.CompilerParams(dimension_semantics=sem, vmem_limit_bytes=VMEM_LIMIT)


def _ada_kernel(n_rows, condt_ref, w_ref, b_ref, o_ref):
    c = condt_ref[...]
    s = c * _sigmoid(c)
    w = w_ref[0]
    rows = [jnp.sum(w * s[:, r:r + 1], axis=0, keepdims=True) for r in range(n_rows)]
    rows.append(jnp.zeros((N_COND - n_rows, w.shape[1]), F32))
    o_ref[0] = jnp.concatenate(rows, axis=0) + b_ref[0]


def _ada(cond_t, w_ada, b_ada, n_rows):
    depth, d, n6 = w_ada.shape
    tn = 1024
    return pl.pallas_call(
        functools.partial(_ada_kernel, n_rows),
        grid=(depth, n6 // tn),
        in_specs=[pl.BlockSpec((d, N_COND), lambda l, n: (0, 0)),
                  pl.BlockSpec((1, d, tn), lambda l, n: (l, 0, n)),
                  pl.BlockSpec((1, 1, tn), lambda l, n: (l, 0, n))],
        out_specs=pl.BlockSpec((1, N_COND, tn), lambda l, n: (l, 0, n)),
        out_shape=jax.ShapeDtypeStruct((depth, N_COND, n6), F32),
        compiler_params=_params(("arbitrary", "arbitrary")),
        name="ada",
    )(cond_t, w_ada, b_ada.reshape(depth, 1, n6))


def _rope(z, cos, sin, first):
    sw = jnp.where(first, pltpu.roll(z, LANES - 16, axis=1), pltpu.roll(z, 16, axis=1))
    return z * cos + sw * sin


def _inproj_kernel(n_lat, xa_ref, xb_ref, mod_ref, w_ref, gp_ref, cos_ref, sin_ref,
                   hq_ref, hv_ref, hg_ref, lf_ref, kk_ref, nqkv_ref, sq_ref, kx_ref, vx_ref):
    mod = mod_ref[0, 0]
    x = jnp.where(pl.program_id(0) < n_lat, xa_ref[...], xb_ref[...])
    h = (x * (1.0 + mod[1:2]) + mod[0:1]).astype(BF16)
    gp = gp_ref[0]
    cos = cos_ref[...]
    sin = sin_ref[...]
    lane = lax.broadcasted_iota(jnp.int32, cos.shape, 1)
    first = (lane & 16) == 0
    scale = HEAD_DIM ** -0.5

    def chunk(c):
        return _dot(h, w_ref[0, :, c * MXU_N:(c + 1) * MXU_N])

    def per_query_head(z2):
        swapped = pltpu.roll(z2, HEAD_DIM, axis=1)
        low = lane < HEAD_DIM
        h0 = jnp.where(low, z2, swapped)
        h1 = jnp.where(low, swapped, z2)
        return jnp.concatenate([h0, h0], axis=1), jnp.concatenate([h1, h1], axis=1)

    def gates(z, d):
        cols = slice(d * MXU_N, (d + 1) * MXU_N)
        a = gp[0:1, cols]
        b = gp[1:2, cols] + (jnp.minimum(z, 0.0) - jnp.log1p(jnp.exp(-jnp.abs(z))))
        logf = jnp.maximum(a, b) + jnp.log1p(jnp.exp(-jnp.abs(a - b)))
        k = gp[2:3, cols] * (1.0 / (1.0 + jnp.exp(z)))
        return logf, k

    def rope2(z):
        return jnp.concatenate([_rope(z[:, 0:LANES], cos, sin, first),
                                _rope(z[:, LANES:2 * LANES], cos, sin, first)], axis=1)

    for d in range(2):
        logf, k = gates(chunk(1 + d), d)
        lf_ref[:, d * MXU_N:(d + 1) * MXU_N] = logf
        kk_ref[:, d * MXU_N:(d + 1) * MXU_N] = k.astype(BF16)
    for d in range(2):
        sq_ref[:, d * MXU_N:(d + 1) * MXU_N] = (rope2(chunk(8 + d)) * scale).astype(BF16)
    zkv = chunk(10)
    k0, k1 = per_query_head(_rope(zkv[:, 0:LANES], cos, sin, first))
    v0, v1 = per_query_head(zkv[:, LANES:2 * LANES])
    kx_ref[:, 0:MXU_N] = k0.astype(BF16)
    kx_ref[:, MXU_N:2 * MXU_N] = k1.astype(BF16)
    vx_ref[:, 0:MXU_N] = v0.astype(BF16)
    vx_ref[:, MXU_N:2 * MXU_N] = v1.astype(BF16)
    hq_ref[...] = chunk(0).astype(BF16)
    hv_ref[...] = chunk(3).astype(BF16)
    hg_ref[...] = chunk(4).astype(BF16)
    nqkv_ref[:, 0:MXU_N] = (chunk(5) * scale).astype(BF16)
    nqkv_ref[:, MXU_N:2 * MXU_N] = chunk(6).astype(BF16)
    nqkv_ref[:, 2 * MXU_N:3 * MXU_N] = chunk(7).astype(BF16)


def _inproj(xa, xb, ctx_blk, mod, w, gp, layer, cos_t, sin_t, n_b, l_):
    d = xa.shape[1]
    tpb = l_ // TM
    n_lat = n_b * tpb
    nt = (n_lat + 1) * TM

    def mod_idx(j):
        return (layer, jnp.minimum(j // tpb, n_b), 0, 0)

    def rope_idx(j):
        return (jnp.where(j < n_lat, j % tpb, tpb), 0)

    def rows(width):
        return pl.BlockSpec((TM, width), lambda j: (j, 0))

    widths = [(HGRN_W, BF16), (HGRN_W, BF16), (HGRN_W, BF16), (2 * HGRN_W, F32), (2 * HGRN_W, BF16),
              (3 * 256, BF16), (512, BF16), (512, BF16), (512, BF16)]
    return pl.pallas_call(
        functools.partial(_inproj_kernel, n_lat),
        grid=(nt // TM,),
        in_specs=[pl.BlockSpec((TM, d), lambda j: (jnp.minimum(j, n_lat - 1), 0)),
                  pl.BlockSpec((TM, d), lambda j: (ctx_blk, 0)),
                  pl.BlockSpec((1, 1, 6, d), mod_idx),
                  pl.BlockSpec((1,) + w.shape[1:], lambda j: (layer, 0, 0)),
                  pl.BlockSpec((1,) + gp.shape[1:], lambda j: (layer, 0, 0)),
                  pl.BlockSpec((TM, LANES), rope_idx),
                  pl.BlockSpec((TM, LANES), rope_idx)],
        out_specs=[rows(wd) for wd, _ in widths],
        out_shape=[jax.ShapeDtypeStruct((nt, wd), dt) for wd, dt in widths],
        compiler_params=_params(("arbitrary",)),
        name="inproj",
    )(xa, xb, mod, w, gp, cos_t, sin_t)


def _hgrn_consts():
    t = np.arange(HB)[:, None]
    u = np.arange(HB)[None, :]
    same = (t // HC) == (u // HC)
    tr, ur = t % HC, u % HC
    half = HC // 2
    mats, codes = [], []
    for fwd in (True, False):
        if fwd:
            incl = ur <= tr
            mid = ur <= (tr // half) * half + half // 2 - 1
            edge = ur <= half - 1
            code = np.where((t // half == u // half) & (u <= t), 1, np.where(same & (tr >= half) & (ur < half), 2, 0))
        else:
            incl = ur >= tr
            mid = ur >= (tr // half) * half + half // 2
            edge = ur >= half
            code = np.where((t // half == u // half) & (u >= t), 1, np.where(same & (tr < half) & (ur >= half), 2, 0))
        mats.append(np.concatenate([same & incl, same, same & mid, same & edge], axis=0))
        codes.append(code)
    return (np.stack(mats).astype(np.float32).astype(jnp.bfloat16), np.stack(codes).astype(np.float32))


def _hgrn_kernel(qf_ref, vf_ref, lff_ref, kf_ref, qb_ref, vb_ref, lfb_ref, kb_ref, cm_ref, code_ref,
                 of_ref, ob_ref, st_ref, qd_ref, kd_ref, vv_ref, gt_ref, it_ref):
    nch = HB // HC
    n_heads = HGRN_W // HEAD_DIM

    @pl.when(pl.program_id(1) == 0)
    def _():
        st_ref[...] = jnp.zeros(st_ref.shape, F32)

    lane3 = lax.broadcasted_iota(jnp.int32, (nch, HC, HGRN_W), 2) // HEAD_DIM

    def head_rows(x):
        x3 = x.astype(BF16).reshape(nch, HC, HGRN_W)
        zero = jnp.zeros_like(x3)
        return jnp.concatenate([jnp.where(lane3 == h, x3, zero) for h in range(n_heads)], axis=1)

    def prep(d, q_ref, v_ref, lf_ref, k_ref):
        hi, lo = _split_bf16(lf_ref[...])
        cm = cm_ref[d]
        acc = _dot(jnp.concatenate([cm[0:2 * HB], cm[0:2 * HB]], axis=1), jnp.concatenate([hi, lo], axis=0))
        ref = _dot(cm[2 * HB:4 * HB], hi)
        cum, tot = acc[0:HB], acc[HB:2 * HB]
        mid, edge = ref[0:HB], ref[HB:2 * HB]
        q = q_ref[...].astype(F32)
        k = k_ref[...].astype(F32)
        v = v_ref[...]
        qd_ref[d] = head_rows(q * jnp.exp(cum))
        kd_ref[d] = head_rows(k * jnp.exp(tot - cum))
        vv_ref[d] = jnp.concatenate([v[:, h * HEAD_DIM:(h + 1) * HEAD_DIM].reshape(nch, HC, HEAD_DIM)
                                     for h in range(n_heads)], axis=1)
        gt_ref[d] = jnp.exp(tot)
        q1 = (q * jnp.exp(cum - mid)).astype(BF16)
        k1 = (k * jnp.exp(mid - cum)).astype(BF16)
        upper = (lax.broadcasted_iota(jnp.int32, q.shape, 0) % HC >= HC // 2) == (d == 0)
        q2 = jnp.where(upper, q * jnp.exp(cum - edge), 0.0).astype(BF16)
        k2 = jnp.where(upper, 0.0, k * jnp.exp(edge - cum)).astype(BF16)
        s1 = _dot_nt(_head_stack(q1, n_heads), k1)
        s2 = _dot_nt(_head_stack(q2, n_heads), k2)
        code = jnp.concatenate([code_ref[d]] * n_heads, axis=0)
        p = jnp.where(code == 1.0, s1, jnp.where(code == 2.0, s2, 0.0))
        return _head_unstack(_dot(p.astype(BF16), v), n_heads)

    od_f = prep(0, qf_ref, vf_ref, lff_ref, kf_ref)
    od_b = prep(1, qb_ref, vb_ref, lfb_ref, kb_ref)

    for i in range(nch):
        for d, c in ((0, i), (1, nch - 1 - i)):
            st = st_ref[d]
            it_ref[d, c] = _dot_nt(qd_ref[d, c], st.astype(BF16))
            upd = _dot_tn(vv_ref[d, c], kd_ref[d, c])
            st_ref[d] = st * gt_ref[d, c * HC:c * HC + 1, :] + upd

    def inter(d):
        it = it_ref[d]
        return jnp.concatenate([it[:, h * HC:(h + 1) * HC, :].reshape(HB, HEAD_DIM) for h in range(n_heads)],
                               axis=1)

    of_ref[...] = inter(0) + od_f
    ob_ref[...] = inter(1) + od_b


def _hgrn(hq, hv, lf, kk, n_b, l_, ctx_len):
    nt = hq.shape[0]
    assert ctx_len == HB
    nlb = l_ // HB
    ctx0 = n_b * nlb
    cmats, codes = _hgrn_consts()
    heads = HGRN_W // HEAD_DIM

    def fwd_idx(col):
        return lambda b, s: (jnp.where(s == 0, ctx0 + b, b * nlb + s - 1), col)

    def bwd_idx(col):
        return lambda b, s: (jnp.where(s == 0, ctx0 + b, b * nlb + nlb - s), col)

    def blk(idx):
        return pl.BlockSpec((HB, HGRN_W), idx)

    full = lambda a: pl.BlockSpec(a.shape, lambda b, s: (0,) * a.ndim)
    return pl.pallas_call(
        _hgrn_kernel,
        grid=(n_b, nlb + 1),
        in_specs=[blk(fwd_idx(0)), blk(fwd_idx(0)), blk(fwd_idx(0)), blk(fwd_idx(0)),
                  blk(bwd_idx(0)), blk(bwd_idx(0)), blk(bwd_idx(1)), blk(bwd_idx(1)),
                  full(cmats), full(codes)],
        out_specs=[blk(fwd_idx(0)), blk(bwd_idx(0))],
        out_shape=[jax.ShapeDtypeStruct((nt, HGRN_W), F32)] * 2,
        scratch_shapes=[pltpu.VMEM((2, HEAD_DIM, HGRN_W), F32),
                        pltpu.VMEM((2, HB // HC, heads * HC, HGRN_W), BF16),
                        pltpu.VMEM((2, HB // HC, heads * HC, HGRN_W), BF16),
                        pltpu.VMEM((2, HB // HC, heads * HC, HEAD_DIM), BF16),
                        pltpu.VMEM((2, HB, HGRN_W), F32),
                        pltpu.VMEM((2, HB // HC, heads * HC, HEAD_DIM), F32)],
        compiler_params=_params(("arbitrary", "arbitrary")),
        name="hgrn",
    )(hq, hv, lf, kk, hq, hv, lf, kk, cmats, codes)


def _head_stack(q, n_heads):
    lane = lax.broadcasted_iota(jnp.int32, q.shape, 1)
    zero = jnp.zeros_like(q)
    return jnp.concatenate([jnp.where(lane // HEAD_DIM == h, q, zero) for h in range(n_heads)], axis=0)


def _head_unstack(o, n_heads):
    rows = o.shape[0] // n_heads
    lane = lax.broadcasted_iota(jnp.int32, (rows, o.shape[1]), 1)
    acc = jnp.zeros((rows, o.shape[1]), F32)
    for h in range(n_heads):
        acc = acc + jnp.where(lane // HEAD_DIM == h, o[h * rows:(h + 1) * rows], 0.0)
    return acc


def _na_kernel(n_rows, n_lat_steps, q_ref, k_ref, v_ref, kc_ref, vc_ref, bias_ref, o_ref):
    j = pl.program_id(1)
    kc = kc_ref[...]
    vc = vc_ref[...]
    nwin = NA_WIN_ROWS * GRID_W

    def attend(rr, win):
        q4 = _head_stack(q_ref[rr * GRID_W:(rr + 1) * GRID_W, :], NA_HEADS)
        s_ctx = _dot_nt(q4, kc)
        m = jnp.max(s_ctx, axis=1, keepdims=True)
        if win is not None:
            kw, vw, bias = win
            s_win = _dot_nt(q4, kw) + bias
            m = jnp.maximum(m, jnp.max(s_win, axis=1, keepdims=True))
            p_win = jnp.exp(s_win - m)
        p_ctx = jnp.exp(s_ctx - m)
        den = jnp.sum(p_ctx, axis=1, keepdims=True)
        o4 = _dot(p_ctx.astype(BF16), vc)
        if win is not None:
            den = den + jnp.sum(p_win, axis=1, keepdims=True)
            o4 = o4 + _dot(p_win.astype(BF16), vw)
        o = _head_unstack(o4 * (1.0 / den), NA_HEADS)
        o_ref[rr * GRID_W:(rr + 1) * GRID_W, :] = o.astype(BF16)

    @pl.when(j < n_lat_steps)
    def _():
        for rr in range(NA_QROWS):
            r = j * NA_QROWS + rr
            rs = jnp.clip(r - NA_WIN_ROWS // 2, 0, n_rows - NA_WIN_ROWS)
            start = pl.multiple_of(rs * GRID_W, GRID_W)
            kw = k_ref[pl.ds(start, nwin), :]
            vw = v_ref[pl.ds(start, nwin), :]
            first = NA_WIN_ROWS - 1 - (r - rs)
            bias = jnp.concatenate([bias_ref[0, first + 2 * p] for p in range(NA_WIN_ROWS // 2)], axis=1)
            attend(rr, (kw, vw, bias))

    @pl.when(j >= n_lat_steps)
    def _():
        for rr in range(NA_QROWS):
            attend(rr, None)


def _na(nqkv, bias, layer, n_b, l_, ctx_len, with_ctx):
    nt = nqkv.shape[0]
    qb = NA_QROWS * GRID_W
    assert ctx_len == qb
    n_lat = l_ // qb
    ctx0 = n_b * n_lat
    n_rows = l_ // GRID_W

    def q_idx(b, j):
        return (jnp.where(j < n_lat, b * n_lat + j, ctx0 + b), 0)

    cblk = (n_b * l_) // ctx_len
    return pl.pallas_call(
        functools.partial(_na_kernel, n_rows, n_lat),
        grid=(n_b, n_lat + (1 if with_ctx else 0)),
        in_specs=[pl.BlockSpec((qb, 256), q_idx),
                  pl.BlockSpec((l_, 256), lambda b, j: (b, 1)),
                  pl.BlockSpec((l_, 256), lambda b, j: (b, 2)),
                  pl.BlockSpec((ctx_len, 256), lambda b, j: (cblk + b, 1)),
                  pl.BlockSpec((ctx_len, 256), lambda b, j: (cblk + b, 2)),
                  pl.BlockSpec((1,) + bias.shape[1:], lambda b, j: (layer, 0, 0, 0))],
        out_specs=pl.BlockSpec((qb, 256), q_idx),
        out_shape=jax.ShapeDtypeStruct((nt if with_ctx else n_b * l_, 256), BF16),
        compiler_params=_params(("arbitrary", "arbitrary")),
        name="natten",
    )(nqkv, nqkv, nqkv, nqkv, nqkv, bias)


def _swa_band_bias(n_blocks):
    sb = SWA_BLOCK
    u = np.arange(sb)[:, None]
    v = np.arange(3 * sb)[None, :]
    tabs = []
    for j in (0, 1, n_blocks - 1):
        kpos = (j - 1) * sb + v
        ok = (kpos >= 0) & (kpos < n_blocks * sb) & (np.abs(j * sb + u - kpos) <= SWA_WINDOW)
        tabs.append(np.tile(np.where(ok, 0.0, NEG), (SWA_GROUP, 1)))
    return np.stack(tabs).astype(np.float32)


def _swa_kernel(n_steps, layer, sink_ref, q_ref, kp_ref, kq_ref, kn_ref, vp_ref, vq_ref, vn_ref,
                kc_ref, vc_ref, bias_ref, o_ref):
    j = pl.program_id(1)
    sb = SWA_BLOCK

    def attend(a, n, band):
        cols = slice(n * MXU_N, (n + 1) * MXU_N)
        rows = slice(a * sb, (a + 1) * sb)
        q4 = _head_stack(q_ref[rows, cols], SWA_GROUP)
        sink = jnp.concatenate([jnp.full((sb, 1), sink_ref[layer, n * SWA_GROUP + g], F32)
                                for g in range(SWA_GROUP)], axis=0)
        if band is None:
            keys, vals = kc_ref[:, cols], vc_ref[:, cols]
            s = _dot_nt(q4, keys)
        else:
            k_refs, v_refs, bias = band
            keys = jnp.concatenate([r[rs, cols] for r, rs in k_refs] + [kc_ref[:, cols]], axis=0)
            vals = jnp.concatenate([r[rs, cols] for r, rs in v_refs] + [vc_ref[:, cols]], axis=0)
            s = _dot_nt(q4, keys)
            s = jnp.concatenate([s[:, 0:3 * sb] + bias, s[:, 3 * sb:]], axis=1)
        m = jnp.maximum(jnp.max(s, axis=1, keepdims=True), sink)
        p = jnp.exp(s - m)
        den = jnp.sum(p, axis=1, keepdims=True) + jnp.exp(sink - m)
        o4 = _dot(p.astype(BF16), vals)
        o_ref[rows, cols] = _head_unstack(o4 * (1.0 / den), SWA_GROUP).astype(BF16)

    lo, hi, whole = slice(0, sb), slice(sb, 2 * sb), slice(0, sb)

    @pl.when(j < n_steps)
    def _():
        bands = [([(kp_ref, whole), (kq_ref, lo), (kq_ref, hi)], [(vp_ref, whole), (vq_ref, lo), (vq_ref, hi)],
                  bias_ref[jnp.where(j == 0, 0, 1)]),
                 ([(kq_ref, lo), (kq_ref, hi), (kn_ref, whole)], [(vq_ref, lo), (vq_ref, hi), (vn_ref, whole)],
                  bias_ref[jnp.where(j == n_steps - 1, 2, 1)])]
        for a in range(SWA_STEP_BLOCKS):
            for n in range(SWA_KV_HEADS):
                attend(a, n, bands[a])

    @pl.when(j >= n_steps)
    def _():
        for a in range(SWA_STEP_BLOCKS):
            for n in range(SWA_KV_HEADS):
                attend(a, n, None)


def _swa(sq, kx, vx, sink, layer, n_b, l_, ctx_len, with_ctx):
    nt = sq.shape[0]
    sb = SWA_BLOCK
    qb = SWA_STEP_BLOCKS * sb
    nb = l_ // sb
    ns = l_ // qb
    assert SWA_STEP_BLOCKS == 2 and nb % 2 == 0 and nb >= 4 and ctx_len == qb
    ctx0 = (n_b * l_) // qb
    bias = _swa_band_bias(nb)

    def q_idx(b, j):
        return (jnp.where(j < ns, b * ns + j, ctx0 + b), 0)

    def cur_idx(b, j):
        return (b * ns + jnp.minimum(j, ns - 1), 0)

    def edge_idx(off):
        return lambda b, j: (b * nb + jnp.clip(SWA_STEP_BLOCKS * jnp.minimum(j, ns - 1) + off, 0, nb - 1), 0)

    edge = lambda idx: pl.BlockSpec((sb, 512), idx)
    wide = lambda idx: pl.BlockSpec((qb, 512), idx)
    cspec = pl.BlockSpec((ctx_len, 512), lambda b, j: (ctx0 + b, 0))
    return pl.pallas_call(
        functools.partial(_swa_kernel, ns, layer),
        grid=(n_b, ns + (1 if with_ctx else 0)),
        in_specs=[pl.BlockSpec(memory_space=pltpu.SMEM),
                  wide(q_idx), edge(edge_idx(-1)), wide(cur_idx), edge(edge_idx(SWA_STEP_BLOCKS)),
                  edge(edge_idx(-1)), wide(cur_idx), edge(edge_idx(SWA_STEP_BLOCKS)), cspec, cspec,
                  pl.BlockSpec(bias.shape, lambda b, j: (0, 0, 0))],
        out_specs=wide(q_idx),
        out_shape=jax.ShapeDtypeStruct((nt if with_ctx else n_b * l_, 512), BF16),
        compiler_params=_params(("arbitrary", "arbitrary")),
        name="swa",
    )(sink, sq, kx, kx, kx, vx, vx, vx, kx, vx, bias)


def _layer_norm(r, g, b):
    mu = jnp.mean(r, axis=-1, keepdims=True)
    rc = r - mu
    var = jnp.mean(rc * rc, axis=-1, keepdims=True)
    return rc * lax.rsqrt(var + LN_EPS) * g + b


def _route(logits_t, bias_col):
    epg = N_EXPERTS // N_GROUPS
    s = _sigmoid(logits_t)
    sel = s + bias_col
    sel_r = [sel[e:e + 1, :] for e in range(N_EXPERTS)]
    s_r = [s[e:e + 1, :] for e in range(N_EXPERTS)]
    grp = []
    for g in range(N_GROUPS):
        a = sel_r[g * epg:(g + 1) * epg]
        m1 = functools.reduce(jnp.maximum, a)
        m2 = functools.reduce(jnp.maximum,
                              [jnp.minimum(a[i], a[k]) for i in range(epg) for k in range(i + 1, epg)])
        grp.append(m1 + m2)
    m_r, w_r = [], []
    for g in range(N_GROUPS):
        best = None
        for k in range(N_GROUPS):
            if k == g:
                continue
            c = (grp[g] > grp[k]) if k < g else (grp[g] >= grp[k])
            best = c if best is None else (best & c)
        for e in range(g * epg, (g + 1) * epg):
            rank = jnp.zeros_like(sel_r[e])
            for k in range(g * epg, (g + 1) * epg):
                if k == e:
                    continue
                ahead = (sel_r[k] >= sel_r[e]) if k < e else (sel_r[k] > sel_r[e])
                rank = rank + jnp.where(ahead, 1.0, 0.0)
            chosen = jnp.where(best & (rank < 1.5), 1.0, 0.0)
            m_r.append(chosen)
            w_r.append(chosen * s_r[e])
    inv = 1.0 / functools.reduce(lambda a, b: a + b, w_r)
    return m_r, [w * inv for w in w_r]


def _pack_bf16_pairs(a, b):
    ua = pltpu.bitcast(a.astype(BF16).astype(F32), jnp.uint32)
    ub = pltpu.bitcast(b.astype(BF16).astype(F32), jnp.uint32)
    return (ua >> 16) | ub


def _unpack_bf16_pairs(u):
    return (pltpu.bitcast(u << 16, F32), pltpu.bitcast(u & jnp.uint32(0xFFFF0000), F32))


def _outproj_kernel(n_lat, of_ref, ob_ref, hg_ref, yb_ref, yc_ref, xa_ref, xb_ref, mod_ref, wo_ref, pv_ref,
                    wrh_ref, wrl_ref, rb_ref, before_ref, x1_ref, h2u_ref, route_ref, gt_ref, cnt_ref):
    @pl.when(pl.program_id(0) == 0)
    def _():
        cnt_ref[...] = jnp.zeros(cnt_ref.shape, F32)

    row = lax.broadcasted_iota(jnp.int32, (HGRN_W, HGRN_W), 0)
    col = lax.broadcasted_iota(jnp.int32, (HGRN_W, HGRN_W), 1)
    head_ones = jnp.where((row // HEAD_DIM) == (col // HEAD_DIM), 1.0, 0.0).astype(BF16)
    pv = pv_ref[0]
    o = of_ref[...] + ob_ref[...]
    hi, lo = _split_bf16(o * o)
    ms = _dot(jnp.concatenate([hi, lo], axis=1), jnp.concatenate([head_ones, head_ones], axis=0)) * (1.0 / HEAD_DIM)
    gate = hg_ref[...].astype(F32)
    ya = o * lax.rsqrt(ms + RMS_EPS) * pv[2:3, 0:HGRN_W] * (gate * _sigmoid(gate))
    y = _dot(jnp.concatenate([ya.astype(BF16), yb_ref[...], yc_ref[...]], axis=1), wo_ref[0])
    mod = mod_ref[0, 0]
    x = jnp.where(pl.program_id(0) < n_lat, xa_ref[...], xb_ref[...])
    x1 = _layer_norm(ALPHA * x + mod[2:3] * y, pv[0:1], pv[1:2])
    x1_ref[...] = x1
    h2 = x1 * (1.0 + mod[4:5]) + mod[3:4]
    hh, hl = _split_bf16(h2)
    half = h2.shape[1] // 2
    h2u_ref[...] = _pack_bf16_pairs(h2[:, 0:half], h2[:, half:2 * half])
    wrh = wrh_ref[...]
    logits_t = _dot_nt(jnp.concatenate([wrh, wrh, wrl_ref[...]], axis=1),
                       jnp.concatenate([hh, hl, hh], axis=1))
    m_r, w_r = _route(logits_t, rb_ref[...])
    m_t = jnp.concatenate(m_r, axis=0)
    rank_t = _dot(m_t.astype(BF16), before_ref[...]) + cnt_ref[:, 0:1]
    cnt_ref[...] = cnt_ref[...] + jnp.sum(m_t, axis=1, keepdims=True)
    seen = jnp.zeros_like(m_r[0])
    e1 = e2 = r1 = r2 = g1 = g2 = jnp.zeros_like(m_r[0])
    for e in range(N_EXPERTS):
        is1 = m_r[e] * (1.0 - seen)
        is2 = m_r[e] * seen
        rk = rank_t[e:e + 1, :]
        e1, e2 = e1 + e * is1, e2 + e * is2
        r1, r2 = r1 + rk * is1, r2 + rk * is2
        g1, g2 = g1 + w_r[e] * is1, g2 + w_r[e] * is2
        seen = seen + is1
    zero = jnp.zeros_like(e1)
    route_ref[...] = jnp.concatenate([e1, e2, r1, r2, g1, g2, zero, zero], axis=0)
    pad = jnp.zeros((LANES - 2, g1.shape[1]), F32)
    gt_ref[...] = jnp.concatenate([g1, g2, pad], axis=0).T


def _outproj(o_f, o_b, hg, yb, yc, xa, xb, ctx_blk, mod, wo, pv, layer, wrh, wrl, rb, n_b, l_, n_tiles):
    d = xa.shape[1]
    tpb = l_ // TM
    n_lat = n_b * tpb
    before = np.triu(np.ones((TM, TM), np.float32), 1).astype(jnp.bfloat16)

    def mod_idx(j):
        return (layer, jnp.minimum(j // tpb, n_b), 0, 0)

    def rows(width):
        return pl.BlockSpec((TM, width), lambda j: (j, 0))

    full = lambda a: pl.BlockSpec(a.shape, lambda j: (0,) * a.ndim)
    of_layer = lambda a: pl.BlockSpec((1,) + a.shape[1:], lambda j: (layer,) + (0,) * (a.ndim - 1))
    nr = n_tiles * TM
    return pl.pallas_call(
        functools.partial(_outproj_kernel, n_lat),
        grid=(n_tiles,),
        in_specs=[rows(256), rows(256), rows(256), rows(256), rows(512),
                  pl.BlockSpec((TM, d), lambda j: (jnp.minimum(j, n_lat - 1), 0)),
                  pl.BlockSpec((TM, d), lambda j: (ctx_blk, 0)),
                  pl.BlockSpec((1, 1, 6, d), mod_idx), of_layer(wo), of_layer(pv), full(wrh), full(wrl), full(rb),
                  full(before)],
        out_specs=[rows(d), rows(d // 2), pl.BlockSpec((8, TM), lambda j: (0, j)), rows(LANES),
                   pl.BlockSpec((N_EXPERTS, LANES), lambda j: (0, 0))],
        out_shape=[jax.ShapeDtypeStruct((nr, d), F32), jax.ShapeDtypeStruct((nr, d // 2), jnp.uint32),
                   jax.ShapeDtypeStruct((8, nr), F32), jax.ShapeDtypeStruct((nr, LANES), F32),
                   jax.ShapeDtypeStruct((N_EXPERTS, LANES), F32)],
        compiler_params=_params(("arbitrary",)),
        name="outproj",
    )(o_f, o_b, hg, yb, yc, xa, xb, mod, wo, pv, wrh, wrl, rb, before)


def _sc_split(rows_per_worker):
    best = None
    for chunk in (64, 48, 32):
        for nbuf in range(SC_MAX_ROWS // chunk, 0, -1):
            if rows_per_worker % (chunk * nbuf) == 0 and (best is None or chunk * nbuf > best[0] * best[1]):
                best = (chunk, nbuf)
    return best


def _sc_workers():
    info = plsc.get_sparse_core_info()
    return info.num_cores, info.num_cores * info.num_subcores


def _sc_scatter_rows(table, pos, n_slots):
    n_tok, width = table.shape
    n_cores, n_workers = _sc_workers()
    per_w = n_tok // n_workers
    assert n_tok % n_workers == 0
    chunk, nbuf = _sc_split(per_w)
    n_chunks = per_w // chunk
    mesh = plsc.VectorSubcoreMesh(core_axis_name="c", subcore_axis_name="s")

    @functools.partial(
        pl.kernel, mesh=mesh,
        out_type=jax.ShapeDtypeStruct((n_slots, width), table.dtype),
        scratch_types=[pltpu.VMEM((2 * n_chunks, chunk), jnp.int32),
                       pltpu.VMEM((nbuf, chunk, width), table.dtype),
                       pltpu.SemaphoreType.DMA((nbuf,)),
                       pltpu.SemaphoreType.DMA((nbuf,))],
    )
    def scatter(table_hbm, pos_hbm, out_hbm, idx_v, rows_v, rsem, wsem):
        wid = lax.axis_index("s") * n_cores + lax.axis_index("c")
        base = wid * per_w
        pltpu.sync_copy(pos_hbm.at[wid], idx_v)

        @pl.loop(0, n_chunks // nbuf)
        def _(g):
            c0 = g * nbuf
            gets = [pltpu.async_copy(table_hbm.at[pl.ds(base + (c0 + b) * chunk, chunk)], rows_v.at[b],
                                     rsem.at[b]) for b in range(nbuf)]
            puts = []
            for b in range(nbuf):
                gets[b].wait()
                for k in range(2):
                    puts.append(pltpu.async_copy(rows_v.at[b], out_hbm.at[idx_v.at[k * n_chunks + c0 + b]],
                                                 wsem.at[b]))
            for put in puts:
                put.wait()

    pos_w = pos.reshape(2, n_workers, n_chunks, chunk).transpose(1, 0, 2, 3).reshape(n_workers, 2 * n_chunks, chunk)
    return scatter(table, pos_w)


def _sc_gather_rows(table, idx):
    n_out = idx.shape[0]
    width = table.shape[1]
    n_cores, n_workers = _sc_workers()
    per_w = n_out // n_workers
    assert n_out % n_workers == 0
    chunk, nbuf = _sc_split(per_w)
    mesh = plsc.VectorSubcoreMesh(core_axis_name="c", subcore_axis_name="s")

    @functools.partial(
        pl.kernel, mesh=mesh,
        out_type=jax.ShapeDtypeStruct((n_out, width), table.dtype),
        scratch_types=[pltpu.VMEM((per_w,), jnp.int32),
                       pltpu.VMEM((nbuf, chunk, width), table.dtype),
                       pltpu.SemaphoreType.DMA((nbuf,)),
                       pltpu.SemaphoreType.DMA((nbuf,))],
    )
    def gather(table_hbm, idx_hbm, out_hbm, idx_v, rows_v, gsem, wsem):
        wid = lax.axis_index("s") * n_cores + lax.axis_index("c")
        base = wid * per_w
        pltpu.sync_copy(idx_hbm.at[pl.ds(base, per_w)], idx_v)

        @pl.loop(0, per_w // (chunk * nbuf))
        def _(g):
            off = g * (chunk * nbuf)
            gets = [pltpu.async_copy(table_hbm.at[idx_v.at[pl.ds(off + b * chunk, chunk)]], rows_v.at[b],
                                     gsem.at[b]) for b in range(nbuf)]
            puts = []
            for b in range(nbuf):
                gets[b].wait()
                puts.append(pltpu.async_copy(rows_v.at[b], out_hbm.at[pl.ds(base + off + b * chunk, chunk)],
                                             wsem.at[b]))
            for put in puts:
                put.wait()

    return gather(table, idx)


def _gmm_kernel(te_ref, nv_ref, nu_ref, xs_ref, wg_ref, wu_ref, wd_ref, ys_ref, wgb_ref, wub_ref, wdb_ref):
    i = pl.program_id(0)
    used = i < nu_ref[0]
    fresh = jnp.logical_or(i == 0, te_ref[i] != te_ref[jnp.maximum(i - 1, 0)])

    @pl.when(jnp.logical_and(used, fresh))
    def _():
        wgb_ref[...] = wg_ref[0, 0].astype(BF16)
        wub_ref[...] = wu_ref[0, 0].astype(BF16)
        wdb_ref[...] = wd_ref[0, 0].astype(BF16)

    @pl.when(used)
    def _():
        rowid = lax.broadcasted_iota(jnp.int32, xs_ref.shape, 0)
        lo, hi = _unpack_bf16_pairs(jnp.where(rowid < nv_ref[i], xs_ref[...], jnp.uint32(0)))
        lo, hi = lo.astype(BF16), hi.astype(BF16)
        half = lo.shape[1]
        x = jnp.concatenate([lo, hi], axis=1)
        a = _dot(x, wgb_ref[...])
        u = _dot(x, wub_ref[...])
        y = _dot((a * _sigmoid(a) * u).astype(BF16), wdb_ref[...])
        ys_ref[...] = _pack_bf16_pairs(y[:, 0:half], y[:, half:2 * half])

    @pl.when(jnp.logical_not(used))
    def _():
        ys_ref[...] = jnp.zeros(ys_ref.shape, jnp.uint32)


def _gmm(tile_expert, tile_valid, n_used, xs, wg, wu, wd, layer):
    p_rows, half = xs.shape
    d = 2 * half
    wspec = lambda r, c: pl.BlockSpec((1, 1, r, c), lambda i, te, nv, nu: (layer, te[i], 0, 0))
    return pl.pallas_call(
        _gmm_kernel,
        grid_spec=pltpu.PrefetchScalarGridSpec(
            num_scalar_prefetch=3,
            grid=(p_rows // TM,),
            in_specs=[pl.BlockSpec((TM, half), lambda i, te, nv, nu: (i, 0)),
                      wspec(d, D_EXPERT), wspec(d, D_EXPERT), wspec(D_EXPERT, d)],
            out_specs=pl.BlockSpec((TM, half), lambda i, te, nv, nu: (i, 0)),
            scratch_shapes=[pltpu.VMEM((d, D_EXPERT), BF16), pltpu.VMEM((d, D_EXPERT), BF16),
                            pltpu.VMEM((D_EXPERT, d), BF16)]),
        out_shape=jax.ShapeDtypeStruct((p_rows, half), jnp.uint32),
        compiler_params=_params(("arbitrary",)),
        name="moe_experts",
    )(tile_expert, tile_valid, n_used, xs, wg, wu, wd)


def _moefin_kernel(o1_ref, o2_ref, gt_ref, x_ref, mod_ref, pv_ref, out_ref):
    gt = gt_ref[...]
    g1, g2 = gt[:, 0:1], gt[:, 1:2]
    a1, b1 = _unpack_bf16_pairs(o1_ref[...])
    a2, b2 = _unpack_bf16_pairs(o2_ref[...])
    f = jnp.concatenate([g1 * a1 + g2 * a2, g1 * b1 + g2 * b2], axis=1)
    mod = mod_ref[0, 0]
    pv = pv_ref[0]
    out_ref[...] = _layer_norm(ALPHA * x_ref[...] + mod[5:6] * f, pv[3:4], pv[4:5])


def _moefin(o12, gt, x1, mod, pv, layer, n_b, l_, n_tiles):
    d = x1.shape[1]
    tpb = l_ // TM

    def mod_idx(j):
        return (layer, jnp.minimum(j // tpb, n_b), 0, 0)

    return pl.pallas_call(
        _moefin_kernel,
        grid=(n_tiles,),
        in_specs=[pl.BlockSpec((TM, d // 2), lambda j: (j, 0)),
                  pl.BlockSpec((TM, d // 2), lambda j: (j + n_tiles, 0)),
                  pl.BlockSpec((TM, LANES), lambda j: (j, 0)),
                  pl.BlockSpec((TM, d), lambda j: (j, 0)),
                  pl.BlockSpec((1, 1, 6, d), mod_idx),
                  pl.BlockSpec((1,) + pv.shape[1:], lambda j: (layer, 0, 0))],
        out_specs=pl.BlockSpec((TM, d), lambda j: (j, 0)),
        out_shape=jax.ShapeDtypeStruct((n_tiles * TM, d), F32),
        compiler_params=_params(("arbitrary",)),
        name="moe_combine",
    )(o12, o12, gt, x1, mod, pv)


def _moe_routed(h2u, route, gt, counts, wg, wu, wd, layer, x1, mod, pv, n_b, l_, n_tiles):
    n_tok = n_tiles * TM
    n_slots = -(-(2 * n_tok + N_EXPERTS * TM) // SC_ROW_QUANTUM) * SC_ROW_QUANTUM
    cnt = counts[:, 0].astype(jnp.int32)
    padded = ((cnt + TM - 1) // TM) * TM
    upto = jnp.arange(N_EXPERTS)[None, :] <= jnp.arange(N_EXPERTS)[:, None]
    ends = jnp.sum(jnp.where(upto, padded[None, :], 0), axis=1)
    offs = ends - padded
    e12 = route[0:2].astype(jnp.int32)
    r12 = route[2:4].astype(jnp.int32)
    onehot = e12[:, :, None] == jnp.arange(N_EXPERTS, dtype=jnp.int32)
    pos = r12 + jnp.sum(jnp.where(onehot, offs, 0), axis=-1)
    tile_start = jnp.arange(n_slots // TM, dtype=jnp.int32) * TM
    tile_expert = jnp.minimum(jnp.sum(tile_start[:, None] >= ends[None, :], axis=1), N_EXPERTS - 1)
    pick = tile_expert[:, None] == jnp.arange(N_EXPERTS, dtype=jnp.int32)
    tile_valid = jnp.clip(jnp.sum(jnp.where(pick, (offs + cnt)[None, :], 0), axis=1) - tile_start, 0, TM)
    n_used = (ends[N_EXPERTS - 1] // TM).reshape(1)
    xs = _sc_scatter_rows(h2u, pos, n_slots)
    ys = _gmm(tile_expert.astype(jnp.int32), tile_valid.astype(jnp.int32), n_used.astype(jnp.int32),
              xs, wg, wu, wd, layer)
    o12 = _sc_gather_rows(ys, pos.reshape(2 * n_tok))
    return _moefin(o12, gt, x1, mod, pv, layer, n_b, l_, n_tiles)


def _rope_tables(l_):
    pos = np.arange(l_)
    nf = HEAD_DIM // 4
    inv = ROPE_BASE ** (-np.arange(nf, dtype=np.float64) / nf)
    ar = (pos // GRID_W)[:, None] * inv
    ac = (pos % GRID_W)[:, None] * inv
    cos = np.concatenate([np.cos(ar), np.cos(ar), np.cos(ac), np.cos(ac)], axis=1)
    sin = np.concatenate([-np.sin(ar), np.sin(ar), -np.sin(ac), np.sin(ac)], axis=1)
    cos = np.concatenate([np.tile(cos, (1, 2)), np.ones((TM, LANES))], axis=0)
    sin = np.concatenate([np.tile(sin, (1, 2)), np.zeros((TM, LANES))], axis=0)
    return cos.astype(np.float32), sin.astype(np.float32)


def _na_bias(rpb):
    u = np.arange(GRID_W)[:, None]
    v = np.arange(GRID_W)[None, :]
    cs = np.clip(u - NA_WIN_COLS // 2, 0, GRID_W - NA_WIN_COLS)
    colmask = (v >= cs) & (v < cs + NA_WIN_COLS)
    coff = np.clip(v - u + NA_WIN_COLS - 1, 0, 2 * NA_WIN_COLS - 2)
    n_off = 2 * NA_WIN_COLS - 1
    pick = (coff[None] == np.arange(n_off)[:, None, None]).astype(np.float32)
    band = jnp.einsum('lhab,buv->lahuv', rpb.astype(F32), pick, precision=lax.Precision.HIGHEST)
    band = jnp.where(colmask[None, None, None], band, NEG)
    band = band.reshape(rpb.shape[0], 2 * NA_WIN_ROWS - 1, NA_HEADS * GRID_W, GRID_W)
    return jnp.concatenate([band[:, :-1], band[:, 1:]], axis=-1)


def kernel(x, c, ctx, c_ctx, w_ada, b_ada, w_in, lb_logits, hgrn_norm, na_rpb, swa_sink, w_out,
           ln1_g, ln1_b, ln2_g, ln2_b, w_router, router_bias, w_gate, w_up, w_down):
    n_b, l_, d = x.shape
    ctx_len = ctx.shape[1]
    depth = w_ada.shape[0]
    assert n_b * ctx_len == TM and l_ % TM == 0 and n_b + 1 <= N_COND
    n_lat_tiles = (n_b * l_) // TM

    p_lb = jax.nn.softmax(lb_logits.astype(F32), axis=0).reshape(depth, 2 * HGRN_W)
    upto = (np.arange(depth)[None, :] <= np.arange(depth)[:, None]).astype(np.float32)
    lb = jnp.maximum(jnp.sum(upto[:, :, None] * p_lb[None], axis=1) - p_lb[0:1], LB_MIN)
    gp = jnp.stack([jnp.log(lb), jnp.log1p(-lb), 1.0 - lb] + [jnp.zeros_like(lb)] * 5, axis=1)

    cond = jnp.concatenate([c, c_ctx[None, :], jnp.zeros((N_COND - n_b - 1, d), F32)], axis=0)
    mod = _ada(cond.T, w_ada, b_ada, n_b + 1).reshape(depth, N_COND, 6, d)

    cos_t, sin_t = _rope_tables(l_)
    wr_t = w_router.T
    wrh = wr_t.astype(BF16)
    wrl = (wr_t - wrh.astype(F32)).astype(BF16)
    rb = router_bias.astype(F32)[:, None]
    w_in_b = w_in.astype(BF16)
    w_out_b = w_out.astype(BF16)
    zeros = jnp.zeros_like(ln1_g)
    pv = jnp.stack([ln1_g, ln1_b, jnp.tile(hgrn_norm, (1, d // HEAD_DIM)), ln2_g, ln2_b, zeros, zeros, zeros],
                   axis=1).astype(F32)
    na_bias = _na_bias(na_rpb)
    sink = swa_sink.astype(F32)

    xa, xb, ctx_blk = x.reshape(n_b * l_, d), ctx.reshape(n_b * ctx_len, d), 0
    for l in range(depth):
        last = l == depth - 1
        hq, hv, hg, lf, kk, nqkv, sq, kx, vx = _inproj(xa, xb, ctx_blk, mod, w_in_b, gp, l, cos_t, sin_t, n_b, l_)
        o_f, o_b = _hgrn(hq, hv, lf, kk, n_b, l_, ctx_len)
        yb = _na(nqkv, na_bias, l, n_b, l_, ctx_len, not last)
        yc = _swa(sq, kx, vx, sink, l, n_b, l_, ctx_len, not last)
        n_tiles = n_lat_tiles if last else n_lat_tiles + 1
        x1, h2u, route, gt, counts = _outproj(o_f, o_b, hg, yb, yc, xa, xb, ctx_blk, mod, w_out_b, pv, l,
                                              wrh, wrl, rb, n_b, l_, n_tiles)
        xa = _moe_routed(h2u, route, gt, counts, w_gate, w_up, w_down, l, x1, mod, pv, n_b, l_, n_tiles)
        xb, ctx_blk = xa, n_lat_tiles
    return xa[:n_b * l_].reshape(n_b, l_, d)
```

```python
import functools

import numpy as np
import jax
import jax.numpy as jnp
from jax import lax
from jax.experimental import pallas as pl
from jax.experimental.pallas import tpu as pltpu
from jax.experimental.pallas import tpu_sc as plsc

F32 = jnp.float32
BF16 = jnp.bfloat16

D_MODEL = 1024
GRID_W = 64
HEAD_DIM = 64
HGRN_W = 256
NA_HEADS = 4
NA_WIN_ROWS = 8
NA_WIN_COLS = 16
SWA_Q_HEADS = 8
SWA_KV_HEADS = 2
SWA_GROUP = 4
SWA_WINDOW = 128
SWA_BLOCK = 128
ROPE_BASE = 10000.0
N_EXPERTS = 16
N_GROUPS = 4
D_EXPERT = 512
LN_EPS = 1e-5
RMS_EPS = 1e-6
NEG = -1e30
LB_MIN = 1e-6
DEPTH = 2
ALPHA = (2.0 * DEPTH) ** 0.25

LANES = 128
MXU_N = 256
TM = 512
HB = 256
HC = 16
NA_QROWS = 4
SWA_STEP_BLOCKS = 2
VMEM_LIMIT = 56 * 1024 * 1024

SC_MAX_ROWS = 192
SC_ROW_QUANTUM = 2048

N_COND = 8


def _dot(a, b):
    return jnp.dot(a, b, preferred_element_type=F32)


def _dot_nt(a, b):
    return lax.dot_general(a, b, (((1,), (1,)), ((), ())), preferred_element_type=F32)


def _dot_tn(a, b):
    return lax.dot_general(a, b, (((0,), (0,)), ((), ())), preferred_element_type=F32)


def _sigmoid(x):
    return 1.0 / (1.0 + jnp.exp(-x))


def _split_bf16(x):
    hi = x.astype(BF16)
    lo = (x - hi.astype(F32)).astype(BF16)
    return hi, lo


def _params(sem):
    return pltpu.CompilerParams(dimension_semantics=sem, vmem_limit_bytes=VMEM_LIMIT)


def _ada_kernel(n_rows, condt_ref, w_ref, b_ref, o_ref):
    c = condt_ref[...]
    s = c * _sigmoid(c)
    w = w_ref[0]
    rows = [jnp.sum(w * s[:, r:r + 1], axis=0, keepdims=True) for r in range(n_rows)]
    rows.append(jnp.zeros((N_COND - n_rows, w.shape[1]), F32))
    o_ref[0] = jnp.concatenate(rows, axis=0) + b_ref[0]


def _ada(cond_t, w_ada, b_ada, n_rows):
    depth, d, n6 = w_ada.shape
    tn = 1024
    return pl.pallas_call(
        functools.partial(_ada_kernel, n_rows),
        grid=(depth, n6 // tn),
        in_specs=[pl.BlockSpec((d, N_COND), lambda l, n: (0, 0)),
                  pl.BlockSpec((1, d, tn), lambda l, n: (l, 0, n)),
                  pl.BlockSpec((1, 1, tn), lambda l, n: (l, 0, n))],
        out_specs=pl.BlockSpec((1, N_COND, tn), lambda l, n: (l, 0, n)),
        out_shape=jax.ShapeDtypeStruct((depth, N_COND, n6), F32),
        compiler_params=_params(("arbitrary", "arbitrary")),
        name="ada",
    )(cond_t, w_ada, b_ada.reshape(depth, 1, n6))


def _rope(z, cos, sin, first):
    sw = jnp.where(first, pltpu.roll(z, LANES - 16, axis=1), pltpu.roll(z, 16, axis=1))
    return z * cos + sw * sin


def _inproj_kernel(n_lat, xa_ref, xb_ref, mod_ref, w_ref, gp_ref, cos_ref, sin_ref,
                   hq_ref, hv_ref, hg_ref, lf_ref, kk_ref, nqkv_ref, sq_ref, kx_ref, vx_ref):
    mod = mod_ref[0, 0]
    x = jnp.where(pl.program_id(0) < n_lat, xa_ref[...], xb_ref[...])
    h = (x * (1.0 + mod[1:2]) + mod[0:1]).astype(BF16)
    gp = gp_ref[0]
    cos = cos_ref[...]
    sin = sin_ref[...]
    lane = lax.broadcasted_iota(jnp.int32, cos.shape, 1)
    first = (lane & 16) == 0
    scale = HEAD_DIM ** -0.5

    def chunk(c):
        return _dot(h, w_ref[0, :, c * MXU_N:(c + 1) * MXU_N])

    def per_query_head(z2):
        swapped = pltpu.roll(z2, HEAD_DIM, axis=1)
        low = lane < HEAD_DIM
        h0 = jnp.where(low, z2, swapped)
        h1 = jnp.where(low, swapped, z2)
        return jnp.concatenate([h0, h0], axis=1), jnp.concatenate([h1, h1], axis=1)

    def gates(z, d):
        cols = slice(d * MXU_N, (d + 1) * MXU_N)
        a = gp[0:1, cols]
        b = gp[1:2, cols] + (jnp.minimum(z, 0.0) - jnp.log1p(jnp.exp(-jnp.abs(z))))
        logf = jnp.maximum(a, b) + jnp.log1p(jnp.exp(-jnp.abs(a - b)))
        k = gp[2:3, cols] * (1.0 / (1.0 + jnp.exp(z)))
        return logf, k

    def rope2(z):
        return jnp.concatenate([_rope(z[:, 0:LANES], cos, sin, first),
                                _rope(z[:, LANES:2 * LANES], cos, sin, first)], axis=1)

    for d in range(2):
        logf, k = gates(chunk(1 + d), d)
        lf_ref[:, d * MXU_N:(d + 1) * MXU_N] = logf
        kk_ref[:, d * MXU_N:(d + 1) * MXU_N] = k.astype(BF16)
    for d in range(2):
        sq_ref[:, d * MXU_N:(d + 1) * MXU_N] = (rope2(chunk(8 + d)) * scale).astype(BF16)
    zkv = chunk(10)
    k0, k1 = per_query_head(_rope(zkv[:, 0:LANES], cos, sin, first))
    v0, v1 = per_query_head(zkv[:, LANES:2 * LANES])
    kx_ref[:, 0:MXU_N] = k0.astype(BF16)
    kx_ref[:, MXU_N:2 * MXU_N] = k1.astype(BF16)
    vx_ref[:, 0:MXU_N] = v0.astype(BF16)
    vx_ref[:, MXU_N:2 * MXU_N] = v1.astype(BF16)
    hq_ref[...] = chunk(0).astype(BF16)
    hv_ref[...] = chunk(3).astype(BF16)
    hg_ref[...] = chunk(4).astype(BF16)
    nqkv_ref[:, 0:MXU_N] = (chunk(5) * scale).astype(BF16)
    nqkv_ref[:, MXU_N:2 * MXU_N] = chunk(6).astype(BF16)
    nqkv_ref[:, 2 * MXU_N:3 * MXU_N] = chunk(7).astype(BF16)


def _inproj(xa, xb, ctx_blk, mod, w, gp, layer, cos_t, sin_t, n_b, l_):
    d = xa.shape[1]
    tpb = l_ // TM
    n_lat = n_b * tpb
    nt = (n_lat + 1) * TM

    def mod_idx(j):
        return (layer, jnp.minimum(j // tpb, n_b), 0, 0)

    def rope_idx(j):
        return (jnp.where(j < n_lat, j % tpb, tpb), 0)

    def rows(width):
        return pl.BlockSpec((TM, width), lambda j: (j, 0))

    widths = [(HGRN_W, BF16), (HGRN_W, BF16), (HGRN_W, BF16), (2 * HGRN_W, F32), (2 * HGRN_W, BF16),
              (3 * 256, BF16), (512, BF16), (512, BF16), (512, BF16)]
    return pl.pallas_call(
        functools.partial(_inproj_kernel, n_lat),
        grid=(nt // TM,),
        in_specs=[pl.BlockSpec((TM, d), lambda j: (jnp.minimum(j, n_lat - 1), 0)),
                  pl.BlockSpec((TM, d), lambda j: (ctx_blk, 0)),
                  pl.BlockSpec((1, 1, 6, d), mod_idx),
                  pl.BlockSpec((1,) + w.shape[1:], lambda j: (layer, 0, 0)),
                  pl.BlockSpec((1,) + gp.shape[1:], lambda j: (layer, 0, 0)),
                  pl.BlockSpec((TM, LANES), rope_idx),
                  pl.BlockSpec((TM, LANES), rope_idx)],
        out_specs=[rows(wd) for wd, _ in widths],
        out_shape=[jax.ShapeDtypeStruct((nt, wd), dt) for wd, dt in widths],
        compiler_params=_params(("arbitrary",)),
        name="inproj",
    )(xa, xb, mod, w, gp, cos_t, sin_t)


def _hgrn_consts():
    t = np.arange(HB)[:, None]
    u = np.arange(HB)[None, :]
    same = (t // HC) == (u // HC)
    tr, ur = t % HC, u % HC
    half = HC // 2
    mats, codes = [], []
    for fwd in (True, False):
        if fwd:
            incl = ur <= tr
            mid = ur <= (tr // half) * half + half // 2 - 1
            edge = ur <= half - 1
            code = np.where((t // half == u // half) & (u <= t), 1, np.where(same & (tr >= half) & (ur < half), 2, 0))
        else:
            incl = ur >= tr
            mid = ur >= (tr // half) * half + half // 2
            edge = ur >= half
            code = np.where((t // half == u // half) & (u >= t), 1, np.where(same & (tr < half) & (ur >= half), 2, 0))
        mats.append(np.concatenate([same & incl, same, same & mid, same & edge], axis=0))
        codes.append(code)
    return (np.stack(mats).astype(np.float32).astype(jnp.bfloat16), np.stack(codes).astype(np.float32))


def _hgrn_kernel(qf_ref, vf_ref, lff_ref, kf_ref, qb_ref, vb_ref, lfb_ref, kb_ref, cm_ref, code_ref,
                 of_ref, ob_ref, st_ref, qd_ref, kd_ref, vv_ref, gt_ref, it_ref):
    nch = HB // HC
    n_heads = HGRN_W // HEAD_DIM

    @pl.when(pl.program_id(1) == 0)
    def _():
        st_ref[...] = jnp.zeros(st_ref.shape, F32)

    lane3 = lax.broadcasted_iota(jnp.int32, (nch, HC, HGRN_W), 2) // HEAD_DIM

    def head_rows(x):
        x3 = x.astype(BF16).reshape(nch, HC, HGRN_W)
        zero = jnp.zeros_like(x3)
        return jnp.concatenate([jnp.where(lane3 == h, x3, zero) for h in range(n_heads)], axis=1)

    def prep(d, q_ref, v_ref, lf_ref, k_ref):
        hi, lo = _split_bf16(lf_ref[...])
        cm = cm_ref[d]
        acc = _dot(jnp.concatenate([cm[0:2 * HB], cm[0:2 * HB]], axis=1), jnp.concatenate([hi, lo], axis=0))
        ref = _dot(cm[2 * HB:4 * HB], hi)
        cum, tot = acc[0:HB], acc[HB:2 * HB]
        mid, edge = ref[0:HB], ref[HB:2 * HB]
        q = q_ref[...].astype(F32)
        k = k_ref[...].astype(F32)
        v = v_ref[...]
        qd_ref[d] = head_rows(q * jnp.exp(cum))
        kd_ref[d] = head_rows(k * jnp.exp(tot - cum))
        vv_ref[d] = jnp.concatenate([v[:, h * HEAD_DIM:(h + 1) * HEAD_DIM].reshape(nch, HC, HEAD_DIM)
                                     for h in range(n_heads)], axis=1)
        gt_ref[d] = jnp.exp(tot)
        q1 = (q * jnp.exp(cum - mid)).astype(BF16)
        k1 = (k * jnp.exp(mid - cum)).astype(BF16)
        upper = (lax.broadcasted_iota(jnp.int32, q.shape, 0) % HC >= HC // 2) == (d == 0)
        q2 = jnp.where(upper, q * jnp.exp(cum - edge), 0.0).astype(BF16)
        k2 = jnp.where(upper, 0.0, k * jnp.exp(edge - cum)).astype(BF16)
        s1 = _dot_nt(_head_stack(q1, n_heads), k1)
        s2 = _dot_nt(_head_stack(q2, n_heads), k2)
        code = jnp.concatenate([code_ref[d]] * n_heads, axis=0)
        p = jnp.where(code == 1.0, s1, jnp.where(code == 2.0, s2, 0.0))
        return _head_unstack(_dot(p.astype(BF16), v), n_heads)

    od_f = prep(0, qf_ref, vf_ref, lff_ref, kf_ref)
    od_b = prep(1, qb_ref, vb_ref, lfb_ref, kb_ref)

    for i in range(nch):
        for d, c in ((0, i), (1, nch - 1 - i)):
            st = st_ref[d]
            it_ref[d, c] = _dot_nt(qd_ref[d, c], st.astype(BF16))
            upd = _dot_tn(vv_ref[d, c], kd_ref[d, c])
            st_ref[d] = st * gt_ref[d, c * HC:c * HC + 1, :] + upd

    def inter(d):
        it = it_ref[d]
        return jnp.concatenate([it[:, h * HC:(h + 1) * HC, :].reshape(HB, HEAD_DIM) for h in range(n_heads)],
                               axis=1)

    of_ref[...] = inter(0) + od_f
    ob_ref[...] = inter(1) + od_b


def _hgrn(hq, hv, lf, kk, n_b, l_, ctx_len):
    nt = hq.shape[0]
    assert ctx_len == HB
    nlb = l_ // HB
    ctx0 = n_b * nlb
    cmats, codes = _hgrn_consts()
    heads = HGRN_W // HEAD_DIM

    def fwd_idx(col):
        return lambda b, s: (jnp.where(s == 0, ctx0 + b, b * nlb + s - 1), col)

    def bwd_idx(col):
        return lambda b, s: (jnp.where(s == 0, ctx0 + b, b * nlb + nlb - s), col)

    def blk(idx):
        return pl.BlockSpec((HB, HGRN_W), idx)

    full = lambda a: pl.BlockSpec(a.shape, lambda b, s: (0,) * a.ndim)
    return pl.pallas_call(
        _hgrn_kernel,
        grid=(n_b, nlb + 1),
        in_specs=[blk(fwd_idx(0)), blk(fwd_idx(0)), blk(fwd_idx(0)), blk(fwd_idx(0)),
                  blk(bwd_idx(0)), blk(bwd_idx(0)), blk(bwd_idx(1)), blk(bwd_idx(1)),
                  full(cmats), full(codes)],
        out_specs=[blk(fwd_idx(0)), blk(bwd_idx(0))],
        out_shape=[jax.ShapeDtypeStruct((nt, HGRN_W), F32)] * 2,
        scratch_shapes=[pltpu.VMEM((2, HEAD_DIM, HGRN_W), F32),
                        pltpu.VMEM((2, HB // HC, heads * HC, HGRN_W), BF16),
                        pltpu.VMEM((2, HB // HC, heads * HC, HGRN_W), BF16),
                        pltpu.VMEM((2, HB // HC, heads * HC, HEAD_DIM), BF16),
                        pltpu.VMEM((2, HB, HGRN_W), F32),
                        pltpu.VMEM((2, HB // HC, heads * HC, HEAD_DIM), F32)],
        compiler_params=_params(("arbitrary", "arbitrary")),
        name="hgrn",
    )(hq, hv, lf, kk, hq, hv, lf, kk, cmats, codes)


def _head_stack(q, n_heads):
    lane = lax.broadcasted_iota(jnp.int32, q.shape, 1)
    zero = jnp.zeros_like(q)
    return jnp.concatenate([jnp.where(lane // HEAD_DIM == h, q, zero) for h in range(n_heads)], axis=0)


def _head_unstack(o, n_heads):
    rows = o.shape[0] // n_heads
    lane = lax.broadcasted_iota(jnp.int32, (rows, o.shape[1]), 1)
    acc = jnp.zeros((rows, o.shape[1]), F32)
    for h in range(n_heads):
        acc = acc + jnp.where(lane // HEAD_DIM == h, o[h * rows:(h + 1) * rows], 0.0)
    return acc


def _na_kernel(n_rows, n_lat_steps, qrows, q_ref, k_ref, v_ref, kc_ref, vc_ref, bias_ref, o_ref):
    j = pl.program_id(1)
    kc = kc_ref[...]
    vc = vc_ref[...]
    nwin = NA_WIN_ROWS * GRID_W

    def attend(rr, win):
        q4 = _head_stack(q_ref[rr * GRID_W:(rr + 1) * GRID_W, :], NA_HEADS)
        s_ctx = _dot_nt(q4, kc)
        m = jnp.max(s_ctx, axis=1, keepdims=True)
        if win is not None:
            kw, vw, bias = win
            s_win = _dot_nt(q4, kw) + bias
            m = jnp.maximum(m, jnp.max(s_win, axis=1, keepdims=True))
            p_win = jnp.exp(s_win - m)
        p_ctx = jnp.exp(s_ctx - m)
        den = jnp.sum(p_ctx, axis=1, keepdims=True)
        o4 = _dot(p_ctx.astype(BF16), vc)
        if win is not None:
            den = den + jnp.sum(p_win, axis=1, keepdims=True)
            o4 = o4 + _dot(p_win.astype(BF16), vw)
        o = _head_unstack(o4 * (1.0 / den), NA_HEADS)
        o_ref[rr * GRID_W:(rr + 1) * GRID_W, :] = o.astype(BF16)

    @pl.when(j < n_lat_steps)
    def _():
        for rr in range(qrows):
            r = j * qrows + rr
            rs = jnp.clip(r - NA_WIN_ROWS // 2, 0, n_rows - NA_WIN_ROWS)
            start = pl.multiple_of(rs * GRID_W, GRID_W)
            kw = k_ref[pl.ds(start, nwin), :]
            vw = v_ref[pl.ds(start, nwin), :]
            first = NA_WIN_ROWS - 1 - (r - rs)
            bias = jnp.concatenate([bias_ref[0, first + 2 * p] for p in range(NA_WIN_ROWS // 2)], axis=1)
            attend(rr, (kw, vw, bias))

    @pl.when(j >= n_lat_steps)
    def _():
        for rr in range(qrows):
            attend(rr, None)


def _na(nqkv, bias, layer, n_b, l_, ctx_len, with_ctx):
    nt = nqkv.shape[0]
    qrows = NA_QROWS if with_ctx else 2 * NA_QROWS
    qb = qrows * GRID_W
    assert ctx_len == qb or not with_ctx
    n_lat = l_ // qb
    ctx0 = n_b * n_lat
    n_rows = l_ // GRID_W

    def q_idx(b, j):
        return (jnp.where(j < n_lat, b * n_lat + j, ctx0 + b), 0)

    cblk = (n_b * l_) // ctx_len
    return pl.pallas_call(
        functools.partial(_na_kernel, n_rows, n_lat, qrows),
        grid=(n_b, n_lat + (1 if with_ctx else 0)),
        in_specs=[pl.BlockSpec((qb, 256), q_idx),
                  pl.BlockSpec((l_, 256), lambda b, j: (b, 1)),
                  pl.BlockSpec((l_, 256), lambda b, j: (b, 2)),
                  pl.BlockSpec((ctx_len, 256), lambda b, j: (cblk + b, 1)),
                  pl.BlockSpec((ctx_len, 256), lambda b, j: (cblk + b, 2)),
                  pl.BlockSpec((1,) + bias.shape[1:], lambda b, j: (layer, 0, 0, 0))],
        out_specs=pl.BlockSpec((qb, 256), q_idx),
        out_shape=jax.ShapeDtypeStruct((nt if with_ctx else n_b * l_, 256), BF16),
        compiler_params=_params(("arbitrary", "arbitrary")),
        name="natten",
    )(nqkv, nqkv, nqkv, nqkv, nqkv, bias)


def _swa_band_bias(n_blocks):
    sb = SWA_BLOCK
    u = np.arange(sb)[:, None]
    v = np.arange(3 * sb)[None, :]
    tabs = []
    for j in (0, 1, n_blocks - 1):
        kpos = (j - 1) * sb + v
        ok = (kpos >= 0) & (kpos < n_blocks * sb) & (np.abs(j * sb + u - kpos) <= SWA_WINDOW)
        tabs.append(np.tile(np.where(ok, 0.0, NEG), (SWA_GROUP, 1)))
    return np.stack(tabs).astype(np.float32)


def _swa_kernel(n_steps, layer, sink_ref, q_ref, kp_ref, kq_ref, kn_ref, vp_ref, vq_ref, vn_ref,
                kc_ref, vc_ref, bias_ref, o_ref):
    j = pl.program_id(1)
    sb = SWA_BLOCK

    def attend(a, n, band):
        cols = slice(n * MXU_N, (n + 1) * MXU_N)
        rows = slice(a * sb, (a + 1) * sb)
        q4 = _head_stack(q_ref[rows, cols], SWA_GROUP)
        sink = jnp.concatenate([jnp.full((sb, 1), sink_ref[layer, n * SWA_GROUP + g], F32)
                                for g in range(SWA_GROUP)], axis=0)
        if band is None:
            keys, vals = kc_ref[:, cols], vc_ref[:, cols]
            s = _dot_nt(q4, keys)
        else:
            k_refs, v_refs, bias = band
            keys = jnp.concatenate([r[rs, cols] for r, rs in k_refs] + [kc_ref[:, cols]], axis=0)
            vals = jnp.concatenate([r[rs, cols] for r, rs in v_refs] + [vc_ref[:, cols]], axis=0)
            s = _dot_nt(q4, keys)
            s = jnp.concatenate([s[:, 0:3 * sb] + bias, s[:, 3 * sb:]], axis=1)
        m = jnp.maximum(jnp.max(s, axis=1, keepdims=True), sink)
        p = jnp.exp(s - m)
        den = jnp.sum(p, axis=1, keepdims=True) + jnp.exp(sink - m)
        o4 = _dot(p.astype(BF16), vals)
        o_ref[rows, cols] = _head_unstack(o4 * (1.0 / den), SWA_GROUP).astype(BF16)

    lo, hi, whole = slice(0, sb), slice(sb, 2 * sb), slice(0, sb)

    @pl.when(j < n_steps)
    def _():
        bands = [([(kp_ref, whole), (kq_ref, lo), (kq_ref, hi)], [(vp_ref, whole), (vq_ref, lo), (vq_ref, hi)],
                  bias_ref[jnp.where(j == 0, 0, 1)]),
                 ([(kq_ref, lo), (kq_ref, hi), (kn_ref, whole)], [(vq_ref, lo), (vq_ref, hi), (vn_ref, whole)],
                  bias_ref[jnp.where(j == n_steps - 1, 2, 1)])]
        for a in range(SWA_STEP_BLOCKS):
            for n in range(SWA_KV_HEADS):
                attend(a, n, bands[a])

    @pl.when(j >= n_steps)
    def _():
        for a in range(SWA_STEP_BLOCKS):
            for n in range(SWA_KV_HEADS):
                attend(a, n, None)


def _swa(sq, kx, vx, sink, layer, n_b, l_, ctx_len, with_ctx):
    nt = sq.shape[0]
    sb = SWA_BLOCK
    qb = SWA_STEP_BLOCKS * sb
    nb = l_ // sb
    ns = l_ // qb
    assert SWA_STEP_BLOCKS == 2 and nb % 2 == 0 and nb >= 4 and ctx_len == qb
    ctx0 = (n_b * l_) // qb
    bias = _swa_band_bias(nb)

    def q_idx(b, j):
        return (jnp.where(j < ns, b * ns + j, ctx0 + b), 0)

    def cur_idx(b, j):
        return (b * ns + jnp.minimum(j, ns - 1), 0)

    def edge_idx(off):
        return lambda b, j: (b * nb + jnp.clip(SWA_STEP_BLOCKS * jnp.minimum(j, ns - 1) + off, 0, nb - 1), 0)

    edge = lambda idx: pl.BlockSpec((sb, 512), idx)
    wide = lambda idx: pl.BlockSpec((qb, 512), idx)
    cspec = pl.BlockSpec((ctx_len, 512), lambda b, j: (ctx0 + b, 0))
    return pl.pallas_call(
        functools.partial(_swa_kernel, ns, layer),
        grid=(n_b, ns + (1 if with_ctx else 0)),
        in_specs=[pl.BlockSpec(memory_space=pltpu.SMEM),
                  wide(q_idx), edge(edge_idx(-1)), wide(cur_idx), edge(edge_idx(SWA_STEP_BLOCKS)),
                  edge(edge_idx(-1)), wide(cur_idx), edge(edge_idx(SWA_STEP_BLOCKS)), cspec, cspec,
                  pl.BlockSpec(bias.shape, lambda b, j: (0, 0, 0))],
        out_specs=wide(q_idx),
        out_shape=jax.ShapeDtypeStruct((nt if with_ctx else n_b * l_, 512), BF16),
        compiler_params=_params(("arbitrary", "arbitrary")),
        name="swa",
    )(sink, sq, kx, kx, kx, vx, vx, vx, kx, vx, bias)


def _layer_norm(r, g, b):
    mu = jnp.mean(r, axis=-1, keepdims=True)
    rc = r - mu
    var = jnp.mean(rc * rc, axis=-1, keepdims=True)
    return rc * lax.rsqrt(var + LN_EPS) * g + b


def _route(logits_t, bias_col):
    epg = N_EXPERTS // N_GROUPS
    s = _sigmoid(logits_t)
    sel = s + bias_col
    sel_r = [sel[e:e + 1, :] for e in range(N_EXPERTS)]
    s_r = [s[e:e + 1, :] for e in range(N_EXPERTS)]
    grp = []
    for g in range(N_GROUPS):
        a = sel_r[g * epg:(g + 1) * epg]
        m1 = functools.reduce(jnp.maximum, a)
        m2 = functools.reduce(jnp.maximum,
                              [jnp.minimum(a[i], a[k]) for i in range(epg) for k in range(i + 1, epg)])
        grp.append(m1 + m2)
    m_r, w_r = [], []
    for g in range(N_GROUPS):
        best = None
        for k in range(N_GROUPS):
            if k == g:
                continue
            c = (grp[g] > grp[k]) if k < g else (grp[g] >= grp[k])
            best = c if best is None else (best & c)
        for e in range(g * epg, (g + 1) * epg):
            rank = jnp.zeros_like(sel_r[e])
            for k in range(g * epg, (g + 1) * epg):
                if k == e:
                    continue
                ahead = (sel_r[k] >= sel_r[e]) if k < e else (sel_r[k] > sel_r[e])
                rank = rank + jnp.where(ahead, 1.0, 0.0)
            chosen = jnp.where(best & (rank < 1.5), 1.0, 0.0)
            m_r.append(chosen)
            w_r.append(chosen * s_r[e])
    inv = 1.0 / functools.reduce(lambda a, b: a + b, w_r)
    return m_r, [w * inv for w in w_r]


def _pack_bf16_pairs(a, b):
    ua = pltpu.bitcast(a.astype(BF16).astype(F32), jnp.uint32)
    ub = pltpu.bitcast(b.astype(BF16).astype(F32), jnp.uint32)
    return (ua >> 16) | ub


def _unpack_bf16_pairs(u):
    return (pltpu.bitcast(u << 16, F32), pltpu.bitcast(u & jnp.uint32(0xFFFF0000), F32))


def _outproj_kernel(n_lat, of_ref, ob_ref, hg_ref, yb_ref, yc_ref, xa_ref, xb_ref, mod_ref, wo_ref, pv_ref,
                    wrh_ref, wrl_ref, rb_ref, before_ref, x1_ref, h2u_ref, route_ref, gt_ref, cnt_ref):
    @pl.when(pl.program_id(0) == 0)
    def _():
        cnt_ref[...] = jnp.zeros(cnt_ref.shape, F32)

    row = lax.broadcasted_iota(jnp.int32, (HGRN_W, HGRN_W), 0)
    col = lax.broadcasted_iota(jnp.int32, (HGRN_W, HGRN_W), 1)
    head_ones = jnp.where((row // HEAD_DIM) == (col // HEAD_DIM), 1.0, 0.0).astype(BF16)
    pv = pv_ref[0]
    o = of_ref[...] + ob_ref[...]
    hi, lo = _split_bf16(o * o)
    ms = _dot(jnp.concatenate([hi, lo], axis=1), jnp.concatenate([head_ones, head_ones], axis=0)) * (1.0 / HEAD_DIM)
    gate = hg_ref[...].astype(F32)
    ya = o * lax.rsqrt(ms + RMS_EPS) * pv[2:3, 0:HGRN_W] * (gate * _sigmoid(gate))
    y = _dot(jnp.concatenate([ya.astype(BF16), yb_ref[...], yc_ref[...]], axis=1), wo_ref[0])
    mod = mod_ref[0, 0]
    x = jnp.where(pl.program_id(0) < n_lat, xa_ref[...], xb_ref[...])
    x1 = _layer_norm(ALPHA * x + mod[2:3] * y, pv[0:1], pv[1:2])
    x1_ref[...] = x1
    h2 = x1 * (1.0 + mod[4:5]) + mod[3:4]
    hh, hl = _split_bf16(h2)
    half = h2.shape[1] // 2
    h2u_ref[...] = _pack_bf16_pairs(h2[:, 0:half], h2[:, half:2 * half])
    wrh = wrh_ref[...]
    logits_t = _dot_nt(jnp.concatenate([wrh, wrh, wrl_ref[...]], axis=1),
                       jnp.concatenate([hh, hl, hh], axis=1))
    m_r, w_r = _route(logits_t, rb_ref[...])
    m_t = jnp.concatenate(m_r, axis=0)
    rank_t = _dot(m_t.astype(BF16), before_ref[...]) + cnt_ref[:, 0:1]
    cnt_ref[...] = cnt_ref[...] + jnp.sum(m_t, axis=1, keepdims=True)
    seen = jnp.zeros_like(m_r[0])
    e1 = e2 = r1 = r2 = g1 = g2 = jnp.zeros_like(m_r[0])
    for e in range(N_EXPERTS):
        is1 = m_r[e] * (1.0 - seen)
        is2 = m_r[e] * seen
        rk = rank_t[e:e + 1, :]
        e1, e2 = e1 + e * is1, e2 + e * is2
        r1, r2 = r1 + rk * is1, r2 + rk * is2
        g1, g2 = g1 + w_r[e] * is1, g2 + w_r[e] * is2
        seen = seen + is1
    zero = jnp.zeros_like(e1)
    route_ref[...] = jnp.concatenate([e1, e2, r1, r2, g1, g2, zero, zero], axis=0)
    pad = jnp.zeros((LANES - 2, g1.shape[1]), F32)
    gt_ref[...] = jnp.concatenate([g1, g2, pad], axis=0).T


def _outproj(o_f, o_b, hg, yb, yc, xa, xb, ctx_blk, mod, wo, pv, layer, wrh, wrl, rb, n_b, l_, n_tiles):
    d = xa.shape[1]
    tpb = l_ // TM
    n_lat = n_b * tpb
    before = np.triu(np.ones((TM, TM), np.float32), 1).astype(jnp.bfloat16)

    def mod_idx(j):
        return (layer, jnp.minimum(j // tpb, n_b), 0, 0)

    def rows(width):
        return pl.BlockSpec((TM, width), lambda j: (j, 0))

    full = lambda a: pl.BlockSpec(a.shape, lambda j: (0,) * a.ndim)
    of_layer = lambda a: pl.BlockSpec((1,) + a.shape[1:], lambda j: (layer,) + (0,) * (a.ndim - 1))
    nr = n_tiles * TM
    return pl.pallas_call(
        functools.partial(_outproj_kernel, n_lat),
        grid=(n_tiles,),
        in_specs=[rows(256), rows(256), rows(256), rows(256), rows(512),
                  pl.BlockSpec((TM, d), lambda j: (jnp.minimum(j, n_lat - 1), 0)),
                  pl.BlockSpec((TM, d), lambda j: (ctx_blk, 0)),
                  pl.BlockSpec((1, 1, 6, d), mod_idx), of_layer(wo), of_layer(pv), full(wrh), full(wrl), full(rb),
                  full(before)],
        out_specs=[rows(d), rows(d // 2), pl.BlockSpec((8, TM), lambda j: (0, j)), rows(LANES),
                   pl.BlockSpec((N_EXPERTS, LANES), lambda j: (0, 0))],
        out_shape=[jax.ShapeDtypeStruct((nr, d), F32), jax.ShapeDtypeStruct((nr, d // 2), jnp.uint32),
                   jax.ShapeDtypeStruct((8, nr), F32), jax.ShapeDtypeStruct((nr, LANES), F32),
                   jax.ShapeDtypeStruct((N_EXPERTS, LANES), F32)],
        compiler_params=_params(("arbitrary",)),
        name="outproj",
    )(o_f, o_b, hg, yb, yc, xa, xb, mod, wo, pv, wrh, wrl, rb, before)


def _sc_split(rows_per_worker):
    best = None
    for chunk in (64, 48, 32):
        for nbuf in range(SC_MAX_ROWS // chunk, 0, -1):
            if rows_per_worker % (chunk * nbuf) == 0 and (best is None or chunk * nbuf > best[0] * best[1]):
                best = (chunk, nbuf)
    return best


def _sc_workers():
    info = plsc.get_sparse_core_info()
    return info.num_cores, info.num_cores * info.num_subcores


def _sc_scatter_rows(table, pos, n_slots):
    n_tok, width = table.shape
    n_cores, n_workers = _sc_workers()
    per_w = n_tok // n_workers
    assert n_tok % n_workers == 0
    chunk, nbuf = _sc_split(per_w)
    n_chunks = per_w // chunk
    mesh = plsc.VectorSubcoreMesh(core_axis_name="c", subcore_axis_name="s")

    @functools.partial(
        pl.kernel, mesh=mesh,
        out_type=jax.ShapeDtypeStruct((n_slots, width), table.dtype),
        scratch_types=[pltpu.VMEM((2 * n_chunks, chunk), jnp.int32),
                       pltpu.VMEM((nbuf, chunk, width), table.dtype),
                       pltpu.SemaphoreType.DMA((nbuf,)),
                       pltpu.SemaphoreType.DMA((nbuf,))],
    )
    def scatter(table_hbm, pos_hbm, out_hbm, idx_v, rows_v, rsem, wsem):
        wid = lax.axis_index("s") * n_cores + lax.axis_index("c")
        base = wid * per_w
        pltpu.sync_copy(pos_hbm.at[wid], idx_v)

        @pl.loop(0, n_chunks // nbuf)
        def _(g):
            c0 = g * nbuf
            gets = [pltpu.async_copy(table_hbm.at[pl.ds(base + (c0 + b) * chunk, chunk)], rows_v.at[b],
                                     rsem.at[b]) for b in range(nbuf)]
            puts = []
            for b in range(nbuf):
                gets[b].wait()
                for k in range(2):
                    puts.append(pltpu.async_copy(rows_v.at[b], out_hbm.at[idx_v.at[k * n_chunks + c0 + b]],
                                                 wsem.at[b]))
            for put in puts:
                put.wait()

    pos_w = pos.reshape(2, n_workers, n_chunks, chunk).transpose(1, 0, 2, 3).reshape(n_workers, 2 * n_chunks, chunk)
    return scatter(table, pos_w)


def _sc_gather_rows(table, idx):
    n_out = idx.shape[0]
    width = table.shape[1]
    n_cores, n_workers = _sc_workers()
    per_w = n_out // n_workers
    assert n_out % n_workers == 0
    chunk, nbuf = _sc_split(per_w)
    mesh = plsc.VectorSubcoreMesh(core_axis_name="c", subcore_axis_name="s")

    @functools.partial(
        pl.kernel, mesh=mesh,
        out_type=jax.ShapeDtypeStruct((n_out, width), table.dtype),
        scratch_types=[pltpu.VMEM((per_w,), jnp.int32),
                       pltpu.VMEM((nbuf, chunk, width), table.dtype),
                       pltpu.SemaphoreType.DMA((nbuf,)),
                       pltpu.SemaphoreType.DMA((nbuf,))],
    )
    def gather(table_hbm, idx_hbm, out_hbm, idx_v, rows_v, gsem, wsem):
        wid = lax.axis_index("s") * n_cores + lax.axis_index("c")
        base = wid * per_w
        pltpu.sync_copy(idx_hbm.at[pl.ds(base, per_w)], idx_v)

        @pl.loop(0, per_w // (chunk * nbuf))
        def _(g):
            off = g * (chunk * nbuf)
            gets = [pltpu.async_copy(table_hbm.at[idx_v.at[pl.ds(off + b * chunk, chunk)]], rows_v.at[b],
                                     gsem.at[b]) for b in range(nbuf)]
            puts = []
            for b in range(nbuf):
                gets[b].wait()
                puts.append(pltpu.async_copy(rows_v.at[b], out_hbm.at[pl.ds(base + off + b * chunk, chunk)],
                                             wsem.at[b]))
            for put in puts:
                put.wait()

    return gather(table, idx)


def _gmm_kernel(te_ref, nv_ref, nu_ref, xs_ref, wg_ref, wu_ref, wd_ref, ys_ref, wgb_ref, wub_ref, wdb_ref):
    i = pl.program_id(0)
    used = i < nu_ref[0]
    fresh = jnp.logical_or(i == 0, te_ref[i] != te_ref[jnp.maximum(i - 1, 0)])

    @pl.when(jnp.logical_and(used, fresh))
    def _():
        wgb_ref[...] = wg_ref[0, 0].astype(BF16)
        wub_ref[...] = wu_ref[0, 0].astype(BF16)
        wdb_ref[...] = wd_ref[0, 0].astype(BF16)

    @pl.when(used)
    def _():
        rowid = lax.broadcasted_iota(jnp.int32, xs_ref.shape, 0)
        lo, hi = _unpack_bf16_pairs(jnp.where(rowid < nv_ref[i], xs_ref[...], jnp.uint32(0)))
        lo, hi = lo.astype(BF16), hi.astype(BF16)
        half = lo.shape[1]
        x = jnp.concatenate([lo, hi], axis=1)
        a = _dot(x, wgb_ref[...])
        u = _dot(x, wub_ref[...])
        y = _dot((a * _sigmoid(a) * u).astype(BF16), wdb_ref[...])
        ys_ref[...] = _pack_bf16_pairs(y[:, 0:half], y[:, half:2 * half])

    @pl.when(jnp.logical_not(used))
    def _():
        ys_ref[...] = jnp.zeros(ys_ref.shape, jnp.uint32)


def _gmm(tile_expert, tile_valid, n_used, xs, wg, wu, wd, layer):
    p_rows, half = xs.shape
    d = 2 * half
    wspec = lambda r, c: pl.BlockSpec((1, 1, r, c), lambda i, te, nv, nu: (layer, te[i], 0, 0))
    return pl.pallas_call(
        _gmm_kernel,
        grid_spec=pltpu.PrefetchScalarGridSpec(
            num_scalar_prefetch=3,
            grid=(p_rows // TM,),
            in_specs=[pl.BlockSpec((TM, half), lambda i, te, nv, nu: (i, 0)),
                      wspec(d, D_EXPERT), wspec(d, D_EXPERT), wspec(D_EXPERT, d)],
            out_specs=pl.BlockSpec((TM, half), lambda i, te, nv, nu: (i, 0)),
            scratch_shapes=[pltpu.VMEM((d, D_EXPERT), BF16), pltpu.VMEM((d, D_EXPERT), BF16),
                            pltpu.VMEM((D_EXPERT, d), BF16)]),
        out_shape=jax.ShapeDtypeStruct((p_rows, half), jnp.uint32),
        compiler_params=_params(("arbitrary",)),
        name="moe_experts",
    )(tile_expert, tile_valid, n_used, xs, wg, wu, wd)


def _moefin_kernel(o1_ref, o2_ref, gt_ref, x_ref, mod_ref, pv_ref, out_ref):
    gt = gt_ref[...]
    g1, g2 = gt[:, 0:1], gt[:, 1:2]
    a1, b1 = _unpack_bf16_pairs(o1_ref[...])
    a2, b2 = _unpack_bf16_pairs(o2_ref[...])
    f = jnp.concatenate([g1 * a1 + g2 * a2, g1 * b1 + g2 * b2], axis=1)
    mod = mod_ref[0, 0]
    pv = pv_ref[0]
    out_ref[...] = _layer_norm(ALPHA * x_ref[...] + mod[5:6] * f, pv[3:4], pv[4:5])


def _moefin(o12, gt, x1, mod, pv, layer, n_b, l_, n_tiles):
    d = x1.shape[1]
    tpb = l_ // TM

    def mod_idx(j):
        return (layer, jnp.minimum(j // tpb, n_b), 0, 0)

    return pl.pallas_call(
        _moefin_kernel,
        grid=(n_tiles,),
        in_specs=[pl.BlockSpec((TM, d // 2), lambda j: (j, 0)),
                  pl.BlockSpec((TM, d // 2), lambda j: (j + n_tiles, 0)),
                  pl.BlockSpec((TM, LANES), lambda j: (j, 0)),
                  pl.BlockSpec((TM, d), lambda j: (j, 0)),
                  pl.BlockSpec((1, 1, 6, d), mod_idx),
                  pl.BlockSpec((1,) + pv.shape[1:], lambda j: (layer, 0, 0))],
        out_specs=pl.BlockSpec((TM, d), lambda j: (j, 0)),
        out_shape=jax.ShapeDtypeStruct((n_tiles * TM, d), F32),
        compiler_params=_params(("arbitrary",)),
        name="moe_combine",
    )(o12, o12, gt, x1, mod, pv)


def _moe_routed(h2u, route, gt, counts, wg, wu, wd, layer, x1, mod, pv, n_b, l_, n_tiles):
    n_tok = n_tiles * TM
    n_slots = -(-(2 * n_tok + N_EXPERTS * TM) // SC_ROW_QUANTUM) * SC_ROW_QUANTUM
    cnt = counts[:, 0].astype(jnp.int32)
    padded = ((cnt + TM - 1) // TM) * TM
    upto = jnp.arange(N_EXPERTS)[None, :] <= jnp.arange(N_EXPERTS)[:, None]
    ends = jnp.sum(jnp.where(upto, padded[None, :], 0), axis=1)
    offs = ends - padded
    e12 = route[0:2].astype(jnp.int32)
    r12 = route[2:4].astype(jnp.int32)
    onehot = e12[:, :, None] == jnp.arange(N_EXPERTS, dtype=jnp.int32)
    pos = r12 + jnp.sum(jnp.where(onehot, offs, 0), axis=-1)
    tile_start = jnp.arange(n_slots // TM, dtype=jnp.int32) * TM
    tile_expert = jnp.minimum(jnp.sum(tile_start[:, None] >= ends[None, :], axis=1), N_EXPERTS - 1)
    pick = tile_expert[:, None] == jnp.arange(N_EXPERTS, dtype=jnp.int32)
    tile_valid = jnp.clip(jnp.sum(jnp.where(pick, (offs + cnt)[None, :], 0), axis=1) - tile_start, 0, TM)
    n_used = (ends[N_EXPERTS - 1] // TM).reshape(1)
    xs = _sc_scatter_rows(h2u, pos, n_slots)
    ys = _gmm(tile_expert.astype(jnp.int32), tile_valid.astype(jnp.int32), n_used.astype(jnp.int32),
              xs, wg, wu, wd, layer)
    o12 = _sc_gather_rows(ys, pos.reshape(2 * n_tok))
    return _moefin(o12, gt, x1, mod, pv, layer, n_b, l_, n_tiles)


def _rope_tables(l_):
    pos = np.arange(l_)
    nf = HEAD_DIM // 4
    inv = ROPE_BASE ** (-np.arange(nf, dtype=np.float64) / nf)
    ar = (pos // GRID_W)[:, None] * inv
    ac = (pos % GRID_W)[:, None] * inv
    cos = np.concatenate([np.cos(ar), np.cos(ar), np.cos(ac), np.cos(ac)], axis=1)
    sin = np.concatenate([-np.sin(ar), np.sin(ar), -np.sin(ac), np.sin(ac)], axis=1)
    cos = np.concatenate([np.tile(cos, (1, 2)), np.ones((TM, LANES))], axis=0)
    sin = np.concatenate([np.tile(sin, (1, 2)), np.zeros((TM, LANES))], axis=0)
    return cos.astype(np.float32), sin.astype(np.float32)


def _na_bias(rpb):
    u = np.arange(GRID_W)[:, None]
    v = np.arange(GRID_W)[None, :]
    cs = np.clip(u - NA_WIN_COLS // 2, 0, GRID_W - NA_WIN_COLS)
    colmask = (v >= cs) & (v < cs + NA_WIN_COLS)
    coff = np.clip(v - u + NA_WIN_COLS - 1, 0, 2 * NA_WIN_COLS - 2)
    n_off = 2 * NA_WIN_COLS - 1
    pick = (coff[None] == np.arange(n_off)[:, None, None]).astype(np.float32)
    band = jnp.einsum('lhab,buv->lahuv', rpb.astype(F32), pick, precision=lax.Precision.HIGHEST)
    band = jnp.where(colmask[None, None, None], band, NEG)
    band = band.reshape(rpb.shape[0], 2 * NA_WIN_ROWS - 1, NA_HEADS * GRID_W, GRID_W)
    return jnp.concatenate([band[:, :-1], band[:, 1:]], axis=-1)


def kernel(x, c, ctx, c_ctx, w_ada, b_ada, w_in, lb_logits, hgrn_norm, na_rpb, swa_sink, w_out,
           ln1_g, ln1_b, ln2_g, ln2_b, w_router, router_bias, w_gate, w_up, w_down):
    n_b, l_, d = x.shape
    ctx_len = ctx.shape[1]
    depth = w_ada.shape[0]
    assert n_b * ctx_len == TM and l_ % TM == 0 and n_b + 1 <= N_COND
    n_lat_tiles = (n_b * l_) // TM

    p_lb = jax.nn.softmax(lb_logits.astype(F32), axis=0).reshape(depth, 2 * HGRN_W)
    upto = (np.arange(depth)[None, :] <= np.arange(depth)[:, None]).astype(np.float32)
    lb = jnp.maximum(jnp.sum(upto[:, :, None] * p_lb[None], axis=1) - p_lb[0:1], LB_MIN)
    gp = jnp.stack([jnp.log(lb), jnp.log1p(-lb), 1.0 - lb] + [jnp.zeros_like(lb)] * 5, axis=1)

    cond = jnp.concatenate([c, c_ctx[None, :], jnp.zeros((N_COND - n_b - 1, d), F32)], axis=0)
    mod = _ada(cond.T, w_ada, b_ada, n_b + 1).reshape(depth, N_COND, 6, d)

    cos_t, sin_t = _rope_tables(l_)
    wr_t = w_router.T
    wrh = wr_t.astype(BF16)
    wrl = (wr_t - wrh.astype(F32)).astype(BF16)
    rb = router_bias.astype(F32)[:, None]
    w_in_b = w_in.astype(BF16)
    w_out_b = w_out.astype(BF16)
    zeros = jnp.zeros_like(ln1_g)
    pv = jnp.stack([ln1_g, ln1_b, jnp.tile(hgrn_norm, (1, d // HEAD_DIM)), ln2_g, ln2_b, zeros, zeros, zeros],
                   axis=1).astype(F32)
    na_bias = _na_bias(na_rpb)
    sink = swa_sink.astype(F32)

    xa, xb, ctx_blk = x.reshape(n_b * l_, d), ctx.reshape(n_b * ctx_len, d), 0
    for l in range(depth):
        last = l == depth - 1
        hq, hv, hg, lf, kk, nqkv, sq, kx, vx = _inproj(xa, xb, ctx_blk, mod, w_in_b, gp, l, cos_t, sin_t, n_b, l_)
        o_f, o_b = _hgrn(hq, hv, lf, kk, n_b, l_, ctx_len)
        yb = _na(nqkv, na_bias, l, n_b, l_, ctx_len, not last)
        yc = _swa(sq, kx, vx, sink, l, n_b, l_, ctx_len, not last)
        n_tiles = n_lat_tiles if last else n_lat_tiles + 1
        x1, h2u, route, gt, counts = _outproj(o_f, o_b, hg, yb, yc, xa, xb, ctx_blk, mod, w_out_b, pv, l,
                                              wrh, wrl, rb, n_b, l_, n_tiles)
        xa = _moe_routed(h2u, route, gt, counts, w_gate, w_up, w_down, l, x1, mod, pv, n_b, l_, n_tiles)
        xb, ctx_blk = xa, n_lat_tiles
    return xa[:n_b * l_].reshape(n_b, l_, d)
```

```python
import functools

import numpy as np
import jax
import jax.numpy as jnp
from jax import lax
from jax.experimental import pallas as pl
from jax.experimental.pallas import tpu as pltpu
from jax.experimental.pallas import tpu_sc as plsc

F32 = jnp.float32
BF16 = jnp.bfloat16

D_MODEL = 1024
GRID_W = 64
HEAD_DIM = 64
HGRN_W = 256
NA_HEADS = 4
NA_WIN_ROWS = 8
NA_WIN_COLS = 16
SWA_Q_HEADS = 8
SWA_KV_HEADS = 2
SWA_GROUP = 4
SWA_WINDOW = 128
SWA_BLOCK = 128
ROPE_BASE = 10000.0
N_EXPERTS = 16
N_GROUPS = 4
D_EXPERT = 512
LN_EPS = 1e-5
RMS_EPS = 1e-6
NEG = -1e30
LB_MIN = 1e-6
DEPTH = 2
ALPHA = (2.0 * DEPTH) ** 0.25

LANES = 128
MXU_N = 256
TM = 512
HB = 256
HC = 16
NA_QROWS = 4
SWA_STEP_BLOCKS = 2
VMEM_LIMIT = 56 * 1024 * 1024

SC_MAX_ROWS = 192
SC_ROW_QUANTUM = 2048

N_COND = 8


def _dot(a, b):
    return jnp.dot(a, b, preferred_element_type=F32)


def _dot_nt(a, b):
    return lax.dot_general(a, b, (((1,), (1,)), ((), ())), preferred_element_type=F32)


def _dot_tn(a, b):
    return lax.dot_general(a, b, (((0,), (0,)), ((), ())), preferred_element_type=F32)


def _sigmoid(x):
    return 1.0 / (1.0 + jnp.exp(-x))


def _split_bf16(x):
    hi = x.astype(BF16)
    lo = (x - hi.astype(F32)).astype(BF16)
    return hi, lo


def _params(sem):
    return pltpu.CompilerParams(dimension_semantics=sem, vmem_limit_bytes=VMEM_LIMIT)


def _ada_kernel(n_rows, condt_ref, w_ref, b_ref, o_ref):
    c = condt_ref[...]
    s = c * _sigmoid(c)
    w = w_ref[0]
    rows = [jnp.sum(w * s[:, r:r + 1], axis=0, keepdims=True) for r in range(n_rows)]
    rows.append(jnp.zeros((N_COND - n_rows, w.shape[1]), F32))
    o_ref[0] = jnp.concatenate(rows, axis=0) + b_ref[0]


def _ada(cond_t, w_ada, b_ada, n_rows):
    depth, d, n6 = w_ada.shape
    tn = 1024
    return pl.pallas_call(
        functools.partial(_ada_kernel, n_rows),
        grid=(depth, n6 // tn),
        in_specs=[pl.BlockSpec((d, N_COND), lambda l, n: (0, 0)),
                  pl.BlockSpec((1, d, tn), lambda l, n: (l, 0, n)),
                  pl.BlockSpec((1, 1, tn), lambda l, n: (l, 0, n))],
        out_specs=pl.BlockSpec((1, N_COND, tn), lambda l, n: (l, 0, n)),
        out_shape=jax.ShapeDtypeStruct((depth, N_COND, n6), F32),
        compiler_params=_params(("arbitrary", "arbitrary")),
        name="ada",
    )(cond_t, w_ada, b_ada.reshape(depth, 1, n6))


def _rope(z, cos, sin, first):
    sw = jnp.where(first, pltpu.roll(z, LANES - 16, axis=1), pltpu.roll(z, 16, axis=1))
    return z * cos + sw * sin


def _inproj_kernel(n_lat, xa_ref, xb_ref, mod_ref, w_ref, gp_ref, cos_ref, sin_ref,
                   hq_ref, hv_ref, hg_ref, lf_ref, kk_ref, nqkv_ref, sq_ref, kx_ref, vx_ref):
    mod = mod_ref[0, 0]
    x = jnp.where(pl.program_id(0) < n_lat, xa_ref[...], xb_ref[...])
    h = (x * (1.0 + mod[1:2]) + mod[0:1]).astype(BF16)
    gp = gp_ref[0]
    cos = cos_ref[...]
    sin = sin_ref[...]
    lane = lax.broadcasted_iota(jnp.int32, cos.shape, 1)
    first = (lane & 16) == 0
    scale = HEAD_DIM ** -0.5

    def chunk(c):
        return _dot(h, w_ref[0, :, c * MXU_N:(c + 1) * MXU_N])

    def per_query_head(z2):
        swapped = pltpu.roll(z2, HEAD_DIM, axis=1)
        low = lane < HEAD_DIM
        h0 = jnp.where(low, z2, swapped)
        h1 = jnp.where(low, swapped, z2)
        return jnp.concatenate([h0, h0], axis=1), jnp.concatenate([h1, h1], axis=1)

    def gates(z, d):
        cols = slice(d * MXU_N, (d + 1) * MXU_N)
        a = gp[0:1, cols]
        b = gp[1:2, cols] + (jnp.minimum(z, 0.0) - jnp.log1p(jnp.exp(-jnp.abs(z))))
        logf = jnp.maximum(a, b) + jnp.log1p(jnp.exp(-jnp.abs(a - b)))
        k = gp[2:3, cols] * (1.0 / (1.0 + jnp.exp(z)))
        return logf, k

    def rope2(z):
        return jnp.concatenate([_rope(z[:, 0:LANES], cos, sin, first),
                                _rope(z[:, LANES:2 * LANES], cos, sin, first)], axis=1)

    for d in range(2):
        logf, k = gates(chunk(1 + d), d)
        lf_ref[:, d * MXU_N:(d + 1) * MXU_N] = logf
        kk_ref[:, d * MXU_N:(d + 1) * MXU_N] = k.astype(BF16)
    for d in range(2):
        sq_ref[:, d * MXU_N:(d + 1) * MXU_N] = (rope2(chunk(8 + d)) * scale).astype(BF16)
    zkv = chunk(10)
    k0, k1 = per_query_head(_rope(zkv[:, 0:LANES], cos, sin, first))
    v0, v1 = per_query_head(zkv[:, LANES:2 * LANES])
    kx_ref[:, 0:MXU_N] = k0.astype(BF16)
    kx_ref[:, MXU_N:2 * MXU_N] = k1.astype(BF16)
    vx_ref[:, 0:MXU_N] = v0.astype(BF16)
    vx_ref[:, MXU_N:2 * MXU_N] = v1.astype(BF16)
    hq_ref[...] = chunk(0).astype(BF16)
    hv_ref[...] = chunk(3).astype(BF16)
    hg_ref[...] = chunk(4).astype(BF16)
    nqkv_ref[:, 0:MXU_N] = (chunk(5) * scale).astype(BF16)
    nqkv_ref[:, MXU_N:2 * MXU_N] = chunk(6).astype(BF16)
    nqkv_ref[:, 2 * MXU_N:3 * MXU_N] = chunk(7).astype(BF16)


def _inproj(xa, xb, ctx_blk, mod, w, gp, layer, cos_t, sin_t, n_b, l_):
    d = xa.shape[1]
    tpb = l_ // TM
    n_lat = n_b * tpb
    nt = (n_lat + 1) * TM

    def mod_idx(j):
        return (layer, jnp.minimum(j // tpb, n_b), 0, 0)

    def rope_idx(j):
        return (jnp.where(j < n_lat, j % tpb, tpb), 0)

    def rows(width):
        return pl.BlockSpec((TM, width), lambda j: (j, 0))

    widths = [(HGRN_W, BF16), (HGRN_W, BF16), (HGRN_W, BF16), (2 * HGRN_W, F32), (2 * HGRN_W, BF16),
              (3 * 256, BF16), (512, BF16), (512, BF16), (512, BF16)]
    return pl.pallas_call(
        functools.partial(_inproj_kernel, n_lat),
        grid=(nt // TM,),
        in_specs=[pl.BlockSpec((TM, d), lambda j: (jnp.minimum(j, n_lat - 1), 0)),
                  pl.BlockSpec((TM, d), lambda j: (ctx_blk, 0)),
                  pl.BlockSpec((1, 1, 6, d), mod_idx),
                  pl.BlockSpec((1,) + w.shape[1:], lambda j: (layer, 0, 0)),
                  pl.BlockSpec((1,) + gp.shape[1:], lambda j: (layer, 0, 0)),
                  pl.BlockSpec((TM, LANES), rope_idx),
                  pl.BlockSpec((TM, LANES), rope_idx)],
        out_specs=[rows(wd) for wd, _ in widths],
        out_shape=[jax.ShapeDtypeStruct((nt, wd), dt) for wd, dt in widths],
        compiler_params=_params(("arbitrary",)),
        name="inproj",
    )(xa, xb, mod, w, gp, cos_t, sin_t)


def _hgrn_consts():
    t = np.arange(HB)[:, None]
    u = np.arange(HB)[None, :]
    same = (t // HC) == (u // HC)
    tr, ur = t % HC, u % HC
    half = HC // 2
    mats, codes = [], []
    for fwd in (True, False):
        if fwd:
            incl = ur <= tr
            mid = ur <= (tr // half) * half + half // 2 - 1
            edge = ur <= half - 1
            code = np.where((t // half == u // half) & (u <= t), 1, np.where(same & (tr >= half) & (ur < half), 2, 0))
        else:
            incl = ur >= tr
            mid = ur >= (tr // half) * half + half // 2
            edge = ur >= half
            code = np.where((t // half == u // half) & (u >= t), 1, np.where(same & (tr < half) & (ur >= half), 2, 0))
        mats.append(np.concatenate([same & incl, same, same & mid, same & edge], axis=0))
        codes.append(code)
    return (np.stack(mats).astype(np.float32).astype(jnp.bfloat16), np.stack(codes).astype(np.float32))


def _hgrn_kernel(qf_ref, vf_ref, lff_ref, kf_ref, qb_ref, vb_ref, lfb_ref, kb_ref, cm_ref, code_ref,
                 of_ref, ob_ref, st_ref, qd_ref, kd_ref, vv_ref, gt_ref, it_ref):
    nch = HB // HC
    n_heads = HGRN_W // HEAD_DIM

    @pl.when(pl.program_id(1) == 0)
    def _():
        st_ref[...] = jnp.zeros(st_ref.shape, F32)

    lane3 = lax.broadcasted_iota(jnp.int32, (nch, HC, HGRN_W), 2) // HEAD_DIM

    def head_rows(x):
        x3 = x.astype(BF16).reshape(nch, HC, HGRN_W)
        zero = jnp.zeros_like(x3)
        return jnp.concatenate([jnp.where(lane3 == h, x3, zero) for h in range(n_heads)], axis=1)

    def prep(d, q_ref, v_ref, lf_ref, k_ref):
        hi, lo = _split_bf16(lf_ref[...])
        cm = cm_ref[d]
        acc = _dot(jnp.concatenate([cm[0:2 * HB], cm[0:2 * HB]], axis=1), jnp.concatenate([hi, lo], axis=0))
        ref = _dot(cm[2 * HB:4 * HB], hi)
        cum, tot = acc[0:HB], acc[HB:2 * HB]
        mid, edge = ref[0:HB], ref[HB:2 * HB]
        q = q_ref[...].astype(F32)
        k = k_ref[...].astype(F32)
        v = v_ref[...]
        qd_ref[d] = head_rows(q * jnp.exp(cum))
        kd_ref[d] = head_rows(k * jnp.exp(tot - cum))
        vv_ref[d] = jnp.concatenate([v[:, h * HEAD_DIM:(h + 1) * HEAD_DIM].reshape(nch, HC, HEAD_DIM)
                                     for h in range(n_heads)], axis=1)
        gt_ref[d] = jnp.exp(tot)
        q1 = (q * jnp.exp(cum - mid)).astype(BF16)
        k1 = (k * jnp.exp(mid - cum)).astype(BF16)
        upper = (lax.broadcasted_iota(jnp.int32, q.shape, 0) % HC >= HC // 2) == (d == 0)
        q2 = jnp.where(upper, q * jnp.exp(cum - edge), 0.0).astype(BF16)
        k2 = jnp.where(upper, 0.0, k * jnp.exp(edge - cum)).astype(BF16)
        s1 = _dot_nt(_head_stack(q1, n_heads), k1)
        s2 = _dot_nt(_head_stack(q2, n_heads), k2)
        code = jnp.concatenate([code_ref[d]] * n_heads, axis=0)
        p = jnp.where(code == 1.0, s1, jnp.where(code == 2.0, s2, 0.0))
        return _head_unstack(_dot(p.astype(BF16), v), n_heads)

    od_f = prep(0, qf_ref, vf_ref, lff_ref, kf_ref)
    od_b = prep(1, qb_ref, vb_ref, lfb_ref, kb_ref)

    for i in range(nch):
        for d, c in ((0, i), (1, nch - 1 - i)):
            st = st_ref[d]
            it_ref[d, c] = _dot_nt(qd_ref[d, c], st.astype(BF16))
            upd = _dot_tn(vv_ref[d, c], kd_ref[d, c])
            st_ref[d] = st * gt_ref[d, c * HC:c * HC + 1, :] + upd

    def inter(d):
        it = it_ref[d]
        return jnp.concatenate([it[:, h * HC:(h + 1) * HC, :].reshape(HB, HEAD_DIM) for h in range(n_heads)],
                               axis=1)

    of_ref[...] = inter(0) + od_f
    ob_ref[...] = inter(1) + od_b


def _hgrn(hq, hv, lf, kk, n_b, l_, ctx_len):
    nt = hq.shape[0]
    assert ctx_len == HB
    nlb = l_ // HB
    ctx0 = n_b * nlb
    cmats, codes = _hgrn_consts()
    heads = HGRN_W // HEAD_DIM

    def fwd_idx(col):
        return lambda b, s: (jnp.where(s == 0, ctx0 + b, b * nlb + s - 1), col)

    def bwd_idx(col):
        return lambda b, s: (jnp.where(s == 0, ctx0 + b, b * nlb + nlb - s), col)

    def blk(idx):
        return pl.BlockSpec((HB, HGRN_W), idx)

    full = lambda a: pl.BlockSpec(a.shape, lambda b, s: (0,) * a.ndim)
    return pl.pallas_call(
        _hgrn_kernel,
        grid=(n_b, nlb + 1),
        in_specs=[blk(fwd_idx(0)), blk(fwd_idx(0)), blk(fwd_idx(0)), blk(fwd_idx(0)),
                  blk(bwd_idx(0)), blk(bwd_idx(0)), blk(bwd_idx(1)), blk(bwd_idx(1)),
                  full(cmats), full(codes)],
        out_specs=[blk(fwd_idx(0)), blk(bwd_idx(0))],
        out_shape=[jax.ShapeDtypeStruct((nt, HGRN_W), F32)] * 2,
        scratch_shapes=[pltpu.VMEM((2, HEAD_DIM, HGRN_W), F32),
                        pltpu.VMEM((2, HB // HC, heads * HC, HGRN_W), BF16),
                        pltpu.VMEM((2, HB // HC, heads * HC, HGRN_W), BF16),
                        pltpu.VMEM((2, HB // HC, heads * HC, HEAD_DIM), BF16),
                        pltpu.VMEM((2, HB, HGRN_W), F32),
                        pltpu.VMEM((2, HB // HC, heads * HC, HEAD_DIM), F32)],
        compiler_params=_params(("arbitrary", "arbitrary")),
        name="hgrn",
    )(hq, hv, lf, kk, hq, hv, lf, kk, cmats, codes)


def _head_stack(q, n_heads):
    lane = lax.broadcasted_iota(jnp.int32, q.shape, 1)
    zero = jnp.zeros_like(q)
    return jnp.concatenate([jnp.where(lane // HEAD_DIM == h, q, zero) for h in range(n_heads)], axis=0)


def _head_unstack(o, n_heads):
    rows = o.shape[0] // n_heads
    lane = lax.broadcasted_iota(jnp.int32, (rows, o.shape[1]), 1)
    acc = jnp.zeros((rows, o.shape[1]), F32)
    for h in range(n_heads):
        acc = acc + jnp.where(lane // HEAD_DIM == h, o[h * rows:(h + 1) * rows], 0.0)
    return acc


def _na_kernel(n_rows, n_lat_steps, qrows, q_ref, k_ref, v_ref, kc_ref, vc_ref, bias_ref, o_ref):
    j = pl.program_id(1)
    kc = kc_ref[...]
    vc = vc_ref[...]
    nwin = NA_WIN_ROWS * GRID_W

    def attend(rr, win):
        q4 = _head_stack(q_ref[rr * GRID_W:(rr + 1) * GRID_W, :], NA_HEADS)
        s_ctx = _dot_nt(q4, kc)
        m = jnp.max(s_ctx, axis=1, keepdims=True)
        if win is not None:
            kw, vw, bias = win
            s_win = _dot_nt(q4, kw) + bias
            m = jnp.maximum(m, jnp.max(s_win, axis=1, keepdims=True))
            p_win = jnp.exp(s_win - m)
        p_ctx = jnp.exp(s_ctx - m)
        den = jnp.sum(p_ctx, axis=1, keepdims=True)
        o4 = _dot(p_ctx.astype(BF16), vc)
        if win is not None:
            den = den + jnp.sum(p_win, axis=1, keepdims=True)
            o4 = o4 + _dot(p_win.astype(BF16), vw)
        o = _head_unstack(o4 * (1.0 / den), NA_HEADS)
        o_ref[rr * GRID_W:(rr + 1) * GRID_W, :] = o.astype(BF16)

    @pl.when(j < n_lat_steps)
    def _():
        for rr in range(qrows):
            r = j * qrows + rr
            rs = jnp.clip(r - NA_WIN_ROWS // 2, 0, n_rows - NA_WIN_ROWS)
            start = pl.multiple_of(rs * GRID_W, GRID_W)
            kw = k_ref[pl.ds(start, nwin), :]
            vw = v_ref[pl.ds(start, nwin), :]
            first = NA_WIN_ROWS - 1 - (r - rs)
            bias = jnp.concatenate([bias_ref[0, first + 2 * p] for p in range(NA_WIN_ROWS // 2)], axis=1)
            attend(rr, (kw, vw, bias))

    @pl.when(j >= n_lat_steps)
    def _():
        for rr in range(qrows):
            attend(rr, None)


def _na(nqkv, bias, layer, n_b, l_, ctx_len, with_ctx):
    nt = nqkv.shape[0]
    qrows = NA_QROWS if with_ctx else 2 * NA_QROWS
    qb = qrows * GRID_W
    assert ctx_len == qb or not with_ctx
    n_lat = l_ // qb
    ctx0 = n_b * n_lat
    n_rows = l_ // GRID_W

    def q_idx(b, j):
        return (jnp.where(j < n_lat, b * n_lat + j, ctx0 + b), 0)

    cblk = (n_b * l_) // ctx_len
    return pl.pallas_call(
        functools.partial(_na_kernel, n_rows, n_lat, qrows),
        grid=(n_b, n_lat + (1 if with_ctx else 0)),
        in_specs=[pl.BlockSpec((qb, 256), q_idx),
                  pl.BlockSpec((l_, 256), lambda b, j: (b, 1)),
                  pl.BlockSpec((l_, 256), lambda b, j: (b, 2)),
                  pl.BlockSpec((ctx_len, 256), lambda b, j: (cblk + b, 1)),
                  pl.BlockSpec((ctx_len, 256), lambda b, j: (cblk + b, 2)),
                  pl.BlockSpec((1,) + bias.shape[1:], lambda b, j: (layer, 0, 0, 0))],
        out_specs=pl.BlockSpec((qb, 256), q_idx),
        out_shape=jax.ShapeDtypeStruct((nt if with_ctx else n_b * l_, 256), BF16),
        compiler_params=_params(("arbitrary", "arbitrary")),
        name="natten",
    )(nqkv, nqkv, nqkv, nqkv, nqkv, bias)


def _swa_band_bias(n_blocks):
    sb = SWA_BLOCK
    u = np.arange(sb)[:, None]
    v = np.arange(3 * sb)[None, :]
    tabs = []
    for j in (0, 1, n_blocks - 1):
        kpos = (j - 1) * sb + v
        ok = (kpos >= 0) & (kpos < n_blocks * sb) & (np.abs(j * sb + u - kpos) <= SWA_WINDOW)
        tabs.append(np.tile(np.where(ok, 0.0, NEG), (SWA_GROUP, 1)))
    return np.stack(tabs).astype(np.float32)


def _swa_kernel(n_steps, layer, sink_ref, q_ref, kp_ref, kq_ref, kn_ref, vp_ref, vq_ref, vn_ref,
                kc_ref, vc_ref, bias_ref, o_ref):
    j = pl.program_id(1)
    sb = SWA_BLOCK

    def attend(a, n, band):
        cols = slice(n * MXU_N, (n + 1) * MXU_N)
        rows = slice(a * sb, (a + 1) * sb)
        q4 = _head_stack(q_ref[rows, cols], SWA_GROUP)
        sink = jnp.concatenate([jnp.full((sb, 1), sink_ref[layer, n * SWA_GROUP + g], F32)
                                for g in range(SWA_GROUP)], axis=0)
        if band is None:
            keys, vals = kc_ref[:, cols], vc_ref[:, cols]
            s = _dot_nt(q4, keys)
        else:
            k_refs, v_refs, bias = band
            keys = jnp.concatenate([r[rs, cols] for r, rs in k_refs] + [kc_ref[:, cols]], axis=0)
            vals = jnp.concatenate([r[rs, cols] for r, rs in v_refs] + [vc_ref[:, cols]], axis=0)
            s = _dot_nt(q4, keys)
            s = jnp.concatenate([s[:, 0:3 * sb] + bias, s[:, 3 * sb:]], axis=1)
        m = jnp.maximum(jnp.max(s, axis=1, keepdims=True), sink)
        p = jnp.exp(s - m)
        den = jnp.sum(p, axis=1, keepdims=True) + jnp.exp(sink - m)
        o4 = _dot(p.astype(BF16), vals)
        o_ref[rows, cols] = _head_unstack(o4 * (1.0 / den), SWA_GROUP).astype(BF16)

    lo, hi, whole = slice(0, sb), slice(sb, 2 * sb), slice(0, sb)

    @pl.when(j < n_steps)
    def _():
        bands = [([(kp_ref, whole), (kq_ref, lo), (kq_ref, hi)], [(vp_ref, whole), (vq_ref, lo), (vq_ref, hi)],
                  bias_ref[jnp.where(j == 0, 0, 1)]),
                 ([(kq_ref, lo), (kq_ref, hi), (kn_ref, whole)], [(vq_ref, lo), (vq_ref, hi), (vn_ref, whole)],
                  bias_ref[jnp.where(j == n_steps - 1, 2, 1)])]
        for a in range(SWA_STEP_BLOCKS):
            for n in range(SWA_KV_HEADS):
                attend(a, n, bands[a])

    @pl.when(j >= n_steps)
    def _():
        for a in range(SWA_STEP_BLOCKS):
            for n in range(SWA_KV_HEADS):
                attend(a, n, None)


def _swa(sq, kx, vx, sink, layer, n_b, l_, ctx_len, with_ctx):
    nt = sq.shape[0]
    sb = SWA_BLOCK
    qb = SWA_STEP_BLOCKS * sb
    nb = l_ // sb
    ns = l_ // qb
    assert SWA_STEP_BLOCKS == 2 and nb % 2 == 0 and nb >= 4 and ctx_len == qb
    ctx0 = (n_b * l_) // qb
    bias = _swa_band_bias(nb)

    def q_idx(b, j):
        return (jnp.where(j < ns, b * ns + j, ctx0 + b), 0)

    def cur_idx(b, j):
        return (b * ns + jnp.minimum(j, ns - 1), 0)

    def edge_idx(off):
        return lambda b, j: (b * nb + jnp.clip(SWA_STEP_BLOCKS * jnp.minimum(j, ns - 1) + off, 0, nb - 1), 0)

    edge = lambda idx: pl.BlockSpec((sb, 512), idx)
    wide = lambda idx: pl.BlockSpec((qb, 512), idx)
    cspec = pl.BlockSpec((ctx_len, 512), lambda b, j: (ctx0 + b, 0))
    return pl.pallas_call(
        functools.partial(_swa_kernel, ns, layer),
        grid=(n_b, ns + (1 if with_ctx else 0)),
        in_specs=[pl.BlockSpec(memory_space=pltpu.SMEM),
                  wide(q_idx), edge(edge_idx(-1)), wide(cur_idx), edge(edge_idx(SWA_STEP_BLOCKS)),
                  edge(edge_idx(-1)), wide(cur_idx), edge(edge_idx(SWA_STEP_BLOCKS)), cspec, cspec,
                  pl.BlockSpec(bias.shape, lambda b, j: (0, 0, 0))],
        out_specs=wide(q_idx),
        out_shape=jax.ShapeDtypeStruct((nt if with_ctx else n_b * l_, 512), BF16),
        compiler_params=_params(("arbitrary", "arbitrary")),
        name="swa",
    )(sink, sq, kx, kx, kx, vx, vx, vx, kx, vx, bias)


def _layer_norm(r, g, b):
    mu = jnp.mean(r, axis=-1, keepdims=True)
    rc = r - mu
    var = jnp.mean(rc * rc, axis=-1, keepdims=True)
    return rc * lax.rsqrt(var + LN_EPS) * g + b


def _route(logits_t, bias_col):
    epg = N_EXPERTS // N_GROUPS
    s = _sigmoid(logits_t)
    sel = s + bias_col
    sel_r = [sel[e:e + 1, :] for e in range(N_EXPERTS)]
    s_r = [s[e:e + 1, :] for e in range(N_EXPERTS)]
    grp = []
    for g in range(N_GROUPS):
        a = sel_r[g * epg:(g + 1) * epg]
        m1 = functools.reduce(jnp.maximum, a)
        m2 = functools.reduce(jnp.maximum,
                              [jnp.minimum(a[i], a[k]) for i in range(epg) for k in range(i + 1, epg)])
        grp.append(m1 + m2)
    m_r, w_r = [], []
    for g in range(N_GROUPS):
        best = None
        for k in range(N_GROUPS):
            if k == g:
                continue
            c = (grp[g] > grp[k]) if k < g else (grp[g] >= grp[k])
            best = c if best is None else (best & c)
        for e in range(g * epg, (g + 1) * epg):
            rank = jnp.zeros_like(sel_r[e])
            for k in range(g * epg, (g + 1) * epg):
                if k == e:
                    continue
                ahead = (sel_r[k] >= sel_r[e]) if k < e else (sel_r[k] > sel_r[e])
                rank = rank + jnp.where(ahead, 1.0, 0.0)
            chosen = jnp.where(best & (rank < 1.5), 1.0, 0.0)
            m_r.append(chosen)
            w_r.append(chosen * s_r[e])
    inv = 1.0 / functools.reduce(lambda a, b: a + b, w_r)
    return m_r, [w * inv for w in w_r]


def _pack_bf16_pairs(a, b):
    ua = pltpu.bitcast(a.astype(BF16).astype(F32), jnp.uint32)
    ub = pltpu.bitcast(b.astype(BF16).astype(F32), jnp.uint32)
    return (ua >> 16) | ub


def _unpack_bf16_pairs(u):
    return (pltpu.bitcast(u << 16, F32), pltpu.bitcast(u & jnp.uint32(0xFFFF0000), F32))


def _outproj_kernel(n_lat, of_ref, ob_ref, hg_ref, yb_ref, yc_ref, xa_ref, xb_ref, mod_ref, wo_ref, pv_ref,
                    wrh_ref, rb_ref, before_ref, x1_ref, h2u_ref, route_ref, gt_ref, cnt_ref):
    @pl.when(pl.program_id(0) == 0)
    def _():
        cnt_ref[...] = jnp.zeros(cnt_ref.shape, F32)

    row = lax.broadcasted_iota(jnp.int32, (HGRN_W, HGRN_W), 0)
    col = lax.broadcasted_iota(jnp.int32, (HGRN_W, HGRN_W), 1)
    head_ones = jnp.where((row // HEAD_DIM) == (col // HEAD_DIM), 1.0, 0.0).astype(BF16)
    pv = pv_ref[0]
    o = of_ref[...] + ob_ref[...]
    hi, lo = _split_bf16(o * o)
    ms = _dot(jnp.concatenate([hi, lo], axis=1), jnp.concatenate([head_ones, head_ones], axis=0)) * (1.0 / HEAD_DIM)
    gate = hg_ref[...].astype(F32)
    ya = o * lax.rsqrt(ms + RMS_EPS) * pv[2:3, 0:HGRN_W] * (gate * _sigmoid(gate))
    y = _dot(jnp.concatenate([ya.astype(BF16), yb_ref[...], yc_ref[...]], axis=1), wo_ref[0])
    mod = mod_ref[0, 0]
    x = jnp.where(pl.program_id(0) < n_lat, xa_ref[...], xb_ref[...])
    x1 = _layer_norm(ALPHA * x + mod[2:3] * y, pv[0:1], pv[1:2])
    x1_ref[...] = x1
    h2 = x1 * (1.0 + mod[4:5]) + mod[3:4]
    hh, hl = _split_bf16(h2)
    half = h2.shape[1] // 2
    h2u_ref[...] = _pack_bf16_pairs(h2[:, 0:half], h2[:, half:2 * half])
    wr = wrh_ref[...]
    logits_t = _dot_nt(jnp.concatenate([wr, wr], axis=1), jnp.concatenate([hh, hl], axis=1))
    m_r, w_r = _route(logits_t, rb_ref[...])
    m_t = jnp.concatenate(m_r, axis=0)
    rank_t = _dot(m_t.astype(BF16), before_ref[...]) + cnt_ref[:, 0:1]
    cnt_ref[...] = cnt_ref[...] + jnp.sum(m_t, axis=1, keepdims=True)
    seen = jnp.zeros_like(m_r[0])
    e1 = e2 = r1 = r2 = g1 = g2 = jnp.zeros_like(m_r[0])
    for e in range(N_EXPERTS):
        is1 = m_r[e] * (1.0 - seen)
        is2 = m_r[e] * seen
        rk = rank_t[e:e + 1, :]
        e1, e2 = e1 + e * is1, e2 + e * is2
        r1, r2 = r1 + rk * is1, r2 + rk * is2
        g1, g2 = g1 + w_r[e] * is1, g2 + w_r[e] * is2
        seen = seen + is1
    zero = jnp.zeros_like(e1)
    route_ref[...] = jnp.concatenate([e1, e2, r1, r2, g1, g2, zero, zero], axis=0)
    pad = jnp.zeros((LANES - 2, g1.shape[1]), F32)
    gt_ref[...] = jnp.concatenate([g1, g2, pad], axis=0).T


def _outproj(o_f, o_b, hg, yb, yc, xa, xb, ctx_blk, mod, wo, pv, layer, wrh, rb, n_b, l_, n_tiles):
    d = xa.shape[1]
    tpb = l_ // TM
    n_lat = n_b * tpb
    before = np.triu(np.ones((TM, TM), np.float32), 1).astype(jnp.bfloat16)

    def mod_idx(j):
        return (layer, jnp.minimum(j // tpb, n_b), 0, 0)

    def rows(width):
        return pl.BlockSpec((TM, width), lambda j: (j, 0))

    full = lambda a: pl.BlockSpec(a.shape, lambda j: (0,) * a.ndim)
    of_layer = lambda a: pl.BlockSpec((1,) + a.shape[1:], lambda j: (layer,) + (0,) * (a.ndim - 1))
    nr = n_tiles * TM
    return pl.pallas_call(
        functools.partial(_outproj_kernel, n_lat),
        grid=(n_tiles,),
        in_specs=[rows(256), rows(256), rows(256), rows(256), rows(512),
                  pl.BlockSpec((TM, d), lambda j: (jnp.minimum(j, n_lat - 1), 0)),
                  pl.BlockSpec((TM, d), lambda j: (ctx_blk, 0)),
                  pl.BlockSpec((1, 1, 6, d), mod_idx), of_layer(wo), of_layer(pv), full(wrh), full(rb),
                  full(before)],
        out_specs=[rows(d), rows(d // 2), pl.BlockSpec((8, TM), lambda j: (0, j)), rows(LANES),
                   pl.BlockSpec((N_EXPERTS, LANES), lambda j: (0, 0))],
        out_shape=[jax.ShapeDtypeStruct((nr, d), F32), jax.ShapeDtypeStruct((nr, d // 2), jnp.uint32),
                   jax.ShapeDtypeStruct((8, nr), F32), jax.ShapeDtypeStruct((nr, LANES), F32),
                   jax.ShapeDtypeStruct((N_EXPERTS, LANES), F32)],
        compiler_params=_params(("arbitrary",)),
        name="outproj",
    )(o_f, o_b, hg, yb, yc, xa, xb, mod, wo, pv, wrh, rb, before)


def _sc_split(rows_per_worker):
    best = None
    for chunk in (64, 48, 32):
        for nbuf in range(SC_MAX_ROWS // chunk, 0, -1):
            if rows_per_worker % (chunk * nbuf) == 0 and (best is None or chunk * nbuf > best[0] * best[1]):
                best = (chunk, nbuf)
    return best


def _sc_workers():
    info = plsc.get_sparse_core_info()
    return info.num_cores, info.num_cores * info.num_subcores


def _sc_scatter_rows(table, pos, n_slots):
    n_tok, width = table.shape
    n_cores, n_workers = _sc_workers()
    per_w = n_tok // n_workers
    assert n_tok % n_workers == 0
    chunk, nbuf = _sc_split(per_w)
    n_chunks = per_w // chunk
    mesh = plsc.VectorSubcoreMesh(core_axis_name="c", subcore_axis_name="s")

    @functools.partial(
        pl.kernel, mesh=mesh,
        out_type=jax.ShapeDtypeStruct((n_slots, width), table.dtype),
        scratch_types=[pltpu.VMEM((2 * n_chunks, chunk), jnp.int32),
                       pltpu.VMEM((nbuf, chunk, width), table.dtype),
                       pltpu.SemaphoreType.DMA((nbuf,)),
                       pltpu.SemaphoreType.DMA((nbuf,))],
    )
    def scatter(table_hbm, pos_hbm, out_hbm, idx_v, rows_v, rsem, wsem):
        wid = lax.axis_index("s") * n_cores + lax.axis_index("c")
        base = wid * per_w
        pltpu.sync_copy(pos_hbm.at[wid], idx_v)

        @pl.loop(0, n_chunks // nbuf)
        def _(g):
            c0 = g * nbuf
            gets = [pltpu.async_copy(table_hbm.at[pl.ds(base + (c0 + b) * chunk, chunk)], rows_v.at[b],
                                     rsem.at[b]) for b in range(nbuf)]
            puts = []
            for b in range(nbuf):
                gets[b].wait()
                for k in range(2):
                    puts.append(pltpu.async_copy(rows_v.at[b], out_hbm.at[idx_v.at[k * n_chunks + c0 + b]],
                                                 wsem.at[b]))
            for put in puts:
                put.wait()

    pos_w = pos.reshape(2, n_workers, n_chunks, chunk).transpose(1, 0, 2, 3).reshape(n_workers, 2 * n_chunks, chunk)
    return scatter(table, pos_w)


def _sc_gather_rows(table, idx):
    n_out = idx.shape[0]
    width = table.shape[1]
    n_cores, n_workers = _sc_workers()
    per_w = n_out // n_workers
    assert n_out % n_workers == 0
    chunk, nbuf = _sc_split(per_w)
    mesh = plsc.VectorSubcoreMesh(core_axis_name="c", subcore_axis_name="s")

    @functools.partial(
        pl.kernel, mesh=mesh,
        out_type=jax.ShapeDtypeStruct((n_out, width), table.dtype),
        scratch_types=[pltpu.VMEM((per_w,), jnp.int32),
                       pltpu.VMEM((nbuf, chunk, width), table.dtype),
                       pltpu.SemaphoreType.DMA((nbuf,)),
                       pltpu.SemaphoreType.DMA((nbuf,))],
    )
    def gather(table_hbm, idx_hbm, out_hbm, idx_v, rows_v, gsem, wsem):
        wid = lax.axis_index("s") * n_cores + lax.axis_index("c")
        base = wid * per_w
        pltpu.sync_copy(idx_hbm.at[pl.ds(base, per_w)], idx_v)

        @pl.loop(0, per_w // (chunk * nbuf))
        def _(g):
            off = g * (chunk * nbuf)
            gets = [pltpu.async_copy(table_hbm.at[idx_v.at[pl.ds(off + b * chunk, chunk)]], rows_v.at[b],
                                     gsem.at[b]) for b in range(nbuf)]
            puts = []
            for b in range(nbuf):
                gets[b].wait()
                puts.append(pltpu.async_copy(rows_v.at[b], out_hbm.at[pl.ds(base + off + b * chunk, chunk)],
                                             wsem.at[b]))
            for put in puts:
                put.wait()

    return gather(table, idx)


def _gmm_kernel(te_ref, nv_ref, nu_ref, xs_ref, wg_ref, wu_ref, wd_ref, ys_ref, wgb_ref, wub_ref, wdb_ref):
    i = pl.program_id(0)
    used = i < nu_ref[0]
    fresh = jnp.logical_or(i == 0, te_ref[i] != te_ref[jnp.maximum(i - 1, 0)])

    @pl.when(jnp.logical_and(used, fresh))
    def _():
        wgb_ref[...] = wg_ref[0, 0].astype(BF16)
        wub_ref[...] = wu_ref[0, 0].astype(BF16)
        wdb_ref[...] = wd_ref[0, 0].astype(BF16)

    @pl.when(used)
    def _():
        rowid = lax.broadcasted_iota(jnp.int32, xs_ref.shape, 0)
        lo, hi = _unpack_bf16_pairs(jnp.where(rowid < nv_ref[i], xs_ref[...], jnp.uint32(0)))
        lo, hi = lo.astype(BF16), hi.astype(BF16)
        half = lo.shape[1]
        x = jnp.concatenate([lo, hi], axis=1)
        a = _dot(x, wgb_ref[...])
        u = _dot(x, wub_ref[...])
        y = _dot((a * _sigmoid(a) * u).astype(BF16), wdb_ref[...])
        ys_ref[...] = _pack_bf16_pairs(y[:, 0:half], y[:, half:2 * half])

    @pl.when(jnp.logical_not(used))
    def _():
        ys_ref[...] = jnp.zeros(ys_ref.shape, jnp.uint32)


def _gmm(tile_expert, tile_valid, n_used, xs, wg, wu, wd, layer):
    p_rows, half = xs.shape
    d = 2 * half
    wspec = lambda r, c: pl.BlockSpec((1, 1, r, c), lambda i, te, nv, nu: (layer, te[i], 0, 0))
    return pl.pallas_call(
        _gmm_kernel,
        grid_spec=pltpu.PrefetchScalarGridSpec(
            num_scalar_prefetch=3,
            grid=(p_rows // TM,),
            in_specs=[pl.BlockSpec((TM, half), lambda i, te, nv, nu: (i, 0)),
                      wspec(d, D_EXPERT), wspec(d, D_EXPERT), wspec(D_EXPERT, d)],
            out_specs=pl.BlockSpec((TM, half), lambda i, te, nv, nu: (i, 0)),
            scratch_shapes=[pltpu.VMEM((d, D_EXPERT), BF16), pltpu.VMEM((d, D_EXPERT), BF16),
                            pltpu.VMEM((D_EXPERT, d), BF16)]),
        out_shape=jax.ShapeDtypeStruct((p_rows, half), jnp.uint32),
        compiler_params=_params(("arbitrary",)),
        name="moe_experts",
    )(tile_expert, tile_valid, n_used, xs, wg, wu, wd)


def _moefin_kernel(o1_ref, o2_ref, gt_ref, x_ref, mod_ref, pv_ref, out_ref):
    gt = gt_ref[...]
    g1, g2 = gt[:, 0:1], gt[:, 1:2]
    a1, b1 = _unpack_bf16_pairs(o1_ref[...])
    a2, b2 = _unpack_bf16_pairs(o2_ref[...])
    f = jnp.concatenate([g1 * a1 + g2 * a2, g1 * b1 + g2 * b2], axis=1)
    mod = mod_ref[0, 0]
    pv = pv_ref[0]
    out_ref[...] = _layer_norm(ALPHA * x_ref[...] + mod[5:6] * f, pv[3:4], pv[4:5])


def _moefin(o12, gt, x1, mod, pv, layer, n_b, l_, n_tiles):
    d = x1.shape[1]
    tpb = l_ // TM

    def mod_idx(j):
        return (layer, jnp.minimum(j // tpb, n_b), 0, 0)

    return pl.pallas_call(
        _moefin_kernel,
        grid=(n_tiles,),
        in_specs=[pl.BlockSpec((TM, d // 2), lambda j: (j, 0)),
                  pl.BlockSpec((TM, d // 2), lambda j: (j + n_tiles, 0)),
                  pl.BlockSpec((TM, LANES), lambda j: (j, 0)),
                  pl.BlockSpec((TM, d), lambda j: (j, 0)),
                  pl.BlockSpec((1, 1, 6, d), mod_idx),
                  pl.BlockSpec((1,) + pv.shape[1:], lambda j: (layer, 0, 0))],
        out_specs=pl.BlockSpec((TM, d), lambda j: (j, 0)),
        out_shape=jax.ShapeDtypeStruct((n_tiles * TM, d), F32),
        compiler_params=_params(("arbitrary",)),
        name="moe_combine",
    )(o12, o12, gt, x1, mod, pv)


def _moe_routed(h2u, route, gt, counts, wg, wu, wd, layer, x1, mod, pv, n_b, l_, n_tiles):
    n_tok = n_tiles * TM
    n_slots = -(-(2 * n_tok + N_EXPERTS * TM) // SC_ROW_QUANTUM) * SC_ROW_QUANTUM
    cnt = counts[:, 0].astype(jnp.int32)
    padded = ((cnt + TM - 1) // TM) * TM
    upto = jnp.arange(N_EXPERTS)[None, :] <= jnp.arange(N_EXPERTS)[:, None]
    ends = jnp.sum(jnp.where(upto, padded[None, :], 0), axis=1)
    offs = ends - padded
    e12 = route[0:2].astype(jnp.int32)
    r12 = route[2:4].astype(jnp.int32)
    onehot = e12[:, :, None] == jnp.arange(N_EXPERTS, dtype=jnp.int32)
    pos = r12 + jnp.sum(jnp.where(onehot, offs, 0), axis=-1)
    tile_start = jnp.arange(n_slots // TM, dtype=jnp.int32) * TM
    tile_expert = jnp.minimum(jnp.sum(tile_start[:, None] >= ends[None, :], axis=1), N_EXPERTS - 1)
    pick = tile_expert[:, None] == jnp.arange(N_EXPERTS, dtype=jnp.int32)
    tile_valid = jnp.clip(jnp.sum(jnp.where(pick, (offs + cnt)[None, :], 0), axis=1) - tile_start, 0, TM)
    n_used = (ends[N_EXPERTS - 1] // TM).reshape(1)
    xs = _sc_scatter_rows(h2u, pos, n_slots)
    ys = _gmm(tile_expert.astype(jnp.int32), tile_valid.astype(jnp.int32), n_used.astype(jnp.int32),
              xs, wg, wu, wd, layer)
    o12 = _sc_gather_rows(ys, pos.reshape(2 * n_tok))
    return _moefin(o12, gt, x1, mod, pv, layer, n_b, l_, n_tiles)


def _rope_tables(l_):
    pos = np.arange(l_)
    nf = HEAD_DIM // 4
    inv = ROPE_BASE ** (-np.arange(nf, dtype=np.float64) / nf)
    ar = (pos // GRID_W)[:, None] * inv
    ac = (pos % GRID_W)[:, None] * inv
    cos = np.concatenate([np.cos(ar), np.cos(ar), np.cos(ac), np.cos(ac)], axis=1)
    sin = np.concatenate([-np.sin(ar), np.sin(ar), -np.sin(ac), np.sin(ac)], axis=1)
    cos = np.concatenate([np.tile(cos, (1, 2)), np.ones((TM, LANES))], axis=0)
    sin = np.concatenate([np.tile(sin, (1, 2)), np.zeros((TM, LANES))], axis=0)
    return cos.astype(np.float32), sin.astype(np.float32)


def _na_bias(rpb):
    u = np.arange(GRID_W)[:, None]
    v = np.arange(GRID_W)[None, :]
    cs = np.clip(u - NA_WIN_COLS // 2, 0, GRID_W - NA_WIN_COLS)
    colmask = (v >= cs) & (v < cs + NA_WIN_COLS)
    coff = np.clip(v - u + NA_WIN_COLS - 1, 0, 2 * NA_WIN_COLS - 2)
    n_off = 2 * NA_WIN_COLS - 1
    pick = (coff[None] == np.arange(n_off)[:, None, None]).astype(np.float32)
    band = jnp.einsum('lhab,buv->lahuv', rpb.astype(F32), pick, precision=lax.Precision.HIGHEST)
    band = jnp.where(colmask[None, None, None], band, NEG)
    band = band.reshape(rpb.shape[0], 2 * NA_WIN_ROWS - 1, NA_HEADS * GRID_W, GRID_W)
    return jnp.concatenate([band[:, :-1], band[:, 1:]], axis=-1)


def kernel(x, c, ctx, c_ctx, w_ada, b_ada, w_in, lb_logits, hgrn_norm, na_rpb, swa_sink, w_out,
           ln1_g, ln1_b, ln2_g, ln2_b, w_router, router_bias, w_gate, w_up, w_down):
    n_b, l_, d = x.shape
    ctx_len = ctx.shape[1]
    depth = w_ada.shape[0]
    assert n_b * ctx_len == TM and l_ % TM == 0 and n_b + 1 <= N_COND
    n_lat_tiles = (n_b * l_) // TM

    p_lb = jax.nn.softmax(lb_logits.astype(F32), axis=0).reshape(depth, 2 * HGRN_W)
    upto = (np.arange(depth)[None, :] <= np.arange(depth)[:, None]).astype(np.float32)
    lb = jnp.maximum(jnp.sum(upto[:, :, None] * p_lb[None], axis=1) - p_lb[0:1], LB_MIN)
    gp = jnp.stack([jnp.log(lb), jnp.log1p(-lb), 1.0 - lb] + [jnp.zeros_like(lb)] * 5, axis=1)

    cond = jnp.concatenate([c, c_ctx[None, :], jnp.zeros((N_COND - n_b - 1, d), F32)], axis=0)
    mod = _ada(cond.T, w_ada, b_ada, n_b + 1).reshape(depth, N_COND, 6, d)

    cos_t, sin_t = _rope_tables(l_)
    wrh = w_router.T.astype(BF16)
    rb = router_bias.astype(F32)[:, None]
    w_in_b = w_in.astype(BF16)
    w_out_b = w_out.astype(BF16)
    zeros = jnp.zeros_like(ln1_g)
    pv = jnp.stack([ln1_g, ln1_b, jnp.tile(hgrn_norm, (1, d // HEAD_DIM)), ln2_g, ln2_b, zeros, zeros, zeros],
                   axis=1).astype(F32)
    na_bias = _na_bias(na_rpb)
    sink = swa_sink.astype(F32)

    xa, xb, ctx_blk = x.reshape(n_b * l_, d), ctx.reshape(n_b * ctx_len, d), 0
    for l in range(depth):
        last = l == depth - 1
        hq, hv, hg, lf, kk, nqkv, sq, kx, vx = _inproj(xa, xb, ctx_blk, mod, w_in_b, gp, l, cos_t, sin_t, n_b, l_)
        o_f, o_b = _hgrn(hq, hv, lf, kk, n_b, l_, ctx_len)
        yb = _na(nqkv, na_bias, l, n_b, l_, ctx_len, not last)
        yc = _swa(sq, kx, vx, sink, l, n_b, l_, ctx_len, not last)
        n_tiles = n_lat_tiles if last else n_lat_tiles + 1
        x1, h2u, route, gt, counts = _outproj(o_f, o_b, hg, yb, yc, xa, xb, ctx_blk, mod, w_out_b, pv, l,
                                              wrh, rb, n_b, l_, n_tiles)
        xa = _moe_routed(h2u, route, gt, counts, w_gate, w_up, w_down, l, x1, mod, pv, n_b, l_, n_tiles)
        xb, ctx_blk = xa, n_lat_tiles
    return xa[:n_b * l_].reshape(n_b, l_, d)
```

```python
import functools

import numpy as np
import jax
import jax.numpy as jnp
from jax import lax
from jax.experimental import pallas as pl
from jax.experimental.pallas import tpu as pltpu
from jax.experimental.pallas import tpu_sc as plsc

F32 = jnp.float32
BF16 = jnp.bfloat16

D_MODEL = 1024
GRID_W = 64
HEAD_DIM = 64
HGRN_W = 256
NA_HEADS = 4
NA_WIN_ROWS = 8
NA_WIN_COLS = 16
SWA_Q_HEADS = 8
SWA_KV_HEADS = 2
SWA_GROUP = 4
SWA_WINDOW = 128
SWA_BLOCK = 128
ROPE_BASE = 10000.0
N_EXPERTS = 16
N_GROUPS = 4
D_EXPERT = 512
LN_EPS = 1e-5
RMS_EPS = 1e-6
NEG = -1e30
LB_MIN = 1e-6
DEPTH = 2
ALPHA = (2.0 * DEPTH) ** 0.25

LANES = 128
MXU_N = 256
TM = 512
HB = 256
HC = 16
NA_QROWS = 4
SWA_STEP_BLOCKS = 2
VMEM_LIMIT = 56 * 1024 * 1024

SC_MAX_ROWS = 192
SC_ROW_QUANTUM = 2048

N_COND = 8


def _dot(a, b):
    return jnp.dot(a, b, preferred_element_type=F32)


def _dot_nt(a, b):
    return lax.dot_general(a, b, (((1,), (1,)), ((), ())), preferred_element_type=F32)


def _dot_tn(a, b):
    return lax.dot_general(a, b, (((0,), (0,)), ((), ())), preferred_element_type=F32)


def _sigmoid(x):
    return 1.0 / (1.0 + jnp.exp(-x))


def _split_bf16(x):
    hi = x.astype(BF16)
    lo = (x - hi.astype(F32)).astype(BF16)
    return hi, lo


def _params(sem):
    return pltpu.CompilerParams(dimension_semantics=sem, vmem_limit_bytes=VMEM_LIMIT)


def _ada_kernel(n_rows, condt_ref, w_ref, b_ref, o_ref):
    c = condt_ref[...]
    s = c * _sigmoid(c)
    w = w_ref[0]
    rows = [jnp.sum(w * s[:, r:r + 1], axis=0, keepdims=True) for r in range(n_rows)]
    rows.append(jnp.zeros((N_COND - n_rows, w.shape[1]), F32))
    o_ref[0] = jnp.concatenate(rows, axis=0) + b_ref[0]


def _ada(cond_t, w_ada, b_ada, n_rows):
    depth, d, n6 = w_ada.shape
    tn = 1024
    return pl.pallas_call(
        functools.partial(_ada_kernel, n_rows),
        grid=(depth, n6 // tn),
        in_specs=[pl.BlockSpec((d, N_COND), lambda l, n: (0, 0)),
                  pl.BlockSpec((1, d, tn), lambda l, n: (l, 0, n)),
                  pl.BlockSpec((1, 1, tn), lambda l, n: (l, 0, n))],
        out_specs=pl.BlockSpec((1, N_COND, tn), lambda l, n: (l, 0, n)),
        out_shape=jax.ShapeDtypeStruct((depth, N_COND, n6), F32),
        compiler_params=_params(("arbitrary", "arbitrary")),
        name="ada",
    )(cond_t, w_ada, b_ada.reshape(depth, 1, n6))


def _rope(z, cos, sin, first):
    sw = jnp.where(first, pltpu.roll(z, LANES - 16, axis=1), pltpu.roll(z, 16, axis=1))
    return z * cos + sw * sin


def _inproj_kernel(n_lat, xa_ref, xb_ref, mod_ref, w_ref, gp_ref, cos_ref, sin_ref,
                   hq_ref, hv_ref, hg_ref, lf_ref, kk_ref, nqkv_ref, sq_ref, kx_ref, vx_ref):
    mod = mod_ref[0, 0]
    x = jnp.where(pl.program_id(0) < n_lat, xa_ref[...], xb_ref[...])
    h = (x * (1.0 + mod[1:2]) + mod[0:1]).astype(BF16)
    gp = gp_ref[0]
    cos = cos_ref[...]
    sin = sin_ref[...]
    lane = lax.broadcasted_iota(jnp.int32, cos.shape, 1)
    first = (lane & 16) == 0
    scale = HEAD_DIM ** -0.5

    def chunk(c):
        return _dot(h, w_ref[0, :, c * MXU_N:(c + 1) * MXU_N])

    def per_query_head(z2):
        swapped = pltpu.roll(z2, HEAD_DIM, axis=1)
        low = lane < HEAD_DIM
        h0 = jnp.where(low, z2, swapped)
        h1 = jnp.where(low, swapped, z2)
        return jnp.concatenate([h0, h0], axis=1), jnp.concatenate([h1, h1], axis=1)

    def gates(z, d):
        cols = slice(d * MXU_N, (d + 1) * MXU_N)
        a = gp[0:1, cols]
        b = gp[1:2, cols] + (jnp.minimum(z, 0.0) - jnp.log1p(jnp.exp(-jnp.abs(z))))
        logf = jnp.maximum(a, b) + jnp.log1p(jnp.exp(-jnp.abs(a - b)))
        k = gp[2:3, cols] * (1.0 / (1.0 + jnp.exp(z)))
        return logf, k

    def rope2(z):
        return jnp.concatenate([_rope(z[:, 0:LANES], cos, sin, first),
                                _rope(z[:, LANES:2 * LANES], cos, sin, first)], axis=1)

    for d in range(2):
        logf, k = gates(chunk(1 + d), d)
        lf_ref[:, d * MXU_N:(d + 1) * MXU_N] = logf
        kk_ref[:, d * MXU_N:(d + 1) * MXU_N] = k.astype(BF16)
    for d in range(2):
        sq_ref[:, d * MXU_N:(d + 1) * MXU_N] = (rope2(chunk(8 + d)) * scale).astype(BF16)
    zkv = chunk(10)
    k0, k1 = per_query_head(_rope(zkv[:, 0:LANES], cos, sin, first))
    v0, v1 = per_query_head(zkv[:, LANES:2 * LANES])
    kx_ref[:, 0:MXU_N] = k0.astype(BF16)
    kx_ref[:, MXU_N:2 * MXU_N] = k1.astype(BF16)
    vx_ref[:, 0:MXU_N] = v0.astype(BF16)
    vx_ref[:, MXU_N:2 * MXU_N] = v1.astype(BF16)
    hq_ref[...] = chunk(0).astype(BF16)
    hv_ref[...] = chunk(3).astype(BF16)
    hg_ref[...] = chunk(4).astype(BF16)
    nqkv_ref[:, 0:MXU_N] = (chunk(5) * scale).astype(BF16)
    nqkv_ref[:, MXU_N:2 * MXU_N] = chunk(6).astype(BF16)
    nqkv_ref[:, 2 * MXU_N:3 * MXU_N] = chunk(7).astype(BF16)


def _inproj(xa, xb, ctx_blk, mod, w, gp, layer, cos_t, sin_t, n_b, l_):
    d = xa.shape[1]
    tpb = l_ // TM
    n_lat = n_b * tpb
    nt = (n_lat + 1) * TM

    def mod_idx(j):
        return (layer, jnp.minimum(j // tpb, n_b), 0, 0)

    def rope_idx(j):
        return (jnp.where(j < n_lat, j % tpb, tpb), 0)

    def rows(width):
        return pl.BlockSpec((TM, width), lambda j: (j, 0))

    widths = [(HGRN_W, BF16), (HGRN_W, BF16), (HGRN_W, BF16), (2 * HGRN_W, F32), (2 * HGRN_W, BF16),
              (3 * 256, BF16), (512, BF16), (512, BF16), (512, BF16)]
    return pl.pallas_call(
        functools.partial(_inproj_kernel, n_lat),
        grid=(nt // TM,),
        in_specs=[pl.BlockSpec((TM, d), lambda j: (jnp.minimum(j, n_lat - 1), 0)),
                  pl.BlockSpec((TM, d), lambda j: (ctx_blk, 0)),
                  pl.BlockSpec((1, 1, 6, d), mod_idx),
                  pl.BlockSpec((1,) + w.shape[1:], lambda j: (layer, 0, 0)),
                  pl.BlockSpec((1,) + gp.shape[1:], lambda j: (layer, 0, 0)),
                  pl.BlockSpec((TM, LANES), rope_idx),
                  pl.BlockSpec((TM, LANES), rope_idx)],
        out_specs=[rows(wd) for wd, _ in widths],
        out_shape=[jax.ShapeDtypeStruct((nt, wd), dt) for wd, dt in widths],
        compiler_params=_params(("arbitrary",)),
        name="inproj",
    )(xa, xb, mod, w, gp, cos_t, sin_t)


def _hgrn_consts():
    t = np.arange(HB)[:, None]
    u = np.arange(HB)[None, :]
    same = (t // HC) == (u // HC)
    tr, ur = t % HC, u % HC
    half = HC // 2
    mats, codes = [], []
    for fwd in (True, False):
        if fwd:
            incl = ur <= tr
            mid = ur <= (tr // half) * half + half // 2 - 1
            edge = ur <= half - 1
            code = np.where((t // half == u // half) & (u <= t), 1, np.where(same & (tr >= half) & (ur < half), 2, 0))
        else:
            incl = ur >= tr
            mid = ur >= (tr // half) * half + half // 2
            edge = ur >= half
            code = np.where((t // half == u // half) & (u >= t), 1, np.where(same & (tr < half) & (ur >= half), 2, 0))
        mats.append(np.concatenate([same & incl, same, same & mid, same & edge], axis=0))
        codes.append(code)
    return (np.stack(mats).astype(np.float32).astype(jnp.bfloat16), np.stack(codes).astype(np.float32))


def _hgrn_kernel(qf_ref, vf_ref, lff_ref, kf_ref, qb_ref, vb_ref, lfb_ref, kb_ref, cm_ref, code_ref,
                 of_ref, ob_ref, st_ref, qd_ref, kd_ref, vv_ref, gt_ref, it_ref):
    nch = HB // HC
    n_heads = HGRN_W // HEAD_DIM

    @pl.when(pl.program_id(1) == 0)
    def _():
        st_ref[...] = jnp.zeros(st_ref.shape, F32)

    lane3 = lax.broadcasted_iota(jnp.int32, (nch, HC, HGRN_W), 2) // HEAD_DIM

    def head_rows(x):
        x3 = x.astype(BF16).reshape(nch, HC, HGRN_W)
        zero = jnp.zeros_like(x3)
        return jnp.concatenate([jnp.where(lane3 == h, x3, zero) for h in range(n_heads)], axis=1)

    def prep(d, q_ref, v_ref, lf_ref, k_ref):
        hi, lo = _split_bf16(lf_ref[...])
        cm = cm_ref[d]
        acc = _dot(jnp.concatenate([cm[0:2 * HB], cm[0:2 * HB]], axis=1), jnp.concatenate([hi, lo], axis=0))
        ref = _dot(cm[2 * HB:4 * HB], hi)
        cum, tot = acc[0:HB], acc[HB:2 * HB]
        mid, edge = ref[0:HB], ref[HB:2 * HB]
        q = q_ref[...].astype(F32)
        k = k_ref[...].astype(F32)
        v = v_ref[...]
        qd_ref[d] = head_rows(q * jnp.exp(cum))
        kd_ref[d] = head_rows(k * jnp.exp(tot - cum))
        vv_ref[d] = jnp.concatenate([v[:, h * HEAD_DIM:(h + 1) * HEAD_DIM].reshape(nch, HC, HEAD_DIM)
                                     for h in range(n_heads)], axis=1)
        gt_ref[d] = jnp.exp(tot)
        q1 = (q * jnp.exp(cum - mid)).astype(BF16)
        k1 = (k * jnp.exp(mid - cum)).astype(BF16)
        upper = (lax.broadcasted_iota(jnp.int32, q.shape, 0) % HC >= HC // 2) == (d == 0)
        q2 = jnp.where(upper, q * jnp.exp(cum - edge), 0.0).astype(BF16)
        k2 = jnp.where(upper, 0.0, k * jnp.exp(edge - cum)).astype(BF16)
        s1 = _dot_nt(_head_stack(q1, n_heads), k1)
        s2 = _dot_nt(_head_stack(q2, n_heads), k2)
        code = jnp.concatenate([code_ref[d]] * n_heads, axis=0)
        p = jnp.where(code == 1.0, s1, jnp.where(code == 2.0, s2, 0.0))
        return _head_unstack(_dot(p.astype(BF16), v), n_heads)

    od_f = prep(0, qf_ref, vf_ref, lff_ref, kf_ref)
    od_b = prep(1, qb_ref, vb_ref, lfb_ref, kb_ref)

    for i in range(nch):
        for d, c in ((0, i), (1, nch - 1 - i)):
            st = st_ref[d]
            it_ref[d, c] = _dot_nt(qd_ref[d, c], st.astype(BF16))
            upd = _dot_tn(vv_ref[d, c], kd_ref[d, c])
            st_ref[d] = st * gt_ref[d, c * HC:c * HC + 1, :] + upd

    def inter(d):
        it = it_ref[d]
        return jnp.concatenate([it[:, h * HC:(h + 1) * HC, :].reshape(HB, HEAD_DIM) for h in range(n_heads)],
                               axis=1)

    of_ref[...] = inter(0) + od_f
    ob_ref[...] = inter(1) + od_b


def _hgrn(hq, hv, lf, kk, n_b, l_, ctx_len):
    nt = hq.shape[0]
    assert ctx_len == HB
    nlb = l_ // HB
    ctx0 = n_b * nlb
    cmats, codes = _hgrn_consts()
    heads = HGRN_W // HEAD_DIM

    def fwd_idx(col):
        return lambda b, s: (jnp.where(s == 0, ctx0 + b, b * nlb + s - 1), col)

    def bwd_idx(col):
        return lambda b, s: (jnp.where(s == 0, ctx0 + b, b * nlb + nlb - s), col)

    def blk(idx):
        return pl.BlockSpec((HB, HGRN_W), idx)

    full = lambda a: pl.BlockSpec(a.shape, lambda b, s: (0,) * a.ndim)
    return pl.pallas_call(
        _hgrn_kernel,
        grid=(n_b, nlb + 1),
        in_specs=[blk(fwd_idx(0)), blk(fwd_idx(0)), blk(fwd_idx(0)), blk(fwd_idx(0)),
                  blk(bwd_idx(0)), blk(bwd_idx(0)), blk(bwd_idx(1)), blk(bwd_idx(1)),
                  full(cmats), full(codes)],
        out_specs=[blk(fwd_idx(0)), blk(bwd_idx(0))],
        out_shape=[jax.ShapeDtypeStruct((nt, HGRN_W), F32)] * 2,
        scratch_shapes=[pltpu.VMEM((2, HEAD_DIM, HGRN_W), F32),
                        pltpu.VMEM((2, HB // HC, heads * HC, HGRN_W), BF16),
                        pltpu.VMEM((2, HB // HC, heads * HC, HGRN_W), BF16),
                        pltpu.VMEM((2, HB // HC, heads * HC, HEAD_DIM), BF16),
                        pltpu.VMEM((2, HB, HGRN_W), F32),
                        pltpu.VMEM((2, HB // HC, heads * HC, HEAD_DIM), F32)],
        compiler_params=_params(("arbitrary", "arbitrary")),
        name="hgrn",
    )(hq, hv, lf, kk, hq, hv, lf, kk, cmats, codes)


def _head_stack(q, n_heads):
    lane = lax.broadcasted_iota(jnp.int32, q.shape, 1)
    zero = jnp.zeros_like(q)
    return jnp.concatenate([jnp.where(lane // HEAD_DIM == h, q, zero) for h in range(n_heads)], axis=0)


def _head_unstack(o, n_heads):
    rows = o.shape[0] // n_heads
    lane = lax.broadcasted_iota(jnp.int32, (rows, o.shape[1]), 1)
    acc = jnp.zeros((rows, o.shape[1]), F32)
    for h in range(n_heads):
        acc = acc + jnp.where(lane // HEAD_DIM == h, o[h * rows:(h + 1) * rows], 0.0)
    return acc


def _na_kernel(n_rows, n_lat_steps, qrows, q_ref, k_ref, v_ref, kc_ref, vc_ref, bias_ref, o_ref):
    j = pl.program_id(1)
    kc = kc_ref[...]
    vc = vc_ref[...]
    nwin = NA_WIN_ROWS * GRID_W

    def attend(rr, win):
        q4 = _head_stack(q_ref[rr * GRID_W:(rr + 1) * GRID_W, :], NA_HEADS)
        s_ctx = _dot_nt(q4, kc)
        m = jnp.max(s_ctx, axis=1, keepdims=True)
        if win is not None:
            kw, vw, bias = win
            s_win = _dot_nt(q4, kw) + bias
            m = jnp.maximum(m, jnp.max(s_win, axis=1, keepdims=True))
            p_win = jnp.exp(s_win - m)
        p_ctx = jnp.exp(s_ctx - m)
        den = jnp.sum(p_ctx, axis=1, keepdims=True)
        o4 = _dot(p_ctx.astype(BF16), vc)
        if win is not None:
            den = den + jnp.sum(p_win, axis=1, keepdims=True)
            o4 = o4 + _dot(p_win.astype(BF16), vw)
        o = _head_unstack(o4 * (1.0 / den), NA_HEADS)
        o_ref[rr * GRID_W:(rr + 1) * GRID_W, :] = o.astype(BF16)

    @pl.when(j < n_lat_steps)
    def _():
        for rr in range(qrows):
            r = j * qrows + rr
            rs = jnp.clip(r - NA_WIN_ROWS // 2, 0, n_rows - NA_WIN_ROWS)
            start = pl.multiple_of(rs * GRID_W, GRID_W)
            kw = k_ref[pl.ds(start, nwin), :]
            vw = v_ref[pl.ds(start, nwin), :]
            first = NA_WIN_ROWS - 1 - (r - rs)
            bias = jnp.concatenate([bias_ref[0, first + 2 * p] for p in range(NA_WIN_ROWS // 2)], axis=1)
            attend(rr, (kw, vw, bias))

    @pl.when(j >= n_lat_steps)
    def _():
        for rr in range(qrows):
            attend(rr, None)


def _na(nqkv, bias, layer, n_b, l_, ctx_len, with_ctx):
    nt = nqkv.shape[0]
    qrows = NA_QROWS if with_ctx else 2 * NA_QROWS
    qb = qrows * GRID_W
    assert ctx_len == qb or not with_ctx
    n_lat = l_ // qb
    ctx0 = n_b * n_lat
    n_rows = l_ // GRID_W

    def q_idx(b, j):
        return (jnp.where(j < n_lat, b * n_lat + j, ctx0 + b), 0)

    cblk = (n_b * l_) // ctx_len
    return pl.pallas_call(
        functools.partial(_na_kernel, n_rows, n_lat, qrows),
        grid=(n_b, n_lat + (1 if with_ctx else 0)),
        in_specs=[pl.BlockSpec((qb, 256), q_idx),
                  pl.BlockSpec((l_, 256), lambda b, j: (b, 1)),
                  pl.BlockSpec((l_, 256), lambda b, j: (b, 2)),
                  pl.BlockSpec((ctx_len, 256), lambda b, j: (cblk + b, 1)),
                  pl.BlockSpec((ctx_len, 256), lambda b, j: (cblk + b, 2)),
                  pl.BlockSpec((1,) + bias.shape[1:], lambda b, j: (layer, 0, 0, 0))],
        out_specs=pl.BlockSpec((qb, 256), q_idx),
        out_shape=jax.ShapeDtypeStruct((nt if with_ctx else n_b * l_, 256), BF16),
        compiler_params=_params(("arbitrary", "arbitrary")),
        name="natten",
    )(nqkv, nqkv, nqkv, nqkv, nqkv, bias)


def _swa_band_bias(n_blocks):
    sb = SWA_BLOCK
    u = np.arange(sb)[:, None]
    v = np.arange(3 * sb)[None, :]
    tabs = []
    for j in (0, 1, n_blocks - 1):
        kpos = (j - 1) * sb + v
        ok = (kpos >= 0) & (kpos < n_blocks * sb) & (np.abs(j * sb + u - kpos) <= SWA_WINDOW)
        tabs.append(np.tile(np.where(ok, 0.0, NEG), (SWA_GROUP, 1)))
    return np.stack(tabs).astype(np.float32)


def _swa_kernel(n_steps, layer, sink_ref, q_ref, kp_ref, kq_ref, kn_ref, vp_ref, vq_ref, vn_ref,
                kc_ref, vc_ref, bias_ref, o_ref):
    j = pl.program_id(1)
    sb = SWA_BLOCK

    def attend(a, n, band):
        cols = slice(n * MXU_N, (n + 1) * MXU_N)
        rows = slice(a * sb, (a + 1) * sb)
        q4 = _head_stack(q_ref[rows, cols], SWA_GROUP)
        sink = jnp.concatenate([jnp.full((sb, 1), sink_ref[layer, n * SWA_GROUP + g], F32)
                                for g in range(SWA_GROUP)], axis=0)
        if band is None:
            keys, vals = kc_ref[:, cols], vc_ref[:, cols]
            s = _dot_nt(q4, keys)
        else:
            k_refs, v_refs, bias = band
            keys = jnp.concatenate([r[rs, cols] for r, rs in k_refs] + [kc_ref[:, cols]], axis=0)
            vals = jnp.concatenate([r[rs, cols] for r, rs in v_refs] + [vc_ref[:, cols]], axis=0)
            s = _dot_nt(q4, keys)
            s = jnp.concatenate([s[:, 0:3 * sb] + bias, s[:, 3 * sb:]], axis=1)
        m = jnp.maximum(jnp.max(s, axis=1, keepdims=True), sink)
        p = jnp.exp(s - m)
        den = jnp.sum(p, axis=1, keepdims=True) + jnp.exp(sink - m)
        o4 = _dot(p.astype(BF16), vals)
        o_ref[rows, cols] = _head_unstack(o4 * (1.0 / den), SWA_GROUP).astype(BF16)

    lo, hi, whole = slice(0, sb), slice(sb, 2 * sb), slice(0, sb)

    @pl.when(j < n_steps)
    def _():
        bands = [([(kp_ref, whole), (kq_ref, lo), (kq_ref, hi)], [(vp_ref, whole), (vq_ref, lo), (vq_ref, hi)],
                  bias_ref[jnp.where(j == 0, 0, 1)]),
                 ([(kq_ref, lo), (kq_ref, hi), (kn_ref, whole)], [(vq_ref, lo), (vq_ref, hi), (vn_ref, whole)],
                  bias_ref[jnp.where(j == n_steps - 1, 2, 1)])]
        for a in range(SWA_STEP_BLOCKS):
            for n in range(SWA_KV_HEADS):
                attend(a, n, bands[a])

    @pl.when(j >= n_steps)
    def _():
        for a in range(SWA_STEP_BLOCKS):
            for n in range(SWA_KV_HEADS):
                attend(a, n, None)


def _swa(sq, kx, vx, sink, layer, n_b, l_, ctx_len, with_ctx):
    nt = sq.shape[0]
    sb = SWA_BLOCK
    qb = SWA_STEP_BLOCKS * sb
    nb = l_ // sb
    ns = l_ // qb
    assert SWA_STEP_BLOCKS == 2 and nb % 2 == 0 and nb >= 4 and ctx_len == qb
    ctx0 = (n_b * l_) // qb
    bias = _swa_band_bias(nb)

    def q_idx(b, j):
        return (jnp.where(j < ns, b * ns + j, ctx0 + b), 0)

    def cur_idx(b, j):
        return (b * ns + jnp.minimum(j, ns - 1), 0)

    def edge_idx(off):
        return lambda b, j: (b * nb + jnp.clip(SWA_STEP_BLOCKS * jnp.minimum(j, ns - 1) + off, 0, nb - 1), 0)

    edge = lambda idx: pl.BlockSpec((sb, 512), idx)
    wide = lambda idx: pl.BlockSpec((qb, 512), idx)
    cspec = pl.BlockSpec((ctx_len, 512), lambda b, j: (ctx0 + b, 0))
    return pl.pallas_call(
        functools.partial(_swa_kernel, ns, layer),
        grid=(n_b, ns + (1 if with_ctx else 0)),
        in_specs=[pl.BlockSpec(memory_space=pltpu.SMEM),
                  wide(q_idx), edge(edge_idx(-1)), wide(cur_idx), edge(edge_idx(SWA_STEP_BLOCKS)),
                  edge(edge_idx(-1)), wide(cur_idx), edge(edge_idx(SWA_STEP_BLOCKS)), cspec, cspec,
                  pl.BlockSpec(bias.shape, lambda b, j: (0, 0, 0))],
        out_specs=wide(q_idx),
        out_shape=jax.ShapeDtypeStruct((nt if with_ctx else n_b * l_, 512), BF16),
        compiler_params=_params(("arbitrary", "arbitrary")),
        name="swa",
    )(sink, sq, kx, kx, kx, vx, vx, vx, kx, vx, bias)


def _layer_norm(r, g, b):
    mu = jnp.mean(r, axis=-1, keepdims=True)
    rc = r - mu
    var = jnp.mean(rc * rc, axis=-1, keepdims=True)
    return rc * lax.rsqrt(var + LN_EPS) * g + b


def _route(logits_t, bias_col):
    epg = N_EXPERTS // N_GROUPS
    s = _sigmoid(logits_t)
    sel = s + bias_col
    sel_r = [sel[e:e + 1, :] for e in range(N_EXPERTS)]
    s_r = [s[e:e + 1, :] for e in range(N_EXPERTS)]
    grp = []
    for g in range(N_GROUPS):
        a = sel_r[g * epg:(g + 1) * epg]
        m1 = functools.reduce(jnp.maximum, a)
        m2 = functools.reduce(jnp.maximum,
                              [jnp.minimum(a[i], a[k]) for i in range(epg) for k in range(i + 1, epg)])
        grp.append(m1 + m2)
    m_r, w_r = [], []
    for g in range(N_GROUPS):
        best = None
        for k in range(N_GROUPS):
            if k == g:
                continue
            c = (grp[g] > grp[k]) if k < g else (grp[g] >= grp[k])
            best = c if best is None else (best & c)
        for e in range(g * epg, (g + 1) * epg):
            rank = jnp.zeros_like(sel_r[e])
            for k in range(g * epg, (g + 1) * epg):
                if k == e:
                    continue
                ahead = (sel_r[k] >= sel_r[e]) if k < e else (sel_r[k] > sel_r[e])
                rank = rank + jnp.where(ahead, 1.0, 0.0)
            chosen = jnp.where(best & (rank < 1.5), 1.0, 0.0)
            m_r.append(chosen)
            w_r.append(chosen * s_r[e])
    inv = 1.0 / functools.reduce(lambda a, b: a + b, w_r)
    return m_r, [w * inv for w in w_r]


def _pack_bf16_pairs(a, b):
    ua = pltpu.bitcast(a.astype(BF16).astype(F32), jnp.uint32)
    ub = pltpu.bitcast(b.astype(BF16).astype(F32), jnp.uint32)
    return (ua >> 16) | ub


def _unpack_bf16_pairs(u):
    return (pltpu.bitcast(u << 16, F32), pltpu.bitcast(u & jnp.uint32(0xFFFF0000), F32))


def _outproj_kernel(n_lat, of_ref, ob_ref, hg_ref, yb_ref, yc_ref, xa_ref, xb_ref, mod_ref, wo_ref, pv_ref,
                    wrh_ref, rb_ref, before_ref, x1_ref, h2u_ref, route_ref, gt_ref, cnt_ref):
    @pl.when(pl.program_id(0) == 0)
    def _():
        cnt_ref[...] = jnp.zeros(cnt_ref.shape, F32)

    row = lax.broadcasted_iota(jnp.int32, (HGRN_W, HGRN_W), 0)
    col = lax.broadcasted_iota(jnp.int32, (HGRN_W, HGRN_W), 1)
    head_ones = jnp.where((row // HEAD_DIM) == (col // HEAD_DIM), 1.0, 0.0).astype(BF16)
    pv = pv_ref[0]
    o = of_ref[...] + ob_ref[...]
    hi, lo = _split_bf16(o * o)
    ms = _dot(jnp.concatenate([hi, lo], axis=1), jnp.concatenate([head_ones, head_ones], axis=0)) * (1.0 / HEAD_DIM)
    gate = hg_ref[...].astype(F32)
    ya = o * lax.rsqrt(ms + RMS_EPS) * pv[2:3, 0:HGRN_W] * (gate * _sigmoid(gate))
    y = _dot(jnp.concatenate([ya.astype(BF16), yb_ref[...], yc_ref[...]], axis=1), wo_ref[0])
    mod = mod_ref[0, 0]
    x = jnp.where(pl.program_id(0) < n_lat, xa_ref[...], xb_ref[...])
    x1 = _layer_norm(ALPHA * x + mod[2:3] * y, pv[0:1], pv[1:2])
    x1_ref[...] = x1
    h2 = x1 * (1.0 + mod[4:5]) + mod[3:4]
    hh, hl = _split_bf16(h2)
    half = h2.shape[1] // 2
    h2u_ref[...] = _pack_bf16_pairs(h2[:, 0:half], h2[:, half:2 * half])
    wr = wrh_ref[...]
    logits_t = _dot_nt(jnp.concatenate([wr, wr], axis=1), jnp.concatenate([hh, hl], axis=1))
    m_r, w_r = _route(logits_t, rb_ref[...])
    m_t = jnp.concatenate(m_r, axis=0)
    rank_t = _dot(m_t.astype(BF16), before_ref[...]) + cnt_ref[:, 0:1]
    cnt_ref[...] = cnt_ref[...] + jnp.sum(m_t, axis=1, keepdims=True)
    seen = jnp.zeros_like(m_r[0])
    e1 = e2 = r1 = r2 = g1 = g2 = jnp.zeros_like(m_r[0])
    for e in range(N_EXPERTS):
        is1 = m_r[e] * (1.0 - seen)
        is2 = m_r[e] * seen
        rk = rank_t[e:e + 1, :]
        e1, e2 = e1 + e * is1, e2 + e * is2
        r1, r2 = r1 + rk * is1, r2 + rk * is2
        g1, g2 = g1 + w_r[e] * is1, g2 + w_r[e] * is2
        seen = seen + is1
    zero = jnp.zeros_like(e1)
    route_ref[...] = jnp.concatenate([e1, e2, r1, r2, g1, g2, zero, zero], axis=0)
    pad = jnp.zeros((LANES - 2, g1.shape[1]), F32)
    gt_ref[...] = jnp.concatenate([g1, g2, pad], axis=0).T


def _outproj(o_f, o_b, hg, yb, yc, xa, xb, ctx_blk, mod, wo, pv, layer, wrh, rb, n_b, l_, n_tiles):
    d = xa.shape[1]
    tpb = l_ // TM
    n_lat = n_b * tpb
    before = np.triu(np.ones((TM, TM), np.float32), 1).astype(jnp.bfloat16)

    def mod_idx(j):
        return (layer, jnp.minimum(j // tpb, n_b), 0, 0)

    def rows(width):
        return pl.BlockSpec((TM, width), lambda j: (j, 0))

    full = lambda a: pl.BlockSpec(a.shape, lambda j: (0,) * a.ndim)
    of_layer = lambda a: pl.BlockSpec((1,) + a.shape[1:], lambda j: (layer,) + (0,) * (a.ndim - 1))
    nr = n_tiles * TM
    return pl.pallas_call(
        functools.partial(_outproj_kernel, n_lat),
        grid=(n_tiles,),
        in_specs=[rows(256), rows(256), rows(256), rows(256), rows(512),
                  pl.BlockSpec((TM, d), lambda j: (jnp.minimum(j, n_lat - 1), 0)),
                  pl.BlockSpec((TM, d), lambda j: (ctx_blk, 0)),
                  pl.BlockSpec((1, 1, 6, d), mod_idx), of_layer(wo), of_layer(pv), full(wrh), full(rb),
                  full(before)],
        out_specs=[rows(d), rows(d // 2), pl.BlockSpec((8, TM), lambda j: (0, j)), rows(LANES),
                   pl.BlockSpec((N_EXPERTS, LANES), lambda j: (0, 0))],
        out_shape=[jax.ShapeDtypeStruct((nr, d), F32), jax.ShapeDtypeStruct((nr, d // 2), jnp.uint32),
                   jax.ShapeDtypeStruct((8, nr), F32), jax.ShapeDtypeStruct((nr, LANES), F32),
                   jax.ShapeDtypeStruct((N_EXPERTS, LANES), F32)],
        compiler_params=_params(("arbitrary",)),
        name="outproj",
    )(o_f, o_b, hg, yb, yc, xa, xb, mod, wo, pv, wrh, rb, before)


def _sc_split(rows_per_worker):
    best = None
    for chunk in (64, 48, 32):
        for nbuf in range(SC_MAX_ROWS // chunk, 0, -1):
            if rows_per_worker % (chunk * nbuf) == 0 and (best is None or chunk * nbuf > best[0] * best[1]):
                best = (chunk, nbuf)
    return best


def _sc_workers():
    info = plsc.get_sparse_core_info()
    return info.num_cores, info.num_cores * info.num_subcores


def _sc_scatter_rows(table, pos, n_slots):
    n_tok, width = table.shape
    n_cores, n_workers = _sc_workers()
    per_w = n_tok // n_workers
    assert n_tok % n_workers == 0
    chunk, nbuf = _sc_split(per_w)
    n_chunks = per_w // chunk
    mesh = plsc.VectorSubcoreMesh(core_axis_name="c", subcore_axis_name="s")

    @functools.partial(
        pl.kernel, mesh=mesh,
        out_type=jax.ShapeDtypeStruct((n_slots, width), table.dtype),
        scratch_types=[pltpu.VMEM((2 * n_chunks, chunk), jnp.int32),
                       pltpu.VMEM((nbuf, chunk, width), table.dtype),
                       pltpu.SemaphoreType.DMA((nbuf,)),
                       pltpu.SemaphoreType.DMA((nbuf,))],
    )
    def scatter(table_hbm, pos_hbm, out_hbm, idx_v, rows_v, rsem, wsem):
        wid = lax.axis_index("s") * n_cores + lax.axis_index("c")
        base = wid * per_w
        pltpu.sync_copy(pos_hbm.at[wid], idx_v)

        @pl.loop(0, n_chunks // nbuf)
        def _(g):
            c0 = g * nbuf
            gets = [pltpu.async_copy(table_hbm.at[pl.ds(base + (c0 + b) * chunk, chunk)], rows_v.at[b],
                                     rsem.at[b]) for b in range(nbuf)]
            puts = []
            for b in range(nbuf):
                gets[b].wait()
                for k in range(2):
                    puts.append(pltpu.async_copy(rows_v.at[b], out_hbm.at[idx_v.at[k * n_chunks + c0 + b]],
                                                 wsem.at[b]))
            for put in puts:
                put.wait()

    pos_w = pos.reshape(2, n_workers, n_chunks, chunk).transpose(1, 0, 2, 3).reshape(n_workers, 2 * n_chunks, chunk)
    return scatter(table, pos_w)


def _sc_gather_rows(table, idx):
    n_out = idx.shape[0]
    width = table.shape[1]
    n_cores, n_workers = _sc_workers()
    per_w = n_out // n_workers
    assert n_out % n_workers == 0
    chunk, nbuf = _sc_split(per_w)
    mesh = plsc.VectorSubcoreMesh(core_axis_name="c", subcore_axis_name="s")

    @functools.partial(
        pl.kernel, mesh=mesh,
        out_type=jax.ShapeDtypeStruct((n_out, width), table.dtype),
        scratch_types=[pltpu.VMEM((per_w,), jnp.int32),
                       pltpu.VMEM((nbuf, chunk, width), table.dtype),
                       pltpu.SemaphoreType.DMA((nbuf,)),
                       pltpu.SemaphoreType.DMA((nbuf,))],
    )
    def gather(table_hbm, idx_hbm, out_hbm, idx_v, rows_v, gsem, wsem):
        wid = lax.axis_index("s") * n_cores + lax.axis_index("c")
        base = wid * per_w
        pltpu.sync_copy(idx_hbm.at[pl.ds(base, per_w)], idx_v)

        @pl.loop(0, per_w // (chunk * nbuf))
        def _(g):
            off = g * (chunk * nbuf)
            gets = [pltpu.async_copy(table_hbm.at[idx_v.at[pl.ds(off + b * chunk, chunk)]], rows_v.at[b],
                                     gsem.at[b]) for b in range(nbuf)]
            puts = []
            for b in range(nbuf):
                gets[b].wait()
                puts.append(pltpu.async_copy(rows_v.at[b], out_hbm.at[pl.ds(base + off + b * chunk, chunk)],
                                             wsem.at[b]))
            for put in puts:
                put.wait()

    return gather(table, idx)


def _gmm_kernel(layer, te_ref, nv_ref, nu_ref, slot_ref, nxt_ref, xs_ref, wg_hbm, wu_hbm, wd_hbm, ys_ref,
                wgf_ref, wuf_ref, wdf_ref, wgb_ref, wub_ref, wdb_ref, sem):
    i = pl.program_id(0)
    used = i < nu_ref[0]
    fresh = jnp.logical_and(used, jnp.logical_or(i == 0, te_ref[i] != te_ref[jnp.maximum(i - 1, 0)]))

    def fetch(e, slot):
        return [pltpu.make_async_copy(wg_hbm.at[layer, e], wgf_ref.at[slot], sem.at[slot, 0]),
                pltpu.make_async_copy(wu_hbm.at[layer, e], wuf_ref.at[slot], sem.at[slot, 1]),
                pltpu.make_async_copy(wd_hbm.at[layer, e], wdf_ref.at[slot], sem.at[slot, 2])]

    @pl.when(jnp.logical_and(fresh, i == 0))
    def _():
        for cp in fetch(te_ref[0], 0):
            cp.start()

    @pl.when(fresh)
    def _():
        slot = slot_ref[i]
        for cp in fetch(te_ref[i], slot):
            cp.wait()
        wgb_ref[...] = wgf_ref[slot].astype(BF16)
        wub_ref[...] = wuf_ref[slot].astype(BF16)
        wdb_ref[...] = wdf_ref[slot].astype(BF16)

        @pl.when(nxt_ref[i] >= 0)
        def _():
            for cp in fetch(nxt_ref[i], 1 - slot):
                cp.start()

    @pl.when(used)
    def _():
        rowid = lax.broadcasted_iota(jnp.int32, xs_ref.shape, 0)
        lo, hi = _unpack_bf16_pairs(jnp.where(rowid < nv_ref[i], xs_ref[...], jnp.uint32(0)))
        lo, hi = lo.astype(BF16), hi.astype(BF16)
        half = lo.shape[1]
        x = jnp.concatenate([lo, hi], axis=1)
        a = _dot(x, wgb_ref[...])
        u = _dot(x, wub_ref[...])
        y = _dot((a * _sigmoid(a) * u).astype(BF16), wdb_ref[...])
        ys_ref[...] = _pack_bf16_pairs(y[:, 0:half], y[:, half:2 * half])

    @pl.when(jnp.logical_not(used))
    def _():
        ys_ref[...] = jnp.zeros(ys_ref.shape, jnp.uint32)


def _gmm(tile_expert, tile_valid, n_used, slot, nxt, xs, wg, wu, wd, layer):
    p_rows, half = xs.shape
    d = 2 * half
    rows = lambda: pl.BlockSpec((TM, half), lambda i, te, nv, nu, sl, nx: (i, 0))
    anyspec = pl.BlockSpec(memory_space=pl.ANY)
    return pl.pallas_call(
        functools.partial(_gmm_kernel, layer),
        grid_spec=pltpu.PrefetchScalarGridSpec(
            num_scalar_prefetch=5,
            grid=(p_rows // TM,),
            in_specs=[rows(), anyspec, anyspec, anyspec],
            out_specs=rows(),
            scratch_shapes=[pltpu.VMEM((2, d, D_EXPERT), F32), pltpu.VMEM((2, d, D_EXPERT), F32),
                            pltpu.VMEM((2, D_EXPERT, d), F32),
                            pltpu.VMEM((d, D_EXPERT), BF16), pltpu.VMEM((d, D_EXPERT), BF16),
                            pltpu.VMEM((D_EXPERT, d), BF16), pltpu.SemaphoreType.DMA((2, 3))]),
        out_shape=jax.ShapeDtypeStruct((p_rows, half), jnp.uint32),
        compiler_params=_params(("arbitrary",)),
        name="moe_experts",
    )(tile_expert, tile_valid, n_used, slot, nxt, xs, wg, wu, wd)


def _moefin_kernel(o1_ref, o2_ref, gt_ref, x_ref, mod_ref, pv_ref, out_ref):
    gt = gt_ref[...]
    g1, g2 = gt[:, 0:1], gt[:, 1:2]
    a1, b1 = _unpack_bf16_pairs(o1_ref[...])
    a2, b2 = _unpack_bf16_pairs(o2_ref[...])
    f = jnp.concatenate([g1 * a1 + g2 * a2, g1 * b1 + g2 * b2], axis=1)
    mod = mod_ref[0, 0]
    pv = pv_ref[0]
    out_ref[...] = _layer_norm(ALPHA * x_ref[...] + mod[5:6] * f, pv[3:4], pv[4:5])


def _moefin(o12, gt, x1, mod, pv, layer, n_b, l_, n_tiles):
    d = x1.shape[1]
    tpb = l_ // TM

    def mod_idx(j):
        return (layer, jnp.minimum(j // tpb, n_b), 0, 0)

    return pl.pallas_call(
        _moefin_kernel,
        grid=(n_tiles,),
        in_specs=[pl.BlockSpec((TM, d // 2), lambda j: (j, 0)),
                  pl.BlockSpec((TM, d // 2), lambda j: (j + n_tiles, 0)),
                  pl.BlockSpec((TM, LANES), lambda j: (j, 0)),
                  pl.BlockSpec((TM, d), lambda j: (j, 0)),
                  pl.BlockSpec((1, 1, 6, d), mod_idx),
                  pl.BlockSpec((1,) + pv.shape[1:], lambda j: (layer, 0, 0))],
        out_specs=pl.BlockSpec((TM, d), lambda j: (j, 0)),
        out_shape=jax.ShapeDtypeStruct((n_tiles * TM, d), F32),
        compiler_params=_params(("arbitrary",)),
        name="moe_combine",
    )(o12, o12, gt, x1, mod, pv)


def _moe_routed(h2u, route, gt, counts, wg, wu, wd, layer, x1, mod, pv, n_b, l_, n_tiles):
    n_tok = n_tiles * TM
    n_slots = -(-(2 * n_tok + N_EXPERTS * TM) // SC_ROW_QUANTUM) * SC_ROW_QUANTUM
    cnt = counts[:, 0].astype(jnp.int32)
    padded = ((cnt + TM - 1) // TM) * TM
    upto = jnp.arange(N_EXPERTS)[None, :] <= jnp.arange(N_EXPERTS)[:, None]
    ends = jnp.sum(jnp.where(upto, padded[None, :], 0), axis=1)
    offs = ends - padded
    e12 = route[0:2].astype(jnp.int32)
    r12 = route[2:4].astype(jnp.int32)
    onehot = e12[:, :, None] == jnp.arange(N_EXPERTS, dtype=jnp.int32)
    pos = r12 + jnp.sum(jnp.where(onehot, offs, 0), axis=-1)
    tile_start = jnp.arange(n_slots // TM, dtype=jnp.int32) * TM
    tile_expert = jnp.minimum(jnp.sum(tile_start[:, None] >= ends[None, :], axis=1), N_EXPERTS - 1)
    pick = tile_expert[:, None] == jnp.arange(N_EXPERTS, dtype=jnp.int32)
    tile_valid = jnp.clip(jnp.sum(jnp.where(pick, (offs + cnt)[None, :], 0), axis=1) - tile_start, 0, TM)
    n_used = (ends[N_EXPERTS - 1] // TM).reshape(1)
    xs = _sc_scatter_rows(h2u, pos, n_slots)
    n_tile = n_slots // TM
    t_idx = jnp.arange(n_tile, dtype=jnp.int32)
    prev_e = jnp.concatenate([jnp.full((1,), -1, jnp.int32), tile_expert[:-1].astype(jnp.int32)])
    first = (t_idx < n_used[0]) & (tile_expert != prev_e)
    switches = jnp.sum(jnp.where((t_idx[None, :] <= t_idx[:, None]) & first[None, :], 1, 0), axis=1)
    slot = (switches + 1) % 2
    ids = jnp.arange(N_EXPERTS, dtype=jnp.int32)
    later = (ids[None, :] > ids[:, None]) & (cnt[None, :] > 0)
    next_e = jnp.min(jnp.where(later, ids[None, :], N_EXPERTS), axis=1)
    next_e = jnp.where(next_e < N_EXPERTS, next_e, -1)
    nxt = jnp.sum(jnp.where(pick, next_e[None, :], 0), axis=1)
    ys = _gmm(tile_expert.astype(jnp.int32), tile_valid.astype(jnp.int32), n_used.astype(jnp.int32),
              slot.astype(jnp.int32), nxt.astype(jnp.int32), xs, wg, wu, wd, layer)
    o12 = _sc_gather_rows(ys, pos.reshape(2 * n_tok))
    return _moefin(o12, gt, x1, mod, pv, layer, n_b, l_, n_tiles)


def _rope_tables(l_):
    pos = np.arange(l_)
    nf = HEAD_DIM // 4
    inv = ROPE_BASE ** (-np.arange(nf, dtype=np.float64) / nf)
    ar = (pos // GRID_W)[:, None] * inv
    ac = (pos % GRID_W)[:, None] * inv
    cos = np.concatenate([np.cos(ar), np.cos(ar), np.cos(ac), np.cos(ac)], axis=1)
    sin = np.concatenate([-np.sin(ar), np.sin(ar), -np.sin(ac), np.sin(ac)], axis=1)
    cos = np.concatenate([np.tile(cos, (1, 2)), np.ones((TM, LANES))], axis=0)
    sin = np.concatenate([np.tile(sin, (1, 2)), np.zeros((TM, LANES))], axis=0)
    return cos.astype(np.float32), sin.astype(np.float32)


def _na_bias(rpb):
    u = np.arange(GRID_W)[:, None]
    v = np.arange(GRID_W)[None, :]
    cs = np.clip(u - NA_WIN_COLS // 2, 0, GRID_W - NA_WIN_COLS)
    colmask = (v >= cs) & (v < cs + NA_WIN_COLS)
    coff = np.clip(v - u + NA_WIN_COLS - 1, 0, 2 * NA_WIN_COLS - 2)
    n_off = 2 * NA_WIN_COLS - 1
    pick = (coff[None] == np.arange(n_off)[:, None, None]).astype(np.float32)
    band = jnp.einsum('lhab,buv->lahuv', rpb.astype(F32), pick, precision=lax.Precision.HIGHEST)
    band = jnp.where(colmask[None, None, None], band, NEG)
    band = band.reshape(rpb.shape[0], 2 * NA_WIN_ROWS - 1, NA_HEADS * GRID_W, GRID_W)
    return jnp.concatenate([band[:, :-1], band[:, 1:]], axis=-1)


def kernel(x, c, ctx, c_ctx, w_ada, b_ada, w_in, lb_logits, hgrn_norm, na_rpb, swa_sink, w_out,
           ln1_g, ln1_b, ln2_g, ln2_b, w_router, router_bias, w_gate, w_up, w_down):
    n_b, l_, d = x.shape
    ctx_len = ctx.shape[1]
    depth = w_ada.shape[0]
    assert n_b * ctx_len == TM and l_ % TM == 0 and n_b + 1 <= N_COND
    n_lat_tiles = (n_b * l_) // TM

    p_lb = jax.nn.softmax(lb_logits.astype(F32), axis=0).reshape(depth, 2 * HGRN_W)
    upto = (np.arange(depth)[None, :] <= np.arange(depth)[:, None]).astype(np.float32)
    lb = jnp.maximum(jnp.sum(upto[:, :, None] * p_lb[None], axis=1) - p_lb[0:1], LB_MIN)
    gp = jnp.stack([jnp.log(lb), jnp.log1p(-lb), 1.0 - lb] + [jnp.zeros_like(lb)] * 5, axis=1)

    cond = jnp.concatenate([c, c_ctx[None, :], jnp.zeros((N_COND - n_b - 1, d), F32)], axis=0)
    mod = _ada(cond.T, w_ada, b_ada, n_b + 1).reshape(depth, N_COND, 6, d)

    cos_t, sin_t = _rope_tables(l_)
    wrh = w_router.T.astype(BF16)
    rb = router_bias.astype(F32)[:, None]
    w_in_b = w_in.astype(BF16)
    w_out_b = w_out.astype(BF16)
    zeros = jnp.zeros_like(ln1_g)
    pv = jnp.stack([ln1_g, ln1_b, jnp.tile(hgrn_norm, (1, d // HEAD_DIM)), ln2_g, ln2_b, zeros, zeros, zeros],
                   axis=1).astype(F32)
    na_bias = _na_bias(na_rpb)
    sink = swa_sink.astype(F32)

    xa, xb, ctx_blk = x.reshape(n_b * l_, d), ctx.reshape(n_b * ctx_len, d), 0
    for l in range(depth):
        last = l == depth - 1
        hq, hv, hg, lf, kk, nqkv, sq, kx, vx = _inproj(xa, xb, ctx_blk, mod, w_in_b, gp, l, cos_t, sin_t, n_b, l_)
        o_f, o_b = _hgrn(hq, hv, lf, kk, n_b, l_, ctx_len)
        yb = _na(nqkv, na_bias, l, n_b, l_, ctx_len, not last)
        yc = _swa(sq, kx, vx, sink, l, n_b, l_, ctx_len, not last)
        n_tiles = n_lat_tiles if last else n_lat_tiles + 1
        x1, h2u, route, gt, counts = _outproj(o_f, o_b, hg, yb, yc, xa, xb, ctx_blk, mod, w_out_b, pv, l,
                                              wrh, rb, n_b, l_, n_tiles)
        xa = _moe_routed(h2u, route, gt, counts, w_gate, w_up, w_down, l, x1, mod, pv, n_b, l_, n_tiles)
        xb, ctx_blk = xa, n_lat_tiles
    return xa[:n_b * l_].reshape(n_b, l_, d)
```

```python
import functools

import numpy as np
import jax
import jax.numpy as jnp
from jax import lax
from jax.experimental import pallas as pl
from jax.experimental.pallas import tpu as pltpu
from jax.experimental.pallas import tpu_sc as plsc

F32 = jnp.float32
BF16 = jnp.bfloat16

D_MODEL = 1024
GRID_W = 64
HEAD_DIM = 64
HGRN_W = 256
NA_HEADS = 4
NA_WIN_ROWS = 8
NA_WIN_COLS = 16
SWA_Q_HEADS = 8
SWA_KV_HEADS = 2
SWA_GROUP = 4
SWA_WINDOW = 128
SWA_BLOCK = 128
ROPE_BASE = 10000.0
N_EXPERTS = 16
N_GROUPS = 4
D_EXPERT = 512
LN_EPS = 1e-5
RMS_EPS = 1e-6
NEG = -1e30
LB_MIN = 1e-6
DEPTH = 2
ALPHA = (2.0 * DEPTH) ** 0.25

LANES = 128
MXU_N = 256
TM = 512
HB = 256
HC = 16
NA_QROWS = 4
SWA_STEP_BLOCKS = 2
VMEM_LIMIT = 56 * 1024 * 1024

SC_MAX_ROWS = 192
SC_ROW_QUANTUM = 2048

N_COND = 8


def _dot(a, b):
    return jnp.dot(a, b, preferred_element_type=F32)


def _dot_nt(a, b):
    return lax.dot_general(a, b, (((1,), (1,)), ((), ())), preferred_element_type=F32)


def _dot_tn(a, b):
    return lax.dot_general(a, b, (((0,), (0,)), ((), ())), preferred_element_type=F32)


def _sigmoid(x):
    return 1.0 / (1.0 + jnp.exp(-x))


def _split_bf16(x):
    hi = x.astype(BF16)
    lo = (x - hi.astype(F32)).astype(BF16)
    return hi, lo


def _params(sem):
    return pltpu.CompilerParams(dimension_semantics=sem, vmem_limit_bytes=VMEM_LIMIT)


def _ada_kernel(n_rows, condt_ref, w_ref, b_ref, o_ref):
    c = condt_ref[...]
    s = c * _sigmoid(c)
    w = w_ref[0]
    rows = [jnp.sum(w * s[:, r:r + 1], axis=0, keepdims=True) for r in range(n_rows)]
    rows.append(jnp.zeros((N_COND - n_rows, w.shape[1]), F32))
    o_ref[0] = jnp.concatenate(rows, axis=0) + b_ref[0]


def _ada(cond_t, w_ada, b_ada, n_rows):
    depth, d, n6 = w_ada.shape
    tn = 1024
    return pl.pallas_call(
        functools.partial(_ada_kernel, n_rows),
        grid=(depth, n6 // tn),
        in_specs=[pl.BlockSpec((d, N_COND), lambda l, n: (0, 0)),
                  pl.BlockSpec((1, d, tn), lambda l, n: (l, 0, n)),
                  pl.BlockSpec((1, 1, tn), lambda l, n: (l, 0, n))],
        out_specs=pl.BlockSpec((1, N_COND, tn), lambda l, n: (l, 0, n)),
        out_shape=jax.ShapeDtypeStruct((depth, N_COND, n6), F32),
        compiler_params=_params(("arbitrary", "arbitrary")),
        name="ada",
    )(cond_t, w_ada, b_ada.reshape(depth, 1, n6))


def _rope(z, cos, sin, first):
    sw = jnp.where(first, pltpu.roll(z, LANES - 16, axis=1), pltpu.roll(z, 16, axis=1))
    return z * cos + sw * sin


def _inproj_kernel(n_lat, xa_ref, xb_ref, mod_ref, w_ref, gp_ref, cos_ref, sin_ref,
                   hq_ref, hv_ref, hg_ref, lf_ref, kk_ref, nqkv_ref, sq_ref, kx_ref, vx_ref, wb_ref):
    @pl.when(pl.program_id(0) == 0)
    def _():
        wb_ref[...] = w_ref[0].astype(BF16)

    mod = mod_ref[0, 0]
    x = jnp.where(pl.program_id(0) < n_lat, xa_ref[...], xb_ref[...])
    h = (x * (1.0 + mod[1:2]) + mod[0:1]).astype(BF16)
    gp = gp_ref[0]
    cos = cos_ref[...]
    sin = sin_ref[...]
    lane = lax.broadcasted_iota(jnp.int32, cos.shape, 1)
    first = (lane & 16) == 0
    scale = HEAD_DIM ** -0.5

    def chunk(c):
        return _dot(h, wb_ref[:, c * MXU_N:(c + 1) * MXU_N])

    def per_query_head(z2):
        swapped = pltpu.roll(z2, HEAD_DIM, axis=1)
        low = lane < HEAD_DIM
        h0 = jnp.where(low, z2, swapped)
        h1 = jnp.where(low, swapped, z2)
        return jnp.concatenate([h0, h0], axis=1), jnp.concatenate([h1, h1], axis=1)

    def gates(z, d):
        cols = slice(d * MXU_N, (d + 1) * MXU_N)
        a = gp[0:1, cols]
        b = gp[1:2, cols] + (jnp.minimum(z, 0.0) - jnp.log1p(jnp.exp(-jnp.abs(z))))
        logf = jnp.maximum(a, b) + jnp.log1p(jnp.exp(-jnp.abs(a - b)))
        k = gp[2:3, cols] * (1.0 / (1.0 + jnp.exp(z)))
        return logf, k

    def rope2(z):
        return jnp.concatenate([_rope(z[:, 0:LANES], cos, sin, first),
                                _rope(z[:, LANES:2 * LANES], cos, sin, first)], axis=1)

    for d in range(2):
        logf, k = gates(chunk(1 + d), d)
        lf_ref[:, d * MXU_N:(d + 1) * MXU_N] = logf
        kk_ref[:, d * MXU_N:(d + 1) * MXU_N] = k.astype(BF16)
    for d in range(2):
        sq_ref[:, d * MXU_N:(d + 1) * MXU_N] = (rope2(chunk(8 + d)) * scale).astype(BF16)
    zkv = chunk(10)
    k0, k1 = per_query_head(_rope(zkv[:, 0:LANES], cos, sin, first))
    v0, v1 = per_query_head(zkv[:, LANES:2 * LANES])
    kx_ref[:, 0:MXU_N] = k0.astype(BF16)
    kx_ref[:, MXU_N:2 * MXU_N] = k1.astype(BF16)
    vx_ref[:, 0:MXU_N] = v0.astype(BF16)
    vx_ref[:, MXU_N:2 * MXU_N] = v1.astype(BF16)
    hq_ref[...] = chunk(0).astype(BF16)
    hv_ref[...] = chunk(3).astype(BF16)
    hg_ref[...] = chunk(4).astype(BF16)
    nqkv_ref[:, 0:MXU_N] = (chunk(5) * scale).astype(BF16)
    nqkv_ref[:, MXU_N:2 * MXU_N] = chunk(6).astype(BF16)
    nqkv_ref[:, 2 * MXU_N:3 * MXU_N] = chunk(7).astype(BF16)


def _inproj(xa, xb, ctx_blk, mod, w, gp, layer, cos_t, sin_t, n_b, l_):
    d = xa.shape[1]
    tpb = l_ // TM
    n_lat = n_b * tpb
    nt = (n_lat + 1) * TM

    def mod_idx(j):
        return (layer, jnp.minimum(j // tpb, n_b), 0, 0)

    def rope_idx(j):
        return (jnp.where(j < n_lat, j % tpb, tpb), 0)

    def rows(width):
        return pl.BlockSpec((TM, width), lambda j: (j, 0))

    widths = [(HGRN_W, BF16), (HGRN_W, BF16), (HGRN_W, BF16), (2 * HGRN_W, F32), (2 * HGRN_W, BF16),
              (3 * 256, BF16), (512, BF16), (512, BF16), (512, BF16)]
    return pl.pallas_call(
        functools.partial(_inproj_kernel, n_lat),
        grid=(nt // TM,),
        in_specs=[pl.BlockSpec((TM, d), lambda j: (jnp.minimum(j, n_lat - 1), 0)),
                  pl.BlockSpec((TM, d), lambda j: (ctx_blk, 0)),
                  pl.BlockSpec((1, 1, 6, d), mod_idx),
                  pl.BlockSpec((1,) + w.shape[1:], lambda j: (layer, 0, 0)),
                  pl.BlockSpec((1,) + gp.shape[1:], lambda j: (layer, 0, 0)),
                  pl.BlockSpec((TM, LANES), rope_idx),
                  pl.BlockSpec((TM, LANES), rope_idx)],
        out_specs=[rows(wd) for wd, _ in widths],
        out_shape=[jax.ShapeDtypeStruct((nt, wd), dt) for wd, dt in widths],
        scratch_shapes=[pltpu.VMEM(w.shape[1:], BF16)],
        compiler_params=_params(("arbitrary",)),
        name="inproj",
    )(xa, xb, mod, w, gp, cos_t, sin_t)


def _hgrn_consts():
    t = np.arange(HB)[:, None]
    u = np.arange(HB)[None, :]
    same = (t // HC) == (u // HC)
    tr, ur = t % HC, u % HC
    half = HC // 2
    mats, codes = [], []
    for fwd in (True, False):
        if fwd:
            incl = ur <= tr
            mid = ur <= (tr // half) * half + half // 2 - 1
            edge = ur <= half - 1
            code = np.where((t // half == u // half) & (u <= t), 1, np.where(same & (tr >= half) & (ur < half), 2, 0))
        else:
            incl = ur >= tr
            mid = ur >= (tr // half) * half + half // 2
            edge = ur >= half
            code = np.where((t // half == u // half) & (u >= t), 1, np.where(same & (tr < half) & (ur >= half), 2, 0))
        mats.append(np.concatenate([same & incl, same, same & mid, same & edge], axis=0))
        codes.append(code)
    return (np.stack(mats).astype(np.float32).astype(jnp.bfloat16), np.stack(codes).astype(np.float32))


def _hgrn_kernel(qf_ref, vf_ref, lff_ref, kf_ref, qb_ref, vb_ref, lfb_ref, kb_ref, cm_ref, code_ref,
                 of_ref, ob_ref, st_ref, qd_ref, kd_ref, vv_ref, gt_ref, it_ref):
    nch = HB // HC
    n_heads = HGRN_W // HEAD_DIM

    @pl.when(pl.program_id(1) == 0)
    def _():
        st_ref[...] = jnp.zeros(st_ref.shape, F32)

    lane3 = lax.broadcasted_iota(jnp.int32, (nch, HC, HGRN_W), 2) // HEAD_DIM

    def head_rows(x):
        x3 = x.astype(BF16).reshape(nch, HC, HGRN_W)
        zero = jnp.zeros_like(x3)
        return jnp.concatenate([jnp.where(lane3 == h, x3, zero) for h in range(n_heads)], axis=1)

    def prep(d, q_ref, v_ref, lf_ref, k_ref):
        hi, lo = _split_bf16(lf_ref[...])
        cm = cm_ref[d]
        acc = _dot(jnp.concatenate([cm[0:2 * HB], cm[0:2 * HB]], axis=1), jnp.concatenate([hi, lo], axis=0))
        ref = _dot(cm[2 * HB:4 * HB], hi)
        cum, tot = acc[0:HB], acc[HB:2 * HB]
        mid, edge = ref[0:HB], ref[HB:2 * HB]
        q = q_ref[...].astype(F32)
        k = k_ref[...].astype(F32)
        v = v_ref[...]
        qd_ref[d] = head_rows(q * jnp.exp(cum))
        kd_ref[d] = head_rows(k * jnp.exp(tot - cum))
        vv_ref[d] = jnp.concatenate([v[:, h * HEAD_DIM:(h + 1) * HEAD_DIM].reshape(nch, HC, HEAD_DIM)
                                     for h in range(n_heads)], axis=1)
        gt_ref[d] = jnp.exp(tot)
        q1 = (q * jnp.exp(cum - mid)).astype(BF16)
        k1 = (k * jnp.exp(mid - cum)).astype(BF16)
        upper = (lax.broadcasted_iota(jnp.int32, q.shape, 0) % HC >= HC // 2) == (d == 0)
        q2 = jnp.where(upper, q * jnp.exp(cum - edge), 0.0).astype(BF16)
        k2 = jnp.where(upper, 0.0, k * jnp.exp(edge - cum)).astype(BF16)
        s1 = _dot_nt(_head_stack(q1, n_heads), k1)
        s2 = _dot_nt(_head_stack(q2, n_heads), k2)
        code = jnp.concatenate([code_ref[d]] * n_heads, axis=0)
        p = jnp.where(code == 1.0, s1, jnp.where(code == 2.0, s2, 0.0))
        return _head_unstack(_dot(p.astype(BF16), v), n_heads)

    od_f = prep(0, qf_ref, vf_ref, lff_ref, kf_ref)
    od_b = prep(1, qb_ref, vb_ref, lfb_ref, kb_ref)

    for i in range(nch):
        for d, c in ((0, i), (1, nch - 1 - i)):
            st = st_ref[d]
            it_ref[d, c] = _dot_nt(qd_ref[d, c], st.astype(BF16))
            upd = _dot_tn(vv_ref[d, c], kd_ref[d, c])
            st_ref[d] = st * gt_ref[d, c * HC:c * HC + 1, :] + upd

    def inter(d):
        it = it_ref[d]
        return jnp.concatenate([it[:, h * HC:(h + 1) * HC, :].reshape(HB, HEAD_DIM) for h in range(n_heads)],
                               axis=1)

    of_ref[...] = inter(0) + od_f
    ob_ref[...] = inter(1) + od_b


def _hgrn(hq, hv, lf, kk, n_b, l_, ctx_len):
    nt = hq.shape[0]
    assert ctx_len == HB
    nlb = l_ // HB
    ctx0 = n_b * nlb
    cmats, codes = _hgrn_consts()
    heads = HGRN_W // HEAD_DIM

    def fwd_idx(col):
        return lambda b, s: (jnp.where(s == 0, ctx0 + b, b * nlb + s - 1), col)

    def bwd_idx(col):
        return lambda b, s: (jnp.where(s == 0, ctx0 + b, b * nlb + nlb - s), col)

    def blk(idx):
        return pl.BlockSpec((HB, HGRN_W), idx)

    full = lambda a: pl.BlockSpec(a.shape, lambda b, s: (0,) * a.ndim)
    return pl.pallas_call(
        _hgrn_kernel,
        grid=(n_b, nlb + 1),
        in_specs=[blk(fwd_idx(0)), blk(fwd_idx(0)), blk(fwd_idx(0)), blk(fwd_idx(0)),
                  blk(bwd_idx(0)), blk(bwd_idx(0)), blk(bwd_idx(1)), blk(bwd_idx(1)),
                  full(cmats), full(codes)],
        out_specs=[blk(fwd_idx(0)), blk(bwd_idx(0))],
        out_shape=[jax.ShapeDtypeStruct((nt, HGRN_W), F32)] * 2,
        scratch_shapes=[pltpu.VMEM((2, HEAD_DIM, HGRN_W), F32),
                        pltpu.VMEM((2, HB // HC, heads * HC, HGRN_W), BF16),
                        pltpu.VMEM((2, HB // HC, heads * HC, HGRN_W), BF16),
                        pltpu.VMEM((2, HB // HC, heads * HC, HEAD_DIM), BF16),
                        pltpu.VMEM((2, HB, HGRN_W), F32),
                        pltpu.VMEM((2, HB // HC, heads * HC, HEAD_DIM), F32)],
        compiler_params=_params(("arbitrary", "arbitrary")),
        name="hgrn",
    )(hq, hv, lf, kk, hq, hv, lf, kk, cmats, codes)


def _head_stack(q, n_heads):
    lane = lax.broadcasted_iota(jnp.int32, q.shape, 1)
    zero = jnp.zeros_like(q)
    return jnp.concatenate([jnp.where(lane // HEAD_DIM == h, q, zero) for h in range(n_heads)], axis=0)


def _head_unstack(o, n_heads):
    rows = o.shape[0] // n_heads
    lane = lax.broadcasted_iota(jnp.int32, (rows, o.shape[1]), 1)
    acc = jnp.zeros((rows, o.shape[1]), F32)
    for h in range(n_heads):
        acc = acc + jnp.where(lane // HEAD_DIM == h, o[h * rows:(h + 1) * rows], 0.0)
    return acc


def _na_kernel(n_rows, n_lat_steps, qrows, q_ref, k_ref, v_ref, kc_ref, vc_ref, bias_ref, o_ref):
    j = pl.program_id(1)
    kc = kc_ref[...]
    vc = vc_ref[...]
    nwin = NA_WIN_ROWS * GRID_W

    def attend(rr, win):
        q4 = _head_stack(q_ref[rr * GRID_W:(rr + 1) * GRID_W, :], NA_HEADS)
        s_ctx = _dot_nt(q4, kc)
        m = jnp.max(s_ctx, axis=1, keepdims=True)
        if win is not None:
            kw, vw, bias = win
            s_win = _dot_nt(q4, kw) + bias
            m = jnp.maximum(m, jnp.max(s_win, axis=1, keepdims=True))
            p_win = jnp.exp(s_win - m)
        p_ctx = jnp.exp(s_ctx - m)
        den = jnp.sum(p_ctx, axis=1, keepdims=True)
        o4 = _dot(p_ctx.astype(BF16), vc)
        if win is not None:
            den = den + jnp.sum(p_win, axis=1, keepdims=True)
            o4 = o4 + _dot(p_win.astype(BF16), vw)
        o = _head_unstack(o4 * (1.0 / den), NA_HEADS)
        o_ref[rr * GRID_W:(rr + 1) * GRID_W, :] = o.astype(BF16)

    @pl.when(j < n_lat_steps)
    def _():
        for rr in range(qrows):
            r = j * qrows + rr
            rs = jnp.clip(r - NA_WIN_ROWS // 2, 0, n_rows - NA_WIN_ROWS)
            start = pl.multiple_of(rs * GRID_W, GRID_W)
            kw = k_ref[pl.ds(start, nwin), :]
            vw = v_ref[pl.ds(start, nwin), :]
            first = NA_WIN_ROWS - 1 - (r - rs)
            bias = jnp.concatenate([bias_ref[0, first + 2 * p] for p in range(NA_WIN_ROWS // 2)], axis=1)
            attend(rr, (kw, vw, bias))

    @pl.when(j >= n_lat_steps)
    def _():
        for rr in range(qrows):
            attend(rr, None)


def _na(nqkv, bias, layer, n_b, l_, ctx_len, with_ctx):
    nt = nqkv.shape[0]
    qrows = NA_QROWS if with_ctx else 2 * NA_QROWS
    qb = qrows * GRID_W
    assert ctx_len == qb or not with_ctx
    n_lat = l_ // qb
    ctx0 = n_b * n_lat
    n_rows = l_ // GRID_W

    def q_idx(b, j):
        return (jnp.where(j < n_lat, b * n_lat + j, ctx0 + b), 0)

    cblk = (n_b * l_) // ctx_len
    return pl.pallas_call(
        functools.partial(_na_kernel, n_rows, n_lat, qrows),
        grid=(n_b, n_lat + (1 if with_ctx else 0)),
        in_specs=[pl.BlockSpec((qb, 256), q_idx),
                  pl.BlockSpec((l_, 256), lambda b, j: (b, 1)),
                  pl.BlockSpec((l_, 256), lambda b, j: (b, 2)),
                  pl.BlockSpec((ctx_len, 256), lambda b, j: (cblk + b, 1)),
                  pl.BlockSpec((ctx_len, 256), lambda b, j: (cblk + b, 2)),
                  pl.BlockSpec((1,) + bias.shape[1:], lambda b, j: (layer, 0, 0, 0))],
        out_specs=pl.BlockSpec((qb, 256), q_idx),
        out_shape=jax.ShapeDtypeStruct((nt if with_ctx else n_b * l_, 256), BF16),
        compiler_params=_params(("arbitrary", "arbitrary")),
        name="natten",
    )(nqkv, nqkv, nqkv, nqkv, nqkv, bias)


def _swa_band_bias(n_blocks):
    sb = SWA_BLOCK
    u = np.arange(sb)[:, None]
    v = np.arange(3 * sb)[None, :]
    tabs = []
    for j in (0, 1, n_blocks - 1):
        kpos = (j - 1) * sb + v
        ok = (kpos >= 0) & (kpos < n_blocks * sb) & (np.abs(j * sb + u - kpos) <= SWA_WINDOW)
        tabs.append(np.tile(np.where(ok, 0.0, NEG), (SWA_GROUP, 1)))
    return np.stack(tabs).astype(np.float32)


def _swa_kernel(n_steps, layer, sink_ref, q_ref, kp_ref, kq_ref, kn_ref, vp_ref, vq_ref, vn_ref,
                kc_ref, vc_ref, bias_ref, o_ref):
    j = pl.program_id(1)
    sb = SWA_BLOCK

    def attend(a, n, band):
        cols = slice(n * MXU_N, (n + 1) * MXU_N)
        rows = slice(a * sb, (a + 1) * sb)
        q4 = _head_stack(q_ref[rows, cols], SWA_GROUP)
        sink = jnp.concatenate([jnp.full((sb, 1), sink_ref[layer, n * SWA_GROUP + g], F32)
                                for g in range(SWA_GROUP)], axis=0)
        if band is None:
            keys, vals = kc_ref[:, cols], vc_ref[:, cols]
            s = _dot_nt(q4, keys)
        else:
            k_refs, v_refs, bias = band
            keys = jnp.concatenate([r[rs, cols] for r, rs in k_refs] + [kc_ref[:, cols]], axis=0)
            vals = jnp.concatenate([r[rs, cols] for r, rs in v_refs] + [vc_ref[:, cols]], axis=0)
            s = _dot_nt(q4, keys)
            s = jnp.concatenate([s[:, 0:3 * sb] + bias, s[:, 3 * sb:]], axis=1)
        m = jnp.maximum(jnp.max(s, axis=1, keepdims=True), sink)
        p = jnp.exp(s - m)
        den = jnp.sum(p, axis=1, keepdims=True) + jnp.exp(sink - m)
        o4 = _dot(p.astype(BF16), vals)
        o_ref[rows, cols] = _head_unstack(o4 * (1.0 / den), SWA_GROUP).astype(BF16)

    lo, hi, whole = slice(0, sb), slice(sb, 2 * sb), slice(0, sb)

    @pl.when(j < n_steps)
    def _():
        bands = [([(kp_ref, whole), (kq_ref, lo), (kq_ref, hi)], [(vp_ref, whole), (vq_ref, lo), (vq_ref, hi)],
                  bias_ref[jnp.where(j == 0, 0, 1)]),
                 ([(kq_ref, lo), (kq_ref, hi), (kn_ref, whole)], [(vq_ref, lo), (vq_ref, hi), (vn_ref, whole)],
                  bias_ref[jnp.where(j == n_steps - 1, 2, 1)])]
        for a in range(SWA_STEP_BLOCKS):
            for n in range(SWA_KV_HEADS):
                attend(a, n, bands[a])

    @pl.when(j >= n_steps)
    def _():
        for a in range(SWA_STEP_BLOCKS):
            for n in range(SWA_KV_HEADS):
                attend(a, n, None)


def _swa(sq, kx, vx, sink, layer, n_b, l_, ctx_len, with_ctx):
    nt = sq.shape[0]
    sb = SWA_BLOCK
    qb = SWA_STEP_BLOCKS * sb
    nb = l_ // sb
    ns = l_ // qb
    assert SWA_STEP_BLOCKS == 2 and nb % 2 == 0 and nb >= 4 and ctx_len == qb
    ctx0 = (n_b * l_) // qb
    bias = _swa_band_bias(nb)

    def q_idx(b, j):
        return (jnp.where(j < ns, b * ns + j, ctx0 + b), 0)

    def cur_idx(b, j):
        return (b * ns + jnp.minimum(j, ns - 1), 0)

    def edge_idx(off):
        return lambda b, j: (b * nb + jnp.clip(SWA_STEP_BLOCKS * jnp.minimum(j, ns - 1) + off, 0, nb - 1), 0)

    edge = lambda idx: pl.BlockSpec((sb, 512), idx)
    wide = lambda idx: pl.BlockSpec((qb, 512), idx)
    cspec = pl.BlockSpec((ctx_len, 512), lambda b, j: (ctx0 + b, 0))
    return pl.pallas_call(
        functools.partial(_swa_kernel, ns, layer),
        grid=(n_b, ns + (1 if with_ctx else 0)),
        in_specs=[pl.BlockSpec(memory_space=pltpu.SMEM),
                  wide(q_idx), edge(edge_idx(-1)), wide(cur_idx), edge(edge_idx(SWA_STEP_BLOCKS)),
                  edge(edge_idx(-1)), wide(cur_idx), edge(edge_idx(SWA_STEP_BLOCKS)), cspec, cspec,
                  pl.BlockSpec(bias.shape, lambda b, j: (0, 0, 0))],
        out_specs=wide(q_idx),
        out_shape=jax.ShapeDtypeStruct((nt if with_ctx else n_b * l_, 512), BF16),
        compiler_params=_params(("arbitrary", "arbitrary")),
        name="swa",
    )(sink, sq, kx, kx, kx, vx, vx, vx, kx, vx, bias)


def _layer_norm(r, g, b):
    mu = jnp.mean(r, axis=-1, keepdims=True)
    rc = r - mu
    var = jnp.mean(rc * rc, axis=-1, keepdims=True)
    return rc * lax.rsqrt(var + LN_EPS) * g + b


def _route(logits_t, bias_col):
    epg = N_EXPERTS // N_GROUPS
    s = _sigmoid(logits_t)
    sel = s + bias_col
    sel_r = [sel[e:e + 1, :] for e in range(N_EXPERTS)]
    s_r = [s[e:e + 1, :] for e in range(N_EXPERTS)]
    grp = []
    for g in range(N_GROUPS):
        a = sel_r[g * epg:(g + 1) * epg]
        m1 = functools.reduce(jnp.maximum, a)
        m2 = functools.reduce(jnp.maximum,
                              [jnp.minimum(a[i], a[k]) for i in range(epg) for k in range(i + 1, epg)])
        grp.append(m1 + m2)
    m_r, w_r = [], []
    for g in range(N_GROUPS):
        best = None
        for k in range(N_GROUPS):
            if k == g:
                continue
            c = (grp[g] > grp[k]) if k < g else (grp[g] >= grp[k])
            best = c if best is None else (best & c)
        for e in range(g * epg, (g + 1) * epg):
            rank = jnp.zeros_like(sel_r[e])
            for k in range(g * epg, (g + 1) * epg):
                if k == e:
                    continue
                ahead = (sel_r[k] >= sel_r[e]) if k < e else (sel_r[k] > sel_r[e])
                rank = rank + jnp.where(ahead, 1.0, 0.0)
            chosen = jnp.where(best & (rank < 1.5), 1.0, 0.0)
            m_r.append(chosen)
            w_r.append(chosen * s_r[e])
    inv = 1.0 / functools.reduce(lambda a, b: a + b, w_r)
    return m_r, [w * inv for w in w_r]


def _pack_bf16_pairs(a, b):
    ua = pltpu.bitcast(a.astype(BF16).astype(F32), jnp.uint32)
    ub = pltpu.bitcast(b.astype(BF16).astype(F32), jnp.uint32)
    return (ua >> 16) | ub


def _unpack_bf16_pairs(u):
    return (pltpu.bitcast(u << 16, F32), pltpu.bitcast(u & jnp.uint32(0xFFFF0000), F32))


def _outproj_kernel(n_lat, of_ref, ob_ref, hg_ref, yb_ref, yc_ref, xa_ref, xb_ref, mod_ref, wo_ref, pv_ref,
                    wrh_ref, rb_ref, before_ref, x1_ref, h2u_ref, route_ref, gt_ref, cnt_ref, wob_ref):
    @pl.when(pl.program_id(0) == 0)
    def _():
        cnt_ref[...] = jnp.zeros(cnt_ref.shape, F32)
        wob_ref[...] = wo_ref[0].astype(BF16)

    row = lax.broadcasted_iota(jnp.int32, (HGRN_W, HGRN_W), 0)
    col = lax.broadcasted_iota(jnp.int32, (HGRN_W, HGRN_W), 1)
    head_ones = jnp.where((row // HEAD_DIM) == (col // HEAD_DIM), 1.0, 0.0).astype(BF16)
    pv = pv_ref[0]
    o = of_ref[...] + ob_ref[...]
    hi, lo = _split_bf16(o * o)
    ms = _dot(jnp.concatenate([hi, lo], axis=1), jnp.concatenate([head_ones, head_ones], axis=0)) * (1.0 / HEAD_DIM)
    gate = hg_ref[...].astype(F32)
    ya = o * lax.rsqrt(ms + RMS_EPS) * pv[2:3, 0:HGRN_W] * (gate * _sigmoid(gate))
    y = _dot(jnp.concatenate([ya.astype(BF16), yb_ref[...], yc_ref[...]], axis=1), wob_ref[...])
    mod = mod_ref[0, 0]
    x = jnp.where(pl.program_id(0) < n_lat, xa_ref[...], xb_ref[...])
    x1 = _layer_norm(ALPHA * x + mod[2:3] * y, pv[0:1], pv[1:2])
    x1_ref[...] = x1
    h2 = x1 * (1.0 + mod[4:5]) + mod[3:4]
    hh, hl = _split_bf16(h2)
    half = h2.shape[1] // 2
    h2u_ref[...] = _pack_bf16_pairs(h2[:, 0:half], h2[:, half:2 * half])
    wr = wrh_ref[...]
    logits_t = _dot_nt(jnp.concatenate([wr, wr], axis=1), jnp.concatenate([hh, hl], axis=1))
    m_r, w_r = _route(logits_t, rb_ref[...])
    m_t = jnp.concatenate(m_r, axis=0)
    rank_t = _dot(m_t.astype(BF16), before_ref[...]) + cnt_ref[:, 0:1]
    cnt_ref[...] = cnt_ref[...] + jnp.sum(m_t, axis=1, keepdims=True)
    seen = jnp.zeros_like(m_r[0])
    e1 = e2 = r1 = r2 = g1 = g2 = jnp.zeros_like(m_r[0])
    for e in range(N_EXPERTS):
        is1 = m_r[e] * (1.0 - seen)
        is2 = m_r[e] * seen
        rk = rank_t[e:e + 1, :]
        e1, e2 = e1 + e * is1, e2 + e * is2
        r1, r2 = r1 + rk * is1, r2 + rk * is2
        g1, g2 = g1 + w_r[e] * is1, g2 + w_r[e] * is2
        seen = seen + is1
    zero = jnp.zeros_like(e1)
    route_ref[...] = jnp.concatenate([e1, e2, r1, r2, g1, g2, zero, zero], axis=0)
    pad = jnp.zeros((LANES - 2, g1.shape[1]), F32)
    gt_ref[...] = jnp.concatenate([g1, g2, pad], axis=0).T


def _outproj(o_f, o_b, hg, yb, yc, xa, xb, ctx_blk, mod, wo, pv, layer, wrh, rb, n_b, l_, n_tiles):
    d = xa.shape[1]
    tpb = l_ // TM
    n_lat = n_b * tpb
    before = np.triu(np.ones((TM, TM), np.float32), 1).astype(jnp.bfloat16)

    def mod_idx(j):
        return (layer, jnp.minimum(j // tpb, n_b), 0, 0)

    def rows(width):
        return pl.BlockSpec((TM, width), lambda j: (j, 0))

    full = lambda a: pl.BlockSpec(a.shape, lambda j: (0,) * a.ndim)
    of_layer = lambda a: pl.BlockSpec((1,) + a.shape[1:], lambda j: (layer,) + (0,) * (a.ndim - 1))
    nr = n_tiles * TM
    return pl.pallas_call(
        functools.partial(_outproj_kernel, n_lat),
        grid=(n_tiles,),
        in_specs=[rows(256), rows(256), rows(256), rows(256), rows(512),
                  pl.BlockSpec((TM, d), lambda j: (jnp.minimum(j, n_lat - 1), 0)),
                  pl.BlockSpec((TM, d), lambda j: (ctx_blk, 0)),
                  pl.BlockSpec((1, 1, 6, d), mod_idx), of_layer(wo), of_layer(pv), full(wrh), full(rb),
                  full(before)],
        out_specs=[rows(d), rows(d // 2), pl.BlockSpec((8, TM), lambda j: (0, j)), rows(LANES),
                   pl.BlockSpec((N_EXPERTS, LANES), lambda j: (0, 0))],
        out_shape=[jax.ShapeDtypeStruct((nr, d), F32), jax.ShapeDtypeStruct((nr, d // 2), jnp.uint32),
                   jax.ShapeDtypeStruct((8, nr), F32), jax.ShapeDtypeStruct((nr, LANES), F32),
                   jax.ShapeDtypeStruct((N_EXPERTS, LANES), F32)],
        scratch_shapes=[pltpu.VMEM(wo.shape[1:], BF16)],
        compiler_params=_params(("arbitrary",)),
        name="outproj",
    )(o_f, o_b, hg, yb, yc, xa, xb, mod, wo, pv, wrh, rb, before)


def _sc_split(rows_per_worker):
    best = None
    for chunk in (64, 48, 32):
        for nbuf in range(SC_MAX_ROWS // chunk, 0, -1):
            if rows_per_worker % (chunk * nbuf) == 0 and (best is None or chunk * nbuf > best[0] * best[1]):
                best = (chunk, nbuf)
    return best


def _sc_workers():
    info = plsc.get_sparse_core_info()
    return info.num_cores, info.num_cores * info.num_subcores


def _sc_scatter_rows(table, pos, n_slots):
    n_tok, width = table.shape
    n_cores, n_workers = _sc_workers()
    per_w = n_tok // n_workers
    assert n_tok % n_workers == 0
    chunk, nbuf = _sc_split(per_w)
    n_chunks = per_w // chunk
    mesh = plsc.VectorSubcoreMesh(core_axis_name="c", subcore_axis_name="s")

    @functools.partial(
        pl.kernel, mesh=mesh,
        out_type=jax.ShapeDtypeStruct((n_slots, width), table.dtype),
        scratch_types=[pltpu.VMEM((2 * n_chunks, chunk), jnp.int32),
                       pltpu.VMEM((nbuf, chunk, width), table.dtype),
                       pltpu.SemaphoreType.DMA((nbuf,)),
                       pltpu.SemaphoreType.DMA((nbuf,))],
    )
    def scatter(table_hbm, pos_hbm, out_hbm, idx_v, rows_v, rsem, wsem):
        wid = lax.axis_index("s") * n_cores + lax.axis_index("c")
        base = wid * per_w
        pltpu.sync_copy(pos_hbm.at[wid], idx_v)

        @pl.loop(0, n_chunks // nbuf)
        def _(g):
            c0 = g * nbuf
            gets = [pltpu.async_copy(table_hbm.at[pl.ds(base + (c0 + b) * chunk, chunk)], rows_v.at[b],
                                     rsem.at[b]) for b in range(nbuf)]
            puts = []
            for b in range(nbuf):
                gets[b].wait()
                for k in range(2):
                    puts.append(pltpu.async_copy(rows_v.at[b], out_hbm.at[idx_v.at[k * n_chunks + c0 + b]],
                                                 wsem.at[b]))
            for put in puts:
                put.wait()

    pos_w = pos.reshape(2, n_workers, n_chunks, chunk).transpose(1, 0, 2, 3).reshape(n_workers, 2 * n_chunks, chunk)
    return scatter(table, pos_w)


def _sc_gather_rows(table, idx):
    n_out = idx.shape[0]
    width = table.shape[1]
    n_cores, n_workers = _sc_workers()
    per_w = n_out // n_workers
    assert n_out % n_workers == 0
    chunk, nbuf = _sc_split(per_w)
    mesh = plsc.VectorSubcoreMesh(core_axis_name="c", subcore_axis_name="s")

    @functools.partial(
        pl.kernel, mesh=mesh,
        out_type=jax.ShapeDtypeStruct((n_out, width), table.dtype),
        scratch_types=[pltpu.VMEM((per_w,), jnp.int32),
                       pltpu.VMEM((nbuf, chunk, width), table.dtype),
                       pltpu.SemaphoreType.DMA((nbuf,)),
                       pltpu.SemaphoreType.DMA((nbuf,))],
    )
    def gather(table_hbm, idx_hbm, out_hbm, idx_v, rows_v, gsem, wsem):
        wid = lax.axis_index("s") * n_cores + lax.axis_index("c")
        base = wid * per_w
        pltpu.sync_copy(idx_hbm.at[pl.ds(base, per_w)], idx_v)

        @pl.loop(0, per_w // (chunk * nbuf))
        def _(g):
            off = g * (chunk * nbuf)
            gets = [pltpu.async_copy(table_hbm.at[idx_v.at[pl.ds(off + b * chunk, chunk)]], rows_v.at[b],
                                     gsem.at[b]) for b in range(nbuf)]
            puts = []
            for b in range(nbuf):
                gets[b].wait()
                puts.append(pltpu.async_copy(rows_v.at[b], out_hbm.at[pl.ds(base + off + b * chunk, chunk)],
                                             wsem.at[b]))
            for put in puts:
                put.wait()

    return gather(table, idx)


def _gmm_kernel(layer, te_ref, nv_ref, nu_ref, slot_ref, nxt_ref, xs_ref, wg_hbm, wu_hbm, wd_hbm, ys_ref,
                wgf_ref, wuf_ref, wdf_ref, wgb_ref, wub_ref, wdb_ref, sem):
    i = pl.program_id(0)
    used = i < nu_ref[0]
    fresh = jnp.logical_and(used, jnp.logical_or(i == 0, te_ref[i] != te_ref[jnp.maximum(i - 1, 0)]))

    def fetch(e, slot):
        return [pltpu.make_async_copy(wg_hbm.at[layer, e], wgf_ref.at[slot], sem.at[slot, 0]),
                pltpu.make_async_copy(wu_hbm.at[layer, e], wuf_ref.at[slot], sem.at[slot, 1]),
                pltpu.make_async_copy(wd_hbm.at[layer, e], wdf_ref.at[slot], sem.at[slot, 2])]

    @pl.when(jnp.logical_and(fresh, i == 0))
    def _():
        for cp in fetch(te_ref[0], 0):
            cp.start()

    @pl.when(fresh)
    def _():
        slot = slot_ref[i]

        @pl.when(nxt_ref[i] >= 0)
        def _():
            for cp in fetch(nxt_ref[i], 1 - slot):
                cp.start()

        for cp in fetch(te_ref[i], slot):
            cp.wait()
        wgb_ref[...] = wgf_ref[slot].astype(BF16)
        wub_ref[...] = wuf_ref[slot].astype(BF16)
        wdb_ref[...] = wdf_ref[slot].astype(BF16)

    @pl.when(used)
    def _():
        rowid = lax.broadcasted_iota(jnp.int32, xs_ref.shape, 0)
        lo, hi = _unpack_bf16_pairs(jnp.where(rowid < nv_ref[i], xs_ref[...], jnp.uint32(0)))
        lo, hi = lo.astype(BF16), hi.astype(BF16)
        half = lo.shape[1]
        x = jnp.concatenate([lo, hi], axis=1)
        a = _dot(x, wgb_ref[...])
        u = _dot(x, wub_ref[...])
        y = _dot((a * _sigmoid(a) * u).astype(BF16), wdb_ref[...])
        ys_ref[...] = _pack_bf16_pairs(y[:, 0:half], y[:, half:2 * half])

    @pl.when(jnp.logical_not(used))
    def _():
        ys_ref[...] = jnp.zeros(ys_ref.shape, jnp.uint32)


def _gmm(tile_expert, tile_valid, n_used, slot, nxt, xs, wg, wu, wd, layer):
    p_rows, half = xs.shape
    d = 2 * half
    rows = lambda: pl.BlockSpec((TM, half), lambda i, te, nv, nu, sl, nx: (i, 0))
    anyspec = pl.BlockSpec(memory_space=pl.ANY)
    return pl.pallas_call(
        functools.partial(_gmm_kernel, layer),
        grid_spec=pltpu.PrefetchScalarGridSpec(
            num_scalar_prefetch=5,
            grid=(p_rows // TM,),
            in_specs=[rows(), anyspec, anyspec, anyspec],
            out_specs=rows(),
            scratch_shapes=[pltpu.VMEM((2, d, D_EXPERT), F32), pltpu.VMEM((2, d, D_EXPERT), F32),
                            pltpu.VMEM((2, D_EXPERT, d), F32),
                            pltpu.VMEM((d, D_EXPERT), BF16), pltpu.VMEM((d, D_EXPERT), BF16),
                            pltpu.VMEM((D_EXPERT, d), BF16), pltpu.SemaphoreType.DMA((2, 3))]),
        out_shape=jax.ShapeDtypeStruct((p_rows, half), jnp.uint32),
        compiler_params=_params(("arbitrary",)),
        name="moe_experts",
    )(tile_expert, tile_valid, n_used, slot, nxt, xs, wg, wu, wd)


def _moefin_kernel(o1_ref, o2_ref, gt_ref, x_ref, mod_ref, pv_ref, out_ref):
    gt = gt_ref[...]
    g1, g2 = gt[:, 0:1], gt[:, 1:2]
    a1, b1 = _unpack_bf16_pairs(o1_ref[...])
    a2, b2 = _unpack_bf16_pairs(o2_ref[...])
    f = jnp.concatenate([g1 * a1 + g2 * a2, g1 * b1 + g2 * b2], axis=1)
    mod = mod_ref[0, 0]
    pv = pv_ref[0]
    out_ref[...] = _layer_norm(ALPHA * x_ref[...] + mod[5:6] * f, pv[3:4], pv[4:5])


def _moefin(o12, gt, x1, mod, pv, layer, n_b, l_, n_tiles):
    d = x1.shape[1]
    tpb = l_ // TM

    def mod_idx(j):
        return (layer, jnp.minimum(j // tpb, n_b), 0, 0)

    return pl.pallas_call(
        _moefin_kernel,
        grid=(n_tiles,),
        in_specs=[pl.BlockSpec((TM, d // 2), lambda j: (j, 0)),
                  pl.BlockSpec((TM, d // 2), lambda j: (j + n_tiles, 0)),
                  pl.BlockSpec((TM, LANES), lambda j: (j, 0)),
                  pl.BlockSpec((TM, d), lambda j: (j, 0)),
                  pl.BlockSpec((1, 1, 6, d), mod_idx),
                  pl.BlockSpec((1,) + pv.shape[1:], lambda j: (layer, 0, 0))],
        out_specs=pl.BlockSpec((TM, d), lambda j: (j, 0)),
        out_shape=jax.ShapeDtypeStruct((n_tiles * TM, d), F32),
        compiler_params=_params(("arbitrary",)),
        name="moe_combine",
    )(o12, o12, gt, x1, mod, pv)


def _moe_routed(h2u, route, gt, counts, wg, wu, wd, layer, x1, mod, pv, n_b, l_, n_tiles):
    n_tok = n_tiles * TM
    n_slots = -(-(2 * n_tok + N_EXPERTS * TM) // SC_ROW_QUANTUM) * SC_ROW_QUANTUM
    cnt = counts[:, 0].astype(jnp.int32)
    padded = ((cnt + TM - 1) // TM) * TM
    upto = jnp.arange(N_EXPERTS)[None, :] <= jnp.arange(N_EXPERTS)[:, None]
    ends = jnp.sum(jnp.where(upto, padded[None, :], 0), axis=1)
    offs = ends - padded
    e12 = route[0:2].astype(jnp.int32)
    r12 = route[2:4].astype(jnp.int32)
    onehot = e12[:, :, None] == jnp.arange(N_EXPERTS, dtype=jnp.int32)
    pos = r12 + jnp.sum(jnp.where(onehot, offs, 0), axis=-1)
    tile_start = jnp.arange(n_slots // TM, dtype=jnp.int32) * TM
    tile_expert = jnp.minimum(jnp.sum(tile_start[:, None] >= ends[None, :], axis=1), N_EXPERTS - 1)
    pick = tile_expert[:, None] == jnp.arange(N_EXPERTS, dtype=jnp.int32)
    tile_valid = jnp.clip(jnp.sum(jnp.where(pick, (offs + cnt)[None, :], 0), axis=1) - tile_start, 0, TM)
    n_used = (ends[N_EXPERTS - 1] // TM).reshape(1)
    xs = _sc_scatter_rows(h2u, pos, n_slots)
    n_tile = n_slots // TM
    t_idx = jnp.arange(n_tile, dtype=jnp.int32)
    prev_e = jnp.concatenate([jnp.full((1,), -1, jnp.int32), tile_expert[:-1].astype(jnp.int32)])
    first = (t_idx < n_used[0]) & (tile_expert != prev_e)
    switches = jnp.sum(jnp.where((t_idx[None, :] <= t_idx[:, None]) & first[None, :], 1, 0), axis=1)
    slot = (switches + 1) % 2
    ids = jnp.arange(N_EXPERTS, dtype=jnp.int32)
    later = (ids[None, :] > ids[:, None]) & (cnt[None, :] > 0)
    next_e = jnp.min(jnp.where(later, ids[None, :], N_EXPERTS), axis=1)
    next_e = jnp.where(next_e < N_EXPERTS, next_e, -1)
    nxt = jnp.sum(jnp.where(pick, next_e[None, :], 0), axis=1)
    ys = _gmm(tile_expert.astype(jnp.int32), tile_valid.astype(jnp.int32), n_used.astype(jnp.int32),
              slot.astype(jnp.int32), nxt.astype(jnp.int32), xs, wg, wu, wd, layer)
    o12 = _sc_gather_rows(ys, pos.reshape(2 * n_tok))
    return _moefin(o12, gt, x1, mod, pv, layer, n_b, l_, n_tiles)


def _rope_tables(l_):
    pos = np.arange(l_)
    nf = HEAD_DIM // 4
    inv = ROPE_BASE ** (-np.arange(nf, dtype=np.float64) / nf)
    ar = (pos // GRID_W)[:, None] * inv
    ac = (pos % GRID_W)[:, None] * inv
    cos = np.concatenate([np.cos(ar), np.cos(ar), np.cos(ac), np.cos(ac)], axis=1)
    sin = np.concatenate([-np.sin(ar), np.sin(ar), -np.sin(ac), np.sin(ac)], axis=1)
    cos = np.concatenate([np.tile(cos, (1, 2)), np.ones((TM, LANES))], axis=0)
    sin = np.concatenate([np.tile(sin, (1, 2)), np.zeros((TM, LANES))], axis=0)
    return cos.astype(np.float32), sin.astype(np.float32)


def _na_bias(rpb):
    u = np.arange(GRID_W)[:, None]
    v = np.arange(GRID_W)[None, :]
    cs = np.clip(u - NA_WIN_COLS // 2, 0, GRID_W - NA_WIN_COLS)
    colmask = (v >= cs) & (v < cs + NA_WIN_COLS)
    coff = np.clip(v - u + NA_WIN_COLS - 1, 0, 2 * NA_WIN_COLS - 2)
    n_off = 2 * NA_WIN_COLS - 1
    pick = (coff[None] == np.arange(n_off)[:, None, None]).astype(np.float32)
    band = jnp.einsum('lhab,buv->lahuv', rpb.astype(F32), pick, precision=lax.Precision.HIGHEST)
    band = jnp.where(colmask[None, None, None], band, NEG)
    band = band.reshape(rpb.shape[0], 2 * NA_WIN_ROWS - 1, NA_HEADS * GRID_W, GRID_W)
    return jnp.concatenate([band[:, :-1], band[:, 1:]], axis=-1)


def kernel(x, c, ctx, c_ctx, w_ada, b_ada, w_in, lb_logits, hgrn_norm, na_rpb, swa_sink, w_out,
           ln1_g, ln1_b, ln2_g, ln2_b, w_router, router_bias, w_gate, w_up, w_down):
    n_b, l_, d = x.shape
    ctx_len = ctx.shape[1]
    depth = w_ada.shape[0]
    assert n_b * ctx_len == TM and l_ % TM == 0 and n_b + 1 <= N_COND
    n_lat_tiles = (n_b * l_) // TM

    p_lb = jax.nn.softmax(lb_logits.astype(F32), axis=0).reshape(depth, 2 * HGRN_W)
    upto = (np.arange(depth)[None, :] <= np.arange(depth)[:, None]).astype(np.float32)
    lb = jnp.maximum(jnp.sum(upto[:, :, None] * p_lb[None], axis=1) - p_lb[0:1], LB_MIN)
    gp = jnp.stack([jnp.log(lb), jnp.log1p(-lb), 1.0 - lb] + [jnp.zeros_like(lb)] * 5, axis=1)

    cond = jnp.concatenate([c, c_ctx[None, :], jnp.zeros((N_COND - n_b - 1, d), F32)], axis=0)
    mod = _ada(cond.T, w_ada, b_ada, n_b + 1).reshape(depth, N_COND, 6, d)

    cos_t, sin_t = _rope_tables(l_)
    wrh = w_router.T.astype(BF16)
    rb = router_bias.astype(F32)[:, None]
    zeros = jnp.zeros_like(ln1_g)
    pv = jnp.stack([ln1_g, ln1_b, jnp.tile(hgrn_norm, (1, d // HEAD_DIM)), ln2_g, ln2_b, zeros, zeros, zeros],
                   axis=1).astype(F32)
    na_bias = _na_bias(na_rpb)
    sink = swa_sink.astype(F32)

    xa, xb, ctx_blk = x.reshape(n_b * l_, d), ctx.reshape(n_b * ctx_len, d), 0
    for l in range(depth):
        last = l == depth - 1
        hq, hv, hg, lf, kk, nqkv, sq, kx, vx = _inproj(xa, xb, ctx_blk, mod, w_in, gp, l, cos_t, sin_t, n_b, l_)
        o_f, o_b = _hgrn(hq, hv, lf, kk, n_b, l_, ctx_len)
        yb = _na(nqkv, na_bias, l, n_b, l_, ctx_len, not last)
        yc = _swa(sq, kx, vx, sink, l, n_b, l_, ctx_len, not last)
        n_tiles = n_lat_tiles if last else n_lat_tiles + 1
        x1, h2u, route, gt, counts = _outproj(o_f, o_b, hg, yb, yc, xa, xb, ctx_blk, mod, w_out, pv, l,
                                              wrh, rb, n_b, l_, n_tiles)
        xa = _moe_routed(h2u, route, gt, counts, w_gate, w_up, w_down, l, x1, mod, pv, n_b, l_, n_tiles)
        xb, ctx_blk = xa, n_lat_tiles
    return xa[:n_b * l_].reshape(n_b, l_, d)
```
